```python
import math
import jax, jax.numpy as jnp
from jax import lax
import numpy as np

D_MODEL = 1024
BATCH = 4
SEQ = 4096
DEPTH = 2
DEC_BATCH = 32
DEC_SEQ = 4
PAST_LEN = 16384
PAGE_SIZE = 128

N_EVEN = (DEPTH + 1) // 2
N_ODD = DEPTH // 2

D_LRU = D_MODEL // 2
LRU_BLOCKS = 8
LRU_BW = D_LRU // LRU_BLOCKS
CONV_W = 4
LRU_C = 8.0
D_S5 = D_MODEL // 2
S5_H = 16
S5_G = D_S5 // S5_H
S5_P = 64
D_IN_EVEN = 2 * D_LRU + D_S5

WINDOWS = (128, 512, 2048)
DILATIONS = (1, 4, 16)
N_GROUPS_C = 3
HG = 8
DH = 64
N_HEADS_C = N_GROUPS_C * HG
D_C = N_HEADS_C * DH
N_STRIDE = WINDOWS[0] // DILATIONS[0]
QB = 128

REL_BUCKETS = 32
REL_MAX_DIST = WINDOWS[-1]

N_MEM = 256
XA_HEADS = 4
XA_DH = D_MODEL // XA_HEADS
D_XA = XA_HEADS * XA_DH

D_FF = 3 * D_MODEL
N_EXPERTS = 8
TOP_K = 2
D_FF_E = 7 * D_MODEL // 2

EPS = 1e-6
NEG = -1e30

kernel_name = 'hybrid_rglru_s5_dilated_decoder_step'


def _rmsnorm(x, g):
    xf = x.astype(jnp.float32)
    y = xf * lax.rsqrt(jnp.mean(xf * xf, axis=-1, keepdims=True) + EPS)
    return (y * g.astype(jnp.float32)).astype(x.dtype)


def _rel_bucket(dist):
    dist = np.clip(np.asarray(dist), 0, None)
    max_exact = REL_BUCKETS // 2
    safe = np.maximum(dist, max_exact).astype(np.float32)
    large = max_exact + np.floor(np.log(safe / max_exact) / math.log(REL_MAX_DIST / max_exact)
                                 * (REL_BUCKETS - max_exact)).astype(np.int32)
    large = np.minimum(large, REL_BUCKETS - 1)
    return np.where(dist < max_exact, dist, large).astype(np.int32)


def _linear_scan(a, b, h0):
    b = b.at[:, 0].add(a[:, 0] * h0)
    def comb(l, r):
        return l[0] * r[0], r[0] * l[1] + r[1]
    return lax.associative_scan(comb, (a, b), axis=1)[1]


def _complex_scan(ar, ai, br, bi, h0r, h0i):
    br = br.at[:, 0].add(ar[:, 0] * h0r - ai[:, 0] * h0i)
    bi = bi.at[:, 0].add(ar[:, 0] * h0i + ai[:, 0] * h0r)
    def comb(l, r):
        lar, lai, lbr, lbi = l
        rar, rai, rbr, rbi = r
        return (rar * lar - rai * lai, rar * lai + rai * lar,
                rar * lbr - rai * lbi + rbr, rar * lbi + rai * lbr + rbi)
    out = lax.associative_scan(comb, (ar, ai, br, bi), axis=1)
    return out[2], out[3]


def _causal_conv(x, prev, w, b):
    t_len = x.shape[1]
    xp = jnp.concatenate([prev.astype(x.dtype), x], axis=1)
    y = b
    for k in range(CONV_W):
        y = y + xp[:, k:k + t_len] * w[k]
    return y, xp[:, -(CONV_W - 1):]


def _mixer_rglru_s5(h, w_in, conv_w, conv_b, lru_wa, lru_ba, lru_wx, lru_bx, lru_lam,
                    s5_lam_re, s5_lam_im, s5_log_dt, s5_b_re, s5_b_im, s5_c_re, s5_c_im,
                    s5_d, s5_w_glu, s5_b_glu, w_out, conv_prev, h_prev, s_prev_re, s_prev_im):
    f32 = jnp.float32
    bn, t_len, _ = h.shape
    z = h @ w_in
    xa, ga, u = z[..., :D_LRU], z[..., D_LRU:2 * D_LRU], z[..., 2 * D_LRU:]
    xc, conv_new = _causal_conv(xa, conv_prev, conv_w, conv_b)
    xcf = xc.astype(f32)
    xb = xcf.reshape(bn, t_len, LRU_BLOCKS, LRU_BW)
    r = jax.nn.sigmoid(jnp.einsum('btni,nij->btnj', xb, lru_wa.astype(f32)).reshape(bn, t_len, D_LRU) + lru_ba)
    ig = jax.nn.sigmoid(jnp.einsum('btni,nij->btnj', xb, lru_wx.astype(f32)).reshape(bn, t_len, D_LRU) + lru_bx)
    log_a = -LRU_C * r * jax.nn.softplus(-lru_lam.astype(f32))
    a = jnp.exp(log_a)
    bterm = jnp.sqrt(-jnp.expm1(2.0 * log_a)) * ig * xcf
    hs = _linear_scan(a, bterm, h_prev.astype(f32))
    ya = hs * jax.nn.gelu(ga.astype(f32))
    dt = jnp.exp(s5_log_dt.astype(f32))[:, None]
    lr = s5_lam_re.astype(f32)
    li = s5_lam_im.astype(f32)
    mag = jnp.exp(lr * dt)
    ab_r = mag * jnp.cos(li * dt)
    ab_i = mag * jnp.sin(li * dt)
    den = lr * lr + li * li
    nr = ab_r - 1.0
    cr = (nr * lr + ab_i * li) / den
    ci = (ab_i * lr - nr * li) / den
    b_r = s5_b_re.astype(f32)
    b_i = s5_b_im.astype(f32)
    bb_r = cr[..., None] * b_r - ci[..., None] * b_i
    bb_i = cr[..., None] * b_i + ci[..., None] * b_r
    uf = u.astype(f32)
    ug = uf.reshape(bn, t_len, S5_G, S5_H)
    bu_r = jnp.einsum('btgh,gph->btgp', ug, bb_r)
    bu_i = jnp.einsum('btgh,gph->btgp', ug, bb_i)
    shp = bu_r.shape
    xr, xi = _complex_scan(jnp.broadcast_to(ab_r, shp), jnp.broadcast_to(ab_i, shp), bu_r, bu_i,
                           s_prev_re.astype(f32), s_prev_im.astype(f32))
    ys = (jnp.einsum('btgp,ghp->btgh', xr, s5_c_re.astype(f32))
          - jnp.einsum('btgp,ghp->btgh', xi, s5_c_im.astype(f32)))
    ys = ys.reshape(bn, t_len, D_S5) + s5_d * uf
    gs = jax.nn.gelu(ys)
    yb = gs * jax.nn.sigmoid(gs @ s5_w_glu.astype(f32) + s5_b_glu)
    y = jnp.concatenate([ya, yb], axis=-1).astype(h.dtype) @ w_out
    return y, conv_new, hs[:, -1], xr[:, -1], xi[:, -1]


def _dilated_prompt(q, k, v, tab, d):
    f32 = jnp.float32
    bn, s_len = q.shape[:2]
    l_len = s_len // d
    qb = min(QB, l_len)
    nb = -(-l_len // qb)
    lp = nb * qb
    def to_blocks(t):
        t = jnp.moveaxis(t.reshape((bn, l_len, d) + t.shape[2:]), 2, 1)
        t = jnp.pad(t, ((0, 0), (0, 0), (0, lp - l_len)) + ((0, 0),) * (t.ndim - 3))
        return t.reshape((bn, d, nb, qb) + t.shape[3:])
    def from_blocks(t):
        t = t.reshape((bn, d, lp) + t.shape[4:])[:, :, :l_len]
        t = jnp.moveaxis(t, 1, 2)
        return t.reshape((bn, s_len) + t.shape[3:])
    def with_prev(t):
        prev = jnp.pad(t, ((0, 0), (0, 0), (1, 0), (0, 0), (0, 0), (0, 0)))[:, :, :-1]
        return jnp.concatenate([prev, t], axis=3)
    qbk = to_blocks(q).astype(f32)
    kk = with_prev(to_blocks(k)).astype(f32)
    vv = with_prev(to_blocks(v)).astype(f32)
    qi = np.arange(qb)[:, None]
    ki = np.arange(2 * qb)[None, :]
    dist = qi + qb - ki
    band = (dist >= 0) & (dist <= N_STRIDE)
    first = band & (ki >= qb)
    valid = np.where(np.arange(nb)[:, None, None] == 0, first[None], band[None])
    bias = jnp.transpose(tab[_rel_bucket(dist * d)], (2, 0, 1)).astype(f32)
    s = jnp.einsum('brnqhe,brnkhe->brnhqk', qbk, kk) * (DH ** -0.5) + bias
    s = jnp.where(valid[None, None, :, None], s, NEG)
    m = jnp.max(s, axis=-1, keepdims=True)
    p = jnp.exp(s - m)
    den = jnp.sum(p, axis=-1, keepdims=True)
    o = jnp.einsum('brnhqk,brnkhe->brnqhe', p, vv) / jnp.swapaxes(den, 3, 4)
    lse = jnp.swapaxes((m + jnp.log(den))[..., 0], 3, 4)
    return from_blocks(o), from_blocks(lse)


def _dilated_sample(q, k, v, buf, tab, d):
    f32 = jnp.float32
    t_len = q.shape[1]
    wb = buf.shape[1]
    full = jnp.concatenate([buf, jnp.stack([k, v], axis=2).astype(buf.dtype)], axis=1)
    idx = wb + np.arange(t_len)[:, None] - d * np.arange(N_STRIDE + 1)[None, :]
    valid = idx >= 0
    idxc = np.maximum(idx, 0)
    kg = full[:, idxc, 0].astype(f32)
    vg = full[:, idxc, 1].astype(f32)
    bias = tab[_rel_bucket(d * np.arange(N_STRIDE + 1))].T.astype(f32)
    s = jnp.einsum('bthe,btkhe->bthk', q.astype(f32), kg) * (DH ** -0.5) + bias
    s = jnp.where(valid[None, :, None, :], s, NEG)
    m = jnp.max(s, axis=-1, keepdims=True)
    p = jnp.exp(s - m)
    den = jnp.sum(p, axis=-1, keepdims=True)
    o = jnp.einsum('bthk,btkhe->bthe', p, vg) / den
    lse = (m + jnp.log(den))[..., 0]
    return o, lse, full[:, -wb:]


def _mixer_dilated(h, w_qkv, q_norm, k_norm, w_o, rel_bias, bufs):
    bn, t_len, _ = h.shape
    qkv = (h @ w_qkv).reshape(bn, t_len, 3, N_HEADS_C, DH)
    q = _rmsnorm(qkv[:, :, 0], q_norm)
    k = _rmsnorm(qkv[:, :, 1], k_norm)
    v = qkv[:, :, 2]
    outs, lses, new_bufs = [], [], []
    for g in range(N_GROUPS_C):
        sl = slice(g * HG, (g + 1) * HG)
        tab = rel_bias[:, sl]
        if bufs is None:
            o, l = _dilated_prompt(q[:, :, sl], k[:, :, sl], v[:, :, sl], tab, DILATIONS[g])
            nbuf = jnp.stack([k[:, :, sl], v[:, :, sl]], axis=2)[:, -min(WINDOWS[g], t_len):]
        else:
            o, l, nbuf = _dilated_sample(q[:, :, sl], k[:, :, sl], v[:, :, sl], bufs[g], tab, DILATIONS[g])
        outs.append(o)
        lses.append(l)
        new_bufs.append(nbuf)
    o_all = jnp.stack(outs, axis=2)
    w = jax.nn.softmax(jnp.stack(lses, axis=2), axis=2)
    o = jnp.sum(w[..., None] * o_all, axis=2).reshape(bn, t_len, HG * DH)
    return o.astype(h.dtype) @ w_o, new_bufs


def _memory_kv(mem, g, w_kv, k_norm):
    bn, n = mem.shape[:2]
    kv = (_rmsnorm(mem, g) @ w_kv).reshape(bn, n, 2, XA_HEADS, XA_DH)
    return _rmsnorm(kv[:, :, 0], k_norm), kv[:, :, 1]


def _cross_attn(h, w_q, q_norm, mk, mv, w_o):
    bn, t_len, _ = h.shape
    q = _rmsnorm((h @ w_q).reshape(bn, t_len, XA_HEADS, XA_DH), q_norm)
    s = jnp.einsum('bthe,bnhe->bhtn', q.astype(jnp.float32), mk.astype(jnp.float32)) * (XA_DH ** -0.5)
    p = jax.nn.softmax(s, axis=-1)
    o = jnp.einsum('bhtn,bnhe->bthe', p, mv.astype(jnp.float32)).reshape(bn, t_len, D_XA)
    return o.astype(h.dtype) @ w_o


def _swiglu(h, w_gu, w_d):
    gu = h @ w_gu
    hid = gu.shape[-1] // 2
    return (jax.nn.silu(gu[..., :hid]) * gu[..., hid:]) @ w_d


def _moe(h, router_w, router_b, w_gu, w_d):
    bn, t_len, dm = h.shape
    hf = h.reshape(bn * t_len, dm)
    logits = hf.astype(jnp.float32) @ router_w.astype(jnp.float32) + router_b
    top_v, top_i = lax.top_k(logits, TOP_K)
    gates = jax.nn.softmax(top_v, axis=-1)
    dense = jnp.einsum('nk,nke->ne', gates, jax.nn.one_hot(top_i, N_EXPERTS, dtype=jnp.float32))
    y = jnp.zeros((bn * t_len, dm), jnp.float32)
    for e in range(N_EXPERTS):
        y = y + dense[:, e:e + 1] * _swiglu(hf, w_gu[e], w_d[e]).astype(jnp.float32)
    return y.reshape(bn, t_len, dm).astype(h.dtype)


def setup_inputs(seed: int = 0) -> dict:
    key = jax.random.key(seed)
    ks = iter(jax.random.split(key, 64))
    f32 = jnp.float32
    def nrm(shape, scale):
        return scale * jax.random.normal(next(ks), shape, f32)
    def gain(shape):
        return 1.0 + 0.02 * jax.random.normal(next(ks), shape, f32)
    def unif(shape, lo, hi):
        return jax.random.uniform(next(ks), shape, f32, lo, hi)
    wins = [min(w, PAST_LEN) for w in WINDOWS]
    inp = {}
    inp['x_prompt'] = nrm((BATCH, SEQ, D_MODEL), 1.0)
    inp['x_sample'] = nrm((DEC_BATCH, DEC_SEQ, D_MODEL), 1.0)
    inp['state_conv'] = nrm((N_EVEN, DEC_BATCH, CONV_W - 1, D_LRU), 1.0)
    inp['state_lru'] = nrm((N_EVEN, DEC_BATCH, D_LRU), 0.5)
    inp['state_s5_re'] = nrm((N_EVEN, DEC_BATCH, S5_G, S5_P), 0.3)
    inp['state_s5_im'] = nrm((N_EVEN, DEC_BATCH, S5_G, S5_P), 0.3)
    inp['cache_swa0_kv'] = nrm((N_ODD, DEC_BATCH, wins[0], 2, HG, DH), 1.0)
    inp['cache_swa1_kv'] = nrm((N_ODD, DEC_BATCH, wins[1], 2, HG, DH), 1.0)
    inp['cache_swa2_kv'] = nrm((N_ODD, DEC_BATCH, wins[2], 2, HG, DH), 1.0)
    inp['cache_mem_k'] = nrm((DEPTH, DEC_BATCH, N_MEM, XA_HEADS, XA_DH), 1.0)
    inp['cache_mem_v'] = nrm((DEPTH, DEC_BATCH, N_MEM, XA_HEADS, XA_DH), 1.0)
    inp['mem_prompt'] = nrm((BATCH, N_MEM, D_MODEL), 1.0)
    inp['norm_mix'] = gain((DEPTH, D_MODEL))
    inp['norm_xa'] = gain((DEPTH, D_MODEL))
    inp['norm_ffn'] = gain((DEPTH, D_MODEL))
    inp['norm_mem'] = gain((DEPTH, D_MODEL))
    inp['w_in_even'] = nrm((N_EVEN, D_MODEL, D_IN_EVEN), D_MODEL ** -0.5)
    inp['conv_w'] = nrm((N_EVEN, CONV_W, D_LRU), CONV_W ** -0.5)
    inp['conv_b'] = nrm((N_EVEN, D_LRU), 0.01)
    inp['lru_wa'] = nrm((N_EVEN, LRU_BLOCKS, LRU_BW, LRU_BW), LRU_BW ** -0.5)
    inp['lru_ba'] = nrm((N_EVEN, D_LRU), 0.1)
    inp['lru_wx'] = nrm((N_EVEN, LRU_BLOCKS, LRU_BW, LRU_BW), LRU_BW ** -0.5)
    inp['lru_bx'] = nrm((N_EVEN, D_LRU), 0.1)
    a0 = unif((N_EVEN, D_LRU), 0.9, 0.999) ** (1.0 / LRU_C)
    inp['lru_lam'] = jnp.log(a0) - jnp.log1p(-a0)
    inp['s5_lam_re'] = -0.5 + nrm((N_EVEN, S5_G, S5_P), 0.01)
    inp['s5_lam_im'] = jnp.pi * jnp.arange(S5_P, dtype=f32)[None, None, :] + nrm((N_EVEN, S5_G, S5_P), 0.01)
    inp['s5_log_dt'] = unif((N_EVEN, S5_G), math.log(0.001), math.log(0.1))
    inp['s5_b_re'] = nrm((N_EVEN, S5_G, S5_P, S5_H), (2 * S5_H) ** -0.5)
    inp['s5_b_im'] = nrm((N_EVEN, S5_G, S5_P, S5_H), (2 * S5_H) ** -0.5)
    inp['s5_c_re'] = nrm((N_EVEN, S5_G, S5_H, S5_P), S5_P ** -0.5)
    inp['s5_c_im'] = nrm((N_EVEN, S5_G, S5_H, S5_P), S5_P ** -0.5)
    inp['s5_d'] = nrm((N_EVEN, D_S5), 1.0)
    inp['s5_w_glu'] = nrm((N_EVEN, D_S5, D_S5), D_S5 ** -0.5)
    inp['s5_b_glu'] = nrm((N_EVEN, D_S5), 0.01)
    inp['w_out_even'] = nrm((N_EVEN, D_LRU + D_S5, D_MODEL), (D_LRU + D_S5) ** -0.5)
    inp['w_qkv_odd'] = nrm((N_ODD, D_MODEL, 3 * D_C), D_MODEL ** -0.5)
    inp['q_norm_odd'] = gain((N_ODD, DH))
    inp['k_norm_odd'] = gain((N_ODD, DH))
    inp['w_o_odd'] = nrm((N_ODD, HG * DH, D_MODEL), (HG * DH) ** -0.5)
    inp['rel_bias'] = nrm((REL_BUCKETS, N_HEADS_C), 0.5)
    inp['xa_wq'] = nrm((DEPTH, D_MODEL, D_XA), D_MODEL ** -0.5)
    inp['xa_wkv'] = nrm((DEPTH, D_MODEL, 2 * D_XA), D_MODEL ** -0.5)
    inp['xa_qn'] = gain((DEPTH, XA_DH))
    inp['xa_kn'] = gain((DEPTH, XA_DH))
    inp['xa_wo'] = nrm((DEPTH, D_XA, D_MODEL), D_XA ** -0.5)
    inp['ffn_w_gu'] = nrm((N_EVEN, D_MODEL, 2 * D_FF), D_MODEL ** -0.5)
    inp['ffn_w_down'] = nrm((N_EVEN, D_FF, D_MODEL), D_FF ** -0.5)
    inp['moe_router_w'] = nrm((N_ODD, D_MODEL, N_EXPERTS), D_MODEL ** -0.5)
    inp['moe_router_b'] = nrm((N_ODD, N_EXPERTS), 0.01)
    inp['moe_w_gu'] = nrm((N_ODD, N_EXPERTS, D_MODEL, 2 * D_FF_E), D_MODEL ** -0.5)
    inp['moe_w_down'] = nrm((N_ODD, N_EXPERTS, D_FF_E, D_MODEL), D_FF_E ** -0.5)
    return inp


def reference(x_prompt, x_sample, state_conv, state_lru, state_s5_re, state_s5_im,
              cache_swa0_kv, cache_swa1_kv, cache_swa2_kv, cache_mem_k, cache_mem_v, mem_prompt,
              norm_mix, norm_xa, norm_ffn, norm_mem,
              w_in_even, conv_w, conv_b, lru_wa, lru_ba, lru_wx, lru_bx, lru_lam,
              s5_lam_re, s5_lam_im, s5_log_dt, s5_b_re, s5_b_im, s5_c_re, s5_c_im, s5_d,
              s5_w_glu, s5_b_glu, w_out_even,
              w_qkv_odd, q_norm_odd, k_norm_odd, w_o_odd, rel_bias,
              xa_wq, xa_wkv, xa_qn, xa_kn, xa_wo,
              ffn_w_gu, ffn_w_down, moe_router_w, moe_router_b, moe_w_gu, moe_w_down):
    yp, ys = x_prompt, x_sample
    bp = x_prompt.shape[0]
    p_conv, p_lru, p_s5r, p_s5i = [], [], [], []
    s_conv, s_lru, s_s5r, s_s5i = [], [], [], []
    p_swa = [[] for _ in range(N_GROUPS_C)]
    s_swa = [[] for _ in range(N_GROUPS_C)]
    p_mk, p_mv = [], []
    for layer in range(DEPTH):
        i = layer // 2
        if layer % 2 == 0:
            w = (w_in_even[i], conv_w[i], conv_b[i], lru_wa[i], lru_ba[i], lru_wx[i], lru_bx[i], lru_lam[i],
                 s5_lam_re[i], s5_lam_im[i], s5_log_dt[i], s5_b_re[i], s5_b_im[i], s5_c_re[i], s5_c_im[i],
                 s5_d[i], s5_w_glu[i], s5_b_glu[i], w_out_even[i])
            z_conv = jnp.zeros((bp, CONV_W - 1, D_LRU), yp.dtype)
            z_h = jnp.zeros((bp, D_LRU), jnp.float32)
            z_s = jnp.zeros((bp, S5_G, S5_P), jnp.float32)
            mp, c, hl, sr, si = _mixer_rglru_s5(_rmsnorm(yp, norm_mix[layer]), *w, z_conv, z_h, z_s, z_s)
            yp = yp + mp
            p_conv.append(c)
            p_lru.append(hl)
            p_s5r.append(sr)
            p_s5i.append(si)
            ms, c, hl, sr, si = _mixer_rglru_s5(_rmsnorm(ys, norm_mix[layer]), *w, state_conv[i], state_lru[i],
                                                state_s5_re[i], state_s5_im[i])
            ys = ys + ms
            s_conv.append(c)
            s_lru.append(hl)
            s_s5r.append(sr)
            s_s5i.append(si)
        else:
            w = (w_qkv_odd[i], q_norm_odd[i], k_norm_odd[i], w_o_odd[i], rel_bias)
            mp, bufs_p = _mixer_dilated(_rmsnorm(yp, norm_mix[layer]), *w, None)
            yp = yp + mp
            ms, bufs_s = _mixer_dilated(_rmsnorm(ys, norm_mix[layer]), *w,
                                        (cache_swa0_kv[i], cache_swa1_kv[i], cache_swa2_kv[i]))
            ys = ys + ms
            for g in range(N_GROUPS_C):
                p_swa[g].append(bufs_p[g])
                s_swa[g].append(bufs_s[g])
        mk, mv = _memory_kv(mem_prompt, norm_mem[layer], xa_wkv[layer], xa_kn[layer])
        p_mk.append(mk)
        p_mv.append(mv)
        yp = yp + _cross_attn(_rmsnorm(yp, norm_xa[layer]), xa_wq[layer], xa_qn[layer], mk, mv, xa_wo[layer])
        ys = ys + _cross_attn(_rmsnorm(ys, norm_xa[layer]), xa_wq[layer], xa_qn[layer],
                              cache_mem_k[layer], cache_mem_v[layer], xa_wo[layer])
        if layer % 2 == 0:
            yp = yp + _swiglu(_rmsnorm(yp, norm_ffn[layer]), ffn_w_gu[i], ffn_w_down[i])
            ys = ys + _swiglu(_rmsnorm(ys, norm_ffn[layer]), ffn_w_gu[i], ffn_w_down[i])
        else:
            yp = yp + _moe(_rmsnorm(yp, norm_ffn[layer]), moe_router_w[i], moe_router_b[i], moe_w_gu[i], moe_w_down[i])
            ys = ys + _moe(_rmsnorm(ys, norm_ffn[layer]), moe_router_w[i], moe_router_b[i], moe_w_gu[i], moe_w_down[i])
    p_state_conv = jnp.stack(p_conv)
    p_state_lru = jnp.stack(p_lru)
    p_state_s5_re = jnp.stack(p_s5r)
    p_state_s5_im = jnp.stack(p_s5i)
    p_cache_swa0_kv = jnp.stack(p_swa[0])
    p_cache_swa1_kv = jnp.stack(p_swa[1])
    p_cache_swa2_kv = jnp.stack(p_swa[2])
    p_cache_mem_k = jnp.stack(p_mk)
    p_cache_mem_v = jnp.stack(p_mv)
    s_state_conv = jnp.stack(s_conv)
    s_state_lru = jnp.stack(s_lru)
    s_state_s5_re = jnp.stack(s_s5r)
    s_state_s5_im = jnp.stack(s_s5i)
    s_cache_swa0_kv = jnp.stack(s_swa[0])
    s_cache_swa1_kv = jnp.stack(s_swa[1])
    s_cache_swa2_kv = jnp.stack(s_swa[2])
    return (yp, ys, p_state_conv, p_state_lru, p_state_s5_re, p_state_s5_im,
            p_cache_swa0_kv, p_cache_swa1_kv, p_cache_swa2_kv, p_cache_mem_k, p_cache_mem_v,
            s_state_conv, s_state_lru, s_state_s5_re, s_state_s5_im,
            s_cache_swa0_kv, s_cache_swa1_kv, s_cache_swa2_kv)
```

```python
import functools
import math

import jax
import jax.numpy as jnp
import numpy as np
from jax import lax
from jax.experimental import pallas as pl
from jax.experimental.pallas import tpu as pltpu

F32 = jnp.float32
BF16 = jnp.bfloat16
I32 = jnp.int32

EPS = 1e-6
NEG = -1e30

V7X_SUBLANES = 8
V7X_LANES = 128
V7X_VMEM_BYTES = 64 * 1024 * 1024
VMEM_LIMIT = V7X_VMEM_BYTES - 8 * 1024 * 1024

LRU_C = 8.0
CONV_W = 4
WINDOWS = (128, 512, 2048)
DILATIONS = (1, 4, 16)
HG = 8
DH = 64
N_STRIDE = 128
QB = 128
REL_BUCKETS = 32
REL_MAX_DIST = WINDOWS[-1]
XA_HEADS = 4
N_EXPERTS = 8
TOP_K = 2
SAMPLE_PAD_T = 16

MIX_TC = 256
MOE_TM = 256
MOE_TH = 512
FFN_TH = 512


def _cparams(sem):
    return pltpu.CompilerParams(dimension_semantics=sem, vmem_limit_bytes=VMEM_LIMIT)


def _const_spec(shape):
    nd = len(shape)
    return pl.BlockSpec(shape, lambda *_: (0,) * nd)


def _rms(x, g):
    return x * lax.rsqrt(jnp.mean(x * x, axis=-1, keepdims=True) + EPS) * g


def _norm_matmul_kernel(x_ref, g_ref, w_ref, hg_ref, hs_ref, o_ref, hn_ref, *, n_norm_tiles, dh):
    j = pl.program_id(1)

    @pl.when(j == 0)
    def _():
        hn_ref[...] = _rms(x_ref[...], g_ref[...]).astype(BF16)

    y = jnp.dot(hn_ref[...], w_ref[...], preferred_element_type=F32)
    if n_norm_tiles == 0:
        o_ref[...] = y
    else:
        @pl.when(j < n_norm_tiles)
        def _():
            ssq = jnp.dot((y * y).astype(BF16), hs_ref[...], preferred_element_type=F32)
            o_ref[...] = y * lax.rsqrt(ssq * (1.0 / dh) + EPS) * hg_ref[...]

        @pl.when(j >= n_norm_tiles)
        def _():
            o_ref[...] = y


def _head_sum_matrix(tn, dh):
    idx = np.arange(tn) // dh
    return jnp.asarray((idx[:, None] == idx[None, :]).astype(np.float32), dtype=BF16)


def _norm_matmul(x, g, w, *, tn=512, head_gain=None, n_norm_cols=0, dh=1, name):
    m, d = x.shape
    n = w.shape[1]
    tm = min(m, 1024)
    assert m % tm == 0 and n % tn == 0 and n_norm_cols % tn == 0
    if head_gain is None:
        head_gain = jnp.ones((1, n), F32)
    hs = _head_sum_matrix(tn, dh)
    kern = functools.partial(_norm_matmul_kernel, n_norm_tiles=n_norm_cols // tn, dh=dh)
    return pl.pallas_call(
        kern,
        out_shape=jax.ShapeDtypeStruct((m, n), F32),
        grid=(m // tm, n // tn),
        in_specs=[
            pl.BlockSpec((tm, d), lambda i, j: (i, 0)),
            pl.BlockSpec((1, d), lambda i, j: (0, 0)),
            pl.BlockSpec((d, tn), lambda i, j: (0, j)),
            pl.BlockSpec((1, tn), lambda i, j: (0, j)),
            pl.BlockSpec((tn, tn), lambda i, j: (0, 0)),
        ],
        out_specs=pl.BlockSpec((tm, tn), lambda i, j: (i, j)),
        scratch_shapes=[pltpu.VMEM((tm, d), BF16)],
        compiler_params=_cparams(("parallel", "arbitrary")),
        name=name,
    )(x, g.reshape(1, d), w, head_gain, hs)


def _s5_prep_kernel(lre_ref, lim_ref, ldt_ref, bre_ref, bim_ref, bbr_ref, bbi_ref, apow_ref):
    lr = lre_ref[...]
    li = lim_ref[...]
    dt = jnp.exp(ldt_ref[...])
    mag = jnp.exp(lr * dt)
    ab_r = mag * jnp.cos(li * dt)
    ab_i = mag * jnp.sin(li * dt)
    den = lr * lr + li * li
    nr = ab_r - 1.0
    cr = (nr * lr + ab_i * li) / den
    ci = (ab_i * lr - nr * li) / den
    b_r = bre_ref[...]
    b_i = bim_ref[...]
    bbr_ref[...] = (cr * b_r - ci * b_i).astype(BF16)
    bbi_ref[...] = (cr * b_i + ci * b_r).astype(BF16)

    n = lr.shape[1]
    row = lax.broadcasted_iota(I32, (V7X_SUBLANES, n), 0)

    def power(kf):
        mg = jnp.exp(kf * (lr * dt))
        return mg * jnp.cos(kf * (li * dt)), mg * jnp.sin(kf * (li * dt))

    for i, s in enumerate((1, 2, 4)):
        pr, pi = power(jnp.full((V7X_SUBLANES, n), s, F32))
        keep = row >= s
        apow_ref[2 * i] = jnp.where(keep, pr, 0.0)
        apow_ref[2 * i + 1] = jnp.where(keep, pi, 0.0)
    pr, pi = power((row + 1).astype(F32))
    apow_ref[6] = pr
    apow_ref[7] = pi


def _s5_prep(lam_re, lam_im, log_dt, b_re, b_im):
    g, p, h = b_re.shape
    n = g * p
    eye = jnp.eye(g, dtype=F32)
    bre_bd = jnp.einsum("gph,gk->khgp", b_re, eye).reshape(g * h, n)
    bim_bd = jnp.einsum("gph,gk->khgp", b_im, eye).reshape(g * h, n)
    ldt = jnp.broadcast_to(log_dt[:, None], (g, p)).reshape(1, n)
    return pl.pallas_call(
        _s5_prep_kernel,
        out_shape=(
            jax.ShapeDtypeStruct((g * h, n), BF16),
            jax.ShapeDtypeStruct((g * h, n), BF16),
            jax.ShapeDtypeStruct((8, V7X_SUBLANES, n), F32),
        ),
        compiler_params=pltpu.CompilerParams(vmem_limit_bytes=VMEM_LIMIT),
        name="s5_prep",
    )(lam_re.reshape(1, n), lam_im.reshape(1, n), ldt, bre_bd, bim_bd)


def _mixer0_kernel(z_ref, x_ref, convi_ref, h0_ref, sr0_ref, si0_ref,
                   cw_ref, cb_ref, wa_ref, ba_ref, wx_ref, bx_ref, lam_ref,
                   bbr_ref, bbi_ref, apow_ref, ccr_ref, cci_ref, d_ref, wglu_ref, bglu_ref, wout_ref,
                   y_ref, hl_ref, srl_ref, sil_ref,
                   ext_ref, hc_ref, src_ref, sic_ref, xr_ref, xi_ref, ha_ref, hb_ref,
                   *, tc, last_row):
    c = pl.program_id(1)
    d_lru = cw_ref.shape[1]
    d_s5 = d_ref.shape[1]
    n_tiles = tc // V7X_SUBLANES

    @pl.when(c == 0)
    def _():
        ext_ref[...] = convi_ref[...]
        hc_ref[...] = h0_ref[...]
        src_ref[...] = sr0_ref[...]
        sic_ref[...] = si0_ref[...]

    z = z_ref[...]
    xa = z[:, :d_lru]
    ga = z[:, d_lru:2 * d_lru]
    u = z[:, 2 * d_lru:]

    ext = jnp.concatenate([ext_ref[...], xa], axis=0)
    xc = cb_ref[...] + xa * cw_ref[CONV_W - 1:CONV_W, :]
    for s in range(1, CONV_W):
        xc = xc + pltpu.roll(ext, s, 0)[V7X_SUBLANES:, :] * cw_ref[CONV_W - 1 - s:CONV_W - s, :]
    ext_ref[...] = xa[tc - V7X_SUBLANES:, :]

    xcb = xc.astype(BF16)
    r = jax.nn.sigmoid(jnp.dot(xcb, wa_ref[...], preferred_element_type=F32) + ba_ref[...])
    ig = jax.nn.sigmoid(jnp.dot(xcb, wx_ref[...], preferred_element_type=F32) + bx_ref[...])
    lam = lam_ref[...]
    softplus_neg = jnp.maximum(-lam, 0.0) + jnp.log1p(jnp.exp(-jnp.abs(lam)))
    log_a = -LRU_C * r * softplus_neg
    a = jnp.exp(log_a)
    bt = jnp.sqrt(-jnp.tanh(log_a) * (a * a + 1.0)) * ig * xc

    a3 = a.reshape(n_tiles, V7X_SUBLANES, d_lru)
    b3 = bt.reshape(n_tiles, V7X_SUBLANES, d_lru)
    row = lax.broadcasted_iota(I32, (1, V7X_SUBLANES, d_lru), 1)
    for s in (1, 2, 4):
        keep = row >= s
        ar = pltpu.roll(a3, s, 1)
        br = pltpu.roll(b3, s, 1)
        b3 = jnp.where(keep, a3 * br + b3, b3)
        a3 = jnp.where(keep, a3 * ar, a3)
    ha_ref[...] = a3.reshape(tc, d_lru)
    hb_ref[...] = b3.reshape(tc, d_lru)

    def lru_tile(i, carry):
        r0 = pl.multiple_of(i * V7X_SUBLANES, V7X_SUBLANES)
        h = ha_ref[pl.ds(r0, V7X_SUBLANES), :] * carry + hb_ref[pl.ds(r0, V7X_SUBLANES), :]
        hb_ref[pl.ds(r0, V7X_SUBLANES), :] = h
        return h[V7X_SUBLANES - 1:, :]

    hc_ref[...] = lax.fori_loop(0, n_tiles, lru_tile, hc_ref[...])
    hs = hb_ref[...]
    hl_ref[...] = hb_ref[last_row:last_row + 1, :]
    ya = hs * jax.nn.gelu(ga)

    ub = u.astype(BF16)
    n_state = bbr_ref.shape[1]
    xr3 = jnp.dot(ub, bbr_ref[...], preferred_element_type=F32).reshape(n_tiles, V7X_SUBLANES, n_state)
    xi3 = jnp.dot(ub, bbi_ref[...], preferred_element_type=F32).reshape(n_tiles, V7X_SUBLANES, n_state)
    for i, s in enumerate((1, 2, 4)):
        cr = apow_ref[2 * i][None]
        ci = apow_ref[2 * i + 1][None]
        rr = pltpu.roll(xr3, s, 1)
        ri = pltpu.roll(xi3, s, 1)
        xr3, xi3 = xr3 + cr * rr - ci * ri, xi3 + cr * ri + ci * rr
    xr_ref[...] = xr3.reshape(tc, n_state)
    xi_ref[...] = xi3.reshape(tc, n_state)

    def s5_tile(i, carry):
        cr_, ci_ = carry
        r0 = pl.multiple_of(i * V7X_SUBLANES, V7X_SUBLANES)
        pr = apow_ref[6]
        pi = apow_ref[7]
        nr_ = xr_ref[pl.ds(r0, V7X_SUBLANES), :] + pr * cr_ - pi * ci_
        ni_ = xi_ref[pl.ds(r0, V7X_SUBLANES), :] + pr * ci_ + pi * cr_
        xr_ref[pl.ds(r0, V7X_SUBLANES), :] = nr_
        xi_ref[pl.ds(r0, V7X_SUBLANES), :] = ni_
        return nr_[V7X_SUBLANES - 1:, :], ni_[V7X_SUBLANES - 1:, :]

    cr_f, ci_f = lax.fori_loop(0, n_tiles, s5_tile, (src_ref[...], sic_ref[...]))
    src_ref[...] = cr_f
    sic_ref[...] = ci_f
    srl_ref[...] = xr_ref[last_row:last_row + 1, :]
    sil_ref[...] = xi_ref[last_row:last_row + 1, :]

    ys = (jnp.dot(xr_ref[...].astype(BF16), ccr_ref[...], preferred_element_type=F32)
          - jnp.dot(xi_ref[...].astype(BF16), cci_ref[...], preferred_element_type=F32))
    ys = ys + d_ref[...] * u
    gs = jax.nn.gelu(ys)
    yb = gs * jax.nn.sigmoid(jnp.dot(gs.astype(BF16), wglu_ref[...], preferred_element_type=F32)
                             + bglu_ref[...])

    y_ref[...] = (x_ref[...]
                  + jnp.dot(ya.astype(BF16), wout_ref[:d_lru, :], preferred_element_type=F32)
                  + jnp.dot(yb.astype(BF16), wout_ref[d_lru:, :], preferred_element_type=F32))


def _mixer0(z, x, conv_init, h0, sr0, si0, wts, *, tc, last_row):
    bn, t_len, d_in = z.shape
    d = x.shape[2]
    d_lru = wts["cw"].shape[1]
    n_state = wts["bbr"].shape[1]
    assert t_len % tc == 0
    kern = functools.partial(_mixer0_kernel, tc=tc, last_row=last_row)
    wnames = ("cw", "cb", "wa", "ba", "wx", "bx", "lam", "bbr", "bbi", "apow",
              "ccr", "cci", "d", "wglu", "bglu", "wout")
    wlist = [wts[k] for k in wnames]
    per_b = lambda shape: pl.BlockSpec((None,) + shape, lambda b, c: (b,) + (0,) * len(shape))
    in_specs = [
        pl.BlockSpec((None, tc, d_in), lambda b, c: (b, c, 0)),
        pl.BlockSpec((None, tc, d), lambda b, c: (b, c, 0)),
        per_b((V7X_SUBLANES, d_lru)), per_b((1, d_lru)), per_b((1, n_state)), per_b((1, n_state)),
    ] + [_const_spec(w.shape) for w in wlist]
    return pl.pallas_call(
        kern,
        out_shape=(
            jax.ShapeDtypeStruct((bn, t_len, d), F32),
            jax.ShapeDtypeStruct((bn, 1, d_lru), F32),
            jax.ShapeDtypeStruct((bn, 1, n_state), F32),
            jax.ShapeDtypeStruct((bn, 1, n_state), F32),
        ),
        grid=(bn, t_len // tc),
        in_specs=in_specs,
        out_specs=(
            pl.BlockSpec((None, tc, d), lambda b, c: (b, c, 0)),
            per_b((1, d_lru)), per_b((1, n_state)), per_b((1, n_state)),
        ),
        scratch_shapes=[
            pltpu.VMEM((V7X_SUBLANES, d_lru), F32),
            pltpu.VMEM((1, d_lru), F32),
            pltpu.VMEM((1, n_state), F32),
            pltpu.VMEM((1, n_state), F32),
            pltpu.VMEM((tc, n_state), F32),
            pltpu.VMEM((tc, n_state), F32),
            pltpu.VMEM((tc, d_lru), F32),
            pltpu.VMEM((tc, d_lru), F32),
        ],
        compiler_params=_cparams(("parallel", "arbitrary")),
        name="mixer0",
    )(z, x, conv_init, h0, sr0, si0, *wlist)


def _xattn_kernel(x_ref, g_ref, wq_ref, qg_ref, mk_ref, mv_ref, wo_ref, o_ref):
    x = x_ref[...]
    q = jnp.dot(_rms(x, g_ref[...]).astype(BF16), wq_ref[...], preferred_element_type=F32)
    dh = qg_ref.shape[1]
    outs = []
    for h in range(XA_HEADS):
        sl = slice(h * dh, (h + 1) * dh)
        qn = _rms(q[:, sl], qg_ref[...]).astype(BF16)
        s = lax.dot_general(qn, mk_ref[:, sl].astype(BF16), (((1,), (1,)), ((), ())),
                            preferred_element_type=F32) * (dh ** -0.5)
        m = jnp.max(s, axis=-1, keepdims=True)
        p = jnp.exp(s - m)
        den = jnp.sum(p, axis=-1, keepdims=True)
        oh = jnp.dot(p.astype(BF16), mv_ref[:, sl].astype(BF16), preferred_element_type=F32) / den
        outs.append(oh.astype(BF16))
    o_ref[...] = x + jnp.dot(jnp.concatenate(outs, axis=-1), wo_ref[...], preferred_element_type=F32)


def _xattn(x, g, wq, qg, mk, mv, wo, *, tm, name):
    bn, t_len, d = x.shape
    n_mem = mk.shape[1]
    assert t_len % tm == 0
    return pl.pallas_call(
        _xattn_kernel,
        out_shape=jax.ShapeDtypeStruct(x.shape, F32),
        grid=(bn, t_len // tm),
        in_specs=[
            pl.BlockSpec((None, tm, d), lambda b, i: (b, i, 0)),
            _const_spec((1, d)),
            _const_spec(wq.shape),
            _const_spec((1, qg.shape[-1])),
            pl.BlockSpec((None, n_mem, d), lambda b, i: (b, 0, 0)),
            pl.BlockSpec((None, n_mem, d), lambda b, i: (b, 0, 0)),
            _const_spec(wo.shape),
        ],
        out_specs=pl.BlockSpec((None, tm, d), lambda b, i: (b, i, 0)),
        compiler_params=_cparams(("parallel", "arbitrary")),
        name=name,
    )(x, g.reshape(1, d), wq, qg.reshape(1, -1), mk, mv, wo)


def _ffn_kernel(x_ref, g_ref, wg_ref, wu_ref, wd_ref, o_ref, hn_ref, acc_ref):
    j = pl.program_id(1)

    @pl.when(j == 0)
    def _():
        hn_ref[...] = _rms(x_ref[...], g_ref[...]).astype(BF16)
        acc_ref[...] = jnp.zeros_like(acc_ref)

    hn = hn_ref[...]
    gate = jnp.dot(hn, wg_ref[...], preferred_element_type=F32)
    up = jnp.dot(hn, wu_ref[...], preferred_element_type=F32)
    act = (jax.nn.silu(gate) * up).astype(BF16)
    acc_ref[...] += jnp.dot(act, wd_ref[...], preferred_element_type=F32)

    @pl.when(j == pl.num_programs(1) - 1)
    def _():
        o_ref[...] = x_ref[...] + acc_ref[...]


def _ffn(x, g, w_gu, w_d, *, name):
    m, d = x.shape
    hid = w_d.shape[0]
    th = FFN_TH
    tm = min(m, 1024)
    nj = hid // th
    assert m % tm == 0 and hid % th == 0
    return pl.pallas_call(
        _ffn_kernel,
        out_shape=jax.ShapeDtypeStruct((m, d), F32),
        grid=(m // tm, nj),
        in_specs=[
            pl.BlockSpec((tm, d), lambda i, j: (i, 0)),
            pl.BlockSpec((1, d), lambda i, j: (0, 0)),
            pl.BlockSpec((d, th), lambda i, j: (0, j)),
            pl.BlockSpec((d, th), lambda i, j: (0, j + nj)),
            pl.BlockSpec((th, d), lambda i, j: (j, 0)),
        ],
        out_specs=pl.BlockSpec((tm, d), lambda i, j: (i, 0)),
        scratch_shapes=[pltpu.VMEM((tm, d), BF16), pltpu.VMEM((tm, d), F32)],
        compiler_params=_cparams(("parallel", "arbitrary")),
        name=name,
    )(x, g.reshape(1, d), w_gu, w_gu, w_d)


def _rel_bucket_np(dist):
    dist = np.clip(np.asarray(dist), 0, None)
    max_exact = REL_BUCKETS // 2
    safe = np.maximum(dist, max_exact).astype(np.float32)
    large = max_exact + np.floor(np.log(safe / max_exact) / math.log(REL_MAX_DIST / max_exact)
                                 * (REL_BUCKETS - max_exact)).astype(np.int32)
    large = np.minimum(large, REL_BUCKETS - 1)
    return np.where(dist < max_exact, dist, large).astype(np.int32)


def _dil_kernel(q_ref, kp_ref, kc_ref, vp_ref, vc_ref, bias_ref, o_ref, l_ref):
    q = q_ref[...].astype(BF16)
    k = jnp.concatenate([kp_ref[...], kc_ref[...]], axis=0).astype(BF16)
    v = jnp.concatenate([vp_ref[...], vc_ref[...]], axis=0).astype(BF16)
    outs, lses = [], []
    for h in range(HG):
        sl = slice(h * DH, (h + 1) * DH)
        s = lax.dot_general(q[:, sl], k[:, sl], (((1,), (1,)), ((), ())),
                            preferred_element_type=F32) * (DH ** -0.5) + bias_ref[h]
        m = jnp.max(s, axis=-1, keepdims=True)
        p = jnp.exp(s - m)
        den = jnp.sum(p, axis=-1, keepdims=True)
        outs.append(jnp.dot(p.astype(BF16), v[:, sl], preferred_element_type=F32) / den)
        lses.append(jnp.broadcast_to(m + jnp.log(den), (q.shape[0], DH)))
    o_ref[...] = jnp.concatenate(outs, axis=-1)
    l_ref[...] = jnp.concatenate(lses, axis=-1)


def _dilated_prompt(qkv, tab, g, d):
    bn, s_len, n_cols = qkv.shape
    gw = HG * DH
    l_len = s_len // d
    assert s_len % d == 0 and l_len % QB == 0
    nb = l_len // QB
    ncb = n_cols // gw
    k_col = ncb // 3
    view = qkv.reshape(bn, l_len, d * n_cols)

    qi = np.arange(QB)[:, None]
    ki = np.arange(2 * QB)[None, :]
    dist = qi + QB - ki
    band = (dist >= 0) & (dist <= N_STRIDE)
    first = band & (ki >= QB)
    bias = jnp.transpose(tab[_rel_bucket_np(dist * d)], (2, 0, 1)).astype(F32)
    bias2 = jnp.stack([jnp.where(first[None], bias, NEG), jnp.where(band[None], bias, NEG)])

    def col(section):
        return lambda b, r, n: (b, n, r * ncb + section * k_col + g)

    def col_prev(section):
        return lambda b, r, n: (b, jnp.maximum(n - 1, 0), r * ncb + section * k_col + g)

    blk = (None, QB, gw)
    out_sds = jax.ShapeDtypeStruct((bn, l_len, d * gw), F32)
    o, lse = pl.pallas_call(
        _dil_kernel,
        out_shape=(out_sds, out_sds),
        grid=(bn, d, nb),
        in_specs=[
            pl.BlockSpec(blk, col(0)),
            pl.BlockSpec(blk, col_prev(1)),
            pl.BlockSpec(blk, col(1)),
            pl.BlockSpec(blk, col_prev(2)),
            pl.BlockSpec(blk, col(2)),
            pl.BlockSpec((None, HG, QB, 2 * QB), lambda b, r, n: (jnp.minimum(n, 1), 0, 0, 0)),
        ],
        out_specs=(pl.BlockSpec(blk, lambda b, r, n: (b, n, r)),
                   pl.BlockSpec(blk, lambda b, r, n: (b, n, r))),
        compiler_params=_cparams(("parallel", "parallel", "arbitrary")),
        name=f"dilated_prompt_g{g}",
    )(view, view, view, view, view, bias2)
    return o.reshape(bn, s_len, gw), lse.reshape(bn, s_len, gw)


def _split_dot(x, w_ref):
    hi = x.astype(BF16)
    lo = (x - hi.astype(F32)).astype(BF16)
    w = w_ref[...]
    return jnp.dot(hi, w, preferred_element_type=F32) + jnp.dot(lo, w, preferred_element_type=F32)


def _dil_sample_kernel(q_ref, k_ref, v_ref, cache_ref, bc_ref, bn_ref, hsum_ref, hexp_ref,
                       o_ref, l_ref, *, t_valid, k_offs, v_offs):
    t_pad, gw = q_ref.shape
    kn = k_ref[...]
    vn = v_ref[...]
    o_rows, den_rows, lse_rows = [], [], []
    for t in range(t_valid):
        qt = q_ref[t:t + 1, :]
        kc = cache_ref[:, k_offs[t]:k_offs[t] + gw]
        vc = cache_ref[:, v_offs[t]:v_offs[t] + gw]
        sc = _split_dot(kc * qt, hsum_ref) * (DH ** -0.5) + bc_ref[t]
        sn = _split_dot(kn * qt, hsum_ref) * (DH ** -0.5) + bn_ref[t]
        m = jnp.maximum(jnp.max(sc, axis=0, keepdims=True), jnp.max(sn, axis=0, keepdims=True))
        pc = jnp.exp(sc - m)
        pn = jnp.exp(sn - m)
        den = jnp.sum(pc, axis=0, keepdims=True) + jnp.sum(pn, axis=0, keepdims=True)
        o_rows.append(jnp.sum(_split_dot(pc, hexp_ref) * vc, axis=0, keepdims=True)
                      + jnp.sum(_split_dot(pn, hexp_ref) * vn, axis=0, keepdims=True))
        den_rows.append(den)
        lse_rows.append(m + jnp.log(den))
    n_pad = t_pad - t_valid
    lanes = hsum_ref.shape[1]
    o_un = jnp.concatenate(o_rows + [jnp.zeros((n_pad, gw), F32)], axis=0)
    den = jnp.concatenate(den_rows + [jnp.ones((n_pad, lanes), F32)], axis=0)
    lse = jnp.concatenate(lse_rows + [jnp.zeros((n_pad, lanes), F32)], axis=0)
    o_ref[...] = o_un / _split_dot(den, hexp_ref)
    l_ref[...] = _split_dot(lse, hexp_ref)


def _dilated_sample(qkv, cache, tab, g, d, t_valid):
    db, t_pad, n_cols = qkv.shape
    wb = cache.shape[1]
    gw = HG * DH
    assert wb == N_STRIDE * d and (d == 1 or t_valid <= d)
    ncb = n_cols // gw
    k_col = ncb // 3
    row_w = 2 * gw
    view = cache.reshape(db, N_STRIDE, d * row_w)
    if d == 1:
        cw = row_w
        k_offs = (0,) * t_valid
    else:
        cw = t_valid * row_w
        k_offs = tuple(t * row_w for t in range(t_valid))
    v_offs = tuple(o + gw for o in k_offs)

    bias_vec = tab[_rel_bucket_np(d * np.arange(N_STRIDE + 1))].astype(F32)
    rows = np.arange(N_STRIDE)[None, :]
    tq = np.arange(t_valid)[:, None]
    tn = np.arange(t_pad)[None, :]
    if d == 1:
        jc = N_STRIDE + tq - rows
        vc = rows >= tq
        jn = tq - tn
        vn = tn <= tq
    else:
        jc = np.broadcast_to(N_STRIDE - rows, (t_valid, N_STRIDE))
        vc = np.ones((t_valid, N_STRIDE), bool)
        jn = np.zeros((t_valid, t_pad), np.int64)
        vn = tn == tq
    lane_pad = ((0, 0), (0, 0), (0, V7X_LANES - HG))
    bc = jnp.pad(jnp.where(vc[..., None], bias_vec[np.clip(jc, 0, N_STRIDE)], NEG), lane_pad)
    bnw = jnp.pad(jnp.where(vn[..., None], bias_vec[np.clip(jn, 0, N_STRIDE)], NEG), lane_pad)
    head = np.arange(gw) // DH
    hsum = jnp.asarray((head[:, None] == np.arange(V7X_LANES)[None, :]).astype(np.float32), dtype=BF16)
    hexp = jnp.asarray((np.arange(V7X_LANES)[:, None] == head[None, :]).astype(np.float32), dtype=BF16)

    kern = functools.partial(_dil_sample_kernel, t_valid=t_valid, k_offs=k_offs, v_offs=v_offs)
    blk = (None, t_pad, gw)
    out_sds = jax.ShapeDtypeStruct((db, t_pad, gw), F32)
    return pl.pallas_call(
        kern,
        out_shape=(out_sds, out_sds),
        grid=(db,),
        in_specs=[
            pl.BlockSpec(blk, lambda b: (b, 0, g)),
            pl.BlockSpec(blk, lambda b: (b, 0, k_col + g)),
            pl.BlockSpec(blk, lambda b: (b, 0, 2 * k_col + g)),
            pl.BlockSpec((None, N_STRIDE, cw), lambda b: (b, 0, 0)),
            _const_spec(bc.shape), _const_spec(bnw.shape), _const_spec(hsum.shape), _const_spec(hexp.shape),
        ],
        out_specs=(pl.BlockSpec(blk, lambda b: (b, 0, 0)), pl.BlockSpec(blk, lambda b: (b, 0, 0))),
        compiler_params=_cparams(("parallel",)),
        name=f"dilated_sample_g{g}",
    )(qkv, qkv, qkv, view, bc, bnw, hsum, hexp)


def _merge_wo_kernel(o0_ref, o1_ref, o2_ref, l0_ref, l1_ref, l2_ref, x_ref, w_ref, out_ref):
    l0, l1, l2 = l0_ref[...], l1_ref[...], l2_ref[...]
    m = jnp.maximum(jnp.maximum(l0, l1), l2)
    e0, e1, e2 = jnp.exp(l0 - m), jnp.exp(l1 - m), jnp.exp(l2 - m)
    o = (e0 * o0_ref[...] + e1 * o1_ref[...] + e2 * o2_ref[...]) / (e0 + e1 + e2)
    out_ref[...] = x_ref[...] + jnp.dot(o.astype(BF16), w_ref[...], preferred_element_type=F32)


def _merge_wo(outs, lses, x, w, *, name):
    m, d = x.shape
    gw = w.shape[0]
    tm = min(m, 512)
    assert m % tm == 0
    row = lambda width: pl.BlockSpec((tm, width), lambda i: (i, 0))
    return pl.pallas_call(
        _merge_wo_kernel,
        out_shape=jax.ShapeDtypeStruct((m, d), F32),
        grid=(m // tm,),
        in_specs=[row(gw)] * 6 + [row(d), _const_spec(w.shape)],
        out_specs=row(d),
        compiler_params=_cparams(("parallel",)),
        name=name,
    )(*[o.reshape(m, gw) for o in outs], *[l.reshape(m, gw) for l in lses], x, w)


def _router_kernel(x_ref, g_ref, wr_ref, br_ref, hn_ref, eid_ref, gate_ref):
    hn = _rms(x_ref[...], g_ref[...])
    hn_ref[...] = hn
    logits = jnp.dot(hn, wr_ref[...], preferred_element_type=F32,
                     precision=lax.Precision.HIGHEST) + br_ref[...]
    ne = logits.shape[1]
    lane = lax.broadcasted_iota(I32, logits.shape, 1)
    m1 = jnp.max(logits, axis=-1, keepdims=True)
    i1 = jnp.min(jnp.where(logits == m1, lane, ne), axis=-1, keepdims=True)
    rest = jnp.where(lane == i1, -jnp.inf, logits)
    m2 = jnp.max(rest, axis=-1, keepdims=True)
    i2 = jnp.min(jnp.where(rest == m2, lane, ne), axis=-1, keepdims=True)
    e2 = jnp.exp(m2 - m1)
    g1 = 1.0 / (1.0 + e2)
    g2 = e2 / (1.0 + e2)
    eid_ref[...] = jnp.where(lane == 0, i1, jnp.where(lane == 1, i2, 0))
    gate_ref[...] = jnp.where(lane == 0, g1, jnp.where(lane == 1, g2, 0.0))


def _router(x, g, wr, br, *, name):
    m, d = x.shape
    ne = wr.shape[1]
    tm = min(m, 512)
    assert m % tm == 0
    return pl.pallas_call(
        _router_kernel,
        out_shape=(jax.ShapeDtypeStruct((m, d), F32),
                   jax.ShapeDtypeStruct((m, ne), I32),
                   jax.ShapeDtypeStruct((m, ne), F32)),
        grid=(m // tm,),
        in_specs=[pl.BlockSpec((tm, d), lambda i: (i, 0)), _const_spec((1, d)),
                  _const_spec(wr.shape), _const_spec((1, ne))],
        out_specs=(pl.BlockSpec((tm, d), lambda i: (i, 0)),
                   pl.BlockSpec((tm, ne), lambda i: (i, 0)),
                   pl.BlockSpec((tm, ne), lambda i: (i, 0))),
        compiler_params=_cparams(("parallel",)),
        name=name,
    )(x, g.reshape(1, d), wr, br.reshape(1, ne))


def _moe_kernel(te_ref, nrow_ref, nused_ref,
                src_ref, dst_ref, gate_ref, hp_ref, hs_ref, wg_ref, wu_ref, wd_ref,
                y2_ref,
                xg_ref, xb_ref, acc_ref, sem,
                *, tm, n_prompt):
    t = pl.program_id(0)
    j = pl.program_id(1)
    nj = pl.num_programs(1)

    def gather_from_prompt(r, tok):
        return pltpu.make_async_copy(hp_ref.at[pl.ds(jnp.minimum(tok, n_prompt - 1), 1)],
                                     xg_ref.at[pl.ds(r, 1)], sem.at[0])

    def gather_from_sample(r, tok):
        return pltpu.make_async_copy(hs_ref.at[pl.ds(jnp.maximum(tok - n_prompt, 0), 1)],
                                     xg_ref.at[pl.ds(r, 1)], sem.at[0])

    def scatter_copy(r):
        return pltpu.make_async_copy(xg_ref.at[pl.ds(r, 1)], y2_ref.at[pl.ds(dst_ref[0, 0, r], 1)], sem.at[1])

    @pl.when(t < nused_ref[0])
    def _():
        @pl.when(j == 0)
        def _():
            def start(r, _):
                tok = src_ref[0, 0, r]

                @pl.when(tok < n_prompt)
                def _():
                    gather_from_prompt(r, tok).start()

                @pl.when(tok >= n_prompt)
                def _():
                    gather_from_sample(r, tok).start()
                return 0

            lax.fori_loop(0, tm, start, 0)

            def wait(r, _):
                gather_from_prompt(r, src_ref[0, 0, r]).wait()
                return 0

            lax.fori_loop(0, tm, wait, 0)
            xb_ref[...] = xg_ref[...].astype(BF16)
            acc_ref[...] = jnp.zeros_like(acc_ref)

        xb = xb_ref[...]
        gate = jnp.dot(xb, wg_ref[...], preferred_element_type=F32)
        up = jnp.dot(xb, wu_ref[...], preferred_element_type=F32)
        act = (jax.nn.silu(gate) * up).astype(BF16)
        acc_ref[...] += jnp.dot(act, wd_ref[...], preferred_element_type=F32)

        @pl.when(j == nj - 1)
        def _():
            xg_ref[...] = acc_ref[...] * gate_ref[...]
            n_rows = nrow_ref[t]

            def start(r, _):
                scatter_copy(r).start()
                return 0

            lax.fori_loop(0, n_rows, start, 0)

            def wait(r, _):
                scatter_copy(r).wait()
                return 0

            lax.fori_loop(0, n_rows, wait, 0)


def _moe_plan(eid, gate, tm, nt):
    n_tok = eid.shape[0]
    e_flat = eid.reshape(-1)
    onehot = (e_flat[:, None] == jnp.arange(N_EXPERTS, dtype=I32)[None, :]).astype(I32)
    csum = jnp.cumsum(onehot, axis=0)
    rank = jnp.take_along_axis(csum, e_flat[:, None], axis=1)[:, 0] - 1
    counts = csum[-1]
    padded = ((counts + tm - 1) // tm) * tm
    ends = jnp.cumsum(padded)
    starts = ends - padded
    pos = starts[e_flat] + rank
    pair = jnp.arange(n_tok * TOP_K, dtype=I32)
    tok = pair // TOP_K
    slot = pair % TOP_K
    src = jnp.zeros((nt * tm,), I32).at[pos].set(tok)
    dst = jnp.zeros((nt * tm,), I32).at[pos].set(slot * n_tok + tok)
    gsort = jnp.zeros((nt * tm,), F32).at[pos].set(gate.reshape(-1))
    tile_start = jnp.arange(nt, dtype=I32) * tm
    n_used = (ends[-1] // tm).astype(I32)
    te = jnp.minimum(jnp.searchsorted(ends, tile_start, side="right").astype(I32), N_EXPERTS - 1)
    n_rows = jnp.clip(starts[te] + counts[te] - tile_start, 0, tm).astype(I32)
    used = jnp.arange(nt, dtype=I32) < n_used
    te = jnp.where(used, te, te[jnp.maximum(n_used - 1, 0)])
    n_rows = jnp.where(used, n_rows, 0)
    return te, n_rows, n_used.reshape(1), src.reshape(nt, 1, tm), dst.reshape(nt, 1, tm), gsort.reshape(nt * tm, 1)


def _moe(hn_p, hn_s, eid, gate, w_gu, w_d):
    n_prompt, d = hn_p.shape
    n_tok = n_prompt + hn_s.shape[0]
    hid = w_d.shape[1]
    tm, th = MOE_TM, MOE_TH
    nj = hid // th
    assert hid % th == 0
    nt = -(-(n_tok * TOP_K) // tm) + N_EXPERTS
    te, n_rows, n_used, src, dst, gsort = _moe_plan(eid, gate, tm, nt)
    kern = functools.partial(_moe_kernel, tm=tm, n_prompt=n_prompt)

    def jj(t, j, nused):
        return jnp.where(t < nused[0], j, nj - 1)

    grid_spec = pltpu.PrefetchScalarGridSpec(
        num_scalar_prefetch=3,
        grid=(nt, nj),
        in_specs=[
            pl.BlockSpec((1, 1, tm), lambda t, j, te, nr, nu: (t, 0, 0), memory_space=pltpu.SMEM),
            pl.BlockSpec((1, 1, tm), lambda t, j, te, nr, nu: (t, 0, 0), memory_space=pltpu.SMEM),
            pl.BlockSpec((tm, 1), lambda t, j, te, nr, nu: (t, 0)),
            pl.BlockSpec(memory_space=pl.ANY),
            pl.BlockSpec(memory_space=pl.ANY),
            pl.BlockSpec((None, d, th), lambda t, j, te, nr, nu: (te[t], 0, jj(t, j, nu))),
            pl.BlockSpec((None, d, th), lambda t, j, te, nr, nu: (te[t], 0, jj(t, j, nu) + nj)),
            pl.BlockSpec((None, th, d), lambda t, j, te, nr, nu: (te[t], jj(t, j, nu), 0)),
        ],
        out_specs=pl.BlockSpec(memory_space=pl.ANY),
        scratch_shapes=[
            pltpu.VMEM((tm, d), F32),
            pltpu.VMEM((tm, d), BF16),
            pltpu.VMEM((tm, d), F32),
            pltpu.SemaphoreType.DMA((2,)),
        ],
    )
    return pl.pallas_call(
        kern,
        out_shape=jax.ShapeDtypeStruct((TOP_K * n_tok, d), F32),
        grid_spec=grid_spec,
        compiler_params=_cparams(("arbitrary", "arbitrary")),
        name="moe_experts",
    )(te, n_rows, n_used, src, dst, gsort, hn_p, hn_s, w_gu, w_gu, w_d)


def _add3_kernel(x_ref, a_ref, b_ref, o_ref):
    o_ref[...] = x_ref[...] + a_ref[...] + b_ref[...]


def _moe_combine(x, y2, row0, *, name):
    m, d = x.shape
    tm = min(m, 1024)
    assert m % tm == 0 and row0 % tm == 0
    off = row0 // tm
    return pl.pallas_call(
        _add3_kernel,
        out_shape=jax.ShapeDtypeStruct((m, d), F32),
        grid=(m // tm,),
        in_specs=[pl.BlockSpec((tm, d), lambda i: (i, 0)),
                  pl.BlockSpec((None, tm, d), lambda i: (0, i + off, 0)),
                  pl.BlockSpec((None, tm, d), lambda i: (1, i + off, 0))],
        out_specs=pl.BlockSpec((tm, d), lambda i: (i, 0)),
        compiler_params=_cparams(("parallel",)),
        name=name,
    )(x, y2, y2)


def _block_diag(w):
    n, a, b = w.shape
    return jnp.einsum("nij,nm->nimj", w, jnp.eye(n, dtype=w.dtype)).reshape(n * a, n * b)


def kernel(x_prompt, x_sample, state_conv, state_lru, state_s5_re, state_s5_im, cache_swa0_kv, cache_swa1_kv, cache_swa2_kv, cache_mem_k, cache_mem_v, mem_prompt, norm_mix, norm_xa, norm_ffn, norm_mem, w_in_even, conv_w, conv_b, lru_wa, lru_ba, lru_wx, lru_bx, lru_lam, s5_lam_re, s5_lam_im, s5_log_dt, s5_b_re, s5_b_im, s5_c_re, s5_c_im, s5_d, s5_w_glu, s5_b_glu, w_out_even, w_qkv_odd, q_norm_odd, k_norm_odd, w_o_odd, rel_bias, xa_wq, xa_wkv, xa_qn, xa_kn, xa_wo, ffn_w_gu, ffn_w_down, moe_router_w, moe_router_b, moe_w_gu, moe_w_down):
    bp, s_len, d = x_prompt.shape
    db, t_dec, _ = x_sample.shape
    n_mem = mem_prompt.shape[1]
    tp = SAMPLE_PAD_T
    d_lru = conv_w.shape[-1]
    s5_g, s5_p, s5_h = s5_b_re.shape[1:]
    n_state = s5_g * s5_p
    d_s5 = s5_g * s5_h
    xa_dh = d // XA_HEADS
    caches = (cache_swa0_kv, cache_swa1_kv, cache_swa2_kv)
    bf = lambda w: w.astype(BF16)

    yp = x_prompt
    ys = jnp.pad(x_sample, ((0, 0), (0, tp - t_dec), (0, 0)))

    w_in = bf(w_in_even[0])
    bbr, bbi, apow = _s5_prep(s5_lam_re[0], s5_lam_im[0], s5_log_dt[0], s5_b_re[0], s5_b_im[0])
    eye_g = jnp.eye(s5_g, dtype=F32)
    mix_w = dict(
        cw=conv_w[0], cb=conv_b[0].reshape(1, d_lru),
        wa=bf(_block_diag(lru_wa[0])), ba=lru_ba[0].reshape(1, d_lru),
        wx=bf(_block_diag(lru_wx[0])), bx=lru_bx[0].reshape(1, d_lru),
        lam=lru_lam[0].reshape(1, d_lru),
        bbr=bbr, bbi=bbi, apow=apow,
        ccr=bf(jnp.einsum("ghp,gk->gpkh", s5_c_re[0], eye_g).reshape(n_state, d_s5)),
        cci=bf(jnp.einsum("ghp,gk->gpkh", s5_c_im[0], eye_g).reshape(n_state, d_s5)),
        d=s5_d[0].reshape(1, d_s5), wglu=bf(s5_w_glu[0]), bglu=s5_b_glu[0].reshape(1, d_s5),
        wout=bf(w_out_even[0]),
    )
    z_p = _norm_matmul(yp.reshape(bp * s_len, d), norm_mix[0], w_in, name="in_proj_p").reshape(bp, s_len, -1)
    yp, p_lru, p_s5r, p_s5i = _mixer0(
        z_p, yp, jnp.zeros((bp, V7X_SUBLANES, d_lru), F32), jnp.zeros((bp, 1, d_lru), F32),
        jnp.zeros((bp, 1, n_state), F32), jnp.zeros((bp, 1, n_state), F32), mix_w,
        tc=MIX_TC, last_row=MIX_TC - 1)
    z_s = _norm_matmul(ys.reshape(db * tp, d), norm_mix[0], w_in, name="in_proj_s").reshape(db, tp, -1)
    conv_init = jnp.pad(state_conv[0], ((0, 0), (V7X_SUBLANES - (CONV_W - 1), 0), (0, 0)))
    ys, s_lru, s_s5r, s_s5i = _mixer0(
        z_s, ys, conv_init, state_lru[0].reshape(db, 1, d_lru),
        state_s5_re[0].reshape(db, 1, n_state), state_s5_im[0].reshape(db, 1, n_state), mix_w,
        tc=tp, last_row=t_dec - 1)
    p_state_conv = z_p[:, s_len - (CONV_W - 1):, :d_lru][None]
    s_state_conv = z_s[:, t_dec - (CONV_W - 1):t_dec, :d_lru][None]

    p_mk, p_mv = [], []

    def cross_attention(layer, yp, ys):
        kn_gain = jnp.concatenate([jnp.tile(xa_kn[layer], XA_HEADS), jnp.ones((d,), F32)]).reshape(1, 2 * d)
        kv = _norm_matmul(mem_prompt.reshape(bp * n_mem, d), norm_mem[layer], bf(xa_wkv[layer]),
                          head_gain=kn_gain, n_norm_cols=d, dh=xa_dh, name=f"mem_kv{layer}")
        kv = kv.reshape(bp, n_mem, 2 * d)
        mk, mv = kv[:, :, :d], kv[:, :, d:]
        p_mk.append(mk.reshape(bp, n_mem, XA_HEADS, xa_dh))
        p_mv.append(mv.reshape(bp, n_mem, XA_HEADS, xa_dh))
        wq, wo = bf(xa_wq[layer]), bf(xa_wo[layer])
        yp = _xattn(yp, norm_xa[layer], wq, xa_qn[layer], mk, mv, wo, tm=256, name=f"xattn_p{layer}")
        ys = _xattn(ys, norm_xa[layer], wq, xa_qn[layer],
                    cache_mem_k[layer].reshape(db, n_mem, d), cache_mem_v[layer].reshape(db, n_mem, d), wo,
                    tm=tp, name=f"xattn_s{layer}")
        return yp, ys

    yp, ys = cross_attention(0, yp, ys)
    w_gu0, w_d0 = bf(ffn_w_gu[0]), bf(ffn_w_down[0])
    yp = _ffn(yp.reshape(bp * s_len, d), norm_ffn[0], w_gu0, w_d0, name="ffn_p").reshape(bp, s_len, d)
    ys = _ffn(ys.reshape(db * tp, d), norm_ffn[0], w_gu0, w_d0, name="ffn_s").reshape(db, tp, d)

    n_heads = len(WINDOWS) * HG
    d_c = n_heads * DH
    gw = HG * DH
    w_qkv = bf(w_qkv_odd[0])
    qk_gain = jnp.concatenate([jnp.tile(q_norm_odd[0], n_heads), jnp.tile(k_norm_odd[0], n_heads),
                               jnp.ones((d_c,), F32)]).reshape(1, 3 * d_c)
    qkv_p = _norm_matmul(yp.reshape(bp * s_len, d), norm_mix[1], w_qkv, head_gain=qk_gain,
                         n_norm_cols=2 * d_c, dh=DH, name="qkv_p").reshape(bp, s_len, 3 * d_c)
    qkv_s = _norm_matmul(ys.reshape(db * tp, d), norm_mix[1], w_qkv, head_gain=qk_gain,
                         n_norm_cols=2 * d_c, dh=DH, name="qkv_s").reshape(db, tp, 3 * d_c)
    w_o = bf(w_o_odd[0])
    outs_p, lses_p, outs_s, lses_s, p_swa, s_swa = [], [], [], [], [], []
    for g, dil in enumerate(DILATIONS):
        tab = rel_bias[:, g * HG:(g + 1) * HG]
        o, l = _dilated_prompt(qkv_p, tab, g, dil)
        outs_p.append(o)
        lses_p.append(l)
        o, l = _dilated_sample(qkv_s, caches[g][0], tab, g, dil, t_dec)
        outs_s.append(o)
        lses_s.append(l)
        ksl = slice(d_c + g * gw, d_c + (g + 1) * gw)
        vsl = slice(2 * d_c + g * gw, 2 * d_c + (g + 1) * gw)
        win = min(WINDOWS[g], s_len)
        kv_p = jnp.stack([qkv_p[:, s_len - win:, ksl], qkv_p[:, s_len - win:, vsl]], axis=2)
        p_swa.append(kv_p.reshape(bp, win, 2, HG, DH)[None])
        kv_s = jnp.stack([qkv_s[:, :t_dec, ksl], qkv_s[:, :t_dec, vsl]], axis=2).reshape(db, t_dec, 2, HG, DH)
        s_swa.append(jnp.concatenate([caches[g][0][:, t_dec:], kv_s], axis=1)[None])
    yp = _merge_wo(outs_p, lses_p, yp.reshape(bp * s_len, d), w_o, name="merge_wo_p").reshape(bp, s_len, d)
    ys = _merge_wo(outs_s, lses_s, ys.reshape(db * tp, d), w_o, name="merge_wo_s").reshape(db, tp, d)

    yp, ys = cross_attention(1, yp, ys)

    yp2 = yp.reshape(bp * s_len, d)
    ys2 = ys[:, :t_dec].reshape(db * t_dec, d)
    hn_p, eid_p, gate_p = _router(yp2, norm_ffn[1], moe_router_w[0], moe_router_b[0], name="router_p")
    hn_s, eid_s, gate_s = _router(ys2, norm_ffn[1], moe_router_w[0], moe_router_b[0], name="router_s")
    eid = jnp.concatenate([eid_p[:, :TOP_K], eid_s[:, :TOP_K]], axis=0)
    gate = jnp.concatenate([gate_p[:, :TOP_K], gate_s[:, :TOP_K]], axis=0)
    n_tok = eid.shape[0]
    y2 = _moe(hn_p, hn_s, eid, gate, bf(moe_w_gu[0]), bf(moe_w_down[0])).reshape(TOP_K, n_tok, d)
    yp = _moe_combine(yp2, y2, 0, name="moe_combine_p").reshape(bp, s_len, d)
    ys = _moe_combine(ys2, y2, bp * s_len, name="moe_combine_s").reshape(db, t_dec, d)

    return (yp, ys,
            p_state_conv, p_lru.reshape(1, bp, d_lru),
            p_s5r.reshape(1, bp, s5_g, s5_p), p_s5i.reshape(1, bp, s5_g, s5_p),
            p_swa[0], p_swa[1], p_swa[2], jnp.stack(p_mk), jnp.stack(p_mv),
            s_state_conv, s_lru.reshape(1, db, d_lru),
            s_s5r.reshape(1, db, s5_g, s5_p), s_s5i.reshape(1, db, s5_g, s5_p),
            s_swa[0], s_swa[1], s_swa[2])
```

```python
import functools
import math

import jax
import jax.numpy as jnp
import numpy as np
from jax import lax
from jax.experimental import pallas as pl
from jax.experimental.pallas import tpu as pltpu

F32 = jnp.float32
BF16 = jnp.bfloat16
I32 = jnp.int32

EPS = 1e-6
NEG = -1e30

V7X_SUBLANES = 8
V7X_LANES = 128
V7X_VMEM_BYTES = 64 * 1024 * 1024
VMEM_LIMIT = V7X_VMEM_BYTES - 8 * 1024 * 1024

LRU_C = 8.0
CONV_W = 4
WINDOWS = (128, 512, 2048)
DILATIONS = (1, 4, 16)
HG = 8
DH = 64
N_STRIDE = 128
QB = 128
REL_BUCKETS = 32
REL_MAX_DIST = WINDOWS[-1]
XA_HEADS = 4
N_EXPERTS = 8
TOP_K = 2
SAMPLE_PAD_T = 16

MIX_TC = 256
MOE_TM = 256
MOE_TH = 512
FFN_TH = 512


def _cparams(sem):
    return pltpu.CompilerParams(dimension_semantics=sem, vmem_limit_bytes=VMEM_LIMIT)


def _const_spec(shape):
    nd = len(shape)
    return pl.BlockSpec(shape, lambda *_: (0,) * nd)


def _rms(x, g):
    return x * lax.rsqrt(jnp.mean(x * x, axis=-1, keepdims=True) + EPS) * g


def _norm_matmul_kernel(x_ref, g_ref, w_ref, hg_ref, hs_ref, o_ref, hn_ref, *, n_norm_tiles, dh, slabs):
    j = pl.program_id(1)

    @pl.when(j == 0)
    def _():
        hn_ref[...] = _rms(x_ref[...], g_ref[...]).astype(BF16)

    def emit(val):
        if slabs:
            for c in range(o_ref.shape[0]):
                o_ref[c] = val[:, c * V7X_LANES:(c + 1) * V7X_LANES]
        else:
            o_ref[...] = val

    y = jnp.dot(hn_ref[...], w_ref[...], preferred_element_type=F32)
    if n_norm_tiles == 0:
        emit(y)
    else:
        @pl.when(j < n_norm_tiles)
        def _():
            ssq = jnp.dot((y * y).astype(BF16), hs_ref[...], preferred_element_type=F32)
            emit(y * lax.rsqrt(ssq * (1.0 / dh) + EPS) * hg_ref[...])

        @pl.when(j >= n_norm_tiles)
        def _():
            emit(y)


def _head_sum_matrix(tn, dh):
    idx = np.arange(tn) // dh
    return jnp.asarray((idx[:, None] == idx[None, :]).astype(np.float32), dtype=BF16)


def _norm_matmul(x, g, w, *, tn=512, head_gain=None, n_norm_cols=0, dh=1, slabs=False, name):
    m, d = x.shape
    n = w.shape[1]
    tm = min(m, 1024)
    assert m % tm == 0 and n % tn == 0 and n_norm_cols % tn == 0
    if head_gain is None:
        head_gain = jnp.ones((1, n), F32)
    hs = _head_sum_matrix(tn, dh)
    kern = functools.partial(_norm_matmul_kernel, n_norm_tiles=n_norm_cols // tn, dh=dh, slabs=slabs)
    if slabs:
        out_shape = jax.ShapeDtypeStruct((n // V7X_LANES, m, V7X_LANES), F32)
        out_spec = pl.BlockSpec((tn // V7X_LANES, tm, V7X_LANES), lambda i, j: (j, i, 0))
    else:
        out_shape = jax.ShapeDtypeStruct((m, n), F32)
        out_spec = pl.BlockSpec((tm, tn), lambda i, j: (i, j))
    return pl.pallas_call(
        kern,
        out_shape=out_shape,
        grid=(m // tm, n // tn),
        in_specs=[
            pl.BlockSpec((tm, d), lambda i, j: (i, 0)),
            pl.BlockSpec((1, d), lambda i, j: (0, 0)),
            pl.BlockSpec((d, tn), lambda i, j: (0, j)),
            pl.BlockSpec((1, tn), lambda i, j: (0, j)),
            pl.BlockSpec((tn, tn), lambda i, j: (0, 0)),
        ],
        out_specs=out_spec,
        scratch_shapes=[pltpu.VMEM((tm, d), BF16)],
        compiler_params=_cparams(("parallel", "arbitrary")),
        name=name,
    )(x, g.reshape(1, d), w, head_gain, hs)


def _s5_prep_kernel(lre_ref, lim_ref, ldt_ref, bre_ref, bim_ref, bbr_ref, bbi_ref, apow_ref):
    lr = lre_ref[...]
    li = lim_ref[...]
    dt = jnp.exp(ldt_ref[...])
    mag = jnp.exp(lr * dt)
    ab_r = mag * jnp.cos(li * dt)
    ab_i = mag * jnp.sin(li * dt)
    den = lr * lr + li * li
    nr = ab_r - 1.0
    cr = (nr * lr + ab_i * li) / den
    ci = (ab_i * lr - nr * li) / den
    b_r = bre_ref[...]
    b_i = bim_ref[...]
    bbr_ref[...] = (cr * b_r - ci * b_i).astype(BF16)
    bbi_ref[...] = (cr * b_i + ci * b_r).astype(BF16)

    n = lr.shape[1]
    row = lax.broadcasted_iota(I32, (V7X_SUBLANES, n), 0)

    def power(kf):
        mg = jnp.exp(kf * (lr * dt))
        return mg * jnp.cos(kf * (li * dt)), mg * jnp.sin(kf * (li * dt))

    for i, s in enumerate((1, 2, 4)):
        pr, pi = power(jnp.full((V7X_SUBLANES, n), s, F32))
        keep = row >= s
        apow_ref[2 * i] = jnp.where(keep, pr, 0.0)
        apow_ref[2 * i + 1] = jnp.where(keep, pi, 0.0)
    pr, pi = power((row + 1).astype(F32))
    apow_ref[6] = pr
    apow_ref[7] = pi


def _s5_prep(lam_re, lam_im, log_dt, b_re, b_im):
    g, p, h = b_re.shape
    n = g * p
    eye = jnp.eye(g, dtype=F32)
    bre_bd = jnp.einsum("gph,gk->khgp", b_re, eye).reshape(g * h, n)
    bim_bd = jnp.einsum("gph,gk->khgp", b_im, eye).reshape(g * h, n)
    ldt = jnp.broadcast_to(log_dt[:, None], (g, p)).reshape(1, n)
    return pl.pallas_call(
        _s5_prep_kernel,
        out_shape=(
            jax.ShapeDtypeStruct((g * h, n), BF16),
            jax.ShapeDtypeStruct((g * h, n), BF16),
            jax.ShapeDtypeStruct((8, V7X_SUBLANES, n), F32),
        ),
        compiler_params=pltpu.CompilerParams(vmem_limit_bytes=VMEM_LIMIT),
        name="s5_prep",
    )(lam_re.reshape(1, n), lam_im.reshape(1, n), ldt, bre_bd, bim_bd)


def _mixer0_kernel(z_ref, x_ref, convi_ref, h0_ref, sr0_ref, si0_ref,
                   cw_ref, cb_ref, wa_ref, ba_ref, wx_ref, bx_ref, lam_ref,
                   bbr_ref, bbi_ref, apow_ref, ccr_ref, cci_ref, d_ref, wglu_ref, bglu_ref, wout_ref,
                   y_ref, hl_ref, srl_ref, sil_ref,
                   ext_ref, hc_ref, src_ref, sic_ref, xr_ref, xi_ref, ha_ref, hb_ref,
                   *, tc, last_row):
    c = pl.program_id(1)
    d_lru = cw_ref.shape[1]
    d_s5 = d_ref.shape[1]
    n_tiles = tc // V7X_SUBLANES

    @pl.when(c == 0)
    def _():
        ext_ref[...] = convi_ref[...]
        hc_ref[...] = h0_ref[...]
        src_ref[...] = sr0_ref[...]
        sic_ref[...] = si0_ref[...]

    z = z_ref[...]
    xa = z[:, :d_lru]
    ga = z[:, d_lru:2 * d_lru]
    u = z[:, 2 * d_lru:]

    ext = jnp.concatenate([ext_ref[...], xa], axis=0)
    xc = cb_ref[...] + xa * cw_ref[CONV_W - 1:CONV_W, :]
    for s in range(1, CONV_W):
        xc = xc + pltpu.roll(ext, s, 0)[V7X_SUBLANES:, :] * cw_ref[CONV_W - 1 - s:CONV_W - s, :]
    ext_ref[...] = xa[tc - V7X_SUBLANES:, :]

    xcb = xc.astype(BF16)
    r = jax.nn.sigmoid(jnp.dot(xcb, wa_ref[...], preferred_element_type=F32) + ba_ref[...])
    ig = jax.nn.sigmoid(jnp.dot(xcb, wx_ref[...], preferred_element_type=F32) + bx_ref[...])
    lam = lam_ref[...]
    softplus_neg = jnp.maximum(-lam, 0.0) + jnp.log1p(jnp.exp(-jnp.abs(lam)))
    log_a = -LRU_C * r * softplus_neg
    a = jnp.exp(log_a)
    bt = jnp.sqrt(-jnp.tanh(log_a) * (a * a + 1.0)) * ig * xc

    a3 = a.reshape(n_tiles, V7X_SUBLANES, d_lru)
    b3 = bt.reshape(n_tiles, V7X_SUBLANES, d_lru)
    row = lax.broadcasted_iota(I32, (1, V7X_SUBLANES, d_lru), 1)
    for s in (1, 2, 4):
        keep = row >= s
        ar = pltpu.roll(a3, s, 1)
        br = pltpu.roll(b3, s, 1)
        b3 = jnp.where(keep, a3 * br + b3, b3)
        a3 = jnp.where(keep, a3 * ar, a3)
    ha_ref[...] = a3.reshape(tc, d_lru)
    hb_ref[...] = b3.reshape(tc, d_lru)

    def lru_tile(i, carry):
        r0 = pl.multiple_of(i * V7X_SUBLANES, V7X_SUBLANES)
        h = ha_ref[pl.ds(r0, V7X_SUBLANES), :] * carry + hb_ref[pl.ds(r0, V7X_SUBLANES), :]
        hb_ref[pl.ds(r0, V7X_SUBLANES), :] = h
        return h[V7X_SUBLANES - 1:, :]

    hc_ref[...] = lax.fori_loop(0, n_tiles, lru_tile, hc_ref[...])
    hs = hb_ref[...]
    hl_ref[...] = hb_ref[last_row:last_row + 1, :]
    ya = hs * jax.nn.gelu(ga)

    ub = u.astype(BF16)
    n_state = bbr_ref.shape[1]
    xr3 = jnp.dot(ub, bbr_ref[...], preferred_element_type=F32).reshape(n_tiles, V7X_SUBLANES, n_state)
    xi3 = jnp.dot(ub, bbi_ref[...], preferred_element_type=F32).reshape(n_tiles, V7X_SUBLANES, n_state)
    for i, s in enumerate((1, 2, 4)):
        cr = apow_ref[2 * i][None]
        ci = apow_ref[2 * i + 1][None]
        rr = pltpu.roll(xr3, s, 1)
        ri = pltpu.roll(xi3, s, 1)
        xr3, xi3 = xr3 + cr * rr - ci * ri, xi3 + cr * ri + ci * rr
    xr_ref[...] = xr3.reshape(tc, n_state)
    xi_ref[...] = xi3.reshape(tc, n_state)

    def s5_tile(i, carry):
        cr_, ci_ = carry
        r0 = pl.multiple_of(i * V7X_SUBLANES, V7X_SUBLANES)
        pr = apow_ref[6]
        pi = apow_ref[7]
        nr_ = xr_ref[pl.ds(r0, V7X_SUBLANES), :] + pr * cr_ - pi * ci_
        ni_ = xi_ref[pl.ds(r0, V7X_SUBLANES), :] + pr * ci_ + pi * cr_
        xr_ref[pl.ds(r0, V7X_SUBLANES), :] = nr_
        xi_ref[pl.ds(r0, V7X_SUBLANES), :] = ni_
        return nr_[V7X_SUBLANES - 1:, :], ni_[V7X_SUBLANES - 1:, :]

    cr_f, ci_f = lax.fori_loop(0, n_tiles, s5_tile, (src_ref[...], sic_ref[...]))
    src_ref[...] = cr_f
    sic_ref[...] = ci_f
    srl_ref[...] = xr_ref[last_row:last_row + 1, :]
    sil_ref[...] = xi_ref[last_row:last_row + 1, :]

    ys = (jnp.dot(xr_ref[...].astype(BF16), ccr_ref[...], preferred_element_type=F32)
          - jnp.dot(xi_ref[...].astype(BF16), cci_ref[...], preferred_element_type=F32))
    ys = ys + d_ref[...] * u
    gs = jax.nn.gelu(ys)
    yb = gs * jax.nn.sigmoid(jnp.dot(gs.astype(BF16), wglu_ref[...], preferred_element_type=F32)
                             + bglu_ref[...])

    y_ref[...] = (x_ref[...]
                  + jnp.dot(ya.astype(BF16), wout_ref[:d_lru, :], preferred_element_type=F32)
                  + jnp.dot(yb.astype(BF16), wout_ref[d_lru:, :], preferred_element_type=F32))


def _mixer0(z, x, conv_init, h0, sr0, si0, wts, *, tc, last_row):
    bn, t_len, d_in = z.shape
    d = x.shape[2]
    d_lru = wts["cw"].shape[1]
    n_state = wts["bbr"].shape[1]
    assert t_len % tc == 0
    kern = functools.partial(_mixer0_kernel, tc=tc, last_row=last_row)
    wnames = ("cw", "cb", "wa", "ba", "wx", "bx", "lam", "bbr", "bbi", "apow",
              "ccr", "cci", "d", "wglu", "bglu", "wout")
    wlist = [wts[k] for k in wnames]
    per_b = lambda shape: pl.BlockSpec((None,) + shape, lambda b, c: (b,) + (0,) * len(shape))
    in_specs = [
        pl.BlockSpec((None, tc, d_in), lambda b, c: (b, c, 0)),
        pl.BlockSpec((None, tc, d), lambda b, c: (b, c, 0)),
        per_b((V7X_SUBLANES, d_lru)), per_b((1, d_lru)), per_b((1, n_state)), per_b((1, n_state)),
    ] + [_const_spec(w.shape) for w in wlist]
    return pl.pallas_call(
        kern,
        out_shape=(
            jax.ShapeDtypeStruct((bn, t_len, d), F32),
            jax.ShapeDtypeStruct((bn, 1, d_lru), F32),
            jax.ShapeDtypeStruct((bn, 1, n_state), F32),
            jax.ShapeDtypeStruct((bn, 1, n_state), F32),
        ),
        grid=(bn, t_len // tc),
        in_specs=in_specs,
        out_specs=(
            pl.BlockSpec((None, tc, d), lambda b, c: (b, c, 0)),
            per_b((1, d_lru)), per_b((1, n_state)), per_b((1, n_state)),
        ),
        scratch_shapes=[
            pltpu.VMEM((V7X_SUBLANES, d_lru), F32),
            pltpu.VMEM((1, d_lru), F32),
            pltpu.VMEM((1, n_state), F32),
            pltpu.VMEM((1, n_state), F32),
            pltpu.VMEM((tc, n_state), F32),
            pltpu.VMEM((tc, n_state), F32),
            pltpu.VMEM((tc, d_lru), F32),
            pltpu.VMEM((tc, d_lru), F32),
        ],
        compiler_params=_cparams(("parallel", "arbitrary")),
        name="mixer0",
    )(z, x, conv_init, h0, sr0, si0, *wlist)


def _xattn_kernel(x_ref, g_ref, wq_ref, qg_ref, mk_ref, mv_ref, wo_ref, o_ref):
    x = x_ref[...]
    q = jnp.dot(_rms(x, g_ref[...]).astype(BF16), wq_ref[...], preferred_element_type=F32)
    dh = qg_ref.shape[1]
    outs = []
    for h in range(XA_HEADS):
        sl = slice(h * dh, (h + 1) * dh)
        qn = _rms(q[:, sl], qg_ref[...]).astype(BF16)
        s = lax.dot_general(qn, mk_ref[:, sl].astype(BF16), (((1,), (1,)), ((), ())),
                            preferred_element_type=F32) * (dh ** -0.5)
        m = jnp.max(s, axis=-1, keepdims=True)
        p = jnp.exp(s - m)
        den = jnp.sum(p, axis=-1, keepdims=True)
        oh = jnp.dot(p.astype(BF16), mv_ref[:, sl].astype(BF16), preferred_element_type=F32) / den
        outs.append(oh.astype(BF16))
    o_ref[...] = x + jnp.dot(jnp.concatenate(outs, axis=-1), wo_ref[...], preferred_element_type=F32)


def _xattn(x, g, wq, qg, mk, mv, wo, *, tm, name):
    bn, t_len, d = x.shape
    n_mem = mk.shape[1]
    assert t_len % tm == 0
    return pl.pallas_call(
        _xattn_kernel,
        out_shape=jax.ShapeDtypeStruct(x.shape, F32),
        grid=(bn, t_len // tm),
        in_specs=[
            pl.BlockSpec((None, tm, d), lambda b, i: (b, i, 0)),
            _const_spec((1, d)),
            _const_spec(wq.shape),
            _const_spec((1, qg.shape[-1])),
            pl.BlockSpec((None, n_mem, d), lambda b, i: (b, 0, 0)),
            pl.BlockSpec((None, n_mem, d), lambda b, i: (b, 0, 0)),
            _const_spec(wo.shape),
        ],
        out_specs=pl.BlockSpec((None, tm, d), lambda b, i: (b, i, 0)),
        compiler_params=_cparams(("parallel", "arbitrary")),
        name=name,
    )(x, g.reshape(1, d), wq, qg.reshape(1, -1), mk, mv, wo)


def _ffn_kernel(x_ref, g_ref, wg_ref, wu_ref, wd_ref, o_ref, hn_ref, acc_ref):
    j = pl.program_id(1)

    @pl.when(j == 0)
    def _():
        hn_ref[...] = _rms(x_ref[...], g_ref[...]).astype(BF16)
        acc_ref[...] = jnp.zeros_like(acc_ref)

    hn = hn_ref[...]
    gate = jnp.dot(hn, wg_ref[...], preferred_element_type=F32)
    up = jnp.dot(hn, wu_ref[...], preferred_element_type=F32)
    act = (jax.nn.silu(gate) * up).astype(BF16)
    acc_ref[...] += jnp.dot(act, wd_ref[...], preferred_element_type=F32)

    @pl.when(j == pl.num_programs(1) - 1)
    def _():
        o_ref[...] = x_ref[...] + acc_ref[...]


def _ffn(x, g, w_gu, w_d, *, name):
    m, d = x.shape
    hid = w_d.shape[0]
    th = FFN_TH
    tm = min(m, 1024)
    nj = hid // th
    assert m % tm == 0 and hid % th == 0
    return pl.pallas_call(
        _ffn_kernel,
        out_shape=jax.ShapeDtypeStruct((m, d), F32),
        grid=(m // tm, nj),
        in_specs=[
            pl.BlockSpec((tm, d), lambda i, j: (i, 0)),
            pl.BlockSpec((1, d), lambda i, j: (0, 0)),
            pl.BlockSpec((d, th), lambda i, j: (0, j)),
            pl.BlockSpec((d, th), lambda i, j: (0, j + nj)),
            pl.BlockSpec((th, d), lambda i, j: (j, 0)),
        ],
        out_specs=pl.BlockSpec((tm, d), lambda i, j: (i, 0)),
        scratch_shapes=[pltpu.VMEM((tm, d), BF16), pltpu.VMEM((tm, d), F32)],
        compiler_params=_cparams(("parallel", "arbitrary")),
        name=name,
    )(x, g.reshape(1, d), w_gu, w_gu, w_d)


def _rel_bucket_np(dist):
    dist = np.clip(np.asarray(dist), 0, None)
    max_exact = REL_BUCKETS // 2
    safe = np.maximum(dist, max_exact).astype(np.float32)
    large = max_exact + np.floor(np.log(safe / max_exact) / math.log(REL_MAX_DIST / max_exact)
                                 * (REL_BUCKETS - max_exact)).astype(np.int32)
    large = np.minimum(large, REL_BUCKETS - 1)
    return np.where(dist < max_exact, dist, large).astype(np.int32)


def _dil_kernel(q_ref, k_ref, v_ref, bias_ref, o_ref, l_ref, kprev_ref, vprev_ref, *, d):
    @pl.when(pl.program_id(1) == 0)
    def _():
        kprev_ref[...] = jnp.zeros_like(kprev_ref)
        vprev_ref[...] = jnp.zeros_like(vprev_ref)

    lane = lax.broadcasted_iota(I32, (QB, V7X_LANES), 1)
    n_slab = q_ref.shape[0]
    heads_per_slab = V7X_LANES // DH

    def residue(r, _):
        rows = pl.ds(r, QB, stride=d)
        lse = jnp.zeros((QB, V7X_LANES), F32)
        for c in range(n_slab):
            q = q_ref[c, rows, :].astype(BF16)
            kc = k_ref[c, rows, :].astype(BF16)
            vc = v_ref[c, rows, :].astype(BF16)
            k = jnp.concatenate([kprev_ref[r, c], kc], axis=0)
            v = jnp.concatenate([vprev_ref[r, c], vc], axis=0)
            outs = []
            for hh in range(heads_per_slab):
                h = c * heads_per_slab + hh
                sl = slice(hh * DH, (hh + 1) * DH)
                s = lax.dot_general(q[:, sl], k[:, sl], (((1,), (1,)), ((), ())),
                                    preferred_element_type=F32) * (DH ** -0.5) + bias_ref[h]
                m = jnp.max(s, axis=-1, keepdims=True)
                p = jnp.exp(s - m)
                den = jnp.sum(p, axis=-1, keepdims=True)
                outs.append(jnp.dot(p.astype(BF16), v[:, sl], preferred_element_type=F32) / den)
                lse = jnp.where(lane == h, m + jnp.log(den), lse)
            o_ref[c, rows, :] = jnp.concatenate(outs, axis=-1)
            kprev_ref[r, c] = kc
            vprev_ref[r, c] = vc
        l_ref[rows, :] = lse
        return 0

    lax.fori_loop(0, d, residue, 0)


def _dilated_prompt(qkv, tab, g, d, bn):
    n_slabs, m_rows, _ = qkv.shape
    s_len = m_rows // bn
    gw = HG * DH
    gs = gw // V7X_LANES
    chunk = QB * d
    n_chunks = s_len // chunk
    assert s_len % chunk == 0
    k_col = n_slabs // gs // 3

    qi = np.arange(QB)[:, None]
    ki = np.arange(2 * QB)[None, :]
    dist = qi + QB - ki
    band = (dist >= 0) & (dist <= N_STRIDE)
    first = band & (ki >= QB)
    bias = jnp.transpose(tab[_rel_bucket_np(dist * d)], (2, 0, 1)).astype(F32)
    bias2 = jnp.stack([jnp.where(first[None], bias, NEG), jnp.where(band[None], bias, NEG)])

    col = lambda section: (lambda b, n: (section * k_col + g, b * n_chunks + n, 0))
    blk = (gs, chunk, V7X_LANES)
    return pl.pallas_call(
        functools.partial(_dil_kernel, d=d),
        out_shape=(jax.ShapeDtypeStruct((gs, m_rows, V7X_LANES), F32),
                   jax.ShapeDtypeStruct((m_rows, V7X_LANES), F32)),
        grid=(bn, n_chunks),
        in_specs=[
            pl.BlockSpec(blk, col(0)), pl.BlockSpec(blk, col(1)), pl.BlockSpec(blk, col(2)),
            pl.BlockSpec((None, HG, QB, 2 * QB), lambda b, n: (jnp.minimum(n, 1), 0, 0, 0)),
        ],
        out_specs=(pl.BlockSpec(blk, lambda b, n: (0, b * n_chunks + n, 0)),
                   pl.BlockSpec((chunk, V7X_LANES), lambda b, n: (b * n_chunks + n, 0))),
        scratch_shapes=[pltpu.VMEM((d, gs, QB, V7X_LANES), BF16), pltpu.VMEM((d, gs, QB, V7X_LANES), BF16)],
        compiler_params=_cparams(("parallel", "arbitrary")),
        name=f"dilated_prompt_g{g}",
    )(qkv, qkv, qkv, bias2)


def _split_dot(x, w_ref):
    hi = x.astype(BF16)
    lo = (x - hi.astype(F32)).astype(BF16)
    w = w_ref[...]
    return jnp.dot(hi, w, preferred_element_type=F32) + jnp.dot(lo, w, preferred_element_type=F32)


def _dil_sample_kernel(q_ref, k_ref, v_ref, cache_ref, bc_ref, bn_ref, hsum_ref, hexp_ref,
                       o_ref, l_ref, *, t_valid, k_offs, v_offs):
    n_slab, t_pad, _ = q_ref.shape
    gw = n_slab * V7X_LANES
    unslab = lambda ref: jnp.concatenate([ref[c] for c in range(n_slab)], axis=-1)
    qn = unslab(q_ref)
    kn = unslab(k_ref)
    vn = unslab(v_ref)
    o_rows, den_rows, lse_rows = [], [], []
    for t in range(t_valid):
        qt = qn[t:t + 1, :]
        kc = cache_ref[:, k_offs[t]:k_offs[t] + gw]
        vc = cache_ref[:, v_offs[t]:v_offs[t] + gw]
        sc = _split_dot(kc * qt, hsum_ref) * (DH ** -0.5) + bc_ref[t]
        sn = _split_dot(kn * qt, hsum_ref) * (DH ** -0.5) + bn_ref[t]
        m = jnp.maximum(jnp.max(sc, axis=0, keepdims=True), jnp.max(sn, axis=0, keepdims=True))
        pc = jnp.exp(sc - m)
        pn = jnp.exp(sn - m)
        den = jnp.sum(pc, axis=0, keepdims=True) + jnp.sum(pn, axis=0, keepdims=True)
        o_rows.append(jnp.sum(_split_dot(pc, hexp_ref) * vc, axis=0, keepdims=True)
                      + jnp.sum(_split_dot(pn, hexp_ref) * vn, axis=0, keepdims=True))
        den_rows.append(den)
        lse_rows.append(m + jnp.log(den))
    n_pad = t_pad - t_valid
    lanes = hsum_ref.shape[1]
    o_un = jnp.concatenate(o_rows + [jnp.zeros((n_pad, gw), F32)], axis=0)
    den = jnp.concatenate(den_rows + [jnp.ones((n_pad, lanes), F32)], axis=0)
    lse = jnp.concatenate(lse_rows + [jnp.zeros((n_pad, lanes), F32)], axis=0)
    o = o_un / _split_dot(den, hexp_ref)
    for c in range(n_slab):
        o_ref[c] = o[:, c * V7X_LANES:(c + 1) * V7X_LANES]
    l_ref[...] = lse


def _dilated_sample(qkv, cache, tab, g, d, t_valid):
    n_slabs, m_rows, _ = qkv.shape
    db, wb = cache.shape[:2]
    t_pad = m_rows // db
    gw = HG * DH
    gs = gw // V7X_LANES
    assert wb == N_STRIDE * d and (d == 1 or t_valid <= d)
    k_col = n_slabs // gs // 3
    row_w = 2 * gw
    view = cache.reshape(db, N_STRIDE, d * row_w)
    if d == 1:
        cw = row_w
        k_offs = (0,) * t_valid
    else:
        cw = t_valid * row_w
        k_offs = tuple(t * row_w for t in range(t_valid))
    v_offs = tuple(o + gw for o in k_offs)

    bias_vec = tab[_rel_bucket_np(d * np.arange(N_STRIDE + 1))].astype(F32)
    rows = np.arange(N_STRIDE)[None, :]
    tq = np.arange(t_valid)[:, None]
    tn = np.arange(t_pad)[None, :]
    if d == 1:
        jc = N_STRIDE + tq - rows
        vc = rows >= tq
        jn = tq - tn
        vn = tn <= tq
    else:
        jc = np.broadcast_to(N_STRIDE - rows, (t_valid, N_STRIDE))
        vc = np.ones((t_valid, N_STRIDE), bool)
        jn = np.zeros((t_valid, t_pad), np.int64)
        vn = tn == tq
    lane_pad = ((0, 0), (0, 0), (0, V7X_LANES - HG))
    bc = jnp.pad(jnp.where(vc[..., None], bias_vec[np.clip(jc, 0, N_STRIDE)], NEG), lane_pad)
    bnw = jnp.pad(jnp.where(vn[..., None], bias_vec[np.clip(jn, 0, N_STRIDE)], NEG), lane_pad)
    head = np.arange(gw) // DH
    hsum = jnp.asarray((head[:, None] == np.arange(V7X_LANES)[None, :]).astype(np.float32), dtype=BF16)
    hexp = jnp.asarray((np.arange(V7X_LANES)[:, None] == head[None, :]).astype(np.float32), dtype=BF16)

    kern = functools.partial(_dil_sample_kernel, t_valid=t_valid, k_offs=k_offs, v_offs=v_offs)
    blk = (gs, t_pad, V7X_LANES)
    return pl.pallas_call(
        kern,
        out_shape=(jax.ShapeDtypeStruct((gs, m_rows, V7X_LANES), F32),
                   jax.ShapeDtypeStruct((m_rows, V7X_LANES), F32)),
        grid=(db,),
        in_specs=[
            pl.BlockSpec(blk, lambda b: (g, b, 0)),
            pl.BlockSpec(blk, lambda b: (k_col + g, b, 0)),
            pl.BlockSpec(blk, lambda b: (2 * k_col + g, b, 0)),
            pl.BlockSpec((None, N_STRIDE, cw), lambda b: (b, 0, 0)),
            _const_spec(bc.shape), _const_spec(bnw.shape), _const_spec(hsum.shape), _const_spec(hexp.shape),
        ],
        out_specs=(pl.BlockSpec(blk, lambda b: (0, b, 0)),
                   pl.BlockSpec((t_pad, V7X_LANES), lambda b: (b, 0))),
        compiler_params=_cparams(("parallel",)),
        name=f"dilated_sample_g{g}",
    )(qkv, qkv, qkv, view, bc, bnw, hsum, hexp)


def _merge_wo_kernel(o0_ref, o1_ref, o2_ref, l0_ref, l1_ref, l2_ref, x_ref, hexp_ref, w_ref, out_ref):
    l0, l1, l2 = l0_ref[...], l1_ref[...], l2_ref[...]
    m = jnp.maximum(jnp.maximum(l0, l1), l2)
    e0, e1, e2 = jnp.exp(l0 - m), jnp.exp(l1 - m), jnp.exp(l2 - m)
    inv = 1.0 / (e0 + e1 + e2)
    unslab = lambda ref: jnp.concatenate([ref[c] for c in range(ref.shape[0])], axis=-1)
    o = (_split_dot(e0 * inv, hexp_ref) * unslab(o0_ref) + _split_dot(e1 * inv, hexp_ref) * unslab(o1_ref)
         + _split_dot(e2 * inv, hexp_ref) * unslab(o2_ref))
    out_ref[...] = x_ref[...] + jnp.dot(o.astype(BF16), w_ref[...], preferred_element_type=F32)


def _merge_wo(outs, lses, x, w, *, name):
    m, d = x.shape
    gw = w.shape[0]
    tm = min(m, 512)
    assert m % tm == 0
    head = np.arange(gw) // DH
    hexp = jnp.asarray((np.arange(V7X_LANES)[:, None] == head[None, :]).astype(np.float32), dtype=BF16)
    row = lambda width: pl.BlockSpec((tm, width), lambda i: (i, 0))
    slab = pl.BlockSpec((gw // V7X_LANES, tm, V7X_LANES), lambda i: (0, i, 0))
    return pl.pallas_call(
        _merge_wo_kernel,
        out_shape=jax.ShapeDtypeStruct((m, d), F32),
        grid=(m // tm,),
        in_specs=[slab] * 3 + [row(V7X_LANES)] * 3 + [row(d), _const_spec(hexp.shape), _const_spec(w.shape)],
        out_specs=row(d),
        compiler_params=_cparams(("parallel",)),
        name=name,
    )(*outs, *lses, x, hexp, w)


def _router_kernel(x_ref, g_ref, wr_ref, br_ref, cnt0_ref, tri_ref,
                   hn_ref, eid_ref, gate_ref, rank_ref, cnt_ref, run_ref):
    @pl.when(pl.program_id(0) == 0)
    def _():
        run_ref[...] = cnt0_ref[...]

    hn = _rms(x_ref[...], g_ref[...])
    hn_ref[...] = hn
    logits = jnp.dot(hn, wr_ref[...], preferred_element_type=F32,
                     precision=lax.Precision.HIGHEST) + br_ref[...]
    ne = logits.shape[1]
    lane = lax.broadcasted_iota(I32, logits.shape, 1)
    m1 = jnp.max(logits, axis=-1, keepdims=True)
    i1 = jnp.min(jnp.where(logits == m1, lane, ne), axis=-1, keepdims=True)
    rest = jnp.where(lane == i1, -jnp.inf, logits)
    m2 = jnp.max(rest, axis=-1, keepdims=True)
    i2 = jnp.min(jnp.where(rest == m2, lane, ne), axis=-1, keepdims=True)
    e2 = jnp.exp(m2 - m1)
    g1 = 1.0 / (1.0 + e2)
    g2 = e2 / (1.0 + e2)
    eid_ref[...] = jnp.where(lane == 0, i1, jnp.where(lane == 1, i2, 0))
    gate_ref[...] = jnp.where(lane == 0, g1, jnp.where(lane == 1, g2, 0.0))
    chosen = jnp.logical_or(lane == i1, lane == i2)
    before = jnp.dot(tri_ref[...], jnp.where(chosen, 1.0, 0.0).astype(BF16),
                     preferred_element_type=F32) + run_ref[...]
    r1 = jnp.sum(jnp.where(lane == i1, before, 0.0), axis=-1, keepdims=True)
    r2 = jnp.sum(jnp.where(lane == i2, before, 0.0), axis=-1, keepdims=True)
    rank_ref[...] = jnp.where(lane == 0, r1, jnp.where(lane == 1, r2, 0.0)).astype(I32)
    run_ref[...] += jnp.sum(jnp.where(chosen, 1.0, 0.0), axis=0, keepdims=True)
    cnt_ref[...] = run_ref[...]


def _router(x, g, wr, br, cnt0, *, name):
    m, d = x.shape
    ne = wr.shape[1]
    tm = min(m, 512)
    assert m % tm == 0
    tri = jnp.asarray(np.tril(np.ones((tm, tm), np.float32), -1), dtype=BF16)
    row = lambda width: pl.BlockSpec((tm, width), lambda i: (i, 0))
    return pl.pallas_call(
        _router_kernel,
        out_shape=(jax.ShapeDtypeStruct((m, d), F32),
                   jax.ShapeDtypeStruct((m, ne), I32),
                   jax.ShapeDtypeStruct((m, ne), F32),
                   jax.ShapeDtypeStruct((m, ne), I32),
                   jax.ShapeDtypeStruct((1, ne), F32)),
        grid=(m // tm,),
        in_specs=[row(d), _const_spec((1, d)), _const_spec(wr.shape), _const_spec((1, ne)),
                  _const_spec((1, ne)), _const_spec((tm, tm))],
        out_specs=(row(d), row(ne), row(ne), row(ne), _const_spec((1, ne))),
        scratch_shapes=[pltpu.VMEM((1, ne), F32)],
        compiler_params=_cparams(("arbitrary",)),
        name=name,
    )(x, g.reshape(1, d), wr, br.reshape(1, ne), cnt0, tri)


def _dispatch_kernel(pos_ref, hn_ref, xs_in_ref, xs_ref, sem, *, tmd):
    del xs_in_ref

    def start(r, _):
        for k in range(TOP_K):
            pltpu.make_async_copy(hn_ref.at[pl.ds(r, 1)],
                                  xs_ref.at[pl.ds(pos_ref[0, 0, TOP_K * r + k], 1)], sem.at[0]).start()
        return 0

    lax.fori_loop(0, tmd, start, 0, unroll=8)
    pltpu.make_async_copy(xs_ref.at[pl.ds(0, TOP_K * tmd)], xs_ref.at[pl.ds(0, TOP_K * tmd)], sem.at[0]).wait()


def _dispatch(hn, pos, xs, *, name):
    m, d = hn.shape
    tmd = min(m, 256)
    assert m % tmd == 0
    kern = functools.partial(_dispatch_kernel, tmd=tmd)
    return pl.pallas_call(
        kern,
        out_shape=jax.ShapeDtypeStruct(xs.shape, xs.dtype),
        grid=(m // tmd,),
        in_specs=[
            pl.BlockSpec((1, 1, TOP_K * tmd), lambda i: (i, 0, 0), memory_space=pltpu.SMEM),
            pl.BlockSpec((tmd, d), lambda i: (i, 0)),
            pl.BlockSpec(memory_space=pl.ANY),
        ],
        out_specs=pl.BlockSpec(memory_space=pl.ANY),
        scratch_shapes=[pltpu.SemaphoreType.DMA((1,))],
        input_output_aliases={2: 0},
        compiler_params=_cparams(("arbitrary",)),
        name=name,
    )(pos.reshape(m // tmd, 1, TOP_K * tmd), hn, xs)


def _moe_ffn_kernel(te_ref, nused_ref, x_ref, wg_ref, wu_ref, wd_ref, o_ref, xb_ref, acc_ref):
    t = pl.program_id(0)
    j = pl.program_id(1)
    last = pl.num_programs(1) - 1
    used = t < nused_ref[0]

    @pl.when(used)
    def _():
        @pl.when(j == 0)
        def _():
            xb_ref[...] = x_ref[...].astype(BF16)
            acc_ref[...] = jnp.zeros_like(acc_ref)

        xb = xb_ref[...]
        gate = jnp.dot(xb, wg_ref[...], preferred_element_type=F32)
        up = jnp.dot(xb, wu_ref[...], preferred_element_type=F32)
        act = (jax.nn.silu(gate) * up).astype(BF16)
        acc_ref[...] += jnp.dot(act, wd_ref[...], preferred_element_type=F32)

        @pl.when(j == last)
        def _():
            o_ref[...] = acc_ref[...]

    @pl.when(jnp.logical_and(jnp.logical_not(used), j == last))
    def _():
        o_ref[...] = jnp.zeros_like(o_ref)


def _moe_ffn(xs, te, n_used, w_gu, w_d):
    n_rows, d = xs.shape
    hid = w_d.shape[1]
    tm, th = MOE_TM, MOE_TH
    nj = hid // th
    nt = n_rows // tm
    assert hid % th == 0 and n_rows % tm == 0

    def jj(t, j, nu):
        return jnp.where(t < nu[0], j, nj - 1)

    grid_spec = pltpu.PrefetchScalarGridSpec(
        num_scalar_prefetch=2,
        grid=(nt, nj),
        in_specs=[
            pl.BlockSpec((tm, d), lambda t, j, te, nu: (jnp.minimum(t, nu[0] - 1), 0)),
            pl.BlockSpec((None, d, th), lambda t, j, te, nu: (te[t], 0, jj(t, j, nu))),
            pl.BlockSpec((None, d, th), lambda t, j, te, nu: (te[t], 0, jj(t, j, nu) + nj)),
            pl.BlockSpec((None, th, d), lambda t, j, te, nu: (te[t], jj(t, j, nu), 0)),
        ],
        out_specs=pl.BlockSpec((tm, d), lambda t, j, te, nu: (t, 0)),
        scratch_shapes=[pltpu.VMEM((tm, d), BF16), pltpu.VMEM((tm, d), F32)],
    )
    return pl.pallas_call(
        _moe_ffn_kernel,
        out_shape=jax.ShapeDtypeStruct((n_rows, d), F32),
        grid_spec=grid_spec,
        compiler_params=_cparams(("arbitrary", "arbitrary")),
        name="moe_experts",
    )(te, n_used, xs, w_gu, w_gu, w_d)


def _combine_kernel(pos_ref, x_ref, gate_ref, ys_ref, o_ref, buf_ref, sem, *, tmc):
    def start(r, _):
        for k in range(TOP_K):
            pltpu.make_async_copy(ys_ref.at[pl.ds(pos_ref[0, 0, TOP_K * r + k], 1)],
                                  buf_ref.at[k, pl.ds(r, 1)], sem.at[0]).start()
        return 0

    lax.fori_loop(0, tmc, start, 0, unroll=8)
    pltpu.make_async_copy(ys_ref.at[pl.ds(0, TOP_K * tmc)], ys_ref.at[pl.ds(0, TOP_K * tmc)], sem.at[0]).wait()
    g = gate_ref[...]
    acc = x_ref[...]
    for k in range(TOP_K):
        acc = acc + g[:, k:k + 1] * buf_ref[k]
    o_ref[...] = acc


def _combine(x, gate, pos, ys, *, name):
    m, d = x.shape
    ne = gate.shape[1]
    tmc = min(m, 256)
    assert m % tmc == 0
    kern = functools.partial(_combine_kernel, tmc=tmc)
    return pl.pallas_call(
        kern,
        out_shape=jax.ShapeDtypeStruct((m, d), F32),
        grid=(m // tmc,),
        in_specs=[
            pl.BlockSpec((1, 1, TOP_K * tmc), lambda i: (i, 0, 0), memory_space=pltpu.SMEM),
            pl.BlockSpec((tmc, d), lambda i: (i, 0)),
            pl.BlockSpec((tmc, ne), lambda i: (i, 0)),
            pl.BlockSpec(memory_space=pl.ANY),
        ],
        out_specs=pl.BlockSpec((tmc, d), lambda i: (i, 0)),
        scratch_shapes=[pltpu.VMEM((TOP_K, tmc, d), F32), pltpu.SemaphoreType.DMA((1,))],
        compiler_params=_cparams(("arbitrary",)),
        name=name,
    )(pos.reshape(m // tmc, 1, TOP_K * tmc), x, gate, ys)


def _moe_plan(counts, eids, ranks, tm, nt):
    padded = ((counts + tm - 1) // tm) * tm
    ends = jnp.cumsum(padded)
    starts = ends - padded

    def positions(eid, rank):
        start = jnp.zeros_like(rank)
        for e in range(N_EXPERTS):
            start = start + jnp.where(eid == e, starts[e], 0)
        return start + rank

    pos = [positions(e[:, :TOP_K], r[:, :TOP_K]) for e, r in zip(eids, ranks)]
    n_used = (ends[-1] // tm).astype(I32)
    tile = jnp.arange(nt, dtype=I32)
    first_row = jnp.minimum(tile, n_used - 1) * tm
    te = jnp.minimum(jnp.sum((first_row[:, None] >= ends[None, :]).astype(I32), axis=1), N_EXPERTS - 1)
    return pos, te, n_used.reshape(1)


def _block_diag(w):
    n, a, b = w.shape
    return jnp.einsum("nij,nm->nimj", w, jnp.eye(n, dtype=w.dtype)).reshape(n * a, n * b)


def kernel(x_prompt, x_sample, state_conv, state_lru, state_s5_re, state_s5_im, cache_swa0_kv, cache_swa1_kv, cache_swa2_kv, cache_mem_k, cache_mem_v, mem_prompt, norm_mix, norm_xa, norm_ffn, norm_mem, w_in_even, conv_w, conv_b, lru_wa, lru_ba, lru_wx, lru_bx, lru_lam, s5_lam_re, s5_lam_im, s5_log_dt, s5_b_re, s5_b_im, s5_c_re, s5_c_im, s5_d, s5_w_glu, s5_b_glu, w_out_even, w_qkv_odd, q_norm_odd, k_norm_odd, w_o_odd, rel_bias, xa_wq, xa_wkv, xa_qn, xa_kn, xa_wo, ffn_w_gu, ffn_w_down, moe_router_w, moe_router_b, moe_w_gu, moe_w_down):
    bp, s_len, d = x_prompt.shape
    db, t_dec, _ = x_sample.shape
    n_mem = mem_prompt.shape[1]
    tp = SAMPLE_PAD_T
    d_lru = conv_w.shape[-1]
    s5_g, s5_p, s5_h = s5_b_re.shape[1:]
    n_state = s5_g * s5_p
    d_s5 = s5_g * s5_h
    xa_dh = d // XA_HEADS
    caches = (cache_swa0_kv, cache_swa1_kv, cache_swa2_kv)
    bf = lambda w: w.astype(BF16)

    yp = x_prompt
    ys = jnp.pad(x_sample, ((0, 0), (0, tp - t_dec), (0, 0)))

    w_in = bf(w_in_even[0])
    bbr, bbi, apow = _s5_prep(s5_lam_re[0], s5_lam_im[0], s5_log_dt[0], s5_b_re[0], s5_b_im[0])
    eye_g = jnp.eye(s5_g, dtype=F32)
    mix_w = dict(
        cw=conv_w[0], cb=conv_b[0].reshape(1, d_lru),
        wa=bf(_block_diag(lru_wa[0])), ba=lru_ba[0].reshape(1, d_lru),
        wx=bf(_block_diag(lru_wx[0])), bx=lru_bx[0].reshape(1, d_lru),
        lam=lru_lam[0].reshape(1, d_lru),
        bbr=bbr, bbi=bbi, apow=apow,
        ccr=bf(jnp.einsum("ghp,gk->gpkh", s5_c_re[0], eye_g).reshape(n_state, d_s5)),
        cci=bf(jnp.einsum("ghp,gk->gpkh", s5_c_im[0], eye_g).reshape(n_state, d_s5)),
        d=s5_d[0].reshape(1, d_s5), wglu=bf(s5_w_glu[0]), bglu=s5_b_glu[0].reshape(1, d_s5),
        wout=bf(w_out_even[0]),
    )
    z_p = _norm_matmul(yp.reshape(bp * s_len, d), norm_mix[0], w_in, name="in_proj_p").reshape(bp, s_len, -1)
    yp, p_lru, p_s5r, p_s5i = _mixer0(
        z_p, yp, jnp.zeros((bp, V7X_SUBLANES, d_lru), F32), jnp.zeros((bp, 1, d_lru), F32),
        jnp.zeros((bp, 1, n_state), F32), jnp.zeros((bp, 1, n_state), F32), mix_w,
        tc=MIX_TC, last_row=MIX_TC - 1)
    z_s = _norm_matmul(ys.reshape(db * tp, d), norm_mix[0], w_in, name="in_proj_s").reshape(db, tp, -1)
    conv_init = jnp.pad(state_conv[0], ((0, 0), (V7X_SUBLANES - (CONV_W - 1), 0), (0, 0)))
    ys, s_lru, s_s5r, s_s5i = _mixer0(
        z_s, ys, conv_init, state_lru[0].reshape(db, 1, d_lru),
        state_s5_re[0].reshape(db, 1, n_state), state_s5_im[0].reshape(db, 1, n_state), mix_w,
        tc=tp, last_row=t_dec - 1)
    p_state_conv = z_p[:, s_len - (CONV_W - 1):, :d_lru][None]
    s_state_conv = z_s[:, t_dec - (CONV_W - 1):t_dec, :d_lru][None]

    p_mk, p_mv = [], []

    def cross_attention(layer, yp, ys):
        kn_gain = jnp.concatenate([jnp.tile(xa_kn[layer], XA_HEADS), jnp.ones((d,), F32)]).reshape(1, 2 * d)
        kv = _norm_matmul(mem_prompt.reshape(bp * n_mem, d), norm_mem[layer], bf(xa_wkv[layer]),
                          head_gain=kn_gain, n_norm_cols=d, dh=xa_dh, name=f"mem_kv{layer}")
        kv = kv.reshape(bp, n_mem, 2 * d)
        mk, mv = kv[:, :, :d], kv[:, :, d:]
        p_mk.append(mk.reshape(bp, n_mem, XA_HEADS, xa_dh))
        p_mv.append(mv.reshape(bp, n_mem, XA_HEADS, xa_dh))
        wq, wo = bf(xa_wq[layer]), bf(xa_wo[layer])
        yp = _xattn(yp, norm_xa[layer], wq, xa_qn[layer], mk, mv, wo, tm=256, name=f"xattn_p{layer}")
        ys = _xattn(ys, norm_xa[layer], wq, xa_qn[layer],
                    cache_mem_k[layer].reshape(db, n_mem, d), cache_mem_v[layer].reshape(db, n_mem, d), wo,
                    tm=tp, name=f"xattn_s{layer}")
        return yp, ys

    yp, ys = cross_attention(0, yp, ys)
    w_gu0, w_d0 = bf(ffn_w_gu[0]), bf(ffn_w_down[0])
    yp = _ffn(yp.reshape(bp * s_len, d), norm_ffn[0], w_gu0, w_d0, name="ffn_p").reshape(bp, s_len, d)
    ys = _ffn(ys.reshape(db * tp, d), norm_ffn[0], w_gu0, w_d0, name="ffn_s").reshape(db, tp, d)

    n_heads = len(WINDOWS) * HG
    d_c = n_heads * DH
    gw = HG * DH
    w_qkv = bf(w_qkv_odd[0])
    qk_gain = jnp.concatenate([jnp.tile(q_norm_odd[0], n_heads), jnp.tile(k_norm_odd[0], n_heads),
                               jnp.ones((d_c,), F32)]).reshape(1, 3 * d_c)
    qkv_p = _norm_matmul(yp.reshape(bp * s_len, d), norm_mix[1], w_qkv, head_gain=qk_gain,
                         n_norm_cols=2 * d_c, dh=DH, slabs=True, name="qkv_p")
    qkv_s = _norm_matmul(ys.reshape(db * tp, d), norm_mix[1], w_qkv, head_gain=qk_gain,
                         n_norm_cols=2 * d_c, dh=DH, slabs=True, name="qkv_s")
    gs = gw // V7X_LANES
    w_o = bf(w_o_odd[0])
    outs_p, lses_p, outs_s, lses_s, p_swa, s_swa = [], [], [], [], [], []

    def kv_rows(qkv, n_batch, t_len, g, lo, hi):
        ks = (len(WINDOWS) + g) * gs
        vs = (2 * len(WINDOWS) + g) * gs
        slabs = qkv.reshape(-1, n_batch, t_len, V7X_LANES)
        kv = jnp.stack([slabs[ks:ks + gs, :, lo:hi], slabs[vs:vs + gs, :, lo:hi]])
        return jnp.transpose(kv, (2, 3, 0, 1, 4)).reshape(n_batch, hi - lo, 2, HG, DH)

    for g, dil in enumerate(DILATIONS):
        tab = rel_bias[:, g * HG:(g + 1) * HG]
        o, l = _dilated_prompt(qkv_p, tab, g, dil, bp)
        outs_p.append(o)
        lses_p.append(l)
        o, l = _dilated_sample(qkv_s, caches[g][0], tab, g, dil, t_dec)
        outs_s.append(o)
        lses_s.append(l)
        win = min(WINDOWS[g], s_len)
        p_swa.append(kv_rows(qkv_p, bp, s_len, g, s_len - win, s_len)[None])
        s_swa.append(jnp.concatenate([caches[g][0][:, t_dec:], kv_rows(qkv_s, db, tp, g, 0, t_dec)], axis=1)[None])
    yp = _merge_wo(outs_p, lses_p, yp.reshape(bp * s_len, d), w_o, name="merge_wo_p").reshape(bp, s_len, d)
    ys = _merge_wo(outs_s, lses_s, ys.reshape(db * tp, d), w_o, name="merge_wo_s").reshape(db, tp, d)

    yp, ys = cross_attention(1, yp, ys)

    yp2 = yp.reshape(bp * s_len, d)
    ys2 = ys[:, :t_dec].reshape(db * t_dec, d)
    zero_cnt = jnp.zeros((1, N_EXPERTS), F32)
    hn_p, eid_p, gate_p, rank_p, cnt_p = _router(yp2, norm_ffn[1], moe_router_w[0], moe_router_b[0], zero_cnt,
                                                 name="router_p")
    hn_s, eid_s, gate_s, rank_s, cnt_s = _router(ys2, norm_ffn[1], moe_router_w[0], moe_router_b[0], cnt_p,
                                                 name="router_s")
    n_tok = yp2.shape[0] + ys2.shape[0]
    nt = -(-(n_tok * TOP_K) // MOE_TM) + N_EXPERTS
    (pos_p, pos_s), te, n_used = _moe_plan(cnt_s[0].astype(I32), (eid_p, eid_s), (rank_p, rank_s), MOE_TM, nt)
    xs = jnp.zeros((nt * MOE_TM, d), F32)
    xs = _dispatch(hn_p, pos_p, xs, name="dispatch_p")
    xs = _dispatch(hn_s, pos_s, xs, name="dispatch_s")
    ysort = _moe_ffn(xs, te, n_used, bf(moe_w_gu[0]), bf(moe_w_down[0]))
    yp = _combine(yp2, gate_p, pos_p, ysort, name="moe_combine_p").reshape(bp, s_len, d)
    ys = _combine(ys2, gate_s, pos_s, ysort, name="moe_combine_s").reshape(db, t_dec, d)

    return (yp, ys,
            p_state_conv, p_lru.reshape(1, bp, d_lru),
            p_s5r.reshape(1, bp, s5_g, s5_p), p_s5i.reshape(1, bp, s5_g, s5_p),
            p_swa[0], p_swa[1], p_swa[2], jnp.stack(p_mk), jnp.stack(p_mv),
            s_state_conv, s_lru.reshape(1, db, d_lru),
            s_s5r.reshape(1, db, s5_g, s5_p), s_s5i.reshape(1, db, s5_g, s5_p),
            s_swa[0], s_swa[1], s_swa[2])
```

```python
import functools
import math

import jax
import jax.numpy as jnp
import numpy as np
from jax import lax
from jax.experimental import pallas as pl
from jax.experimental.pallas import tpu as pltpu

F32 = jnp.float32
BF16 = jnp.bfloat16
I32 = jnp.int32

EPS = 1e-6
NEG = -1e30

V7X_SUBLANES = 8
V7X_LANES = 128
V7X_VMEM_BYTES = 64 * 1024 * 1024
VMEM_LIMIT = V7X_VMEM_BYTES - 8 * 1024 * 1024

LRU_C = 8.0
CONV_W = 4
WINDOWS = (128, 512, 2048)
DILATIONS = (1, 4, 16)
HG = 8
DH = 64
N_STRIDE = 128
QB = 128
REL_BUCKETS = 32
REL_MAX_DIST = WINDOWS[-1]
XA_HEADS = 4
N_EXPERTS = 8
TOP_K = 2
SAMPLE_PAD_T = 16

MIX_TC = 256
MOE_TM = 512
MOE_TH = 512
FFN_TH = 512


def _cparams(sem):
    return pltpu.CompilerParams(dimension_semantics=sem, vmem_limit_bytes=VMEM_LIMIT)


def _const_spec(shape):
    nd = len(shape)
    return pl.BlockSpec(shape, lambda *_: (0,) * nd)


def _rms(x, g):
    return x * lax.rsqrt(jnp.mean(x * x, axis=-1, keepdims=True) + EPS) * g


def _norm_matmul_kernel(x_ref, g_ref, w_ref, hg_ref, hs_ref, o_ref, hn_ref, *, n_norm_tiles, dh, slabs):
    j = pl.program_id(1)

    @pl.when(j == 0)
    def _():
        hn_ref[...] = _rms(x_ref[...], g_ref[...]).astype(BF16)

    def emit(val):
        if slabs:
            for c in range(o_ref.shape[0]):
                o_ref[c] = val[:, c * V7X_LANES:(c + 1) * V7X_LANES]
        else:
            o_ref[...] = val

    y = jnp.dot(hn_ref[...], w_ref[...], preferred_element_type=F32)
    if n_norm_tiles == 0:
        emit(y)
    else:
        @pl.when(j < n_norm_tiles)
        def _():
            ssq = jnp.dot((y * y).astype(BF16), hs_ref[...], preferred_element_type=F32)
            emit(y * lax.rsqrt(ssq * (1.0 / dh) + EPS) * hg_ref[...])

        @pl.when(j >= n_norm_tiles)
        def _():
            emit(y)


def _head_sum_matrix(tn, dh):
    idx = np.arange(tn) // dh
    return jnp.asarray((idx[:, None] == idx[None, :]).astype(np.float32), dtype=BF16)


def _norm_matmul(x, g, w, *, tn=512, head_gain=None, n_norm_cols=0, dh=1, slabs=False, name):
    m, d = x.shape
    n = w.shape[1]
    tm = min(m, 1024)
    assert m % tm == 0 and n % tn == 0 and n_norm_cols % tn == 0
    if head_gain is None:
        head_gain = jnp.ones((1, n), F32)
    hs = _head_sum_matrix(tn, dh)
    kern = functools.partial(_norm_matmul_kernel, n_norm_tiles=n_norm_cols // tn, dh=dh, slabs=slabs)
    if slabs:
        out_shape = jax.ShapeDtypeStruct((n // V7X_LANES, m, V7X_LANES), F32)
        out_spec = pl.BlockSpec((tn // V7X_LANES, tm, V7X_LANES), lambda i, j: (j, i, 0))
    else:
        out_shape = jax.ShapeDtypeStruct((m, n), F32)
        out_spec = pl.BlockSpec((tm, tn), lambda i, j: (i, j))
    return pl.pallas_call(
        kern,
        out_shape=out_shape,
        grid=(m // tm, n // tn),
        in_specs=[
            pl.BlockSpec((tm, d), lambda i, j: (i, 0)),
            pl.BlockSpec((1, d), lambda i, j: (0, 0)),
            pl.BlockSpec((d, tn), lambda i, j: (0, j)),
            pl.BlockSpec((1, tn), lambda i, j: (0, j)),
            pl.BlockSpec((tn, tn), lambda i, j: (0, 0)),
        ],
        out_specs=out_spec,
        scratch_shapes=[pltpu.VMEM((tm, d), BF16)],
        compiler_params=_cparams(("parallel", "arbitrary")),
        name=name,
    )(x, g.reshape(1, d), w, head_gain, hs)


def _s5_prep_kernel(lre_ref, lim_ref, ldt_ref, bre_ref, bim_ref, bbr_ref, bbi_ref, apow_ref):
    lr = lre_ref[...]
    li = lim_ref[...]
    dt = jnp.exp(ldt_ref[...])
    mag = jnp.exp(lr * dt)
    ab_r = mag * jnp.cos(li * dt)
    ab_i = mag * jnp.sin(li * dt)
    den = lr * lr + li * li
    nr = ab_r - 1.0
    cr = (nr * lr + ab_i * li) / den
    ci = (ab_i * lr - nr * li) / den
    b_r = bre_ref[...]
    b_i = bim_ref[...]
    bbr_ref[...] = (cr * b_r - ci * b_i).astype(BF16)
    bbi_ref[...] = (cr * b_i + ci * b_r).astype(BF16)

    n = lr.shape[1]
    row = lax.broadcasted_iota(I32, (V7X_SUBLANES, n), 0)

    def power(kf):
        mg = jnp.exp(kf * (lr * dt))
        return mg * jnp.cos(kf * (li * dt)), mg * jnp.sin(kf * (li * dt))

    for i, s in enumerate((1, 2, 4)):
        pr, pi = power(jnp.full((V7X_SUBLANES, n), s, F32))
        keep = row >= s
        apow_ref[2 * i] = jnp.where(keep, pr, 0.0)
        apow_ref[2 * i + 1] = jnp.where(keep, pi, 0.0)
    pr, pi = power((row + 1).astype(F32))
    apow_ref[6] = pr
    apow_ref[7] = pi


def _s5_prep(lam_re, lam_im, log_dt, b_re, b_im):
    g, p, h = b_re.shape
    n = g * p
    eye = jnp.eye(g, dtype=F32)
    bre_bd = jnp.einsum("gph,gk->khgp", b_re, eye).reshape(g * h, n)
    bim_bd = jnp.einsum("gph,gk->khgp", b_im, eye).reshape(g * h, n)
    ldt = jnp.broadcast_to(log_dt[:, None], (g, p)).reshape(1, n)
    return pl.pallas_call(
        _s5_prep_kernel,
        out_shape=(
            jax.ShapeDtypeStruct((g * h, n), BF16),
            jax.ShapeDtypeStruct((g * h, n), BF16),
            jax.ShapeDtypeStruct((8, V7X_SUBLANES, n), F32),
        ),
        compiler_params=pltpu.CompilerParams(vmem_limit_bytes=VMEM_LIMIT),
        name="s5_prep",
    )(lam_re.reshape(1, n), lam_im.reshape(1, n), ldt, bre_bd, bim_bd)


def _mixer0_kernel(z_ref, x_ref, convi_ref, h0_ref, sr0_ref, si0_ref,
                   cw_ref, cb_ref, wa_ref, ba_ref, wx_ref, bx_ref, lam_ref,
                   bbr_ref, bbi_ref, apow_ref, ccr_ref, cci_ref, d_ref, wglu_ref, bglu_ref, wout_ref,
                   y_ref, hl_ref, srl_ref, sil_ref,
                   ext_ref, hc_ref, src_ref, sic_ref, xr_ref, xi_ref, ha_ref, hb_ref,
                   *, tc, last_row):
    c = pl.program_id(1)
    d_lru = cw_ref.shape[1]
    d_s5 = d_ref.shape[1]
    n_tiles = tc // V7X_SUBLANES

    @pl.when(c == 0)
    def _():
        ext_ref[...] = convi_ref[...]
        hc_ref[...] = h0_ref[...]
        src_ref[...] = sr0_ref[...]
        sic_ref[...] = si0_ref[...]

    z = z_ref[...]
    xa = z[:, :d_lru]
    ga = z[:, d_lru:2 * d_lru]
    u = z[:, 2 * d_lru:]

    ext = jnp.concatenate([ext_ref[...], xa], axis=0)
    xc = cb_ref[...] + xa * cw_ref[CONV_W - 1:CONV_W, :]
    for s in range(1, CONV_W):
        xc = xc + pltpu.roll(ext, s, 0)[V7X_SUBLANES:, :] * cw_ref[CONV_W - 1 - s:CONV_W - s, :]
    ext_ref[...] = xa[tc - V7X_SUBLANES:, :]

    xcb = xc.astype(BF16)
    r = jax.nn.sigmoid(jnp.dot(xcb, wa_ref[...], preferred_element_type=F32) + ba_ref[...])
    ig = jax.nn.sigmoid(jnp.dot(xcb, wx_ref[...], preferred_element_type=F32) + bx_ref[...])
    lam = lam_ref[...]
    softplus_neg = jnp.maximum(-lam, 0.0) + jnp.log1p(jnp.exp(-jnp.abs(lam)))
    log_a = -LRU_C * r * softplus_neg
    a = jnp.exp(log_a)
    bt = jnp.sqrt(-jnp.tanh(log_a) * (a * a + 1.0)) * ig * xc

    a3 = a.reshape(n_tiles, V7X_SUBLANES, d_lru)
    b3 = bt.reshape(n_tiles, V7X_SUBLANES, d_lru)
    row = lax.broadcasted_iota(I32, (1, V7X_SUBLANES, d_lru), 1)
    for s in (1, 2, 4):
        keep = row >= s
        ar = pltpu.roll(a3, s, 1)
        br = pltpu.roll(b3, s, 1)
        b3 = jnp.where(keep, a3 * br + b3, b3)
        a3 = jnp.where(keep, a3 * ar, a3)
    ha_ref[...] = a3.reshape(tc, d_lru)
    hb_ref[...] = b3.reshape(tc, d_lru)

    def lru_tile(i, carry):
        r0 = pl.multiple_of(i * V7X_SUBLANES, V7X_SUBLANES)
        h = ha_ref[pl.ds(r0, V7X_SUBLANES), :] * carry + hb_ref[pl.ds(r0, V7X_SUBLANES), :]
        hb_ref[pl.ds(r0, V7X_SUBLANES), :] = h
        return h[V7X_SUBLANES - 1:, :]

    hc_ref[...] = lax.fori_loop(0, n_tiles, lru_tile, hc_ref[...])
    hs = hb_ref[...]
    hl_ref[...] = hb_ref[last_row:last_row + 1, :]
    ya = hs * jax.nn.gelu(ga)

    ub = u.astype(BF16)
    n_state = bbr_ref.shape[1]
    xr3 = jnp.dot(ub, bbr_ref[...], preferred_element_type=F32).reshape(n_tiles, V7X_SUBLANES, n_state)
    xi3 = jnp.dot(ub, bbi_ref[...], preferred_element_type=F32).reshape(n_tiles, V7X_SUBLANES, n_state)
    for i, s in enumerate((1, 2, 4)):
        cr = apow_ref[2 * i][None]
        ci = apow_ref[2 * i + 1][None]
        rr = pltpu.roll(xr3, s, 1)
        ri = pltpu.roll(xi3, s, 1)
        xr3, xi3 = xr3 + cr * rr - ci * ri, xi3 + cr * ri + ci * rr
    xr_ref[...] = xr3.reshape(tc, n_state)
    xi_ref[...] = xi3.reshape(tc, n_state)

    def s5_tile(i, carry):
        cr_, ci_ = carry
        r0 = pl.multiple_of(i * V7X_SUBLANES, V7X_SUBLANES)
        pr = apow_ref[6]
        pi = apow_ref[7]
        nr_ = xr_ref[pl.ds(r0, V7X_SUBLANES), :] + pr * cr_ - pi * ci_
        ni_ = xi_ref[pl.ds(r0, V7X_SUBLANES), :] + pr * ci_ + pi * cr_
        xr_ref[pl.ds(r0, V7X_SUBLANES), :] = nr_
        xi_ref[pl.ds(r0, V7X_SUBLANES), :] = ni_
        return nr_[V7X_SUBLANES - 1:, :], ni_[V7X_SUBLANES - 1:, :]

    cr_f, ci_f = lax.fori_loop(0, n_tiles, s5_tile, (src_ref[...], sic_ref[...]))
    src_ref[...] = cr_f
    sic_ref[...] = ci_f
    srl_ref[...] = xr_ref[last_row:last_row + 1, :]
    sil_ref[...] = xi_ref[last_row:last_row + 1, :]

    ys = (jnp.dot(xr_ref[...].astype(BF16), ccr_ref[...], preferred_element_type=F32)
          - jnp.dot(xi_ref[...].astype(BF16), cci_ref[...], preferred_element_type=F32))
    ys = ys + d_ref[...] * u
    gs = jax.nn.gelu(ys)
    yb = gs * jax.nn.sigmoid(jnp.dot(gs.astype(BF16), wglu_ref[...], preferred_element_type=F32)
                             + bglu_ref[...])

    y_ref[...] = (x_ref[...]
                  + jnp.dot(ya.astype(BF16), wout_ref[:d_lru, :], preferred_element_type=F32)
                  + jnp.dot(yb.astype(BF16), wout_ref[d_lru:, :], preferred_element_type=F32))


def _mixer0(z, x, conv_init, h0, sr0, si0, wts, *, tc, last_row):
    bn, t_len, d_in = z.shape
    d = x.shape[2]
    d_lru = wts["cw"].shape[1]
    n_state = wts["bbr"].shape[1]
    assert t_len % tc == 0
    kern = functools.partial(_mixer0_kernel, tc=tc, last_row=last_row)
    wnames = ("cw", "cb", "wa", "ba", "wx", "bx", "lam", "bbr", "bbi", "apow",
              "ccr", "cci", "d", "wglu", "bglu", "wout")
    wlist = [wts[k] for k in wnames]
    per_b = lambda shape: pl.BlockSpec((None,) + shape, lambda b, c: (b,) + (0,) * len(shape))
    in_specs = [
        pl.BlockSpec((None, tc, d_in), lambda b, c: (b, c, 0)),
        pl.BlockSpec((None, tc, d), lambda b, c: (b, c, 0)),
        per_b((V7X_SUBLANES, d_lru)), per_b((1, d_lru)), per_b((1, n_state)), per_b((1, n_state)),
    ] + [_const_spec(w.shape) for w in wlist]
    return pl.pallas_call(
        kern,
        out_shape=(
            jax.ShapeDtypeStruct((bn, t_len, d), F32),
            jax.ShapeDtypeStruct((bn, 1, d_lru), F32),
            jax.ShapeDtypeStruct((bn, 1, n_state), F32),
            jax.ShapeDtypeStruct((bn, 1, n_state), F32),
        ),
        grid=(bn, t_len // tc),
        in_specs=in_specs,
        out_specs=(
            pl.BlockSpec((None, tc, d), lambda b, c: (b, c, 0)),
            per_b((1, d_lru)), per_b((1, n_state)), per_b((1, n_state)),
        ),
        scratch_shapes=[
            pltpu.VMEM((V7X_SUBLANES, d_lru), F32),
            pltpu.VMEM((1, d_lru), F32),
            pltpu.VMEM((1, n_state), F32),
            pltpu.VMEM((1, n_state), F32),
            pltpu.VMEM((tc, n_state), F32),
            pltpu.VMEM((tc, n_state), F32),
            pltpu.VMEM((tc, d_lru), F32),
            pltpu.VMEM((tc, d_lru), F32),
        ],
        compiler_params=_cparams(("parallel", "arbitrary")),
        name="mixer0",
    )(z, x, conv_init, h0, sr0, si0, *wlist)


def _xattn_kernel(x_ref, g_ref, wq_ref, qg_ref, mk_ref, mv_ref, wo_ref, o_ref):
    x = x_ref[...]
    q = jnp.dot(_rms(x, g_ref[...]).astype(BF16), wq_ref[...], preferred_element_type=F32)
    dh = qg_ref.shape[1]
    outs = []
    head_of = (lambda ref, h: ref[:, h, :]) if len(mk_ref.shape) == 3 else (
        lambda ref, h: ref[:, h * dh:(h + 1) * dh])
    for h in range(XA_HEADS):
        sl = slice(h * dh, (h + 1) * dh)
        qn = _rms(q[:, sl], qg_ref[...]).astype(BF16)
        s = lax.dot_general(qn, head_of(mk_ref, h).astype(BF16), (((1,), (1,)), ((), ())),
                            preferred_element_type=F32) * (dh ** -0.5)
        m = jnp.max(s, axis=-1, keepdims=True)
        p = jnp.exp(s - m)
        den = jnp.sum(p, axis=-1, keepdims=True)
        oh = jnp.dot(p.astype(BF16), head_of(mv_ref, h).astype(BF16), preferred_element_type=F32) / den
        outs.append(oh.astype(BF16))
    o_ref[...] = x + jnp.dot(jnp.concatenate(outs, axis=-1), wo_ref[...], preferred_element_type=F32)


def _xattn(x, g, wq, qg, mk, mv, wo, *, tm, name, layer=None):
    bn, t_len, d = x.shape
    assert t_len % tm == 0
    if layer is None:
        mem_spec = pl.BlockSpec((None,) + mk.shape[1:], lambda b, i: (b, 0, 0))
    else:
        mem_spec = pl.BlockSpec((None, None) + mk.shape[2:], lambda b, i: (layer, b, 0, 0, 0))
    return pl.pallas_call(
        _xattn_kernel,
        out_shape=jax.ShapeDtypeStruct(x.shape, F32),
        grid=(bn, t_len // tm),
        in_specs=[
            pl.BlockSpec((None, tm, d), lambda b, i: (b, i, 0)),
            _const_spec((1, d)),
            _const_spec(wq.shape),
            _const_spec((1, qg.shape[-1])),
            mem_spec,
            mem_spec,
            _const_spec(wo.shape),
        ],
        out_specs=pl.BlockSpec((None, tm, d), lambda b, i: (b, i, 0)),
        compiler_params=_cparams(("parallel", "arbitrary")),
        name=name,
    )(x, g.reshape(1, d), wq, qg.reshape(1, -1), mk, mv, wo)


def _ffn_kernel(x_ref, g_ref, wg_ref, wu_ref, wd_ref, o_ref, hn_ref, acc_ref):
    j = pl.program_id(1)

    @pl.when(j == 0)
    def _():
        hn_ref[...] = _rms(x_ref[...], g_ref[...]).astype(BF16)
        acc_ref[...] = jnp.zeros_like(acc_ref)

    hn = hn_ref[...]
    gate = jnp.dot(hn, wg_ref[...], preferred_element_type=F32)
    up = jnp.dot(hn, wu_ref[...], preferred_element_type=F32)
    act = (jax.nn.silu(gate) * up).astype(BF16)
    acc_ref[...] += jnp.dot(act, wd_ref[...], preferred_element_type=F32)

    @pl.when(j == pl.num_programs(1) - 1)
    def _():
        o_ref[...] = x_ref[...] + acc_ref[...]


def _ffn(x, g, w_gu, w_d, *, name):
    m, d = x.shape
    hid = w_d.shape[0]
    th = FFN_TH
    tm = min(m, 1024)
    nj = hid // th
    assert m % tm == 0 and hid % th == 0
    return pl.pallas_call(
        _ffn_kernel,
        out_shape=jax.ShapeDtypeStruct((m, d), F32),
        grid=(m // tm, nj),
        in_specs=[
            pl.BlockSpec((tm, d), lambda i, j: (i, 0)),
            pl.BlockSpec((1, d), lambda i, j: (0, 0)),
            pl.BlockSpec((d, th), lambda i, j: (0, j)),
            pl.BlockSpec((d, th), lambda i, j: (0, j + nj)),
            pl.BlockSpec((th, d), lambda i, j: (j, 0)),
        ],
        out_specs=pl.BlockSpec((tm, d), lambda i, j: (i, 0)),
        scratch_shapes=[pltpu.VMEM((tm, d), BF16), pltpu.VMEM((tm, d), F32)],
        compiler_params=_cparams(("parallel", "arbitrary")),
        name=name,
    )(x, g.reshape(1, d), w_gu, w_gu, w_d)


def _rel_bucket_np(dist):
    dist = np.clip(np.asarray(dist), 0, None)
    max_exact = REL_BUCKETS // 2
    safe = np.maximum(dist, max_exact).astype(np.float32)
    large = max_exact + np.floor(np.log(safe / max_exact) / math.log(REL_MAX_DIST / max_exact)
                                 * (REL_BUCKETS - max_exact)).astype(np.int32)
    large = np.minimum(large, REL_BUCKETS - 1)
    return np.where(dist < max_exact, dist, large).astype(np.int32)


def _dil_kernel(q_ref, k_ref, v_ref, bias_ref, o_ref, l_ref, kprev_ref, vprev_ref, *, d):
    @pl.when(pl.program_id(1) == 0)
    def _():
        kprev_ref[...] = jnp.zeros_like(kprev_ref)
        vprev_ref[...] = jnp.zeros_like(vprev_ref)

    lane = lax.broadcasted_iota(I32, (QB, V7X_LANES), 1)
    n_slab = q_ref.shape[0]
    heads_per_slab = V7X_LANES // DH

    def residue(r, _):
        rows = pl.ds(r, QB, stride=d)
        lse = jnp.zeros((QB, V7X_LANES), F32)
        for c in range(n_slab):
            q = q_ref[c, rows, :].astype(BF16)
            kc = k_ref[c, rows, :].astype(BF16)
            vc = v_ref[c, rows, :].astype(BF16)
            k = jnp.concatenate([kprev_ref[r, c], kc], axis=0)
            v = jnp.concatenate([vprev_ref[r, c], vc], axis=0)
            outs = []
            for hh in range(heads_per_slab):
                h = c * heads_per_slab + hh
                sl = slice(hh * DH, (hh + 1) * DH)
                s = lax.dot_general(q[:, sl], k[:, sl], (((1,), (1,)), ((), ())),
                                    preferred_element_type=F32) * (DH ** -0.5) + bias_ref[h]
                m = jnp.max(s, axis=-1, keepdims=True)
                p = jnp.exp(s - m)
                den = jnp.sum(p, axis=-1, keepdims=True)
                outs.append(jnp.dot(p.astype(BF16), v[:, sl], preferred_element_type=F32) / den)
                lse = jnp.where(lane == h, m + jnp.log(den), lse)
            o_ref[c, rows, :] = jnp.concatenate(outs, axis=-1)
            kprev_ref[r, c] = kc
            vprev_ref[r, c] = vc
        l_ref[rows, :] = lse
        return 0

    lax.fori_loop(0, d, residue, 0)


def _dilated_prompt(qkv, tab, g, d, bn):
    n_slabs, m_rows, _ = qkv.shape
    s_len = m_rows // bn
    gw = HG * DH
    gs = gw // V7X_LANES
    chunk = QB * d
    n_chunks = s_len // chunk
    assert s_len % chunk == 0
    k_col = n_slabs // gs // 3

    qi = np.arange(QB)[:, None]
    ki = np.arange(2 * QB)[None, :]
    dist = qi + QB - ki
    band = (dist >= 0) & (dist <= N_STRIDE)
    first = band & (ki >= QB)
    onehot = (np.arange(REL_BUCKETS)[:, None] == _rel_bucket_np(dist * d).reshape(1, -1)).astype(np.float32)
    bias = jnp.dot(tab.T.astype(F32), jnp.asarray(onehot),
                   precision=lax.Precision.HIGHEST).reshape(HG, QB, 2 * QB)
    bias2 = jnp.stack([jnp.where(first[None], bias, NEG), jnp.where(band[None], bias, NEG)])

    col = lambda section: (lambda b, n: (section * k_col + g, b * n_chunks + n, 0))
    blk = (gs, chunk, V7X_LANES)
    return pl.pallas_call(
        functools.partial(_dil_kernel, d=d),
        out_shape=(jax.ShapeDtypeStruct((gs, m_rows, V7X_LANES), F32),
                   jax.ShapeDtypeStruct((m_rows, V7X_LANES), F32)),
        grid=(bn, n_chunks),
        in_specs=[
            pl.BlockSpec(blk, col(0)), pl.BlockSpec(blk, col(1)), pl.BlockSpec(blk, col(2)),
            pl.BlockSpec((None, HG, QB, 2 * QB), lambda b, n: (jnp.minimum(n, 1), 0, 0, 0)),
        ],
        out_specs=(pl.BlockSpec(blk, lambda b, n: (0, b * n_chunks + n, 0)),
                   pl.BlockSpec((chunk, V7X_LANES), lambda b, n: (b * n_chunks + n, 0))),
        scratch_shapes=[pltpu.VMEM((d, gs, QB, V7X_LANES), BF16), pltpu.VMEM((d, gs, QB, V7X_LANES), BF16)],
        compiler_params=_cparams(("parallel", "arbitrary")),
        name=f"dilated_prompt_g{g}",
    )(qkv, qkv, qkv, bias2)


def _split_dot(x, w_ref):
    hi = x.astype(BF16)
    lo = (x - hi.astype(F32)).astype(BF16)
    w = w_ref[...]
    return jnp.dot(hi, w, preferred_element_type=F32) + jnp.dot(lo, w, preferred_element_type=F32)


def _dil_sample_kernel(q_ref, k_ref, v_ref, cache_ref, bias_ref, biasn_ref, o_ref, l_ref, cout_ref, *, t_valid):
    n_slab, t_pad, _ = q_ref.shape
    w = cache_ref.shape[-1]
    unslab = lambda ref: jnp.concatenate([ref[c] for c in range(n_slab)], axis=-1)
    qn = unslab(q_ref)
    kn = unslab(k_ref)
    vn = unslab(v_ref)

    def as_last_columns(x):
        shifted = pltpu.roll(x, t_pad - t_valid, 0)
        tile = jnp.concatenate([jnp.zeros((V7X_LANES - t_pad, x.shape[1]), F32), shifted], axis=0)
        return tile.T

    new_cols = (as_last_columns(kn), as_last_columns(vn))
    lane = lax.broadcasted_iota(I32, (t_pad, V7X_LANES), 1)
    is_new = lax.broadcasted_iota(I32, (DH, V7X_LANES), 1) >= V7X_LANES - t_valid
    qb, kb, vb = qn.astype(BF16), kn.astype(BF16), vn.astype(BF16)
    lse = jnp.zeros((t_pad, V7X_LANES), F32)
    outs = []
    for h in range(HG):
        sl = slice(h * DH, (h + 1) * DH)
        k_t = cache_ref[0, h]
        v_t = cache_ref[1, h]
        s = jnp.dot(qb[:, sl], k_t.astype(BF16), preferred_element_type=F32) * (DH ** -0.5) + bias_ref[h]
        sn = lax.dot_general(qb[:, sl], kb[:, sl], (((1,), (1,)), ((), ())),
                             preferred_element_type=F32) * (DH ** -0.5) + biasn_ref[h]
        m = jnp.maximum(jnp.max(s, axis=-1, keepdims=True), jnp.max(sn, axis=-1, keepdims=True))
        p = jnp.exp(s - m)
        pn = jnp.exp(sn - m)
        den = jnp.sum(p, axis=-1, keepdims=True) + jnp.sum(pn, axis=-1, keepdims=True)
        pv = lax.dot_general(p.astype(BF16), v_t.astype(BF16), (((1,), (1,)), ((), ())),
                             preferred_element_type=F32)
        outs.append((pv + jnp.dot(pn.astype(BF16), vb[:, sl], preferred_element_type=F32)) / den)
        lse = jnp.where(lane == h, m + jnp.log(den), lse)
        for kv, old in enumerate((k_t, v_t)):
            moved = pltpu.roll(old, w - t_valid, 1)
            cout_ref[kv, h] = moved
            cout_ref[kv, h, :, w - V7X_LANES:] = jnp.where(is_new, new_cols[kv][sl, :], moved[:, w - V7X_LANES:])
    o = jnp.concatenate(outs, axis=-1)
    for c in range(n_slab):
        o_ref[c] = o[:, c * V7X_LANES:(c + 1) * V7X_LANES]
    l_ref[...] = lse


def _dilated_sample(qkv, cache, tab, g, d, t_valid):
    n_slabs, m_rows, _ = qkv.shape
    db, wb = cache.shape[0], cache.shape[-1]
    t_pad = m_rows // db
    gw = HG * DH
    gs = gw // V7X_LANES
    assert wb == N_STRIDE * d and t_valid <= V7X_LANES and wb % V7X_LANES == 0
    k_col = n_slabs // gs // 3

    vec = tab[_rel_bucket_np(d * np.arange(N_STRIDE + 1))].astype(F32).T
    spread = jnp.full((HG, N_STRIDE, d), NEG, F32).at[:, :, 0].set(vec[:, N_STRIDE:0:-1]).reshape(HG, wb)
    pos = np.arange(wb)[None, :]
    tq = np.arange(t_pad)[:, None]
    bias = jnp.stack([jnp.where(pos >= t, jnp.roll(spread, t, axis=1), NEG) for t in range(t_pad)],
                     axis=1)
    tn = np.arange(t_pad)[None, :]
    jn = (tq - tn) // d
    ok = (tn <= tq) & ((tq - tn) % d == 0) & (jn <= N_STRIDE) & (tn < t_valid)
    bias_new = jnp.where(ok[None], vec[:, np.clip(jn, 0, N_STRIDE)], NEG)

    kern = functools.partial(_dil_sample_kernel, t_valid=t_valid)
    blk = (gs, t_pad, V7X_LANES)
    cblk = pl.BlockSpec((None,) + cache.shape[1:], lambda b: (b, 0, 0, 0, 0))
    return pl.pallas_call(
        kern,
        out_shape=(jax.ShapeDtypeStruct((gs, m_rows, V7X_LANES), F32),
                   jax.ShapeDtypeStruct((m_rows, V7X_LANES), F32),
                   jax.ShapeDtypeStruct(cache.shape, F32)),
        grid=(db,),
        in_specs=[
            pl.BlockSpec(blk, lambda b: (g, b, 0)),
            pl.BlockSpec(blk, lambda b: (k_col + g, b, 0)),
            pl.BlockSpec(blk, lambda b: (2 * k_col + g, b, 0)),
            cblk, _const_spec(bias.shape), _const_spec(bias_new.shape),
        ],
        out_specs=(pl.BlockSpec(blk, lambda b: (0, b, 0)),
                   pl.BlockSpec((t_pad, V7X_LANES), lambda b: (b, 0)),
                   cblk),
        compiler_params=_cparams(("parallel",)),
        name=f"dilated_sample_g{g}",
    )(qkv, qkv, qkv, cache, bias, bias_new)


def _merge_wo_kernel(o0_ref, o1_ref, o2_ref, l0_ref, l1_ref, l2_ref, x_ref, hexp_ref, w_ref, out_ref):
    l0, l1, l2 = l0_ref[...], l1_ref[...], l2_ref[...]
    m = jnp.maximum(jnp.maximum(l0, l1), l2)
    e0, e1, e2 = jnp.exp(l0 - m), jnp.exp(l1 - m), jnp.exp(l2 - m)
    inv = 1.0 / (e0 + e1 + e2)
    unslab = lambda ref: jnp.concatenate([ref[c] for c in range(ref.shape[0])], axis=-1)
    o = (_split_dot(e0 * inv, hexp_ref) * unslab(o0_ref) + _split_dot(e1 * inv, hexp_ref) * unslab(o1_ref)
         + _split_dot(e2 * inv, hexp_ref) * unslab(o2_ref))
    out_ref[...] = x_ref[...] + jnp.dot(o.astype(BF16), w_ref[...], preferred_element_type=F32)


def _merge_wo(outs, lses, x, w, *, name):
    m, d = x.shape
    gw = w.shape[0]
    tm = min(m, 512)
    assert m % tm == 0
    head = np.arange(gw) // DH
    hexp = jnp.asarray((np.arange(V7X_LANES)[:, None] == head[None, :]).astype(np.float32), dtype=BF16)
    row = lambda width: pl.BlockSpec((tm, width), lambda i: (i, 0))
    slab = pl.BlockSpec((gw // V7X_LANES, tm, V7X_LANES), lambda i: (0, i, 0))
    return pl.pallas_call(
        _merge_wo_kernel,
        out_shape=jax.ShapeDtypeStruct((m, d), F32),
        grid=(m // tm,),
        in_specs=[slab] * 3 + [row(V7X_LANES)] * 3 + [row(d), _const_spec(hexp.shape), _const_spec(w.shape)],
        out_specs=row(d),
        compiler_params=_cparams(("parallel",)),
        name=name,
    )(*outs, *lses, x, hexp, w)


def _router_kernel(x_ref, g_ref, wr_ref, br_ref, cnt0_ref, tri_ref,
                   hn_ref, eid_ref, gate_ref, rank_ref, cnt_ref, run_ref):
    @pl.when(pl.program_id(0) == 0)
    def _():
        run_ref[...] = cnt0_ref[...]

    hn = _rms(x_ref[...], g_ref[...])
    hn_ref[...] = hn
    logits = jnp.dot(hn, wr_ref[...], preferred_element_type=F32,
                     precision=lax.Precision.HIGHEST) + br_ref[...]
    ne = logits.shape[1]
    lane = lax.broadcasted_iota(I32, logits.shape, 1)
    m1 = jnp.max(logits, axis=-1, keepdims=True)
    i1 = jnp.min(jnp.where(logits == m1, lane, ne), axis=-1, keepdims=True)
    rest = jnp.where(lane == i1, -jnp.inf, logits)
    m2 = jnp.max(rest, axis=-1, keepdims=True)
    i2 = jnp.min(jnp.where(rest == m2, lane, ne), axis=-1, keepdims=True)
    e2 = jnp.exp(m2 - m1)
    g1 = 1.0 / (1.0 + e2)
    g2 = e2 / (1.0 + e2)
    eid_ref[...] = jnp.where(lane == 0, i1, jnp.where(lane == 1, i2, 0))
    gate_ref[...] = jnp.where(lane == 0, g1, jnp.where(lane == 1, g2, 0.0))
    chosen = jnp.logical_or(lane == i1, lane == i2)
    before = jnp.dot(tri_ref[...], jnp.where(chosen, 1.0, 0.0).astype(BF16),
                     preferred_element_type=F32) + run_ref[...]
    r1 = jnp.sum(jnp.where(lane == i1, before, 0.0), axis=-1, keepdims=True)
    r2 = jnp.sum(jnp.where(lane == i2, before, 0.0), axis=-1, keepdims=True)
    rank_ref[...] = jnp.where(lane == 0, r1, jnp.where(lane == 1, r2, 0.0)).astype(I32)
    run_ref[...] += jnp.sum(jnp.where(chosen, 1.0, 0.0), axis=0, keepdims=True)
    cnt_ref[...] = run_ref[...]


def _router(x, g, wr, br, cnt0, *, name):
    m, d = x.shape
    ne = wr.shape[1]
    tm = min(m, 512)
    assert m % tm == 0
    tri = jnp.asarray(np.tril(np.ones((tm, tm), np.float32), -1), dtype=BF16)
    row = lambda width: pl.BlockSpec((tm, width), lambda i: (i, 0))
    return pl.pallas_call(
        _router_kernel,
        out_shape=(jax.ShapeDtypeStruct((m, d), F32),
                   jax.ShapeDtypeStruct((m, ne), I32),
                   jax.ShapeDtypeStruct((m, ne), F32),
                   jax.ShapeDtypeStruct((m, ne), I32),
                   jax.ShapeDtypeStruct((1, ne), F32)),
        grid=(m // tm,),
        in_specs=[row(d), _const_spec((1, d)), _const_spec(wr.shape), _const_spec((1, ne)),
                  _const_spec((1, ne)), _const_spec((tm, tm))],
        out_specs=(row(d), row(ne), row(ne), row(ne), _const_spec((1, ne))),
        scratch_shapes=[pltpu.VMEM((1, ne), F32)],
        compiler_params=_cparams(("arbitrary",)),
        name=name,
    )(x, g.reshape(1, d), wr, br.reshape(1, ne), cnt0, tri)


def _dispatch_kernel(pos_ref, hn_ref, xs_in_ref, xs_ref, sem, *, tmd):
    del xs_in_ref

    def start(r, _):
        for k in range(TOP_K):
            pltpu.make_async_copy(hn_ref.at[pl.ds(r, 1)],
                                  xs_ref.at[pl.ds(pos_ref[0, 0, TOP_K * r + k], 1)], sem.at[0]).start()
        return 0

    lax.fori_loop(0, tmd, start, 0, unroll=8)
    pltpu.make_async_copy(xs_ref.at[pl.ds(0, TOP_K * tmd)], xs_ref.at[pl.ds(0, TOP_K * tmd)], sem.at[0]).wait()


def _dispatch(hn, pos, xs, *, name):
    m, d = hn.shape
    tmd = min(m, 256)
    assert m % tmd == 0
    kern = functools.partial(_dispatch_kernel, tmd=tmd)
    return pl.pallas_call(
        kern,
        out_shape=jax.ShapeDtypeStruct(xs.shape, xs.dtype),
        grid=(m // tmd,),
        in_specs=[
            pl.BlockSpec((1, 1, TOP_K * tmd), lambda i: (i, 0, 0), memory_space=pltpu.SMEM),
            pl.BlockSpec((tmd, d), lambda i: (i, 0)),
            pl.BlockSpec(memory_space=pl.ANY),
        ],
        out_specs=pl.BlockSpec(memory_space=pl.ANY),
        scratch_shapes=[pltpu.SemaphoreType.DMA((1,))],
        input_output_aliases={2: 0},
        compiler_params=_cparams(("arbitrary",)),
        name=name,
    )(pos.reshape(m // tmd, 1, TOP_K * tmd), hn, xs)


def _moe_ffn_kernel(te_ref, nused_ref, x_ref, wg_ref, wu_ref, wd_ref, o_ref, xb_ref, acc_ref):
    t = pl.program_id(0)
    j = pl.program_id(1)
    last = pl.num_programs(1) - 1
    used = t < nused_ref[0]

    @pl.when(used)
    def _():
        @pl.when(j == 0)
        def _():
            xb_ref[...] = x_ref[...].astype(BF16)
            acc_ref[...] = jnp.zeros_like(acc_ref)

        xb = xb_ref[...]
        gate = jnp.dot(xb, wg_ref[...], preferred_element_type=F32)
        up = jnp.dot(xb, wu_ref[...], preferred_element_type=F32)
        act = (jax.nn.silu(gate) * up).astype(BF16)
        acc_ref[...] += jnp.dot(act, wd_ref[...], preferred_element_type=F32)

        @pl.when(j == last)
        def _():
            o_ref[...] = acc_ref[...]

    @pl.when(jnp.logical_and(jnp.logical_not(used), j == last))
    def _():
        o_ref[...] = jnp.zeros_like(o_ref)


def _moe_ffn(xs, te, n_used, w_gu, w_d):
    n_rows, d = xs.shape
    hid = w_d.shape[1]
    tm, th = MOE_TM, MOE_TH
    nj = hid // th
    nt = n_rows // tm
    assert hid % th == 0 and n_rows % tm == 0

    def jj(t, j, nu):
        return jnp.where(t < nu[0], j, nj - 1)

    grid_spec = pltpu.PrefetchScalarGridSpec(
        num_scalar_prefetch=2,
        grid=(nt, nj),
        in_specs=[
            pl.BlockSpec((tm, d), lambda t, j, te, nu: (jnp.maximum(jnp.minimum(t, nu[0] - 1), 0), 0)),
            pl.BlockSpec((None, d, th), lambda t, j, te, nu: (te[t], 0, jj(t, j, nu))),
            pl.BlockSpec((None, d, th), lambda t, j, te, nu: (te[t], 0, jj(t, j, nu) + nj)),
            pl.BlockSpec((None, th, d), lambda t, j, te, nu: (te[t], jj(t, j, nu), 0)),
        ],
        out_specs=pl.BlockSpec((tm, d), lambda t, j, te, nu: (t, 0)),
        scratch_shapes=[pltpu.VMEM((tm, d), BF16), pltpu.VMEM((tm, d), F32)],
    )
    return pl.pallas_call(
        _moe_ffn_kernel,
        out_shape=jax.ShapeDtypeStruct((n_rows, d), F32),
        grid_spec=grid_spec,
        compiler_params=_cparams(("arbitrary", "arbitrary")),
        name="moe_experts",
    )(te, n_used, xs, w_gu, w_gu, w_d)


def _combine_kernel(pos_ref, x_ref, gate_ref, ys_ref, o_ref, buf_ref, sem, *, tmc):
    def start(r, _):
        for k in range(TOP_K):
            pltpu.make_async_copy(ys_ref.at[pl.ds(pos_ref[0, 0, TOP_K * r + k], 1)],
                                  buf_ref.at[k, pl.ds(r, 1)], sem.at[0]).start()
        return 0

    lax.fori_loop(0, tmc, start, 0, unroll=8)
    pltpu.make_async_copy(ys_ref.at[pl.ds(0, TOP_K * tmc)], ys_ref.at[pl.ds(0, TOP_K * tmc)], sem.at[0]).wait()
    g = gate_ref[...]
    acc = x_ref[...]
    for k in range(TOP_K):
        acc = acc + g[:, k:k + 1] * buf_ref[k]
    o_ref[...] = acc


def _combine(x, gate, pos, ys, *, name):
    m, d = x.shape
    ne = gate.shape[1]
    tmc = min(m, 256)
    assert m % tmc == 0
    kern = functools.partial(_combine_kernel, tmc=tmc)
    return pl.pallas_call(
        kern,
        out_shape=jax.ShapeDtypeStruct((m, d), F32),
        grid=(m // tmc,),
        in_specs=[
            pl.BlockSpec((1, 1, TOP_K * tmc), lambda i: (i, 0, 0), memory_space=pltpu.SMEM),
            pl.BlockSpec((tmc, d), lambda i: (i, 0)),
            pl.BlockSpec((tmc, ne), lambda i: (i, 0)),
            pl.BlockSpec(memory_space=pl.ANY),
        ],
        out_specs=pl.BlockSpec((tmc, d), lambda i: (i, 0)),
        scratch_shapes=[pltpu.VMEM((TOP_K, tmc, d), F32), pltpu.SemaphoreType.DMA((1,))],
        compiler_params=_cparams(("arbitrary",)),
        name=name,
    )(pos.reshape(m // tmc, 1, TOP_K * tmc), x, gate, ys)


def _moe_plan(counts, eids, ranks, tm, nt):
    padded = ((counts + tm - 1) // tm) * tm
    ends = jnp.cumsum(padded)
    starts = ends - padded

    def positions(eid, rank):
        start = jnp.zeros_like(rank)
        for e in range(N_EXPERTS):
            start = start + jnp.where(eid == e, starts[e], 0)
        return start + rank

    pos = [positions(e[:, :TOP_K], r[:, :TOP_K]) for e, r in zip(eids, ranks)]
    n_used = (ends[-1] // tm).astype(I32)
    tile = jnp.arange(nt, dtype=I32)
    first_row = jnp.minimum(tile, n_used - 1) * tm
    te = jnp.minimum(jnp.sum((first_row[:, None] >= ends[None, :]).astype(I32), axis=1), N_EXPERTS - 1)
    return pos, te, n_used.reshape(1)


def _block_diag(w):
    n, a, b = w.shape
    return jnp.einsum("nij,nm->nimj", w, jnp.eye(n, dtype=w.dtype)).reshape(n * a, n * b)


def kernel(x_prompt, x_sample, state_conv, state_lru, state_s5_re, state_s5_im, cache_swa0_kv, cache_swa1_kv, cache_swa2_kv, cache_mem_k, cache_mem_v, mem_prompt, norm_mix, norm_xa, norm_ffn, norm_mem, w_in_even, conv_w, conv_b, lru_wa, lru_ba, lru_wx, lru_bx, lru_lam, s5_lam_re, s5_lam_im, s5_log_dt, s5_b_re, s5_b_im, s5_c_re, s5_c_im, s5_d, s5_w_glu, s5_b_glu, w_out_even, w_qkv_odd, q_norm_odd, k_norm_odd, w_o_odd, rel_bias, xa_wq, xa_wkv, xa_qn, xa_kn, xa_wo, ffn_w_gu, ffn_w_down, moe_router_w, moe_router_b, moe_w_gu, moe_w_down):
    bp, s_len, d = x_prompt.shape
    db, t_dec, _ = x_sample.shape
    n_mem = mem_prompt.shape[1]
    tp = SAMPLE_PAD_T
    d_lru = conv_w.shape[-1]
    s5_g, s5_p, s5_h = s5_b_re.shape[1:]
    n_state = s5_g * s5_p
    d_s5 = s5_g * s5_h
    xa_dh = d // XA_HEADS
    caches = (cache_swa0_kv, cache_swa1_kv, cache_swa2_kv)
    bf = lambda w: w.astype(BF16)

    yp = x_prompt
    ys = jnp.pad(x_sample, ((0, 0), (0, tp - t_dec), (0, 0)))

    w_in = bf(w_in_even[0])
    bbr, bbi, apow = _s5_prep(s5_lam_re[0], s5_lam_im[0], s5_log_dt[0], s5_b_re[0], s5_b_im[0])
    eye_g = jnp.eye(s5_g, dtype=F32)
    mix_w = dict(
        cw=conv_w[0], cb=conv_b[0].reshape(1, d_lru),
        wa=bf(_block_diag(lru_wa[0])), ba=lru_ba[0].reshape(1, d_lru),
        wx=bf(_block_diag(lru_wx[0])), bx=lru_bx[0].reshape(1, d_lru),
        lam=lru_lam[0].reshape(1, d_lru),
        bbr=bbr, bbi=bbi, apow=apow,
        ccr=bf(jnp.einsum("ghp,gk->gpkh", s5_c_re[0], eye_g).reshape(n_state, d_s5)),
        cci=bf(jnp.einsum("ghp,gk->gpkh", s5_c_im[0], eye_g).reshape(n_state, d_s5)),
        d=s5_d[0].reshape(1, d_s5), wglu=bf(s5_w_glu[0]), bglu=s5_b_glu[0].reshape(1, d_s5),
        wout=bf(w_out_even[0]),
    )
    z_p = _norm_matmul(yp.reshape(bp * s_len, d), norm_mix[0], w_in, name="in_proj_p").reshape(bp, s_len, -1)
    yp, p_lru, p_s5r, p_s5i = _mixer0(
        z_p, yp, jnp.zeros((bp, V7X_SUBLANES, d_lru), F32), jnp.zeros((bp, 1, d_lru), F32),
        jnp.zeros((bp, 1, n_state), F32), jnp.zeros((bp, 1, n_state), F32), mix_w,
        tc=MIX_TC, last_row=MIX_TC - 1)
    z_s = _norm_matmul(ys.reshape(db * tp, d), norm_mix[0], w_in, name="in_proj_s").reshape(db, tp, -1)
    conv_init = jnp.pad(state_conv[0], ((0, 0), (V7X_SUBLANES - (CONV_W - 1), 0), (0, 0)))
    ys, s_lru, s_s5r, s_s5i = _mixer0(
        z_s, ys, conv_init, state_lru[0].reshape(db, 1, d_lru),
        state_s5_re[0].reshape(db, 1, n_state), state_s5_im[0].reshape(db, 1, n_state), mix_w,
        tc=tp, last_row=t_dec - 1)
    p_state_conv = z_p[:, s_len - (CONV_W - 1):, :d_lru][None]
    s_state_conv = z_s[:, t_dec - (CONV_W - 1):t_dec, :d_lru][None]

    p_mk, p_mv = [], []

    def cross_attention(layer, yp, ys):
        kn_gain = jnp.concatenate([jnp.tile(xa_kn[layer], XA_HEADS), jnp.ones((d,), F32)]).reshape(1, 2 * d)
        kv = _norm_matmul(mem_prompt.reshape(bp * n_mem, d), norm_mem[layer], bf(xa_wkv[layer]),
                          head_gain=kn_gain, n_norm_cols=d, dh=xa_dh, name=f"mem_kv{layer}")
        kv = kv.reshape(bp, n_mem, 2 * d)
        mk, mv = kv[:, :, :d], kv[:, :, d:]
        p_mk.append(mk.reshape(bp, n_mem, XA_HEADS, xa_dh))
        p_mv.append(mv.reshape(bp, n_mem, XA_HEADS, xa_dh))
        wq, wo = bf(xa_wq[layer]), bf(xa_wo[layer])
        yp = _xattn(yp, norm_xa[layer], wq, xa_qn[layer], mk, mv, wo, tm=256, name=f"xattn_p{layer}")
        ys = _xattn(ys, norm_xa[layer], wq, xa_qn[layer], cache_mem_k, cache_mem_v, wo,
                    tm=tp, name=f"xattn_s{layer}", layer=layer)
        return yp, ys

    yp, ys = cross_attention(0, yp, ys)
    w_gu0, w_d0 = bf(ffn_w_gu[0]), bf(ffn_w_down[0])
    yp = _ffn(yp.reshape(bp * s_len, d), norm_ffn[0], w_gu0, w_d0, name="ffn_p").reshape(bp, s_len, d)
    ys = _ffn(ys.reshape(db * tp, d), norm_ffn[0], w_gu0, w_d0, name="ffn_s").reshape(db, tp, d)

    n_heads = len(WINDOWS) * HG
    d_c = n_heads * DH
    gw = HG * DH
    w_qkv = bf(w_qkv_odd[0])
    qk_gain = jnp.concatenate([jnp.tile(q_norm_odd[0], n_heads), jnp.tile(k_norm_odd[0], n_heads),
                               jnp.ones((d_c,), F32)]).reshape(1, 3 * d_c)
    qkv_p = _norm_matmul(yp.reshape(bp * s_len, d), norm_mix[1], w_qkv, head_gain=qk_gain,
                         n_norm_cols=2 * d_c, dh=DH, slabs=True, name="qkv_p")
    qkv_s = _norm_matmul(ys.reshape(db * tp, d), norm_mix[1], w_qkv, head_gain=qk_gain,
                         n_norm_cols=2 * d_c, dh=DH, slabs=True, name="qkv_s")
    gs = gw // V7X_LANES
    w_o = bf(w_o_odd[0])
    outs_p, lses_p, outs_s, lses_s, p_swa, s_swa = [], [], [], [], [], []

    def kv_rows(qkv, n_batch, t_len, g, lo, hi):
        ks = (len(WINDOWS) + g) * gs
        vs = (2 * len(WINDOWS) + g) * gs
        slabs = qkv.reshape(-1, n_batch, t_len, V7X_LANES)
        kv = jnp.stack([slabs[ks:ks + gs, :, lo:hi], slabs[vs:vs + gs, :, lo:hi]])
        return jnp.transpose(kv, (2, 3, 0, 1, 4)).reshape(n_batch, hi - lo, 2, HG, DH)

    for g, dil in enumerate(DILATIONS):
        tab = rel_bias[:, g * HG:(g + 1) * HG]
        o, l = _dilated_prompt(qkv_p, tab, g, dil, bp)
        outs_p.append(o)
        lses_p.append(l)
        cache_t = jnp.transpose(caches[g][0], (0, 2, 3, 4, 1))
        o, l, cache_t = _dilated_sample(qkv_s, cache_t, tab, g, dil, t_dec)
        outs_s.append(o)
        lses_s.append(l)
        s_swa.append(jnp.transpose(cache_t, (0, 4, 1, 2, 3))[None])
        win = min(WINDOWS[g], s_len)
        p_swa.append(kv_rows(qkv_p, bp, s_len, g, s_len - win, s_len)[None])
    yp = _merge_wo(outs_p, lses_p, yp.reshape(bp * s_len, d), w_o, name="merge_wo_p").reshape(bp, s_len, d)
    ys = _merge_wo(outs_s, lses_s, ys.reshape(db * tp, d), w_o, name="merge_wo_s").reshape(db, tp, d)

    yp, ys = cross_attention(1, yp, ys)

    yp2 = yp.reshape(bp * s_len, d)
    ys2 = ys[:, :t_dec].reshape(db * t_dec, d)
    zero_cnt = jnp.zeros((1, N_EXPERTS), F32)
    hn_p, eid_p, gate_p, rank_p, cnt_p = _router(yp2, norm_ffn[1], moe_router_w[0], moe_router_b[0], zero_cnt,
                                                 name="router_p")
    hn_s, eid_s, gate_s, rank_s, cnt_s = _router(ys2, norm_ffn[1], moe_router_w[0], moe_router_b[0], cnt_p,
                                                 name="router_s")
    n_tok = yp2.shape[0] + ys2.shape[0]
    nt = -(-(n_tok * TOP_K) // MOE_TM) + N_EXPERTS
    (pos_p, pos_s), te, n_used = _moe_plan(cnt_s[0].astype(I32), (eid_p, eid_s), (rank_p, rank_s), MOE_TM, nt)
    xs = jnp.zeros((nt * MOE_TM, d), F32)
    xs = _dispatch(hn_p, pos_p, xs, name="dispatch_p")
    xs = _dispatch(hn_s, pos_s, xs, name="dispatch_s")
    ysort = _moe_ffn(xs, te, n_used, bf(moe_w_gu[0]), bf(moe_w_down[0]))
    yp = _combine(yp2, gate_p, pos_p, ysort, name="moe_combine_p").reshape(bp, s_len, d)
    ys = _combine(ys2, gate_s, pos_s, ysort, name="moe_combine_s").reshape(db, t_dec, d)

    return (yp, ys,
            p_state_conv, p_lru.reshape(1, bp, d_lru),
            p_s5r.reshape(1, bp, s5_g, s5_p), p_s5i.reshape(1, bp, s5_g, s5_p),
            p_swa[0], p_swa[1], p_swa[2], jnp.stack(p_mk), jnp.stack(p_mv),
            s_state_conv, s_lru.reshape(1, db, d_lru),
            s_s5r.reshape(1, db, s5_g, s5_p), s_s5i.reshape(1, db, s5_g, s5_p),
            s_swa[0], s_swa[1], s_swa[2])
```

```python
import functools
import math

import jax
import jax.numpy as jnp
import numpy as np
from jax import lax
from jax.experimental import pallas as pl
from jax.experimental.pallas import tpu as pltpu

F32 = jnp.float32
BF16 = jnp.bfloat16
I32 = jnp.int32

EPS = 1e-6
NEG = -1e30

V7X_SUBLANES = 8
V7X_LANES = 128
V7X_VMEM_BYTES = 64 * 1024 * 1024
VMEM_LIMIT = V7X_VMEM_BYTES - 8 * 1024 * 1024

LRU_C = 8.0
CONV_W = 4
WINDOWS = (128, 512, 2048)
DILATIONS = (1, 4, 16)
HG = 8
DH = 64
N_STRIDE = 128
QB = 128
REL_BUCKETS = 32
REL_MAX_DIST = WINDOWS[-1]
XA_HEADS = 4
N_EXPERTS = 8
TOP_K = 2
SAMPLE_PAD_T = 16

MIX_TC = 256
MOE_TM = 512
MOE_TH = 512
FFN_TH = 512


def _cparams(sem):
    return pltpu.CompilerParams(dimension_semantics=sem, vmem_limit_bytes=VMEM_LIMIT)


def _const_spec(shape):
    nd = len(shape)
    return pl.BlockSpec(shape, lambda *_: (0,) * nd)


def _rms(x, g):
    return x * lax.rsqrt(jnp.mean(x * x, axis=-1, keepdims=True) + EPS) * g


def _norm_matmul_kernel(x_ref, g_ref, w_ref, hg_ref, hs_ref, o_ref, hn_ref, *, n_norm_tiles, dh, slabs):
    j = pl.program_id(1)

    @pl.when(j == 0)
    def _():
        hn_ref[...] = _rms(x_ref[...], g_ref[...]).astype(BF16)

    def emit(val):
        if slabs:
            for c in range(o_ref.shape[0]):
                o_ref[c] = val[:, c * V7X_LANES:(c + 1) * V7X_LANES]
        else:
            o_ref[...] = val

    y = jnp.dot(hn_ref[...], w_ref[...], preferred_element_type=F32)
    if n_norm_tiles == 0:
        emit(y)
    else:
        @pl.when(j < n_norm_tiles)
        def _():
            ssq = jnp.dot((y * y).astype(BF16), hs_ref[...], preferred_element_type=F32)
            emit(y * lax.rsqrt(ssq * (1.0 / dh) + EPS) * hg_ref[...])

        @pl.when(j >= n_norm_tiles)
        def _():
            emit(y)


def _head_sum_matrix(tn, dh):
    idx = np.arange(tn) // dh
    return jnp.asarray((idx[:, None] == idx[None, :]).astype(np.float32), dtype=BF16)


def _norm_matmul(x, g, w, *, tn=512, head_gain=None, n_norm_cols=0, dh=1, slabs=False, name):
    m, d = x.shape
    n = w.shape[1]
    tm = min(m, 1024)
    assert m % tm == 0 and n % tn == 0 and n_norm_cols % tn == 0
    if head_gain is None:
        head_gain = jnp.ones((1, n), F32)
    hs = _head_sum_matrix(tn, dh)
    kern = functools.partial(_norm_matmul_kernel, n_norm_tiles=n_norm_cols // tn, dh=dh, slabs=slabs)
    if slabs:
        out_shape = jax.ShapeDtypeStruct((n // V7X_LANES, m, V7X_LANES), F32)
        out_spec = pl.BlockSpec((tn // V7X_LANES, tm, V7X_LANES), lambda i, j: (j, i, 0))
    else:
        out_shape = jax.ShapeDtypeStruct((m, n), F32)
        out_spec = pl.BlockSpec((tm, tn), lambda i, j: (i, j))
    return pl.pallas_call(
        kern,
        out_shape=out_shape,
        grid=(m // tm, n // tn),
        in_specs=[
            pl.BlockSpec((tm, d), lambda i, j: (i, 0)),
            pl.BlockSpec((1, d), lambda i, j: (0, 0)),
            pl.BlockSpec((d, tn), lambda i, j: (0, j)),
            pl.BlockSpec((1, tn), lambda i, j: (0, j)),
            pl.BlockSpec((tn, tn), lambda i, j: (0, 0)),
        ],
        out_specs=out_spec,
        scratch_shapes=[pltpu.VMEM((tm, d), BF16)],
        compiler_params=_cparams(("parallel", "arbitrary")),
        name=name,
    )(x, g.reshape(1, d), w, head_gain, hs)


def _s5_prep_kernel(lre_ref, lim_ref, ldt_ref, bre_ref, bim_ref, bbr_ref, bbi_ref, apow_ref):
    lr = lre_ref[...]
    li = lim_ref[...]
    dt = jnp.exp(ldt_ref[...])
    mag = jnp.exp(lr * dt)
    ab_r = mag * jnp.cos(li * dt)
    ab_i = mag * jnp.sin(li * dt)
    den = lr * lr + li * li
    nr = ab_r - 1.0
    cr = (nr * lr + ab_i * li) / den
    ci = (ab_i * lr - nr * li) / den
    b_r = bre_ref[...]
    b_i = bim_ref[...]
    bbr_ref[...] = (cr * b_r - ci * b_i).astype(BF16)
    bbi_ref[...] = (cr * b_i + ci * b_r).astype(BF16)

    n = lr.shape[1]
    row = lax.broadcasted_iota(I32, (V7X_SUBLANES, n), 0)

    def power(kf):
        mg = jnp.exp(kf * (lr * dt))
        return mg * jnp.cos(kf * (li * dt)), mg * jnp.sin(kf * (li * dt))

    for i, s in enumerate((1, 2, 4)):
        pr, pi = power(jnp.full((V7X_SUBLANES, n), s, F32))
        keep = row >= s
        apow_ref[2 * i] = jnp.where(keep, pr, 0.0)
        apow_ref[2 * i + 1] = jnp.where(keep, pi, 0.0)
    pr, pi = power((row + 1).astype(F32))
    apow_ref[6] = pr
    apow_ref[7] = pi


def _s5_prep(lam_re, lam_im, log_dt, b_re, b_im):
    g, p, h = b_re.shape
    n = g * p
    eye = jnp.eye(g, dtype=F32)
    bre_bd = jnp.einsum("gph,gk->khgp", b_re, eye).reshape(g * h, n)
    bim_bd = jnp.einsum("gph,gk->khgp", b_im, eye).reshape(g * h, n)
    ldt = jnp.broadcast_to(log_dt[:, None], (g, p)).reshape(1, n)
    return pl.pallas_call(
        _s5_prep_kernel,
        out_shape=(
            jax.ShapeDtypeStruct((g * h, n), BF16),
            jax.ShapeDtypeStruct((g * h, n), BF16),
            jax.ShapeDtypeStruct((8, V7X_SUBLANES, n), F32),
        ),
        compiler_params=pltpu.CompilerParams(vmem_limit_bytes=VMEM_LIMIT),
        name="s5_prep",
    )(lam_re.reshape(1, n), lam_im.reshape(1, n), ldt, bre_bd, bim_bd)


def _mixer0_kernel(z_ref, x_ref, convi_ref, h0_ref, sr0_ref, si0_ref,
                   cw_ref, cb_ref, wa_ref, ba_ref, wx_ref, bx_ref, lam_ref,
                   bbr_ref, bbi_ref, apow_ref, ccr_ref, cci_ref, d_ref, wglu_ref, bglu_ref, wout_ref,
                   y_ref, hl_ref, srl_ref, sil_ref,
                   ext_ref, hc_ref, src_ref, sic_ref, xr_ref, xi_ref, ha_ref, hb_ref,
                   *, tc, last_row):
    c = pl.program_id(1)
    d_lru = cw_ref.shape[1]
    d_s5 = d_ref.shape[1]
    n_tiles = tc // V7X_SUBLANES

    @pl.when(c == 0)
    def _():
        ext_ref[...] = convi_ref[...]
        hc_ref[...] = h0_ref[...]
        src_ref[...] = sr0_ref[...]
        sic_ref[...] = si0_ref[...]

    z = z_ref[...]
    xa = z[:, :d_lru]
    ga = z[:, d_lru:2 * d_lru]
    u = z[:, 2 * d_lru:]

    ext = jnp.concatenate([ext_ref[...], xa], axis=0)
    xc = cb_ref[...] + xa * cw_ref[CONV_W - 1:CONV_W, :]
    for s in range(1, CONV_W):
        xc = xc + pltpu.roll(ext, s, 0)[V7X_SUBLANES:, :] * cw_ref[CONV_W - 1 - s:CONV_W - s, :]
    ext_ref[...] = xa[tc - V7X_SUBLANES:, :]

    xcb = xc.astype(BF16)
    r = jax.nn.sigmoid(jnp.dot(xcb, wa_ref[...], preferred_element_type=F32) + ba_ref[...])
    ig = jax.nn.sigmoid(jnp.dot(xcb, wx_ref[...], preferred_element_type=F32) + bx_ref[...])
    lam = lam_ref[...]
    softplus_neg = jnp.maximum(-lam, 0.0) + jnp.log1p(jnp.exp(-jnp.abs(lam)))
    log_a = -LRU_C * r * softplus_neg
    a = jnp.exp(log_a)
    bt = jnp.sqrt(-jnp.tanh(log_a) * (a * a + 1.0)) * ig * xc

    a3 = a.reshape(n_tiles, V7X_SUBLANES, d_lru)
    b3 = bt.reshape(n_tiles, V7X_SUBLANES, d_lru)
    row = lax.broadcasted_iota(I32, (1, V7X_SUBLANES, d_lru), 1)
    for s in (1, 2, 4):
        keep = row >= s
        ar = pltpu.roll(a3, s, 1)
        br = pltpu.roll(b3, s, 1)
        b3 = jnp.where(keep, a3 * br + b3, b3)
        a3 = jnp.where(keep, a3 * ar, a3)
    ha_ref[...] = a3.reshape(tc, d_lru)
    hb_ref[...] = b3.reshape(tc, d_lru)

    def lru_tile(i, carry):
        r0 = pl.multiple_of(i * V7X_SUBLANES, V7X_SUBLANES)
        h = ha_ref[pl.ds(r0, V7X_SUBLANES), :] * carry + hb_ref[pl.ds(r0, V7X_SUBLANES), :]
        hb_ref[pl.ds(r0, V7X_SUBLANES), :] = h
        return h[V7X_SUBLANES - 1:, :]

    hc_ref[...] = lax.fori_loop(0, n_tiles, lru_tile, hc_ref[...])
    hs = hb_ref[...]
    hl_ref[...] = hb_ref[last_row:last_row + 1, :]
    ya = hs * jax.nn.gelu(ga)

    ub = u.astype(BF16)
    n_state = bbr_ref.shape[1]
    n_slab = d_s5 // V7X_LANES
    sw = n_state // n_slab

    def input_dot(w_ref):
        return jnp.concatenate(
            [jnp.dot(ub[:, c * V7X_LANES:(c + 1) * V7X_LANES],
                     w_ref[c * V7X_LANES:(c + 1) * V7X_LANES, c * sw:(c + 1) * sw],
                     preferred_element_type=F32) for c in range(n_slab)], axis=-1)

    xr3 = input_dot(bbr_ref).reshape(n_tiles, V7X_SUBLANES, n_state)
    xi3 = input_dot(bbi_ref).reshape(n_tiles, V7X_SUBLANES, n_state)
    for i, s in enumerate((1, 2, 4)):
        cr = apow_ref[2 * i][None]
        ci = apow_ref[2 * i + 1][None]
        rr = pltpu.roll(xr3, s, 1)
        ri = pltpu.roll(xi3, s, 1)
        xr3, xi3 = xr3 + cr * rr - ci * ri, xi3 + cr * ri + ci * rr
    xr_ref[...] = xr3.reshape(tc, n_state)
    xi_ref[...] = xi3.reshape(tc, n_state)

    def s5_tile(i, carry):
        cr_, ci_ = carry
        r0 = pl.multiple_of(i * V7X_SUBLANES, V7X_SUBLANES)
        pr = apow_ref[6]
        pi = apow_ref[7]
        nr_ = xr_ref[pl.ds(r0, V7X_SUBLANES), :] + pr * cr_ - pi * ci_
        ni_ = xi_ref[pl.ds(r0, V7X_SUBLANES), :] + pr * ci_ + pi * cr_
        xr_ref[pl.ds(r0, V7X_SUBLANES), :] = nr_
        xi_ref[pl.ds(r0, V7X_SUBLANES), :] = ni_
        return nr_[V7X_SUBLANES - 1:, :], ni_[V7X_SUBLANES - 1:, :]

    cr_f, ci_f = lax.fori_loop(0, n_tiles, s5_tile, (src_ref[...], sic_ref[...]))
    src_ref[...] = cr_f
    sic_ref[...] = ci_f
    srl_ref[...] = xr_ref[last_row:last_row + 1, :]
    sil_ref[...] = xi_ref[last_row:last_row + 1, :]

    def output_dot(x_ref_, w_ref):
        return jnp.concatenate(
            [jnp.dot(x_ref_[:, c * sw:(c + 1) * sw].astype(BF16),
                     w_ref[c * sw:(c + 1) * sw, c * V7X_LANES:(c + 1) * V7X_LANES],
                     preferred_element_type=F32) for c in range(n_slab)], axis=-1)

    ys = output_dot(xr_ref, ccr_ref) - output_dot(xi_ref, cci_ref)
    ys = ys + d_ref[...] * u
    gs = jax.nn.gelu(ys)
    yb = gs * jax.nn.sigmoid(jnp.dot(gs.astype(BF16), wglu_ref[...], preferred_element_type=F32)
                             + bglu_ref[...])

    y_ref[...] = (x_ref[...]
                  + jnp.dot(ya.astype(BF16), wout_ref[:d_lru, :], preferred_element_type=F32)
                  + jnp.dot(yb.astype(BF16), wout_ref[d_lru:, :], preferred_element_type=F32))


def _mixer0(z, x, conv_init, h0, sr0, si0, wts, *, tc, last_row):
    bn, t_len, d_in = z.shape
    d = x.shape[2]
    d_lru = wts["cw"].shape[1]
    n_state = wts["bbr"].shape[1]
    assert t_len % tc == 0
    kern = functools.partial(_mixer0_kernel, tc=tc, last_row=last_row)
    wnames = ("cw", "cb", "wa", "ba", "wx", "bx", "lam", "bbr", "bbi", "apow",
              "ccr", "cci", "d", "wglu", "bglu", "wout")
    wlist = [wts[k] for k in wnames]
    per_b = lambda shape: pl.BlockSpec((None,) + shape, lambda b, c: (b,) + (0,) * len(shape))
    in_specs = [
        pl.BlockSpec((None, tc, d_in), lambda b, c: (b, c, 0)),
        pl.BlockSpec((None, tc, d), lambda b, c: (b, c, 0)),
        per_b((V7X_SUBLANES, d_lru)), per_b((1, d_lru)), per_b((1, n_state)), per_b((1, n_state)),
    ] + [_const_spec(w.shape) for w in wlist]
    return pl.pallas_call(
        kern,
        out_shape=(
            jax.ShapeDtypeStruct((bn, t_len, d), F32),
            jax.ShapeDtypeStruct((bn, 1, d_lru), F32),
            jax.ShapeDtypeStruct((bn, 1, n_state), F32),
            jax.ShapeDtypeStruct((bn, 1, n_state), F32),
        ),
        grid=(bn, t_len // tc),
        in_specs=in_specs,
        out_specs=(
            pl.BlockSpec((None, tc, d), lambda b, c: (b, c, 0)),
            per_b((1, d_lru)), per_b((1, n_state)), per_b((1, n_state)),
        ),
        scratch_shapes=[
            pltpu.VMEM((V7X_SUBLANES, d_lru), F32),
            pltpu.VMEM((1, d_lru), F32),
            pltpu.VMEM((1, n_state), F32),
            pltpu.VMEM((1, n_state), F32),
            pltpu.VMEM((tc, n_state), F32),
            pltpu.VMEM((tc, n_state), F32),
            pltpu.VMEM((tc, d_lru), F32),
            pltpu.VMEM((tc, d_lru), F32),
        ],
        compiler_params=_cparams(("parallel", "arbitrary")),
        name="mixer0",
    )(z, x, conv_init, h0, sr0, si0, *wlist)


def _xattn_kernel(x_ref, g_ref, wq_ref, qg_ref, mk_ref, mv_ref, wo_ref, o_ref):
    nb, tm, d = x_ref.shape
    x = x_ref[...].reshape(nb * tm, d)
    q = jnp.dot(_rms(x, g_ref[...]).astype(BF16), wq_ref[...], preferred_element_type=F32)
    dh = qg_ref.shape[1]
    head_major = len(mk_ref.shape) == 4
    rows = []
    for b in range(nb):
        if head_major:
            kb = pltpu.einshape("nhd->hnd", mk_ref[b])
            vb = pltpu.einshape("nhd->hnd", mv_ref[b])
            head_of = lambda arr, h: arr[h]
        else:
            kb, vb = mk_ref[b], mv_ref[b]
            head_of = lambda arr, h: arr[:, h * dh:(h + 1) * dh]
        outs = []
        for h in range(XA_HEADS):
            qn = _rms(q[b * tm:(b + 1) * tm, h * dh:(h + 1) * dh], qg_ref[...]).astype(BF16)
            s = lax.dot_general(qn, head_of(kb, h).astype(BF16), (((1,), (1,)), ((), ())),
                                preferred_element_type=F32) * (dh ** -0.5)
            m = jnp.max(s, axis=-1, keepdims=True)
            p = jnp.exp(s - m)
            den = jnp.sum(p, axis=-1, keepdims=True)
            oh = jnp.dot(p.astype(BF16), head_of(vb, h).astype(BF16), preferred_element_type=F32) / den
            outs.append(oh.astype(BF16))
        rows.append(jnp.concatenate(outs, axis=-1))
    o = rows[0] if nb == 1 else jnp.concatenate(rows, axis=0)
    o_ref[...] = (x + jnp.dot(o, wo_ref[...], preferred_element_type=F32)).reshape(nb, tm, d)


def _xattn(x, g, wq, qg, mk, mv, wo, *, tm, nb=1, name, layer=None):
    bn, t_len, d = x.shape
    assert t_len % tm == 0 and bn % nb == 0
    if layer is None:
        mem_spec = pl.BlockSpec((nb,) + mk.shape[1:], lambda b, i: (b, 0, 0))
    else:
        mem_spec = pl.BlockSpec((None, nb) + mk.shape[2:], lambda b, i: (layer, b, 0, 0, 0))
    return pl.pallas_call(
        _xattn_kernel,
        out_shape=jax.ShapeDtypeStruct(x.shape, F32),
        grid=(bn // nb, t_len // tm),
        in_specs=[
            pl.BlockSpec((nb, tm, d), lambda b, i: (b, i, 0)),
            _const_spec((1, d)),
            _const_spec(wq.shape),
            _const_spec((1, qg.shape[-1])),
            mem_spec,
            mem_spec,
            _const_spec(wo.shape),
        ],
        out_specs=pl.BlockSpec((nb, tm, d), lambda b, i: (b, i, 0)),
        compiler_params=_cparams(("parallel", "arbitrary")),
        name=name,
    )(x, g.reshape(1, d), wq, qg.reshape(1, -1), mk, mv, wo)


def _ffn_kernel(x_ref, g_ref, wg_ref, wu_ref, wd_ref, o_ref, hn_ref, acc_ref):
    j = pl.program_id(1)

    @pl.when(j == 0)
    def _():
        hn_ref[...] = _rms(x_ref[...], g_ref[...]).astype(BF16)
        acc_ref[...] = jnp.zeros_like(acc_ref)

    hn = hn_ref[...]
    gate = jnp.dot(hn, wg_ref[...], preferred_element_type=F32)
    up = jnp.dot(hn, wu_ref[...], preferred_element_type=F32)
    act = (jax.nn.silu(gate) * up).astype(BF16)
    acc_ref[...] += jnp.dot(act, wd_ref[...], preferred_element_type=F32)

    @pl.when(j == pl.num_programs(1) - 1)
    def _():
        o_ref[...] = x_ref[...] + acc_ref[...]


def _ffn(x, g, w_gu, w_d, *, name):
    m, d = x.shape
    hid = w_d.shape[0]
    th = FFN_TH
    tm = min(m, 1024)
    nj = hid // th
    assert m % tm == 0 and hid % th == 0
    return pl.pallas_call(
        _ffn_kernel,
        out_shape=jax.ShapeDtypeStruct((m, d), F32),
        grid=(m // tm, nj),
        in_specs=[
            pl.BlockSpec((tm, d), lambda i, j: (i, 0)),
            pl.BlockSpec((1, d), lambda i, j: (0, 0)),
            pl.BlockSpec((d, th), lambda i, j: (0, j)),
            pl.BlockSpec((d, th), lambda i, j: (0, j + nj)),
            pl.BlockSpec((th, d), lambda i, j: (j, 0)),
        ],
        out_specs=pl.BlockSpec((tm, d), lambda i, j: (i, 0)),
        scratch_shapes=[pltpu.VMEM((tm, d), BF16), pltpu.VMEM((tm, d), F32)],
        compiler_params=_cparams(("parallel", "arbitrary")),
        name=name,
    )(x, g.reshape(1, d), w_gu, w_gu, w_d)


def _rel_bucket_np(dist):
    dist = np.clip(np.asarray(dist), 0, None)
    max_exact = REL_BUCKETS // 2
    safe = np.maximum(dist, max_exact).astype(np.float32)
    large = max_exact + np.floor(np.log(safe / max_exact) / math.log(REL_MAX_DIST / max_exact)
                                 * (REL_BUCKETS - max_exact)).astype(np.int32)
    large = np.minimum(large, REL_BUCKETS - 1)
    return np.where(dist < max_exact, dist, large).astype(np.int32)


def _dil_kernel(q_ref, k_ref, v_ref, bias_ref, o_ref, l_ref, kprev_ref, vprev_ref, *, d, nblk):
    n = pl.program_id(1)

    @pl.when(n == 0)
    def _():
        kprev_ref[...] = jnp.zeros_like(kprev_ref)
        vprev_ref[...] = jnp.zeros_like(vprev_ref)

    first_sel = jnp.minimum(n, 1)
    lane = lax.broadcasted_iota(I32, (QB, V7X_LANES), 1)
    n_slab = q_ref.shape[0]
    heads_per_slab = V7X_LANES // DH

    def residue(r, _):
        prev_k = [kprev_ref[r, c] for c in range(n_slab)]
        prev_v = [vprev_ref[r, c] for c in range(n_slab)]
        for blk in range(nblk):
            rows = pl.ds(blk * QB * d + r, QB, stride=d)
            bsel = first_sel if blk == 0 else 1
            lse = jnp.zeros((QB, V7X_LANES), F32)
            for c in range(n_slab):
                q = (q_ref[c, rows, :] * (DH ** -0.5)).astype(BF16)
                kc = k_ref[c, rows, :].astype(BF16)
                vc = v_ref[c, rows, :].astype(BF16)
                k = jnp.concatenate([prev_k[c], kc], axis=0)
                v = jnp.concatenate([prev_v[c], vc], axis=0)
                outs = []
                for hh in range(heads_per_slab):
                    h = c * heads_per_slab + hh
                    sl = slice(hh * DH, (hh + 1) * DH)
                    s = lax.dot_general(q[:, sl], k[:, sl], (((1,), (1,)), ((), ())),
                                        preferred_element_type=F32) + bias_ref[bsel, h]
                    m = jnp.max(s, axis=-1, keepdims=True)
                    p = jnp.exp(s - m)
                    den = jnp.sum(p, axis=-1, keepdims=True)
                    outs.append(jnp.dot(p.astype(BF16), v[:, sl], preferred_element_type=F32) / den)
                    lse = jnp.where(lane == h, m + jnp.log(den), lse)
                o_ref[c, rows, :] = jnp.concatenate(outs, axis=-1)
                prev_k[c] = kc
                prev_v[c] = vc
            l_ref[rows, :] = lse
        for c in range(n_slab):
            kprev_ref[r, c] = prev_k[c]
            vprev_ref[r, c] = prev_v[c]
        return 0

    lax.fori_loop(0, d, residue, 0, unroll=min(d, 2))


def _dilated_prompt(qkv, tab, g, d, bn):
    n_slabs, m_rows, _ = qkv.shape
    s_len = m_rows // bn
    gw = HG * DH
    gs = gw // V7X_LANES
    nblk = 2 if d == 1 else 1
    chunk = nblk * QB * d
    n_chunks = s_len // chunk
    assert s_len % chunk == 0
    k_col = n_slabs // gs // 3

    qi = np.arange(QB)[:, None]
    ki = np.arange(2 * QB)[None, :]
    dist = qi + QB - ki
    band = (dist >= 0) & (dist <= N_STRIDE)
    first = band & (ki >= QB)
    onehot = (np.arange(REL_BUCKETS)[:, None] == _rel_bucket_np(dist * d).reshape(1, -1)).astype(np.float32)
    bias = jnp.dot(tab.T.astype(F32), jnp.asarray(onehot),
                   precision=lax.Precision.HIGHEST).reshape(HG, QB, 2 * QB)
    bias2 = jnp.stack([jnp.where(first[None], bias, NEG), jnp.where(band[None], bias, NEG)])

    col = lambda section: (lambda b, n: (section * k_col + g, b * n_chunks + n, 0))
    blk = (gs, chunk, V7X_LANES)
    return pl.pallas_call(
        functools.partial(_dil_kernel, d=d, nblk=nblk),
        out_shape=(jax.ShapeDtypeStruct((gs, m_rows, V7X_LANES), F32),
                   jax.ShapeDtypeStruct((m_rows, V7X_LANES), F32)),
        grid=(bn, n_chunks),
        in_specs=[
            pl.BlockSpec(blk, col(0)), pl.BlockSpec(blk, col(1)), pl.BlockSpec(blk, col(2)),
            _const_spec(bias2.shape),
        ],
        out_specs=(pl.BlockSpec(blk, lambda b, n: (0, b * n_chunks + n, 0)),
                   pl.BlockSpec((chunk, V7X_LANES), lambda b, n: (b * n_chunks + n, 0))),
        scratch_shapes=[pltpu.VMEM((d, gs, QB, V7X_LANES), BF16), pltpu.VMEM((d, gs, QB, V7X_LANES), BF16)],
        compiler_params=_cparams(("parallel", "arbitrary")),
        name=f"dilated_prompt_g{g}",
    )(qkv, qkv, qkv, bias2)


def _split_dot(x, w_ref):
    hi = x.astype(BF16)
    lo = (x - hi.astype(F32)).astype(BF16)
    w = w_ref[...]
    return jnp.dot(hi, w, preferred_element_type=F32) + jnp.dot(lo, w, preferred_element_type=F32)


def _dil_sample_kernel(q_ref, k_ref, v_ref, cache_ref, bias_ref, biasn_ref, o_ref, l_ref, cout_ref, *, t_valid):
    n_slab, t_pad, _ = q_ref.shape
    w = cache_ref.shape[-1]
    unslab = lambda ref: jnp.concatenate([ref[c] for c in range(n_slab)], axis=-1)
    qn = unslab(q_ref)
    kn = unslab(k_ref)
    vn = unslab(v_ref)

    def as_last_columns(x):
        shifted = pltpu.roll(x, t_pad - t_valid, 0)
        tile = jnp.concatenate([jnp.zeros((V7X_LANES - t_pad, x.shape[1]), F32), shifted], axis=0)
        return tile.T

    new_cols = (as_last_columns(kn), as_last_columns(vn))
    lane = lax.broadcasted_iota(I32, (t_pad, V7X_LANES), 1)
    is_new = lax.broadcasted_iota(I32, (DH, V7X_LANES), 1) >= V7X_LANES - t_valid
    qb, kb, vb = qn.astype(BF16), kn.astype(BF16), vn.astype(BF16)
    lse = jnp.zeros((t_pad, V7X_LANES), F32)
    outs = []
    for h in range(HG):
        sl = slice(h * DH, (h + 1) * DH)
        k_t = cache_ref[0, h]
        v_t = cache_ref[1, h]
        s = jnp.dot(qb[:, sl], k_t.astype(BF16), preferred_element_type=F32) * (DH ** -0.5) + bias_ref[h]
        sn = lax.dot_general(qb[:, sl], kb[:, sl], (((1,), (1,)), ((), ())),
                             preferred_element_type=F32) * (DH ** -0.5) + biasn_ref[h]
        m = jnp.maximum(jnp.max(s, axis=-1, keepdims=True), jnp.max(sn, axis=-1, keepdims=True))
        p = jnp.exp(s - m)
        pn = jnp.exp(sn - m)
        den = jnp.sum(p, axis=-1, keepdims=True) + jnp.sum(pn, axis=-1, keepdims=True)
        pv = lax.dot_general(p.astype(BF16), v_t.astype(BF16), (((1,), (1,)), ((), ())),
                             preferred_element_type=F32)
        outs.append((pv + jnp.dot(pn.astype(BF16), vb[:, sl], preferred_element_type=F32)) / den)
        lse = jnp.where(lane == h, m + jnp.log(den), lse)
        for kv, old in enumerate((k_t, v_t)):
            moved = pltpu.roll(old, w - t_valid, 1)
            cout_ref[kv, h] = moved
            cout_ref[kv, h, :, w - V7X_LANES:] = jnp.where(is_new, new_cols[kv][sl, :], moved[:, w - V7X_LANES:])
    o = jnp.concatenate(outs, axis=-1)
    for c in range(n_slab):
        o_ref[c] = o[:, c * V7X_LANES:(c + 1) * V7X_LANES]
    l_ref[...] = lse


def _dilated_sample(qkv, cache, tab, g, d, t_valid):
    n_slabs, m_rows, _ = qkv.shape
    db, wb = cache.shape[0], cache.shape[-1]
    t_pad = m_rows // db
    gw = HG * DH
    gs = gw // V7X_LANES
    assert wb == N_STRIDE * d and t_valid <= V7X_LANES and wb % V7X_LANES == 0
    k_col = n_slabs // gs // 3

    vec = tab[_rel_bucket_np(d * np.arange(N_STRIDE + 1))].astype(F32).T
    spread = jnp.full((HG, N_STRIDE, d), NEG, F32).at[:, :, 0].set(vec[:, N_STRIDE:0:-1]).reshape(HG, wb)
    pos = np.arange(wb)[None, :]
    tq = np.arange(t_pad)[:, None]
    bias = jnp.stack([jnp.where(pos >= t, jnp.roll(spread, t, axis=1), NEG) for t in range(t_pad)],
                     axis=1)
    tn = np.arange(t_pad)[None, :]
    jn = (tq - tn) // d
    ok = (tn <= tq) & ((tq - tn) % d == 0) & (jn <= N_STRIDE) & (tn < t_valid)
    bias_new = jnp.where(ok[None], vec[:, np.clip(jn, 0, N_STRIDE)], NEG)

    kern = functools.partial(_dil_sample_kernel, t_valid=t_valid)
    blk = (gs, t_pad, V7X_LANES)
    cblk = pl.BlockSpec((None,) + cache.shape[1:], lambda b: (b, 0, 0, 0, 0))
    return pl.pallas_call(
        kern,
        out_shape=(jax.ShapeDtypeStruct((gs, m_rows, V7X_LANES), F32),
                   jax.ShapeDtypeStruct((m_rows, V7X_LANES), F32),
                   jax.ShapeDtypeStruct(cache.shape, F32)),
        grid=(db,),
        in_specs=[
            pl.BlockSpec(blk, lambda b: (g, b, 0)),
            pl.BlockSpec(blk, lambda b: (k_col + g, b, 0)),
            pl.BlockSpec(blk, lambda b: (2 * k_col + g, b, 0)),
            cblk, _const_spec(bias.shape), _const_spec(bias_new.shape),
        ],
        out_specs=(pl.BlockSpec(blk, lambda b: (0, b, 0)),
                   pl.BlockSpec((t_pad, V7X_LANES), lambda b: (b, 0)),
                   cblk),
        compiler_params=_cparams(("parallel",)),
        name=f"dilated_sample_g{g}",
    )(qkv, qkv, qkv, cache, bias, bias_new)


def _merge_wo_kernel(o0_ref, o1_ref, o2_ref, l0_ref, l1_ref, l2_ref, x_ref, hexp_ref, w_ref, out_ref):
    l0, l1, l2 = l0_ref[...], l1_ref[...], l2_ref[...]
    m = jnp.maximum(jnp.maximum(l0, l1), l2)
    e0, e1, e2 = jnp.exp(l0 - m), jnp.exp(l1 - m), jnp.exp(l2 - m)
    inv = 1.0 / (e0 + e1 + e2)
    unslab = lambda ref: jnp.concatenate([ref[c] for c in range(ref.shape[0])], axis=-1)
    o = (_split_dot(e0 * inv, hexp_ref) * unslab(o0_ref) + _split_dot(e1 * inv, hexp_ref) * unslab(o1_ref)
         + _split_dot(e2 * inv, hexp_ref) * unslab(o2_ref))
    out_ref[...] = x_ref[...] + jnp.dot(o.astype(BF16), w_ref[...], preferred_element_type=F32)


def _merge_wo(outs, lses, x, w, *, name):
    m, d = x.shape
    gw = w.shape[0]
    tm = min(m, 512)
    assert m % tm == 0
    head = np.arange(gw) // DH
    hexp = jnp.asarray((np.arange(V7X_LANES)[:, None] == head[None, :]).astype(np.float32), dtype=BF16)
    row = lambda width: pl.BlockSpec((tm, width), lambda i: (i, 0))
    slab = pl.BlockSpec((gw // V7X_LANES, tm, V7X_LANES), lambda i: (0, i, 0))
    return pl.pallas_call(
        _merge_wo_kernel,
        out_shape=jax.ShapeDtypeStruct((m, d), F32),
        grid=(m // tm,),
        in_specs=[slab] * 3 + [row(V7X_LANES)] * 3 + [row(d), _const_spec(hexp.shape), _const_spec(w.shape)],
        out_specs=row(d),
        compiler_params=_cparams(("parallel",)),
        name=name,
    )(*outs, *lses, x, hexp, w)


def _router_kernel(x_ref, g_ref, wr_ref, br_ref, cnt0_ref, tri_ref,
                   hn_ref, eid_ref, gate_ref, rank_ref, cnt_ref, run_ref):
    @pl.when(pl.program_id(0) == 0)
    def _():
        run_ref[...] = cnt0_ref[...]

    hn = _rms(x_ref[...], g_ref[...])
    hn_ref[...] = hn
    wr = wr_ref[...]
    h_hi = hn.astype(BF16)
    h_lo = (hn - h_hi.astype(F32)).astype(BF16)
    w_hi = wr.astype(BF16)
    w_lo = (wr - w_hi.astype(F32)).astype(BF16)
    logits = (jnp.dot(h_hi, w_hi, preferred_element_type=F32) + jnp.dot(h_hi, w_lo, preferred_element_type=F32)
              + jnp.dot(h_lo, w_hi, preferred_element_type=F32)) + br_ref[...]
    ne = logits.shape[1]
    lane = lax.broadcasted_iota(I32, logits.shape, 1)
    m1 = jnp.max(logits, axis=-1, keepdims=True)
    i1 = jnp.min(jnp.where(logits == m1, lane, ne), axis=-1, keepdims=True)
    rest = jnp.where(lane == i1, -jnp.inf, logits)
    m2 = jnp.max(rest, axis=-1, keepdims=True)
    i2 = jnp.min(jnp.where(rest == m2, lane, ne), axis=-1, keepdims=True)
    e2 = jnp.exp(m2 - m1)
    g1 = 1.0 / (1.0 + e2)
    g2 = e2 / (1.0 + e2)
    eid_ref[...] = jnp.where(lane == 0, i1, jnp.where(lane == 1, i2, 0))
    gate_ref[...] = jnp.where(lane == 0, g1, jnp.where(lane == 1, g2, 0.0))
    chosen = jnp.logical_or(lane == i1, lane == i2)
    before = jnp.dot(tri_ref[...], jnp.where(chosen, 1.0, 0.0).astype(BF16),
                     preferred_element_type=F32) + run_ref[...]
    r1 = jnp.sum(jnp.where(lane == i1, before, 0.0), axis=-1, keepdims=True)
    r2 = jnp.sum(jnp.where(lane == i2, before, 0.0), axis=-1, keepdims=True)
    rank_ref[...] = jnp.where(lane == 0, r1, jnp.where(lane == 1, r2, 0.0)).astype(I32)
    run_ref[...] += jnp.sum(jnp.where(chosen, 1.0, 0.0), axis=0, keepdims=True)
    cnt_ref[...] = run_ref[...]


def _router(x, g, wr, br, cnt0, *, name):
    m, d = x.shape
    ne = wr.shape[1]
    tm = min(m, 512)
    assert m % tm == 0
    tri = jnp.asarray(np.tril(np.ones((tm, tm), np.float32), -1), dtype=BF16)
    row = lambda width: pl.BlockSpec((tm, width), lambda i: (i, 0))
    return pl.pallas_call(
        _router_kernel,
        out_shape=(jax.ShapeDtypeStruct((m, d), F32),
                   jax.ShapeDtypeStruct((m, ne), I32),
                   jax.ShapeDtypeStruct((m, ne), F32),
                   jax.ShapeDtypeStruct((m, ne), I32),
                   jax.ShapeDtypeStruct((1, ne), F32)),
        grid=(m // tm,),
        in_specs=[row(d), _const_spec((1, d)), _const_spec(wr.shape), _const_spec((1, ne)),
                  _const_spec((1, ne)), _const_spec((tm, tm))],
        out_specs=(row(d), row(ne), row(ne), row(ne), _const_spec((1, ne))),
        scratch_shapes=[pltpu.VMEM((1, ne), F32)],
        compiler_params=_cparams(("arbitrary",)),
        name=name,
    )(x, g.reshape(1, d), wr, br.reshape(1, ne), cnt0, tri)


def _dispatch_kernel(pos_ref, hn_ref, xs_in_ref, xs_ref, sem, *, tmd):
    del xs_in_ref

    def start(r, _):
        for k in range(TOP_K):
            pltpu.make_async_copy(hn_ref.at[pl.ds(r, 1)],
                                  xs_ref.at[pl.ds(pos_ref[0, 0, TOP_K * r + k], 1)], sem.at[0]).start()
        return 0

    lax.fori_loop(0, tmd, start, 0, unroll=8)
    pltpu.make_async_copy(xs_ref.at[pl.ds(0, TOP_K * tmd)], xs_ref.at[pl.ds(0, TOP_K * tmd)], sem.at[0]).wait()


def _dispatch(hn, pos, xs, *, name):
    m, d = hn.shape
    tmd = min(m, 256)
    assert m % tmd == 0
    kern = functools.partial(_dispatch_kernel, tmd=tmd)
    return pl.pallas_call(
        kern,
        out_shape=jax.ShapeDtypeStruct(xs.shape, xs.dtype),
        grid=(m // tmd,),
        in_specs=[
            pl.BlockSpec((1, 1, TOP_K * tmd), lambda i: (i, 0, 0), memory_space=pltpu.SMEM),
            pl.BlockSpec((tmd, d), lambda i: (i, 0)),
            pl.BlockSpec(memory_space=pl.ANY),
        ],
        out_specs=pl.BlockSpec(memory_space=pl.ANY),
        scratch_shapes=[pltpu.SemaphoreType.DMA((1,))],
        input_output_aliases={2: 0},
        compiler_params=_cparams(("arbitrary",)),
        name=name,
    )(pos.reshape(m // tmd, 1, TOP_K * tmd), hn, xs)


def _moe_ffn_kernel(te_ref, nused_ref, x_ref, wg_ref, wu_ref, wd_ref, o_ref, xb_ref, acc_ref):
    t = pl.program_id(0)
    j = pl.program_id(1)
    last = pl.num_programs(1) - 1
    used = t < nused_ref[0]

    @pl.when(used)
    def _():
        @pl.when(j == 0)
        def _():
            xb_ref[...] = x_ref[...].astype(BF16)
            acc_ref[...] = jnp.zeros_like(acc_ref)

        xb = xb_ref[...]
        gate = jnp.dot(xb, wg_ref[...], preferred_element_type=F32)
        up = jnp.dot(xb, wu_ref[...], preferred_element_type=F32)
        act = (jax.nn.silu(gate) * up).astype(BF16)
        acc_ref[...] += jnp.dot(act, wd_ref[...], preferred_element_type=F32)

        @pl.when(j == last)
        def _():
            o_ref[...] = acc_ref[...]

    @pl.when(jnp.logical_and(jnp.logical_not(used), j == last))
    def _():
        o_ref[...] = jnp.zeros_like(o_ref)


def _moe_ffn(xs, te, n_used, w_gu, w_d):
    n_rows, d = xs.shape
    hid = w_d.shape[1]
    tm, th = MOE_TM, MOE_TH
    nj = hid // th
    nt = n_rows // tm
    assert hid % th == 0 and n_rows % tm == 0

    def jj(t, j, nu):
        return jnp.where(t < nu[0], j, nj - 1)

    grid_spec = pltpu.PrefetchScalarGridSpec(
        num_scalar_prefetch=2,
        grid=(nt, nj),
        in_specs=[
            pl.BlockSpec((tm, d), lambda t, j, te, nu: (jnp.maximum(jnp.minimum(t, nu[0] - 1), 0), 0)),
            pl.BlockSpec((None, d, th), lambda t, j, te, nu: (te[t], 0, jj(t, j, nu))),
            pl.BlockSpec((None, d, th), lambda t, j, te, nu: (te[t], 0, jj(t, j, nu) + nj)),
            pl.BlockSpec((None, th, d), lambda t, j, te, nu: (te[t], jj(t, j, nu), 0)),
        ],
        out_specs=pl.BlockSpec((tm, d), lambda t, j, te, nu: (t, 0)),
        scratch_shapes=[pltpu.VMEM((tm, d), BF16), pltpu.VMEM((tm, d), F32)],
    )
    return pl.pallas_call(
        _moe_ffn_kernel,
        out_shape=jax.ShapeDtypeStruct((n_rows, d), F32),
        grid_spec=grid_spec,
        compiler_params=_cparams(("arbitrary", "arbitrary")),
        name="moe_experts",
    )(te, n_used, xs, w_gu, w_gu, w_d)


def _combine_kernel(pos_ref, x_ref, gate_ref, ys_ref, o_ref, buf_ref, sem, *, tmc):
    def start(r, _):
        for k in range(TOP_K):
            pltpu.make_async_copy(ys_ref.at[pl.ds(pos_ref[0, 0, TOP_K * r + k], 1)],
                                  buf_ref.at[k, pl.ds(r, 1)], sem.at[0]).start()
        return 0

    lax.fori_loop(0, tmc, start, 0, unroll=8)
    pltpu.make_async_copy(ys_ref.at[pl.ds(0, TOP_K * tmc)], ys_ref.at[pl.ds(0, TOP_K * tmc)], sem.at[0]).wait()
    g = gate_ref[...]
    acc = x_ref[...]
    for k in range(TOP_K):
        acc = acc + g[:, k:k + 1] * buf_ref[k]
    o_ref[...] = acc


def _combine(x, gate, pos, ys, *, name):
    m, d = x.shape
    ne = gate.shape[1]
    tmc = min(m, 256)
    assert m % tmc == 0
    kern = functools.partial(_combine_kernel, tmc=tmc)
    return pl.pallas_call(
        kern,
        out_shape=jax.ShapeDtypeStruct((m, d), F32),
        grid=(m // tmc,),
        in_specs=[
            pl.BlockSpec((1, 1, TOP_K * tmc), lambda i: (i, 0, 0), memory_space=pltpu.SMEM),
            pl.BlockSpec((tmc, d), lambda i: (i, 0)),
            pl.BlockSpec((tmc, ne), lambda i: (i, 0)),
            pl.BlockSpec(memory_space=pl.ANY),
        ],
        out_specs=pl.BlockSpec((tmc, d), lambda i: (i, 0)),
        scratch_shapes=[pltpu.VMEM((TOP_K, tmc, d), F32), pltpu.SemaphoreType.DMA((1,))],
        compiler_params=_cparams(("arbitrary",)),
        name=name,
    )(pos.reshape(m // tmc, 1, TOP_K * tmc), x, gate, ys)


def _moe_plan(counts, eids, ranks, tm, nt):
    padded = ((counts + tm - 1) // tm) * tm
    ends = jnp.cumsum(padded)
    starts = ends - padded

    def positions(eid, rank):
        start = jnp.zeros_like(rank)
        for e in range(N_EXPERTS):
            start = start + jnp.where(eid == e, starts[e], 0)
        return start + rank

    pos = [positions(e[:, :TOP_K], r[:, :TOP_K]) for e, r in zip(eids, ranks)]
    n_used = (ends[-1] // tm).astype(I32)
    tile = jnp.arange(nt, dtype=I32)
    first_row = jnp.minimum(tile, n_used - 1) * tm
    te = jnp.minimum(jnp.sum((first_row[:, None] >= ends[None, :]).astype(I32), axis=1), N_EXPERTS - 1)
    return pos, te, n_used.reshape(1)


def _block_diag(w):
    n, a, b = w.shape
    return jnp.einsum("nij,nm->nimj", w, jnp.eye(n, dtype=w.dtype)).reshape(n * a, n * b)


def kernel(x_prompt, x_sample, state_conv, state_lru, state_s5_re, state_s5_im, cache_swa0_kv, cache_swa1_kv, cache_swa2_kv, cache_mem_k, cache_mem_v, mem_prompt, norm_mix, norm_xa, norm_ffn, norm_mem, w_in_even, conv_w, conv_b, lru_wa, lru_ba, lru_wx, lru_bx, lru_lam, s5_lam_re, s5_lam_im, s5_log_dt, s5_b_re, s5_b_im, s5_c_re, s5_c_im, s5_d, s5_w_glu, s5_b_glu, w_out_even, w_qkv_odd, q_norm_odd, k_norm_odd, w_o_odd, rel_bias, xa_wq, xa_wkv, xa_qn, xa_kn, xa_wo, ffn_w_gu, ffn_w_down, moe_router_w, moe_router_b, moe_w_gu, moe_w_down):
    bp, s_len, d = x_prompt.shape
    db, t_dec, _ = x_sample.shape
    n_mem = mem_prompt.shape[1]
    tp = SAMPLE_PAD_T
    d_lru = conv_w.shape[-1]
    s5_g, s5_p, s5_h = s5_b_re.shape[1:]
    n_state = s5_g * s5_p
    d_s5 = s5_g * s5_h
    xa_dh = d // XA_HEADS
    caches = (cache_swa0_kv, cache_swa1_kv, cache_swa2_kv)
    bf = lambda w: w.astype(BF16)

    yp = x_prompt
    ys = jnp.pad(x_sample, ((0, 0), (0, tp - t_dec), (0, 0)))

    w_in = bf(w_in_even[0])
    bbr, bbi, apow = _s5_prep(s5_lam_re[0], s5_lam_im[0], s5_log_dt[0], s5_b_re[0], s5_b_im[0])
    eye_g = jnp.eye(s5_g, dtype=F32)
    mix_w = dict(
        cw=conv_w[0], cb=conv_b[0].reshape(1, d_lru),
        wa=bf(_block_diag(lru_wa[0])), ba=lru_ba[0].reshape(1, d_lru),
        wx=bf(_block_diag(lru_wx[0])), bx=lru_bx[0].reshape(1, d_lru),
        lam=lru_lam[0].reshape(1, d_lru),
        bbr=bbr, bbi=bbi, apow=apow,
        ccr=bf(jnp.einsum("ghp,gk->gpkh", s5_c_re[0], eye_g).reshape(n_state, d_s5)),
        cci=bf(jnp.einsum("ghp,gk->gpkh", s5_c_im[0], eye_g).reshape(n_state, d_s5)),
        d=s5_d[0].reshape(1, d_s5), wglu=bf(s5_w_glu[0]), bglu=s5_b_glu[0].reshape(1, d_s5),
        wout=bf(w_out_even[0]),
    )
    z_p = _norm_matmul(yp.reshape(bp * s_len, d), norm_mix[0], w_in, name="in_proj_p").reshape(bp, s_len, -1)
    yp, p_lru, p_s5r, p_s5i = _mixer0(
        z_p, yp, jnp.zeros((bp, V7X_SUBLANES, d_lru), F32), jnp.zeros((bp, 1, d_lru), F32),
        jnp.zeros((bp, 1, n_state), F32), jnp.zeros((bp, 1, n_state), F32), mix_w,
        tc=MIX_TC, last_row=MIX_TC - 1)
    z_s = _norm_matmul(ys.reshape(db * tp, d), norm_mix[0], w_in, name="in_proj_s").reshape(db, tp, -1)
    conv_init = jnp.pad(state_conv[0], ((0, 0), (V7X_SUBLANES - (CONV_W - 1), 0), (0, 0)))
    ys, s_lru, s_s5r, s_s5i = _mixer0(
        z_s, ys, conv_init, state_lru[0].reshape(db, 1, d_lru),
        state_s5_re[0].reshape(db, 1, n_state), state_s5_im[0].reshape(db, 1, n_state), mix_w,
        tc=tp, last_row=t_dec - 1)
    p_state_conv = z_p[:, s_len - (CONV_W - 1):, :d_lru][None]
    s_state_conv = z_s[:, t_dec - (CONV_W - 1):t_dec, :d_lru][None]

    p_mk, p_mv = [], []

    def cross_attention(layer, yp, ys):
        kn_gain = jnp.concatenate([jnp.tile(xa_kn[layer], XA_HEADS), jnp.ones((d,), F32)]).reshape(1, 2 * d)
        kv = _norm_matmul(mem_prompt.reshape(bp * n_mem, d), norm_mem[layer], bf(xa_wkv[layer]),
                          head_gain=kn_gain, n_norm_cols=d, dh=xa_dh, name=f"mem_kv{layer}")
        kv = kv.reshape(bp, n_mem, 2 * d)
        mk, mv = kv[:, :, :d], kv[:, :, d:]
        p_mk.append(mk.reshape(bp, n_mem, XA_HEADS, xa_dh))
        p_mv.append(mv.reshape(bp, n_mem, XA_HEADS, xa_dh))
        wq, wo = bf(xa_wq[layer]), bf(xa_wo[layer])
        yp = _xattn(yp, norm_xa[layer], wq, xa_qn[layer], mk, mv, wo, tm=256, name=f"xattn_p{layer}")
        ys = _xattn(ys, norm_xa[layer], wq, xa_qn[layer], cache_mem_k, cache_mem_v, wo,
                    tm=tp, nb=4, name=f"xattn_s{layer}", layer=layer)
        return yp, ys

    yp, ys = cross_attention(0, yp, ys)
    w_gu0, w_d0 = bf(ffn_w_gu[0]), bf(ffn_w_down[0])
    yp = _ffn(yp.reshape(bp * s_len, d), norm_ffn[0], w_gu0, w_d0, name="ffn_p").reshape(bp, s_len, d)
    ys = _ffn(ys.reshape(db * tp, d), norm_ffn[0], w_gu0, w_d0, name="ffn_s").reshape(db, tp, d)

    n_heads = len(WINDOWS) * HG
    d_c = n_heads * DH
    gw = HG * DH
    w_qkv = bf(w_qkv_odd[0])
    qk_gain = jnp.concatenate([jnp.tile(q_norm_odd[0], n_heads), jnp.tile(k_norm_odd[0], n_heads),
                               jnp.ones((d_c,), F32)]).reshape(1, 3 * d_c)
    qkv_p = _norm_matmul(yp.reshape(bp * s_len, d), norm_mix[1], w_qkv, head_gain=qk_gain,
                         n_norm_cols=2 * d_c, dh=DH, slabs=True, name="qkv_p")
    qkv_s = _norm_matmul(ys.reshape(db * tp, d), norm_mix[1], w_qkv, head_gain=qk_gain,
                         n_norm_cols=2 * d_c, dh=DH, slabs=True, name="qkv_s")
    gs = gw // V7X_LANES
    w_o = bf(w_o_odd[0])
    outs_p, lses_p, outs_s, lses_s, p_swa, s_swa = [], [], [], [], [], []

    def kv_rows(qkv, n_batch, t_len, g, lo, hi):
        ks = (len(WINDOWS) + g) * gs
        vs = (2 * len(WINDOWS) + g) * gs
        slabs = qkv.reshape(-1, n_batch, t_len, V7X_LANES)
        kv = jnp.stack([slabs[ks:ks + gs, :, lo:hi], slabs[vs:vs + gs, :, lo:hi]])
        return jnp.transpose(kv, (2, 3, 0, 1, 4)).reshape(n_batch, hi - lo, 2, HG, DH)

    for g, dil in enumerate(DILATIONS):
        tab = rel_bias[:, g * HG:(g + 1) * HG]
        o, l = _dilated_prompt(qkv_p, tab, g, dil, bp)
        outs_p.append(o)
        lses_p.append(l)
        cache_t = jnp.transpose(caches[g][0], (0, 2, 3, 4, 1))
        o, l, cache_t = _dilated_sample(qkv_s, cache_t, tab, g, dil, t_dec)
        outs_s.append(o)
        lses_s.append(l)
        s_swa.append(jnp.transpose(cache_t, (0, 4, 1, 2, 3))[None])
        win = min(WINDOWS[g], s_len)
        p_swa.append(kv_rows(qkv_p, bp, s_len, g, s_len - win, s_len)[None])
    yp = _merge_wo(outs_p, lses_p, yp.reshape(bp * s_len, d), w_o, name="merge_wo_p").reshape(bp, s_len, d)
    ys = _merge_wo(outs_s, lses_s, ys.reshape(db * tp, d), w_o, name="merge_wo_s").reshape(db, tp, d)

    yp, ys = cross_attention(1, yp, ys)

    yp2 = yp.reshape(bp * s_len, d)
    ys2 = ys[:, :t_dec].reshape(db * t_dec, d)
    zero_cnt = jnp.zeros((1, N_EXPERTS), F32)
    hn_p, eid_p, gate_p, rank_p, cnt_p = _router(yp2, norm_ffn[1], moe_router_w[0], moe_router_b[0], zero_cnt,
                                                 name="router_p")
    hn_s, eid_s, gate_s, rank_s, cnt_s = _router(ys2, norm_ffn[1], moe_router_w[0], moe_router_b[0], cnt_p,
                                                 name="router_s")
    n_tok = yp2.shape[0] + ys2.shape[0]
    nt = -(-(n_tok * TOP_K) // MOE_TM) + N_EXPERTS
    (pos_p, pos_s), te, n_used = _moe_plan(cnt_s[0].astype(I32), (eid_p, eid_s), (rank_p, rank_s), MOE_TM, nt)
    xs = jnp.zeros((nt * MOE_TM, d), F32)
    xs = _dispatch(hn_p, pos_p, xs, name="dispatch_p")
    xs = _dispatch(hn_s, pos_s, xs, name="dispatch_s")
    ysort = _moe_ffn(xs, te, n_used, bf(moe_w_gu[0]), bf(moe_w_down[0]))
    yp = _combine(yp2, gate_p, pos_p, ysort, name="moe_combine_p").reshape(bp, s_len, d)
    ys = _combine(ys2, gate_s, pos_s, ysort, name="moe_combine_s").reshape(db, t_dec, d)

    return (yp, ys,
            p_state_conv, p_lru.reshape(1, bp, d_lru),
            p_s5r.reshape(1, bp, s5_g, s5_p), p_s5i.reshape(1, bp, s5_g, s5_p),
            p_swa[0], p_swa[1], p_swa[2], jnp.stack(p_mk), jnp.stack(p_mv),
            s_state_conv, s_lru.reshape(1, db, d_lru),
            s_s5r.reshape(1, db, s5_g, s5_p), s_s5i.reshape(1, db, s5_g, s5_p),
            s_swa[0], s_swa[1], s_swa[2])
```

```python
import functools
import math

import jax
import jax.numpy as jnp
import numpy as np
from jax import lax
from jax.experimental import pallas as pl
from jax.experimental.pallas import tpu as pltpu

F32 = jnp.float32
BF16 = jnp.bfloat16
I32 = jnp.int32

EPS = 1e-6
NEG = -1e30

V7X_SUBLANES = 8
V7X_LANES = 128
V7X_VMEM_BYTES = 64 * 1024 * 1024
VMEM_LIMIT = V7X_VMEM_BYTES - 8 * 1024 * 1024

LRU_C = 8.0
CONV_W = 4
WINDOWS = (128, 512, 2048)
DILATIONS = (1, 4, 16)
HG = 8
DH = 64
N_STRIDE = 128
QB = 128
REL_BUCKETS = 32
REL_MAX_DIST = WINDOWS[-1]
XA_HEADS = 4
N_EXPERTS = 8
TOP_K = 2
SAMPLE_PAD_T = 16

MIX_TC = 256
MOE_TM = 1024
MOE_TH = 512
FFN_TH = 512


def _cparams(sem):
    return pltpu.CompilerParams(dimension_semantics=sem, vmem_limit_bytes=VMEM_LIMIT)


def _const_spec(shape):
    nd = len(shape)
    return pl.BlockSpec(shape, lambda *_: (0,) * nd)


def _rms(x, g):
    return x * lax.rsqrt(jnp.mean(x * x, axis=-1, keepdims=True) + EPS) * g


def _norm_matmul_kernel(x_ref, g_ref, w_ref, hg_ref, hs_ref, o_ref, hn_ref, *, n_norm_tiles, dh, slabs,
                        row_split):
    j = pl.program_id(1)

    @pl.when(j == 0)
    def _():
        hn_ref[...] = _rms(x_ref[...], g_ref[...]).astype(BF16)

    tm = hn_ref.shape[0]
    rs = tm // row_split

    def emit(r, val):
        rows = slice(r * rs, (r + 1) * rs)
        if slabs:
            for c in range(o_ref.shape[0]):
                o_ref[c, rows, :] = val[:, c * V7X_LANES:(c + 1) * V7X_LANES]
        else:
            o_ref[rows, :] = val

    def tile(r, normed):
        y = jnp.dot(hn_ref[r * rs:(r + 1) * rs, :], w_ref[...], preferred_element_type=F32)
        if normed:
            ssq = jnp.dot((y * y).astype(BF16), hs_ref[...], preferred_element_type=F32)
            y = y * lax.rsqrt(ssq * (1.0 / dh) + EPS) * hg_ref[...]
        emit(r, y)

    if n_norm_tiles == 0:
        for r in range(row_split):
            tile(r, False)
    else:
        @pl.when(j < n_norm_tiles)
        def _():
            for r in range(row_split):
                tile(r, True)

        @pl.when(j >= n_norm_tiles)
        def _():
            for r in range(row_split):
                tile(r, False)


def _head_sum_matrix(tn, dh):
    idx = np.arange(tn) // dh
    return jnp.asarray((idx[:, None] == idx[None, :]).astype(np.float32), dtype=BF16)


def _norm_matmul(x, g, w, *, tn=512, head_gain=None, n_norm_cols=0, dh=1, slabs=False, name):
    m, d = x.shape
    n = w.shape[1]
    tm = min(m, 1024)
    assert m % tm == 0 and n % tn == 0 and n_norm_cols % tn == 0
    if head_gain is None:
        head_gain = jnp.ones((1, n), F32)
    hs = _head_sum_matrix(tn, dh)
    row_split = 2 if tm % 512 == 0 else 1
    kern = functools.partial(_norm_matmul_kernel, n_norm_tiles=n_norm_cols // tn, dh=dh, slabs=slabs,
                             row_split=row_split)
    if slabs:
        out_shape = jax.ShapeDtypeStruct((n // V7X_LANES, m, V7X_LANES), F32)
        out_spec = pl.BlockSpec((tn // V7X_LANES, tm, V7X_LANES), lambda i, j: (j, i, 0))
    else:
        out_shape = jax.ShapeDtypeStruct((m, n), F32)
        out_spec = pl.BlockSpec((tm, tn), lambda i, j: (i, j))
    return pl.pallas_call(
        kern,
        out_shape=out_shape,
        grid=(m // tm, n // tn),
        in_specs=[
            pl.BlockSpec((tm, d), lambda i, j: (i, 0)),
            pl.BlockSpec((1, d), lambda i, j: (0, 0)),
            pl.BlockSpec((d, tn), lambda i, j: (0, j)),
            pl.BlockSpec((1, tn), lambda i, j: (0, j)),
            pl.BlockSpec((tn, tn), lambda i, j: (0, 0)),
        ],
        out_specs=out_spec,
        scratch_shapes=[pltpu.VMEM((tm, d), BF16)],
        compiler_params=_cparams(("parallel", "arbitrary")),
        name=name,
    )(x, g.reshape(1, d), w, head_gain, hs)


def _s5_prep_kernel(lre_ref, lim_ref, ldt_ref, bre_ref, bim_ref, bbr_ref, bbi_ref, apow_ref):
    lr = lre_ref[...]
    li = lim_ref[...]
    dt = jnp.exp(ldt_ref[...])
    mag = jnp.exp(lr * dt)
    ab_r = mag * jnp.cos(li * dt)
    ab_i = mag * jnp.sin(li * dt)
    den = lr * lr + li * li
    nr = ab_r - 1.0
    cr = (nr * lr + ab_i * li) / den
    ci = (ab_i * lr - nr * li) / den
    b_r = bre_ref[...]
    b_i = bim_ref[...]
    bbr_ref[...] = (cr * b_r - ci * b_i).astype(BF16)
    bbi_ref[...] = (cr * b_i + ci * b_r).astype(BF16)

    n = lr.shape[1]
    row = lax.broadcasted_iota(I32, (V7X_SUBLANES, n), 0)

    def power(kf):
        mg = jnp.exp(kf * (lr * dt))
        return mg * jnp.cos(kf * (li * dt)), mg * jnp.sin(kf * (li * dt))

    for i, s in enumerate((1, 2, 4)):
        pr, pi = power(jnp.full((V7X_SUBLANES, n), s, F32))
        keep = row >= s
        apow_ref[2 * i] = jnp.where(keep, pr, 0.0)
        apow_ref[2 * i + 1] = jnp.where(keep, pi, 0.0)
    pr, pi = power((row + 1).astype(F32))
    apow_ref[6] = pr
    apow_ref[7] = pi


def _s5_prep(lam_re, lam_im, log_dt, b_re, b_im):
    g, p, h = b_re.shape
    n = g * p
    eye = jnp.eye(g, dtype=F32)
    bre_bd = jnp.einsum("gph,gk->khgp", b_re, eye).reshape(g * h, n)
    bim_bd = jnp.einsum("gph,gk->khgp", b_im, eye).reshape(g * h, n)
    ldt = jnp.broadcast_to(log_dt[:, None], (g, p)).reshape(1, n)
    return pl.pallas_call(
        _s5_prep_kernel,
        out_shape=(
            jax.ShapeDtypeStruct((g * h, n), BF16),
            jax.ShapeDtypeStruct((g * h, n), BF16),
            jax.ShapeDtypeStruct((8, V7X_SUBLANES, n), F32),
        ),
        compiler_params=pltpu.CompilerParams(vmem_limit_bytes=VMEM_LIMIT),
        name="s5_prep",
    )(lam_re.reshape(1, n), lam_im.reshape(1, n), ldt, bre_bd, bim_bd)


def _mixer0_kernel(z_ref, x_ref, convi_ref, h0_ref, sr0_ref, si0_ref,
                   cw_ref, cb_ref, wa_ref, ba_ref, wx_ref, bx_ref, lam_ref,
                   bbr_ref, bbi_ref, apow_ref, ccr_ref, cci_ref, d_ref, wglu_ref, bglu_ref, wout_ref,
                   y_ref, hl_ref, srl_ref, sil_ref,
                   ext_ref, hc_ref, src_ref, sic_ref, xr_ref, xi_ref, ha_ref, hb_ref,
                   *, tc, last_row):
    c = pl.program_id(1)
    d_lru = cw_ref.shape[1]
    d_s5 = d_ref.shape[1]
    n_tiles = tc // V7X_SUBLANES

    @pl.when(c == 0)
    def _():
        ext_ref[...] = convi_ref[...]
        hc_ref[...] = h0_ref[...]
        src_ref[...] = sr0_ref[...]
        sic_ref[...] = si0_ref[...]

    z = z_ref[...]
    xa = z[:, :d_lru]
    ga = z[:, d_lru:2 * d_lru]
    u = z[:, 2 * d_lru:]

    ext = jnp.concatenate([ext_ref[...], xa], axis=0)
    xc = cb_ref[...] + xa * cw_ref[CONV_W - 1:CONV_W, :]
    for s in range(1, CONV_W):
        xc = xc + pltpu.roll(ext, s, 0)[V7X_SUBLANES:, :] * cw_ref[CONV_W - 1 - s:CONV_W - s, :]
    ext_ref[...] = xa[tc - V7X_SUBLANES:, :]

    xcb = xc.astype(BF16)
    r = jax.nn.sigmoid(jnp.dot(xcb, wa_ref[...], preferred_element_type=F32) + ba_ref[...])
    ig = jax.nn.sigmoid(jnp.dot(xcb, wx_ref[...], preferred_element_type=F32) + bx_ref[...])
    lam = lam_ref[...]
    softplus_neg = jnp.maximum(-lam, 0.0) + jnp.log1p(jnp.exp(-jnp.abs(lam)))
    log_a = -LRU_C * r * softplus_neg
    a = jnp.exp(log_a)
    bt = jnp.sqrt(-jnp.tanh(log_a) * (a * a + 1.0)) * ig * xc

    a3 = a.reshape(n_tiles, V7X_SUBLANES, d_lru)
    b3 = bt.reshape(n_tiles, V7X_SUBLANES, d_lru)
    row = lax.broadcasted_iota(I32, (1, V7X_SUBLANES, d_lru), 1)
    for s in (1, 2, 4):
        keep = row >= s
        ar = pltpu.roll(a3, s, 1)
        br = pltpu.roll(b3, s, 1)
        b3 = jnp.where(keep, a3 * br + b3, b3)
        a3 = jnp.where(keep, a3 * ar, a3)
    ha_ref[...] = a3.reshape(tc, d_lru)
    hb_ref[...] = b3.reshape(tc, d_lru)

    def lru_tile(i, carry):
        r0 = pl.multiple_of(i * V7X_SUBLANES, V7X_SUBLANES)
        h = ha_ref[pl.ds(r0, V7X_SUBLANES), :] * carry + hb_ref[pl.ds(r0, V7X_SUBLANES), :]
        hb_ref[pl.ds(r0, V7X_SUBLANES), :] = h
        return h[V7X_SUBLANES - 1:, :]

    hc_ref[...] = lax.fori_loop(0, n_tiles, lru_tile, hc_ref[...])
    hs = hb_ref[...]
    hl_ref[...] = hb_ref[last_row:last_row + 1, :]
    ya = hs * jax.nn.gelu(ga)

    ub = u.astype(BF16)
    n_state = bbr_ref.shape[1]
    n_slab = d_s5 // V7X_LANES
    sw = n_state // n_slab

    def input_dot(w_ref):
        return jnp.concatenate(
            [jnp.dot(ub[:, c * V7X_LANES:(c + 1) * V7X_LANES],
                     w_ref[c * V7X_LANES:(c + 1) * V7X_LANES, c * sw:(c + 1) * sw],
                     preferred_element_type=F32) for c in range(n_slab)], axis=-1)

    xr3 = input_dot(bbr_ref).reshape(n_tiles, V7X_SUBLANES, n_state)
    xi3 = input_dot(bbi_ref).reshape(n_tiles, V7X_SUBLANES, n_state)
    for i, s in enumerate((1, 2, 4)):
        cr = apow_ref[2 * i][None]
        ci = apow_ref[2 * i + 1][None]
        rr = pltpu.roll(xr3, s, 1)
        ri = pltpu.roll(xi3, s, 1)
        xr3, xi3 = xr3 + cr * rr - ci * ri, xi3 + cr * ri + ci * rr
    xr_ref[...] = xr3.reshape(tc, n_state)
    xi_ref[...] = xi3.reshape(tc, n_state)

    def s5_tile(i, carry):
        cr_, ci_ = carry
        r0 = pl.multiple_of(i * V7X_SUBLANES, V7X_SUBLANES)
        pr = apow_ref[6]
        pi = apow_ref[7]
        nr_ = xr_ref[pl.ds(r0, V7X_SUBLANES), :] + pr * cr_ - pi * ci_
        ni_ = xi_ref[pl.ds(r0, V7X_SUBLANES), :] + pr * ci_ + pi * cr_
        xr_ref[pl.ds(r0, V7X_SUBLANES), :] = nr_
        xi_ref[pl.ds(r0, V7X_SUBLANES), :] = ni_
        return nr_[V7X_SUBLANES - 1:, :], ni_[V7X_SUBLANES - 1:, :]

    cr_f, ci_f = lax.fori_loop(0, n_tiles, s5_tile, (src_ref[...], sic_ref[...]))
    src_ref[...] = cr_f
    sic_ref[...] = ci_f
    srl_ref[...] = xr_ref[last_row:last_row + 1, :]
    sil_ref[...] = xi_ref[last_row:last_row + 1, :]

    def output_dot(x_ref_, w_ref):
        return jnp.concatenate(
            [jnp.dot(x_ref_[:, c * sw:(c + 1) * sw].astype(BF16),
                     w_ref[c * sw:(c + 1) * sw, c * V7X_LANES:(c + 1) * V7X_LANES],
                     preferred_element_type=F32) for c in range(n_slab)], axis=-1)

    ys = output_dot(xr_ref, ccr_ref) - output_dot(xi_ref, cci_ref)
    ys = ys + d_ref[...] * u
    gs = jax.nn.gelu(ys)
    yb = gs * jax.nn.sigmoid(jnp.dot(gs.astype(BF16), wglu_ref[...], preferred_element_type=F32)
                             + bglu_ref[...])

    y_ref[...] = (x_ref[...]
                  + jnp.dot(ya.astype(BF16), wout_ref[:d_lru, :], preferred_element_type=F32)
                  + jnp.dot(yb.astype(BF16), wout_ref[d_lru:, :], preferred_element_type=F32))


def _mixer0(z, x, conv_init, h0, sr0, si0, wts, *, tc, last_row):
    bn, t_len, d_in = z.shape
    d = x.shape[2]
    d_lru = wts["cw"].shape[1]
    n_state = wts["bbr"].shape[1]
    assert t_len % tc == 0
    kern = functools.partial(_mixer0_kernel, tc=tc, last_row=last_row)
    wnames = ("cw", "cb", "wa", "ba", "wx", "bx", "lam", "bbr", "bbi", "apow",
              "ccr", "cci", "d", "wglu", "bglu", "wout")
    wlist = [wts[k] for k in wnames]
    per_b = lambda shape: pl.BlockSpec((None,) + shape, lambda b, c: (b,) + (0,) * len(shape))
    in_specs = [
        pl.BlockSpec((None, tc, d_in), lambda b, c: (b, c, 0)),
        pl.BlockSpec((None, tc, d), lambda b, c: (b, c, 0)),
        per_b((V7X_SUBLANES, d_lru)), per_b((1, d_lru)), per_b((1, n_state)), per_b((1, n_state)),
    ] + [_const_spec(w.shape) for w in wlist]
    return pl.pallas_call(
        kern,
        out_shape=(
            jax.ShapeDtypeStruct((bn, t_len, d), F32),
            jax.ShapeDtypeStruct((bn, 1, d_lru), F32),
            jax.ShapeDtypeStruct((bn, 1, n_state), F32),
            jax.ShapeDtypeStruct((bn, 1, n_state), F32),
        ),
        grid=(bn, t_len // tc),
        in_specs=in_specs,
        out_specs=(
            pl.BlockSpec((None, tc, d), lambda b, c: (b, c, 0)),
            per_b((1, d_lru)), per_b((1, n_state)), per_b((1, n_state)),
        ),
        scratch_shapes=[
            pltpu.VMEM((V7X_SUBLANES, d_lru), F32),
            pltpu.VMEM((1, d_lru), F32),
            pltpu.VMEM((1, n_state), F32),
            pltpu.VMEM((1, n_state), F32),
            pltpu.VMEM((tc, n_state), F32),
            pltpu.VMEM((tc, n_state), F32),
            pltpu.VMEM((tc, d_lru), F32),
            pltpu.VMEM((tc, d_lru), F32),
        ],
        compiler_params=_cparams(("parallel", "arbitrary")),
        name="mixer0",
    )(z, x, conv_init, h0, sr0, si0, *wlist)


def _xattn_kernel(x_ref, g_ref, wq_ref, qg_ref, mk_ref, mv_ref, wo_ref, o_ref):
    nb, tm, d = x_ref.shape
    x = x_ref[...].reshape(nb * tm, d)
    q = jnp.dot(_rms(x, g_ref[...]).astype(BF16), wq_ref[...], preferred_element_type=F32)
    dh = qg_ref.shape[1]
    head_major = len(mk_ref.shape) == 4
    rows = []
    for b in range(nb):
        if head_major:
            kb = pltpu.einshape("nhd->hnd", mk_ref[b])
            vb = pltpu.einshape("nhd->hnd", mv_ref[b])
            head_of = lambda arr, h: arr[h]
        else:
            kb, vb = mk_ref[b], mv_ref[b]
            head_of = lambda arr, h: arr[:, h * dh:(h + 1) * dh]
        outs = []
        for h in range(XA_HEADS):
            qn = _rms(q[b * tm:(b + 1) * tm, h * dh:(h + 1) * dh], qg_ref[...]).astype(BF16)
            s = lax.dot_general(qn, head_of(kb, h).astype(BF16), (((1,), (1,)), ((), ())),
                                preferred_element_type=F32) * (dh ** -0.5)
            m = jnp.max(s, axis=-1, keepdims=True)
            p = jnp.exp(s - m)
            den = jnp.sum(p, axis=-1, keepdims=True)
            oh = jnp.dot(p.astype(BF16), head_of(vb, h).astype(BF16), preferred_element_type=F32) / den
            outs.append(oh.astype(BF16))
        rows.append(jnp.concatenate(outs, axis=-1))
    o = rows[0] if nb == 1 else jnp.concatenate(rows, axis=0)
    o_ref[...] = (x + jnp.dot(o, wo_ref[...], preferred_element_type=F32)).reshape(nb, tm, d)


def _xattn(x, g, wq, qg, mk, mv, wo, *, tm, nb=1, name, layer=None):
    bn, t_len, d = x.shape
    assert t_len % tm == 0 and bn % nb == 0
    if layer is None:
        mem_spec = pl.BlockSpec((nb,) + mk.shape[1:], lambda b, i: (b, 0, 0))
    else:
        mem_spec = pl.BlockSpec((None, nb) + mk.shape[2:], lambda b, i: (layer, b, 0, 0, 0))
    return pl.pallas_call(
        _xattn_kernel,
        out_shape=jax.ShapeDtypeStruct(x.shape, F32),
        grid=(bn // nb, t_len // tm),
        in_specs=[
            pl.BlockSpec((nb, tm, d), lambda b, i: (b, i, 0)),
            _const_spec((1, d)),
            _const_spec(wq.shape),
            _const_spec((1, qg.shape[-1])),
            mem_spec,
            mem_spec,
            _const_spec(wo.shape),
        ],
        out_specs=pl.BlockSpec((nb, tm, d), lambda b, i: (b, i, 0)),
        compiler_params=_cparams(("parallel", "arbitrary")),
        name=name,
    )(x, g.reshape(1, d), wq, qg.reshape(1, -1), mk, mv, wo)


def _ffn_kernel(x_ref, g_ref, wg_ref, wu_ref, wd_ref, o_ref, hn_ref, acc_ref):
    j = pl.program_id(1)

    @pl.when(j == 0)
    def _():
        hn_ref[...] = _rms(x_ref[...], g_ref[...]).astype(BF16)
        acc_ref[...] = jnp.zeros_like(acc_ref)

    hn = hn_ref[...]
    gate = jnp.dot(hn, wg_ref[...], preferred_element_type=F32)
    up = jnp.dot(hn, wu_ref[...], preferred_element_type=F32)
    act = (jax.nn.silu(gate) * up).astype(BF16)
    acc_ref[...] += jnp.dot(act, wd_ref[...], preferred_element_type=F32)

    @pl.when(j == pl.num_programs(1) - 1)
    def _():
        o_ref[...] = x_ref[...] + acc_ref[...]


def _ffn(x, g, w_gu, w_d, *, name):
    m, d = x.shape
    hid = w_d.shape[0]
    th = FFN_TH
    tm = min(m, 1024)
    nj = hid // th
    assert m % tm == 0 and hid % th == 0
    return pl.pallas_call(
        _ffn_kernel,
        out_shape=jax.ShapeDtypeStruct((m, d), F32),
        grid=(m // tm, nj),
        in_specs=[
            pl.BlockSpec((tm, d), lambda i, j: (i, 0)),
            pl.BlockSpec((1, d), lambda i, j: (0, 0)),
            pl.BlockSpec((d, th), lambda i, j: (0, j)),
            pl.BlockSpec((d, th), lambda i, j: (0, j + nj)),
            pl.BlockSpec((th, d), lambda i, j: (j, 0)),
        ],
        out_specs=pl.BlockSpec((tm, d), lambda i, j: (i, 0)),
        scratch_shapes=[pltpu.VMEM((tm, d), BF16), pltpu.VMEM((tm, d), F32)],
        compiler_params=_cparams(("parallel", "arbitrary")),
        name=name,
    )(x, g.reshape(1, d), w_gu, w_gu, w_d)


def _rel_bucket_np(dist):
    dist = np.clip(np.asarray(dist), 0, None)
    max_exact = REL_BUCKETS // 2
    safe = np.maximum(dist, max_exact).astype(np.float32)
    large = max_exact + np.floor(np.log(safe / max_exact) / math.log(REL_MAX_DIST / max_exact)
                                 * (REL_BUCKETS - max_exact)).astype(np.int32)
    large = np.minimum(large, REL_BUCKETS - 1)
    return np.where(dist < max_exact, dist, large).astype(np.int32)


def _dil_kernel(q_ref, k_ref, v_ref, bias_ref, o_ref, l_ref, kprev_ref, vprev_ref, *, d, nblk):
    n = pl.program_id(1)

    @pl.when(n == 0)
    def _():
        kprev_ref[...] = jnp.zeros_like(kprev_ref)
        vprev_ref[...] = jnp.zeros_like(vprev_ref)

    first_sel = jnp.minimum(n, 1)
    lane = lax.broadcasted_iota(I32, (QB, V7X_LANES), 1)
    n_slab = q_ref.shape[0]
    heads_per_slab = V7X_LANES // DH

    def residue(r, _):
        prev_k = [kprev_ref[r, c] for c in range(n_slab)]
        prev_v = [vprev_ref[r, c] for c in range(n_slab)]
        for blk in range(nblk):
            rows = pl.ds(blk * QB * d + r, QB, stride=d)
            bsel = first_sel if blk == 0 else 1
            lse = jnp.zeros((QB, V7X_LANES), F32)
            for c in range(n_slab):
                q = (q_ref[c, rows, :] * (DH ** -0.5)).astype(BF16)
                kc = k_ref[c, rows, :].astype(BF16)
                vc = v_ref[c, rows, :].astype(BF16)
                k = jnp.concatenate([prev_k[c], kc], axis=0)
                v = jnp.concatenate([prev_v[c], vc], axis=0)
                outs = []
                for hh in range(heads_per_slab):
                    h = c * heads_per_slab + hh
                    sl = slice(hh * DH, (hh + 1) * DH)
                    s = lax.dot_general(q[:, sl], k[:, sl], (((1,), (1,)), ((), ())),
                                        preferred_element_type=F32) + bias_ref[bsel, h]
                    m = jnp.max(s, axis=-1, keepdims=True)
                    p = jnp.exp(s - m)
                    den = jnp.sum(p, axis=-1, keepdims=True)
                    outs.append(jnp.dot(p.astype(BF16), v[:, sl], preferred_element_type=F32) / den)
                    lse = jnp.where(lane == h, m + jnp.log(den), lse)
                o_ref[c, rows, :] = jnp.concatenate(outs, axis=-1)
                prev_k[c] = kc
                prev_v[c] = vc
            l_ref[rows, :] = lse
        for c in range(n_slab):
            kprev_ref[r, c] = prev_k[c]
            vprev_ref[r, c] = prev_v[c]
        return 0

    lax.fori_loop(0, d, residue, 0, unroll=min(d, 2))


def _dilated_prompt(qkv, tab, g, d, bn):
    n_slabs, m_rows, _ = qkv.shape
    s_len = m_rows // bn
    gw = HG * DH
    gs = gw // V7X_LANES
    nblk = 2 if d == 1 else 1
    chunk = nblk * QB * d
    n_chunks = s_len // chunk
    assert s_len % chunk == 0
    k_col = n_slabs // gs // 3

    qi = np.arange(QB)[:, None]
    ki = np.arange(2 * QB)[None, :]
    dist = qi + QB - ki
    band = (dist >= 0) & (dist <= N_STRIDE)
    first = band & (ki >= QB)
    onehot = (np.arange(REL_BUCKETS)[:, None] == _rel_bucket_np(dist * d).reshape(1, -1)).astype(np.float32)
    bias = jnp.dot(tab.T.astype(F32), jnp.asarray(onehot),
                   precision=lax.Precision.HIGHEST).reshape(HG, QB, 2 * QB)
    bias2 = jnp.stack([jnp.where(first[None], bias, NEG), jnp.where(band[None], bias, NEG)])

    col = lambda section: (lambda b, n: (section * k_col + g, b * n_chunks + n, 0))
    blk = (gs, chunk, V7X_LANES)
    return pl.pallas_call(
        functools.partial(_dil_kernel, d=d, nblk=nblk),
        out_shape=(jax.ShapeDtypeStruct((gs, m_rows, V7X_LANES), F32),
                   jax.ShapeDtypeStruct((m_rows, V7X_LANES), F32)),
        grid=(bn, n_chunks),
        in_specs=[
            pl.BlockSpec(blk, col(0)), pl.BlockSpec(blk, col(1)), pl.BlockSpec(blk, col(2)),
            _const_spec(bias2.shape),
        ],
        out_specs=(pl.BlockSpec(blk, lambda b, n: (0, b * n_chunks + n, 0)),
                   pl.BlockSpec((chunk, V7X_LANES), lambda b, n: (b * n_chunks + n, 0))),
        scratch_shapes=[pltpu.VMEM((d, gs, QB, V7X_LANES), BF16), pltpu.VMEM((d, gs, QB, V7X_LANES), BF16)],
        compiler_params=_cparams(("parallel", "arbitrary")),
        name=f"dilated_prompt_g{g}",
    )(qkv, qkv, qkv, bias2)


def _split_dot(x, w_ref):
    hi = x.astype(BF16)
    lo = (x - hi.astype(F32)).astype(BF16)
    w = w_ref[...]
    return jnp.dot(hi, w, preferred_element_type=F32) + jnp.dot(lo, w, preferred_element_type=F32)


def _dil_sample_kernel(q_ref, k_ref, v_ref, cache_ref, bias_ref, biasn_ref, o_ref, l_ref, cout_ref, *, t_valid):
    n_slab, t_pad, _ = q_ref.shape
    w = cache_ref.shape[-1]
    unslab = lambda ref: jnp.concatenate([ref[c] for c in range(n_slab)], axis=-1)
    qn = unslab(q_ref)
    kn = unslab(k_ref)
    vn = unslab(v_ref)

    def as_last_columns(x):
        shifted = pltpu.roll(x, t_pad - t_valid, 0)
        tile = jnp.concatenate([jnp.zeros((V7X_LANES - t_pad, x.shape[1]), F32), shifted], axis=0)
        return tile.T

    new_cols = (as_last_columns(kn), as_last_columns(vn))
    lane = lax.broadcasted_iota(I32, (t_pad, V7X_LANES), 1)
    is_new = lax.broadcasted_iota(I32, (DH, V7X_LANES), 1) >= V7X_LANES - t_valid
    qb, kb, vb = qn.astype(BF16), kn.astype(BF16), vn.astype(BF16)
    lse = jnp.zeros((t_pad, V7X_LANES), F32)
    outs = []
    for h in range(HG):
        sl = slice(h * DH, (h + 1) * DH)
        k_t = cache_ref[0, h]
        v_t = cache_ref[1, h]
        s = jnp.dot(qb[:, sl], k_t.astype(BF16), preferred_element_type=F32) * (DH ** -0.5) + bias_ref[h]
        sn = lax.dot_general(qb[:, sl], kb[:, sl], (((1,), (1,)), ((), ())),
                             preferred_element_type=F32) * (DH ** -0.5) + biasn_ref[h]
        m = jnp.maximum(jnp.max(s, axis=-1, keepdims=True), jnp.max(sn, axis=-1, keepdims=True))
        p = jnp.exp(s - m)
        pn = jnp.exp(sn - m)
        den = jnp.sum(p, axis=-1, keepdims=True) + jnp.sum(pn, axis=-1, keepdims=True)
        pv = lax.dot_general(p.astype(BF16), v_t.astype(BF16), (((1,), (1,)), ((), ())),
                             preferred_element_type=F32)
        outs.append((pv + jnp.dot(pn.astype(BF16), vb[:, sl], preferred_element_type=F32)) / den)
        lse = jnp.where(lane == h, m + jnp.log(den), lse)
        for kv, old in enumerate((k_t, v_t)):
            moved = pltpu.roll(old, w - t_valid, 1)
            cout_ref[kv, h] = moved
            cout_ref[kv, h, :, w - V7X_LANES:] = jnp.where(is_new, new_cols[kv][sl, :], moved[:, w - V7X_LANES:])
    o = jnp.concatenate(outs, axis=-1)
    for c in range(n_slab):
        o_ref[c] = o[:, c * V7X_LANES:(c + 1) * V7X_LANES]
    l_ref[...] = lse


def _dilated_sample(qkv, cache, tab, g, d, t_valid):
    n_slabs, m_rows, _ = qkv.shape
    db, wb = cache.shape[0], cache.shape[-1]
    t_pad = m_rows // db
    gw = HG * DH
    gs = gw // V7X_LANES
    assert wb == N_STRIDE * d and t_valid <= V7X_LANES and wb % V7X_LANES == 0
    k_col = n_slabs // gs // 3

    def bias_of(j, ok):
        onehot = (np.arange(REL_BUCKETS)[:, None] == _rel_bucket_np(d * j).reshape(1, -1)) & ok.reshape(1, -1)
        looked_up = jnp.dot(tab.T.astype(F32), jnp.asarray(onehot.astype(np.float32)),
                            precision=lax.Precision.HIGHEST).reshape((HG,) + j.shape)
        return jnp.where(ok[None], looked_up, NEG)

    tq = np.arange(t_pad)[:, None]
    pos = np.arange(wb)[None, :]
    jc = (wb + tq - pos) // d
    bias = bias_of(jc, ((wb + tq - pos) % d == 0) & (jc >= 1) & (jc <= N_STRIDE))
    tn = np.arange(t_pad)[None, :]
    jn = (tq - tn) // d
    bias_new = bias_of(jn, (tn <= tq) & ((tq - tn) % d == 0) & (jn <= N_STRIDE) & (tn < t_valid))

    kern = functools.partial(_dil_sample_kernel, t_valid=t_valid)
    blk = (gs, t_pad, V7X_LANES)
    cblk = pl.BlockSpec((None,) + cache.shape[1:], lambda b: (b, 0, 0, 0, 0))
    return pl.pallas_call(
        kern,
        out_shape=(jax.ShapeDtypeStruct((gs, m_rows, V7X_LANES), F32),
                   jax.ShapeDtypeStruct((m_rows, V7X_LANES), F32),
                   jax.ShapeDtypeStruct(cache.shape, F32)),
        grid=(db,),
        in_specs=[
            pl.BlockSpec(blk, lambda b: (g, b, 0)),
            pl.BlockSpec(blk, lambda b: (k_col + g, b, 0)),
            pl.BlockSpec(blk, lambda b: (2 * k_col + g, b, 0)),
            cblk, _const_spec(bias.shape), _const_spec(bias_new.shape),
        ],
        out_specs=(pl.BlockSpec(blk, lambda b: (0, b, 0)),
                   pl.BlockSpec((t_pad, V7X_LANES), lambda b: (b, 0)),
                   cblk),
        compiler_params=_cparams(("parallel",)),
        name=f"dilated_sample_g{g}",
    )(qkv, qkv, qkv, cache, bias, bias_new)


def _merge_wo_kernel(o0_ref, o1_ref, o2_ref, l0_ref, l1_ref, l2_ref, x_ref, hexp_ref, w_ref, out_ref):
    l0, l1, l2 = l0_ref[...], l1_ref[...], l2_ref[...]
    m = jnp.maximum(jnp.maximum(l0, l1), l2)
    e0, e1, e2 = jnp.exp(l0 - m), jnp.exp(l1 - m), jnp.exp(l2 - m)
    inv = 1.0 / (e0 + e1 + e2)
    unslab = lambda ref: jnp.concatenate([ref[c] for c in range(ref.shape[0])], axis=-1)
    o = (_split_dot(e0 * inv, hexp_ref) * unslab(o0_ref) + _split_dot(e1 * inv, hexp_ref) * unslab(o1_ref)
         + _split_dot(e2 * inv, hexp_ref) * unslab(o2_ref))
    out_ref[...] = x_ref[...] + jnp.dot(o.astype(BF16), w_ref[...], preferred_element_type=F32)


def _merge_wo(outs, lses, x, w, *, name):
    m, d = x.shape
    gw = w.shape[0]
    tm = min(m, 512)
    assert m % tm == 0
    head = np.arange(gw) // DH
    hexp = jnp.asarray((np.arange(V7X_LANES)[:, None] == head[None, :]).astype(np.float32), dtype=BF16)
    row = lambda width: pl.BlockSpec((tm, width), lambda i: (i, 0))
    slab = pl.BlockSpec((gw // V7X_LANES, tm, V7X_LANES), lambda i: (0, i, 0))
    return pl.pallas_call(
        _merge_wo_kernel,
        out_shape=jax.ShapeDtypeStruct((m, d), F32),
        grid=(m // tm,),
        in_specs=[slab] * 3 + [row(V7X_LANES)] * 3 + [row(d), _const_spec(hexp.shape), _const_spec(w.shape)],
        out_specs=row(d),
        compiler_params=_cparams(("parallel",)),
        name=name,
    )(*outs, *lses, x, hexp, w)


def _router_kernel(x_ref, g_ref, wr_ref, br_ref, cnt0_ref, tri_ref,
                   hn_ref, eid_ref, gate_ref, rank_ref, cnt_ref, run_ref):
    @pl.when(pl.program_id(0) == 0)
    def _():
        run_ref[...] = cnt0_ref[...]

    hn = _rms(x_ref[...], g_ref[...])
    hn_ref[...] = hn
    wr = wr_ref[...]
    h_hi = hn.astype(BF16)
    h_lo = (hn - h_hi.astype(F32)).astype(BF16)
    w_hi = wr.astype(BF16)
    w_lo = (wr - w_hi.astype(F32)).astype(BF16)
    logits = (jnp.dot(h_hi, w_hi, preferred_element_type=F32) + jnp.dot(h_hi, w_lo, preferred_element_type=F32)
              + jnp.dot(h_lo, w_hi, preferred_element_type=F32)) + br_ref[...]
    ne = logits.shape[1]
    lane = lax.broadcasted_iota(I32, logits.shape, 1)
    m1 = jnp.max(logits, axis=-1, keepdims=True)
    i1 = jnp.min(jnp.where(logits == m1, lane, ne), axis=-1, keepdims=True)
    rest = jnp.where(lane == i1, -jnp.inf, logits)
    m2 = jnp.max(rest, axis=-1, keepdims=True)
    i2 = jnp.min(jnp.where(rest == m2, lane, ne), axis=-1, keepdims=True)
    e2 = jnp.exp(m2 - m1)
    g1 = 1.0 / (1.0 + e2)
    g2 = e2 / (1.0 + e2)
    eid_ref[...] = jnp.where(lane == 0, i1, jnp.where(lane == 1, i2, 0))
    gate_ref[...] = jnp.where(lane == 0, g1, jnp.where(lane == 1, g2, 0.0))
    chosen = jnp.logical_or(lane == i1, lane == i2)
    before = jnp.dot(tri_ref[...], jnp.where(chosen, 1.0, 0.0).astype(BF16),
                     preferred_element_type=F32) + run_ref[...]
    r1 = jnp.sum(jnp.where(lane == i1, before, 0.0), axis=-1, keepdims=True)
    r2 = jnp.sum(jnp.where(lane == i2, before, 0.0), axis=-1, keepdims=True)
    rank_ref[...] = jnp.where(lane == 0, r1, jnp.where(lane == 1, r2, 0.0)).astype(I32)
    run_ref[...] += jnp.sum(jnp.where(chosen, 1.0, 0.0), axis=0, keepdims=True)
    cnt_ref[...] = run_ref[...]


def _router(x, g, wr, br, cnt0, *, name):
    m, d = x.shape
    ne = wr.shape[1]
    tm = min(m, 512)
    assert m % tm == 0
    tri = jnp.asarray(np.tril(np.ones((tm, tm), np.float32), -1), dtype=BF16)
    row = lambda width: pl.BlockSpec((tm, width), lambda i: (i, 0))
    return pl.pallas_call(
        _router_kernel,
        out_shape=(jax.ShapeDtypeStruct((m, d), F32),
                   jax.ShapeDtypeStruct((m, ne), I32),
                   jax.ShapeDtypeStruct((m, ne), F32),
                   jax.ShapeDtypeStruct((m, ne), I32),
                   jax.ShapeDtypeStruct((1, ne), F32)),
        grid=(m // tm,),
        in_specs=[row(d), _const_spec((1, d)), _const_spec(wr.shape), _const_spec((1, ne)),
                  _const_spec((1, ne)), _const_spec((tm, tm))],
        out_specs=(row(d), row(ne), row(ne), row(ne), _const_spec((1, ne))),
        scratch_shapes=[pltpu.VMEM((1, ne), F32)],
        compiler_params=_cparams(("arbitrary",)),
        name=name,
    )(x, g.reshape(1, d), wr, br.reshape(1, ne), cnt0, tri)


def _dispatch_kernel(pos_ref, hn_ref, xs_in_ref, xs_ref, sem, *, tmd):
    del xs_in_ref

    def start(r, _):
        for k in range(TOP_K):
            pltpu.make_async_copy(hn_ref.at[pl.ds(r, 1)],
                                  xs_ref.at[pl.ds(pos_ref[0, 0, TOP_K * r + k], 1)], sem.at[0]).start()
        return 0

    lax.fori_loop(0, tmd, start, 0, unroll=8)
    pltpu.make_async_copy(xs_ref.at[pl.ds(0, TOP_K * tmd)], xs_ref.at[pl.ds(0, TOP_K * tmd)], sem.at[0]).wait()


def _dispatch(hn, pos, xs, *, name):
    m, d = hn.shape
    tmd = min(m, 256)
    assert m % tmd == 0
    kern = functools.partial(_dispatch_kernel, tmd=tmd)
    return pl.pallas_call(
        kern,
        out_shape=jax.ShapeDtypeStruct(xs.shape, xs.dtype),
        grid=(m // tmd,),
        in_specs=[
            pl.BlockSpec((1, 1, TOP_K * tmd), lambda i: (i, 0, 0), memory_space=pltpu.SMEM),
            pl.BlockSpec((tmd, d), lambda i: (i, 0)),
            pl.BlockSpec(memory_space=pl.ANY),
        ],
        out_specs=pl.BlockSpec(memory_space=pl.ANY),
        scratch_shapes=[pltpu.SemaphoreType.DMA((1,))],
        input_output_aliases={2: 0},
        compiler_params=_cparams(("arbitrary",)),
        name=name,
    )(pos.reshape(m // tmd, 1, TOP_K * tmd), hn, xs)


def _moe_ffn_kernel(te_ref, nused_ref, x_ref, wg_ref, wu_ref, wd_ref, o_ref, xb_ref, acc_ref):
    t = pl.program_id(0)
    j = pl.program_id(1)
    last = pl.num_programs(1) - 1
    used = t < nused_ref[0]

    @pl.when(used)
    def _():
        @pl.when(j == 0)
        def _():
            xb_ref[...] = x_ref[...].astype(BF16)
            acc_ref[...] = jnp.zeros_like(acc_ref)

        xb = xb_ref[...]
        gate = jnp.dot(xb, wg_ref[...].astype(BF16), preferred_element_type=F32)
        up = jnp.dot(xb, wu_ref[...].astype(BF16), preferred_element_type=F32)
        act = (jax.nn.silu(gate) * up).astype(BF16)
        acc_ref[...] += jnp.dot(act, wd_ref[...].astype(BF16), preferred_element_type=F32)

        @pl.when(j == last)
        def _():
            o_ref[...] = acc_ref[...]

    @pl.when(jnp.logical_and(jnp.logical_not(used), j == last))
    def _():
        o_ref[...] = jnp.zeros_like(o_ref)


def _moe_ffn(xs, te, n_used, w_gu, w_d):
    n_rows, d = xs.shape
    hid = w_d.shape[1]
    tm, th = MOE_TM, MOE_TH
    nj = hid // th
    nt = n_rows // tm
    assert hid % th == 0 and n_rows % tm == 0

    def jj(t, j, nu):
        return jnp.where(t < nu[0], j, nj - 1)

    grid_spec = pltpu.PrefetchScalarGridSpec(
        num_scalar_prefetch=2,
        grid=(nt, nj),
        in_specs=[
            pl.BlockSpec((tm, d), lambda t, j, te, nu: (jnp.maximum(jnp.minimum(t, nu[0] - 1), 0), 0)),
            pl.BlockSpec((None, d, th), lambda t, j, te, nu: (te[t], 0, jj(t, j, nu))),
            pl.BlockSpec((None, d, th), lambda t, j, te, nu: (te[t], 0, jj(t, j, nu) + nj)),
            pl.BlockSpec((None, th, d), lambda t, j, te, nu: (te[t], jj(t, j, nu), 0)),
        ],
        out_specs=pl.BlockSpec((tm, d), lambda t, j, te, nu: (t, 0)),
        scratch_shapes=[pltpu.VMEM((tm, d), BF16), pltpu.VMEM((tm, d), F32)],
    )
    return pl.pallas_call(
        _moe_ffn_kernel,
        out_shape=jax.ShapeDtypeStruct((n_rows, d), F32),
        grid_spec=grid_spec,
        compiler_params=_cparams(("arbitrary", "arbitrary")),
        name="moe_experts",
    )(te, n_used, xs, w_gu, w_gu, w_d)


def _combine_kernel(pos_ref, x_ref, gate_ref, ys_ref, o_ref, buf_ref, sem, *, tmc):
    def start(r, _):
        for k in range(TOP_K):
            pltpu.make_async_copy(ys_ref.at[pl.ds(pos_ref[0, 0, TOP_K * r + k], 1)],
                                  buf_ref.at[k, pl.ds(r, 1)], sem.at[0]).start()
        return 0

    lax.fori_loop(0, tmc, start, 0, unroll=8)
    pltpu.make_async_copy(ys_ref.at[pl.ds(0, TOP_K * tmc)], ys_ref.at[pl.ds(0, TOP_K * tmc)], sem.at[0]).wait()
    g = gate_ref[...]
    acc = x_ref[...]
    for k in range(TOP_K):
        acc = acc + g[:, k:k + 1] * buf_ref[k]
    o_ref[...] = acc


def _combine(x, gate, pos, ys, *, name):
    m, d = x.shape
    ne = gate.shape[1]
    tmc = min(m, 256)
    assert m % tmc == 0
    kern = functools.partial(_combine_kernel, tmc=tmc)
    return pl.pallas_call(
        kern,
        out_shape=jax.ShapeDtypeStruct((m, d), F32),
        grid=(m // tmc,),
        in_specs=[
            pl.BlockSpec((1, 1, TOP_K * tmc), lambda i: (i, 0, 0), memory_space=pltpu.SMEM),
            pl.BlockSpec((tmc, d), lambda i: (i, 0)),
            pl.BlockSpec((tmc, ne), lambda i: (i, 0)),
            pl.BlockSpec(memory_space=pl.ANY),
        ],
        out_specs=pl.BlockSpec((tmc, d), lambda i: (i, 0)),
        scratch_shapes=[pltpu.VMEM((TOP_K, tmc, d), F32), pltpu.SemaphoreType.DMA((1,))],
        compiler_params=_cparams(("arbitrary",)),
        name=name,
    )(pos.reshape(m // tmc, 1, TOP_K * tmc), x, gate, ys)


def _moe_plan(counts, eids, ranks, tm, nt):
    padded = ((counts + tm - 1) // tm) * tm
    ends = jnp.cumsum(padded)
    starts = ends - padded

    def positions(eid, rank):
        start = jnp.zeros_like(rank)
        for e in range(N_EXPERTS):
            start = start + jnp.where(eid == e, starts[e], 0)
        return start + rank

    pos = [positions(e[:, :TOP_K], r[:, :TOP_K]) for e, r in zip(eids, ranks)]
    n_used = (ends[-1] // tm).astype(I32)
    tile = jnp.arange(nt, dtype=I32)
    first_row = jnp.minimum(tile, n_used - 1) * tm
    te = jnp.minimum(jnp.sum((first_row[:, None] >= ends[None, :]).astype(I32), axis=1), N_EXPERTS - 1)
    return pos, te, n_used.reshape(1)


def _block_diag(w):
    n, a, b = w.shape
    return jnp.einsum("nij,nm->nimj", w, jnp.eye(n, dtype=w.dtype)).reshape(n * a, n * b)


def kernel(x_prompt, x_sample, state_conv, state_lru, state_s5_re, state_s5_im, cache_swa0_kv, cache_swa1_kv, cache_swa2_kv, cache_mem_k, cache_mem_v, mem_prompt, norm_mix, norm_xa, norm_ffn, norm_mem, w_in_even, conv_w, conv_b, lru_wa, lru_ba, lru_wx, lru_bx, lru_lam, s5_lam_re, s5_lam_im, s5_log_dt, s5_b_re, s5_b_im, s5_c_re, s5_c_im, s5_d, s5_w_glu, s5_b_glu, w_out_even, w_qkv_odd, q_norm_odd, k_norm_odd, w_o_odd, rel_bias, xa_wq, xa_wkv, xa_qn, xa_kn, xa_wo, ffn_w_gu, ffn_w_down, moe_router_w, moe_router_b, moe_w_gu, moe_w_down):
    bp, s_len, d = x_prompt.shape
    db, t_dec, _ = x_sample.shape
    n_mem = mem_prompt.shape[1]
    tp = SAMPLE_PAD_T
    d_lru = conv_w.shape[-1]
    s5_g, s5_p, s5_h = s5_b_re.shape[1:]
    n_state = s5_g * s5_p
    d_s5 = s5_g * s5_h
    xa_dh = d // XA_HEADS
    caches = (cache_swa0_kv, cache_swa1_kv, cache_swa2_kv)
    bf = lambda w: w.astype(BF16)

    yp = x_prompt
    ys = jnp.pad(x_sample, ((0, 0), (0, tp - t_dec), (0, 0)))

    w_in = bf(w_in_even[0])
    bbr, bbi, apow = _s5_prep(s5_lam_re[0], s5_lam_im[0], s5_log_dt[0], s5_b_re[0], s5_b_im[0])
    eye_g = jnp.eye(s5_g, dtype=F32)
    mix_w = dict(
        cw=conv_w[0], cb=conv_b[0].reshape(1, d_lru),
        wa=bf(_block_diag(lru_wa[0])), ba=lru_ba[0].reshape(1, d_lru),
        wx=bf(_block_diag(lru_wx[0])), bx=lru_bx[0].reshape(1, d_lru),
        lam=lru_lam[0].reshape(1, d_lru),
        bbr=bbr, bbi=bbi, apow=apow,
        ccr=bf(jnp.einsum("ghp,gk->gpkh", s5_c_re[0], eye_g).reshape(n_state, d_s5)),
        cci=bf(jnp.einsum("ghp,gk->gpkh", s5_c_im[0], eye_g).reshape(n_state, d_s5)),
        d=s5_d[0].reshape(1, d_s5), wglu=bf(s5_w_glu[0]), bglu=s5_b_glu[0].reshape(1, d_s5),
        wout=bf(w_out_even[0]),
    )
    z_p = _norm_matmul(yp.reshape(bp * s_len, d), norm_mix[0], w_in, name="in_proj_p").reshape(bp, s_len, -1)
    yp, p_lru, p_s5r, p_s5i = _mixer0(
        z_p, yp, jnp.zeros((bp, V7X_SUBLANES, d_lru), F32), jnp.zeros((bp, 1, d_lru), F32),
        jnp.zeros((bp, 1, n_state), F32), jnp.zeros((bp, 1, n_state), F32), mix_w,
        tc=MIX_TC, last_row=MIX_TC - 1)
    z_s = _norm_matmul(ys.reshape(db * tp, d), norm_mix[0], w_in, name="in_proj_s").reshape(db, tp, -1)
    conv_init = jnp.pad(state_conv[0], ((0, 0), (V7X_SUBLANES - (CONV_W - 1), 0), (0, 0)))
    ys, s_lru, s_s5r, s_s5i = _mixer0(
        z_s, ys, conv_init, state_lru[0].reshape(db, 1, d_lru),
        state_s5_re[0].reshape(db, 1, n_state), state_s5_im[0].reshape(db, 1, n_state), mix_w,
        tc=tp, last_row=t_dec - 1)
    p_state_conv = z_p[:, s_len - (CONV_W - 1):, :d_lru][None]
    s_state_conv = z_s[:, t_dec - (CONV_W - 1):t_dec, :d_lru][None]

    p_mk, p_mv = [], []

    def cross_attention(layer, yp, ys):
        kn_gain = jnp.concatenate([jnp.tile(xa_kn[layer], XA_HEADS), jnp.ones((d,), F32)]).reshape(1, 2 * d)
        kv = _norm_matmul(mem_prompt.reshape(bp * n_mem, d), norm_mem[layer], bf(xa_wkv[layer]),
                          head_gain=kn_gain, n_norm_cols=d, dh=xa_dh, name=f"mem_kv{layer}")
        kv = kv.reshape(bp, n_mem, 2 * d)
        mk, mv = kv[:, :, :d], kv[:, :, d:]
        p_mk.append(mk.reshape(bp, n_mem, XA_HEADS, xa_dh))
        p_mv.append(mv.reshape(bp, n_mem, XA_HEADS, xa_dh))
        wq, wo = bf(xa_wq[layer]), bf(xa_wo[layer])
        yp = _xattn(yp, norm_xa[layer], wq, xa_qn[layer], mk, mv, wo, tm=512, name=f"xattn_p{layer}")
        ys = _xattn(ys, norm_xa[layer], wq, xa_qn[layer], cache_mem_k, cache_mem_v, wo,
                    tm=tp, nb=4, name=f"xattn_s{layer}", layer=layer)
        return yp, ys

    yp, ys = cross_attention(0, yp, ys)
    w_gu0, w_d0 = bf(ffn_w_gu[0]), bf(ffn_w_down[0])
    yp = _ffn(yp.reshape(bp * s_len, d), norm_ffn[0], w_gu0, w_d0, name="ffn_p").reshape(bp, s_len, d)
    ys = _ffn(ys.reshape(db * tp, d), norm_ffn[0], w_gu0, w_d0, name="ffn_s").reshape(db, tp, d)

    n_heads = len(WINDOWS) * HG
    d_c = n_heads * DH
    gw = HG * DH
    w_qkv = bf(w_qkv_odd[0])
    qk_gain = jnp.concatenate([jnp.tile(q_norm_odd[0], n_heads), jnp.tile(k_norm_odd[0], n_heads),
                               jnp.ones((d_c,), F32)]).reshape(1, 3 * d_c)
    qkv_p = _norm_matmul(yp.reshape(bp * s_len, d), norm_mix[1], w_qkv, head_gain=qk_gain,
                         n_norm_cols=2 * d_c, dh=DH, slabs=True, name="qkv_p")
    qkv_s = _norm_matmul(ys.reshape(db * tp, d), norm_mix[1], w_qkv, head_gain=qk_gain,
                         n_norm_cols=2 * d_c, dh=DH, slabs=True, name="qkv_s")
    gs = gw // V7X_LANES
    w_o = bf(w_o_odd[0])
    outs_p, lses_p, outs_s, lses_s, p_swa, s_swa = [], [], [], [], [], []

    def kv_rows(qkv, n_batch, t_len, g, lo, hi):
        ks = (len(WINDOWS) + g) * gs
        vs = (2 * len(WINDOWS) + g) * gs
        slabs = qkv.reshape(-1, n_batch, t_len, V7X_LANES)
        kv = jnp.stack([slabs[ks:ks + gs, :, lo:hi], slabs[vs:vs + gs, :, lo:hi]])
        return jnp.transpose(kv, (2, 3, 0, 1, 4)).reshape(n_batch, hi - lo, 2, HG, DH)

    for g, dil in enumerate(DILATIONS):
        tab = rel_bias[:, g * HG:(g + 1) * HG]
        o, l = _dilated_prompt(qkv_p, tab, g, dil, bp)
        outs_p.append(o)
        lses_p.append(l)
        cache_t = jnp.transpose(caches[g][0], (0, 2, 3, 4, 1))
        o, l, cache_t = _dilated_sample(qkv_s, cache_t, tab, g, dil, t_dec)
        outs_s.append(o)
        lses_s.append(l)
        s_swa.append(jnp.transpose(cache_t, (0, 4, 1, 2, 3))[None])
        win = min(WINDOWS[g], s_len)
        p_swa.append(kv_rows(qkv_p, bp, s_len, g, s_len - win, s_len)[None])
    yp = _merge_wo(outs_p, lses_p, yp.reshape(bp * s_len, d), w_o, name="merge_wo_p").reshape(bp, s_len, d)
    ys = _merge_wo(outs_s, lses_s, ys.reshape(db * tp, d), w_o, name="merge_wo_s").reshape(db, tp, d)

    yp, ys = cross_attention(1, yp, ys)

    yp2 = yp.reshape(bp * s_len, d)
    ys2 = ys[:, :t_dec].reshape(db * t_dec, d)
    zero_cnt = jnp.zeros((1, N_EXPERTS), F32)
    hn_p, eid_p, gate_p, rank_p, cnt_p = _router(yp2, norm_ffn[1], moe_router_w[0], moe_router_b[0], zero_cnt,
                                                 name="router_p")
    hn_s, eid_s, gate_s, rank_s, cnt_s = _router(ys2, norm_ffn[1], moe_router_w[0], moe_router_b[0], cnt_p,
                                                 name="router_s")
    n_tok = yp2.shape[0] + ys2.shape[0]
    nt = -(-(n_tok * TOP_K) // MOE_TM) + N_EXPERTS
    (pos_p, pos_s), te, n_used = _moe_plan(cnt_s[0].astype(I32), (eid_p, eid_s), (rank_p, rank_s), MOE_TM, nt)
    xs = jnp.zeros((nt * MOE_TM, d), F32)
    xs = _dispatch(hn_p, pos_p, xs, name="dispatch_p")
    xs = _dispatch(hn_s, pos_s, xs, name="dispatch_s")
    ysort = _moe_ffn(xs, te, n_used, moe_w_gu[0], moe_w_down[0])
    yp = _combine(yp2, gate_p, pos_p, ysort, name="moe_combine_p").reshape(bp, s_len, d)
    ys = _combine(ys2, gate_s, pos_s, ysort, name="moe_combine_s").reshape(db, t_dec, d)

    return (yp, ys,
            p_state_conv, p_lru.reshape(1, bp, d_lru),
            p_s5r.reshape(1, bp, s5_g, s5_p), p_s5i.reshape(1, bp, s5_g, s5_p),
            p_swa[0], p_swa[1], p_swa[2], jnp.stack(p_mk), jnp.stack(p_mv),
            s_state_conv, s_lru.reshape(1, db, d_lru),
            s_s5r.reshape(1, db, s5_g, s5_p), s_s5i.reshape(1, db, s5_g, s5_p),
            s_swa[0], s_swa[1], s_swa[2])
```

```python
import functools
import math

import jax
import jax.numpy as jnp
import numpy as np
from jax import lax
from jax.experimental import pallas as pl
from jax.experimental.pallas import tpu as pltpu

F32 = jnp.float32
BF16 = jnp.bfloat16
I32 = jnp.int32

EPS = 1e-6
NEG = -1e30

V7X_SUBLANES = 8
V7X_LANES = 128
V7X_VMEM_BYTES = 64 * 1024 * 1024
VMEM_LIMIT = V7X_VMEM_BYTES - 8 * 1024 * 1024

LRU_C = 8.0
CONV_W = 4
WINDOWS = (128, 512, 2048)
DILATIONS = (1, 4, 16)
HG = 8
DH = 64
N_STRIDE = 128
QB = 128
REL_BUCKETS = 32
REL_MAX_DIST = WINDOWS[-1]
XA_HEADS = 4
N_EXPERTS = 8
TOP_K = 2
SAMPLE_PAD_T = 16

MIX_TC = 256
MOE_TM = 1024
MOE_TH = 512
FFN_TH = 512


def _cparams(sem):
    return pltpu.CompilerParams(dimension_semantics=sem, vmem_limit_bytes=VMEM_LIMIT)


def _const_spec(shape):
    nd = len(shape)
    return pl.BlockSpec(shape, lambda *_: (0,) * nd)


def _rms(x, g):
    return x * lax.rsqrt(jnp.mean(x * x, axis=-1, keepdims=True) + EPS) * g


def _norm_matmul_kernel(x_ref, g_ref, w_ref, hg_ref, hs_ref, o_ref, hn_ref, *, n_norm_tiles, dh, slabs,
                        row_split):
    j = pl.program_id(1)

    @pl.when(j == 0)
    def _():
        hn_ref[...] = _rms(x_ref[...], g_ref[...]).astype(BF16)

    tm = hn_ref.shape[0]
    rs = tm // row_split

    def emit(r, val):
        rows = slice(r * rs, (r + 1) * rs)
        if slabs:
            for c in range(o_ref.shape[0]):
                o_ref[c, rows, :] = val[:, c * V7X_LANES:(c + 1) * V7X_LANES]
        else:
            o_ref[rows, :] = val

    def tile(r, normed):
        y = jnp.dot(hn_ref[r * rs:(r + 1) * rs, :], w_ref[...], preferred_element_type=F32)
        if normed:
            ssq = jnp.dot((y * y).astype(BF16), hs_ref[...], preferred_element_type=F32)
            y = y * lax.rsqrt(ssq * (1.0 / dh) + EPS) * hg_ref[...]
        emit(r, y)

    if n_norm_tiles == 0:
        for r in range(row_split):
            tile(r, False)
    else:
        @pl.when(j < n_norm_tiles)
        def _():
            for r in range(row_split):
                tile(r, True)

        @pl.when(j >= n_norm_tiles)
        def _():
            for r in range(row_split):
                tile(r, False)


def _head_sum_matrix(tn, dh):
    idx = np.arange(tn) // dh
    return jnp.asarray((idx[:, None] == idx[None, :]).astype(np.float32), dtype=BF16)


def _norm_matmul(x, g, w, *, tn=512, head_gain=None, n_norm_cols=0, dh=1, slabs=False, name):
    m, d = x.shape
    n = w.shape[1]
    tm = min(m, 1024)
    assert m % tm == 0 and n % tn == 0 and n_norm_cols % tn == 0
    if head_gain is None:
        head_gain = jnp.ones((1, n), F32)
    hs = _head_sum_matrix(tn, dh)
    row_split = 2 if tm % 512 == 0 else 1
    kern = functools.partial(_norm_matmul_kernel, n_norm_tiles=n_norm_cols // tn, dh=dh, slabs=slabs,
                             row_split=row_split)
    if slabs:
        out_shape = jax.ShapeDtypeStruct((n // V7X_LANES, m, V7X_LANES), F32)
        out_spec = pl.BlockSpec((tn // V7X_LANES, tm, V7X_LANES), lambda i, j: (j, i, 0))
    else:
        out_shape = jax.ShapeDtypeStruct((m, n), F32)
        out_spec = pl.BlockSpec((tm, tn), lambda i, j: (i, j))
    return pl.pallas_call(
        kern,
        out_shape=out_shape,
        grid=(m // tm, n // tn),
        in_specs=[
            pl.BlockSpec((tm, d), lambda i, j: (i, 0)),
            pl.BlockSpec((1, d), lambda i, j: (0, 0)),
            pl.BlockSpec((d, tn), lambda i, j: (0, j)),
            pl.BlockSpec((1, tn), lambda i, j: (0, j)),
            pl.BlockSpec((tn, tn), lambda i, j: (0, 0)),
        ],
        out_specs=out_spec,
        scratch_shapes=[pltpu.VMEM((tm, d), BF16)],
        compiler_params=_cparams(("parallel", "arbitrary")),
        name=name,
    )(x, g.reshape(1, d), w, head_gain, hs)


def _s5_prep_kernel(lre_ref, lim_ref, ldt_ref, bre_ref, bim_ref, bbr_ref, bbi_ref, apow_ref):
    lr = lre_ref[...]
    li = lim_ref[...]
    dt = jnp.exp(ldt_ref[...])
    mag = jnp.exp(lr * dt)
    ab_r = mag * jnp.cos(li * dt)
    ab_i = mag * jnp.sin(li * dt)
    den = lr * lr + li * li
    nr = ab_r - 1.0
    cr = (nr * lr + ab_i * li) / den
    ci = (ab_i * lr - nr * li) / den
    b_r = bre_ref[...]
    b_i = bim_ref[...]
    bbr_ref[...] = (cr * b_r - ci * b_i).astype(BF16)
    bbi_ref[...] = (cr * b_i + ci * b_r).astype(BF16)

    n = lr.shape[1]
    row = lax.broadcasted_iota(I32, (V7X_SUBLANES, n), 0)

    def power(kf):
        mg = jnp.exp(kf * (lr * dt))
        return mg * jnp.cos(kf * (li * dt)), mg * jnp.sin(kf * (li * dt))

    for i, s in enumerate((1, 2, 4)):
        pr, pi = power(jnp.full((V7X_SUBLANES, n), s, F32))
        keep = row >= s
        apow_ref[2 * i] = jnp.where(keep, pr, 0.0)
        apow_ref[2 * i + 1] = jnp.where(keep, pi, 0.0)
    pr, pi = power((row + 1).astype(F32))
    apow_ref[6] = pr
    apow_ref[7] = pi


def _s5_prep(lam_re, lam_im, log_dt, b_re, b_im):
    g, p, h = b_re.shape
    n = g * p
    eye = jnp.eye(g, dtype=F32)
    bre_bd = jnp.einsum("gph,gk->khgp", b_re, eye).reshape(g * h, n)
    bim_bd = jnp.einsum("gph,gk->khgp", b_im, eye).reshape(g * h, n)
    ldt = jnp.broadcast_to(log_dt[:, None], (g, p)).reshape(1, n)
    return pl.pallas_call(
        _s5_prep_kernel,
        out_shape=(
            jax.ShapeDtypeStruct((g * h, n), BF16),
            jax.ShapeDtypeStruct((g * h, n), BF16),
            jax.ShapeDtypeStruct((8, V7X_SUBLANES, n), F32),
        ),
        compiler_params=pltpu.CompilerParams(vmem_limit_bytes=VMEM_LIMIT),
        name="s5_prep",
    )(lam_re.reshape(1, n), lam_im.reshape(1, n), ldt, bre_bd, bim_bd)


def _mixer0_kernel(z_ref, x_ref, convi_ref, h0_ref, sr0_ref, si0_ref,
                   cw_ref, cb_ref, wa_ref, ba_ref, wx_ref, bx_ref, lam_ref,
                   bbr_ref, bbi_ref, apow_ref, ccr_ref, cci_ref, d_ref, wglu_ref, bglu_ref, wout_ref,
                   y_ref, hl_ref, srl_ref, sil_ref,
                   ext_ref, hc_ref, src_ref, sic_ref, xr_ref, xi_ref, ha_ref, hb_ref,
                   *, tc, last_row):
    c = pl.program_id(1)
    d_lru = cw_ref.shape[1]
    d_s5 = d_ref.shape[1]
    n_tiles = tc // V7X_SUBLANES

    @pl.when(c == 0)
    def _():
        ext_ref[...] = convi_ref[...]
        hc_ref[...] = h0_ref[...]
        src_ref[...] = sr0_ref[...]
        sic_ref[...] = si0_ref[...]

    z = z_ref[...]
    xa = z[:, :d_lru]
    ga = z[:, d_lru:2 * d_lru]
    u = z[:, 2 * d_lru:]

    ext = jnp.concatenate([ext_ref[...], xa], axis=0)
    xc = cb_ref[...] + xa * cw_ref[CONV_W - 1:CONV_W, :]
    for s in range(1, CONV_W):
        xc = xc + pltpu.roll(ext, s, 0)[V7X_SUBLANES:, :] * cw_ref[CONV_W - 1 - s:CONV_W - s, :]
    ext_ref[...] = xa[tc - V7X_SUBLANES:, :]

    xcb = xc.astype(BF16)
    r = jax.nn.sigmoid(jnp.dot(xcb, wa_ref[...], preferred_element_type=F32) + ba_ref[...])
    ig = jax.nn.sigmoid(jnp.dot(xcb, wx_ref[...], preferred_element_type=F32) + bx_ref[...])
    lam = lam_ref[...]
    softplus_neg = jnp.maximum(-lam, 0.0) + jnp.log1p(jnp.exp(-jnp.abs(lam)))
    log_a = -LRU_C * r * softplus_neg
    a = jnp.exp(log_a)
    bt = jnp.sqrt(-jnp.tanh(log_a) * (a * a + 1.0)) * ig * xc

    a3 = a.reshape(n_tiles, V7X_SUBLANES, d_lru)
    b3 = bt.reshape(n_tiles, V7X_SUBLANES, d_lru)
    row = lax.broadcasted_iota(I32, (1, V7X_SUBLANES, d_lru), 1)
    for s in (1, 2, 4):
        keep = row >= s
        ar = pltpu.roll(a3, s, 1)
        br = pltpu.roll(b3, s, 1)
        b3 = jnp.where(keep, a3 * br + b3, b3)
        a3 = jnp.where(keep, a3 * ar, a3)
    ha_ref[...] = a3.reshape(tc, d_lru)
    hb_ref[...] = b3.reshape(tc, d_lru)

    def lru_tile(i, carry):
        r0 = pl.multiple_of(i * V7X_SUBLANES, V7X_SUBLANES)
        h = ha_ref[pl.ds(r0, V7X_SUBLANES), :] * carry + hb_ref[pl.ds(r0, V7X_SUBLANES), :]
        hb_ref[pl.ds(r0, V7X_SUBLANES), :] = h
        return h[V7X_SUBLANES - 1:, :]

    hc_ref[...] = lax.fori_loop(0, n_tiles, lru_tile, hc_ref[...])
    hs = hb_ref[...]
    hl_ref[...] = hb_ref[last_row:last_row + 1, :]
    ya = hs * jax.nn.gelu(ga)

    ub = u.astype(BF16)
    n_state = bbr_ref.shape[1]
    n_slab = d_s5 // V7X_LANES
    sw = n_state // n_slab

    def input_dot(w_ref):
        return jnp.concatenate(
            [jnp.dot(ub[:, c * V7X_LANES:(c + 1) * V7X_LANES],
                     w_ref[c * V7X_LANES:(c + 1) * V7X_LANES, c * sw:(c + 1) * sw],
                     preferred_element_type=F32) for c in range(n_slab)], axis=-1)

    xr3 = input_dot(bbr_ref).reshape(n_tiles, V7X_SUBLANES, n_state)
    xi3 = input_dot(bbi_ref).reshape(n_tiles, V7X_SUBLANES, n_state)
    for i, s in enumerate((1, 2, 4)):
        cr = apow_ref[2 * i][None]
        ci = apow_ref[2 * i + 1][None]
        rr = pltpu.roll(xr3, s, 1)
        ri = pltpu.roll(xi3, s, 1)
        xr3, xi3 = xr3 + cr * rr - ci * ri, xi3 + cr * ri + ci * rr
    xr_ref[...] = xr3.reshape(tc, n_state)
    xi_ref[...] = xi3.reshape(tc, n_state)

    def s5_tile(i, carry):
        cr_, ci_ = carry
        r0 = pl.multiple_of(i * V7X_SUBLANES, V7X_SUBLANES)
        pr = apow_ref[6]
        pi = apow_ref[7]
        nr_ = xr_ref[pl.ds(r0, V7X_SUBLANES), :] + pr * cr_ - pi * ci_
        ni_ = xi_ref[pl.ds(r0, V7X_SUBLANES), :] + pr * ci_ + pi * cr_
        xr_ref[pl.ds(r0, V7X_SUBLANES), :] = nr_
        xi_ref[pl.ds(r0, V7X_SUBLANES), :] = ni_
        return nr_[V7X_SUBLANES - 1:, :], ni_[V7X_SUBLANES - 1:, :]

    cr_f, ci_f = lax.fori_loop(0, n_tiles, s5_tile, (src_ref[...], sic_ref[...]))
    src_ref[...] = cr_f
    sic_ref[...] = ci_f
    srl_ref[...] = xr_ref[last_row:last_row + 1, :]
    sil_ref[...] = xi_ref[last_row:last_row + 1, :]

    def output_dot(x_ref_, w_ref):
        return jnp.concatenate(
            [jnp.dot(x_ref_[:, c * sw:(c + 1) * sw].astype(BF16),
                     w_ref[c * sw:(c + 1) * sw, c * V7X_LANES:(c + 1) * V7X_LANES],
                     preferred_element_type=F32) for c in range(n_slab)], axis=-1)

    ys = output_dot(xr_ref, ccr_ref) - output_dot(xi_ref, cci_ref)
    ys = ys + d_ref[...] * u
    gs = jax.nn.gelu(ys)
    yb = gs * jax.nn.sigmoid(jnp.dot(gs.astype(BF16), wglu_ref[...], preferred_element_type=F32)
                             + bglu_ref[...])

    y_ref[...] = (x_ref[...]
                  + jnp.dot(ya.astype(BF16), wout_ref[:d_lru, :], preferred_element_type=F32)
                  + jnp.dot(yb.astype(BF16), wout_ref[d_lru:, :], preferred_element_type=F32))


def _mixer0(z, x, conv_init, h0, sr0, si0, wts, *, tc, last_row):
    bn, t_len, d_in = z.shape
    d = x.shape[2]
    d_lru = wts["cw"].shape[1]
    n_state = wts["bbr"].shape[1]
    assert t_len % tc == 0
    kern = functools.partial(_mixer0_kernel, tc=tc, last_row=last_row)
    wnames = ("cw", "cb", "wa", "ba", "wx", "bx", "lam", "bbr", "bbi", "apow",
              "ccr", "cci", "d", "wglu", "bglu", "wout")
    wlist = [wts[k] for k in wnames]
    per_b = lambda shape: pl.BlockSpec((None,) + shape, lambda b, c: (b,) + (0,) * len(shape))
    in_specs = [
        pl.BlockSpec((None, tc, d_in), lambda b, c: (b, c, 0)),
        pl.BlockSpec((None, tc, d), lambda b, c: (b, c, 0)),
        per_b((V7X_SUBLANES, d_lru)), per_b((1, d_lru)), per_b((1, n_state)), per_b((1, n_state)),
    ] + [_const_spec(w.shape) for w in wlist]
    return pl.pallas_call(
        kern,
        out_shape=(
            jax.ShapeDtypeStruct((bn, t_len, d), F32),
            jax.ShapeDtypeStruct((bn, 1, d_lru), F32),
            jax.ShapeDtypeStruct((bn, 1, n_state), F32),
            jax.ShapeDtypeStruct((bn, 1, n_state), F32),
        ),
        grid=(bn, t_len // tc),
        in_specs=in_specs,
        out_specs=(
            pl.BlockSpec((None, tc, d), lambda b, c: (b, c, 0)),
            per_b((1, d_lru)), per_b((1, n_state)), per_b((1, n_state)),
        ),
        scratch_shapes=[
            pltpu.VMEM((V7X_SUBLANES, d_lru), F32),
            pltpu.VMEM((1, d_lru), F32),
            pltpu.VMEM((1, n_state), F32),
            pltpu.VMEM((1, n_state), F32),
            pltpu.VMEM((tc, n_state), F32),
            pltpu.VMEM((tc, n_state), F32),
            pltpu.VMEM((tc, d_lru), F32),
            pltpu.VMEM((tc, d_lru), F32),
        ],
        compiler_params=_cparams(("parallel", "arbitrary")),
        name="mixer0",
    )(z, x, conv_init, h0, sr0, si0, *wlist)


def _xattn_kernel(x_ref, g_ref, wq_ref, qg_ref, mk_ref, mv_ref, wo_ref, o_ref):
    nb, tm, d = x_ref.shape
    x = x_ref[...].reshape(nb * tm, d)
    q = jnp.dot(_rms(x, g_ref[...]).astype(BF16), wq_ref[...], preferred_element_type=F32)
    dh = qg_ref.shape[1]
    head_major = len(mk_ref.shape) == 4
    rows = []
    for b in range(nb):
        if head_major:
            kb = pltpu.einshape("nhd->hnd", mk_ref[b])
            vb = pltpu.einshape("nhd->hnd", mv_ref[b])
            head_of = lambda arr, h: arr[h]
        else:
            kb, vb = mk_ref[b], mv_ref[b]
            head_of = lambda arr, h: arr[:, h * dh:(h + 1) * dh]
        outs = []
        for h in range(XA_HEADS):
            qn = _rms(q[b * tm:(b + 1) * tm, h * dh:(h + 1) * dh], qg_ref[...]).astype(BF16)
            s = lax.dot_general(qn, head_of(kb, h).astype(BF16), (((1,), (1,)), ((), ())),
                                preferred_element_type=F32) * (dh ** -0.5)
            m = jnp.max(s, axis=-1, keepdims=True)
            p = jnp.exp(s - m)
            den = jnp.sum(p, axis=-1, keepdims=True)
            oh = jnp.dot(p.astype(BF16), head_of(vb, h).astype(BF16), preferred_element_type=F32) / den
            outs.append(oh.astype(BF16))
        rows.append(jnp.concatenate(outs, axis=-1))
    o = rows[0] if nb == 1 else jnp.concatenate(rows, axis=0)
    o_ref[...] = (x + jnp.dot(o, wo_ref[...], preferred_element_type=F32)).reshape(nb, tm, d)


def _xattn(x, g, wq, qg, mk, mv, wo, *, tm, nb=1, name, layer=None):
    bn, t_len, d = x.shape
    assert t_len % tm == 0 and bn % nb == 0
    if layer is None:
        mem_spec = pl.BlockSpec((nb,) + mk.shape[1:], lambda b, i: (b, 0, 0))
    else:
        mem_spec = pl.BlockSpec((None, nb) + mk.shape[2:], lambda b, i: (layer, b, 0, 0, 0))
    return pl.pallas_call(
        _xattn_kernel,
        out_shape=jax.ShapeDtypeStruct(x.shape, F32),
        grid=(bn // nb, t_len // tm),
        in_specs=[
            pl.BlockSpec((nb, tm, d), lambda b, i: (b, i, 0)),
            _const_spec((1, d)),
            _const_spec(wq.shape),
            _const_spec((1, qg.shape[-1])),
            mem_spec,
            mem_spec,
            _const_spec(wo.shape),
        ],
        out_specs=pl.BlockSpec((nb, tm, d), lambda b, i: (b, i, 0)),
        compiler_params=_cparams(("parallel", "arbitrary")),
        name=name,
    )(x, g.reshape(1, d), wq, qg.reshape(1, -1), mk, mv, wo)


def _ffn_kernel(x_ref, g_ref, wg_ref, wu_ref, wd_ref, o_ref, hn_ref, acc_ref):
    j = pl.program_id(1)

    @pl.when(j == 0)
    def _():
        hn_ref[...] = _rms(x_ref[...], g_ref[...]).astype(BF16)
        acc_ref[...] = jnp.zeros_like(acc_ref)

    hn = hn_ref[...]
    gate = jnp.dot(hn, wg_ref[...], preferred_element_type=F32)
    up = jnp.dot(hn, wu_ref[...], preferred_element_type=F32)
    act = (jax.nn.silu(gate) * up).astype(BF16)
    acc_ref[...] += jnp.dot(act, wd_ref[...], preferred_element_type=F32)

    @pl.when(j == pl.num_programs(1) - 1)
    def _():
        o_ref[...] = x_ref[...] + acc_ref[...]


def _ffn(x, g, w_gu, w_d, *, name):
    m, d = x.shape
    hid = w_d.shape[0]
    th = FFN_TH
    tm = min(m, 1024)
    nj = hid // th
    assert m % tm == 0 and hid % th == 0
    return pl.pallas_call(
        _ffn_kernel,
        out_shape=jax.ShapeDtypeStruct((m, d), F32),
        grid=(m // tm, nj),
        in_specs=[
            pl.BlockSpec((tm, d), lambda i, j: (i, 0)),
            pl.BlockSpec((1, d), lambda i, j: (0, 0)),
            pl.BlockSpec((d, th), lambda i, j: (0, j)),
            pl.BlockSpec((d, th), lambda i, j: (0, j + nj)),
            pl.BlockSpec((th, d), lambda i, j: (j, 0)),
        ],
        out_specs=pl.BlockSpec((tm, d), lambda i, j: (i, 0)),
        scratch_shapes=[pltpu.VMEM((tm, d), BF16), pltpu.VMEM((tm, d), F32)],
        compiler_params=_cparams(("parallel", "arbitrary")),
        name=name,
    )(x, g.reshape(1, d), w_gu, w_gu, w_d)


def _rel_bucket_np(dist):
    dist = np.clip(np.asarray(dist), 0, None)
    max_exact = REL_BUCKETS // 2
    safe = np.maximum(dist, max_exact).astype(np.float32)
    large = max_exact + np.floor(np.log(safe / max_exact) / math.log(REL_MAX_DIST / max_exact)
                                 * (REL_BUCKETS - max_exact)).astype(np.int32)
    large = np.minimum(large, REL_BUCKETS - 1)
    return np.where(dist < max_exact, dist, large).astype(np.int32)


def _dil_kernel(q_ref, k_ref, v_ref, bias_ref, o_ref, l_ref, kprev_ref, vprev_ref, *, d, nblk):
    n = pl.program_id(1)

    @pl.when(n == 0)
    def _():
        kprev_ref[...] = jnp.zeros_like(kprev_ref)
        vprev_ref[...] = jnp.zeros_like(vprev_ref)

    first_sel = jnp.minimum(n, 1)
    lane = lax.broadcasted_iota(I32, (QB, V7X_LANES), 1)
    n_slab = q_ref.shape[0]
    heads_per_slab = V7X_LANES // DH

    def residue(r, _):
        prev_k = [kprev_ref[r, c] for c in range(n_slab)]
        prev_v = [vprev_ref[r, c] for c in range(n_slab)]
        for blk in range(nblk):
            rows = pl.ds(blk * QB * d + r, QB, stride=d)
            bsel = first_sel if blk == 0 else 1
            lse = jnp.zeros((QB, V7X_LANES), F32)
            for c in range(n_slab):
                q = (q_ref[c, rows, :] * (DH ** -0.5)).astype(BF16)
                kc = k_ref[c, rows, :].astype(BF16)
                vc = v_ref[c, rows, :].astype(BF16)
                k = jnp.concatenate([prev_k[c], kc], axis=0)
                v = jnp.concatenate([prev_v[c], vc], axis=0)
                outs = []
                for hh in range(heads_per_slab):
                    h = c * heads_per_slab + hh
                    sl = slice(hh * DH, (hh + 1) * DH)
                    s = lax.dot_general(q[:, sl], k[:, sl], (((1,), (1,)), ((), ())),
                                        preferred_element_type=F32) + bias_ref[bsel, h]
                    m = jnp.max(s, axis=-1, keepdims=True)
                    p = jnp.exp(s - m)
                    den = jnp.sum(p, axis=-1, keepdims=True)
                    outs.append(jnp.dot(p.astype(BF16), v[:, sl], preferred_element_type=F32) / den)
                    lse = jnp.where(lane == h, m + jnp.log(den), lse)
                o_ref[c, rows, :] = jnp.concatenate(outs, axis=-1)
                prev_k[c] = kc
                prev_v[c] = vc
            l_ref[rows, :] = lse
        for c in range(n_slab):
            kprev_ref[r, c] = prev_k[c]
            vprev_ref[r, c] = prev_v[c]
        return 0

    lax.fori_loop(0, d, residue, 0, unroll=min(d, 2))


def _dilated_prompt(qkv, tab, g, d, bn):
    n_slabs, m_rows, _ = qkv.shape
    s_len = m_rows // bn
    gw = HG * DH
    gs = gw // V7X_LANES
    nblk = 2 if d == 1 else 1
    chunk = nblk * QB * d
    n_chunks = s_len // chunk
    assert s_len % chunk == 0
    k_col = n_slabs // gs // 3

    qi = np.arange(QB)[:, None]
    ki = np.arange(2 * QB)[None, :]
    dist = qi + QB - ki
    band = (dist >= 0) & (dist <= N_STRIDE)
    first = band & (ki >= QB)
    onehot = (np.arange(REL_BUCKETS)[:, None] == _rel_bucket_np(dist * d).reshape(1, -1)).astype(np.float32)
    bias = jnp.dot(tab.T.astype(F32), jnp.asarray(onehot),
                   precision=lax.Precision.HIGHEST).reshape(HG, QB, 2 * QB)
    bias2 = jnp.stack([jnp.where(first[None], bias, NEG), jnp.where(band[None], bias, NEG)])

    col = lambda section: (lambda b, n: (section * k_col + g, b * n_chunks + n, 0))
    blk = (gs, chunk, V7X_LANES)
    return pl.pallas_call(
        functools.partial(_dil_kernel, d=d, nblk=nblk),
        out_shape=(jax.ShapeDtypeStruct((gs, m_rows, V7X_LANES), F32),
                   jax.ShapeDtypeStruct((m_rows, V7X_LANES), F32)),
        grid=(bn, n_chunks),
        in_specs=[
            pl.BlockSpec(blk, col(0)), pl.BlockSpec(blk, col(1)), pl.BlockSpec(blk, col(2)),
            _const_spec(bias2.shape),
        ],
        out_specs=(pl.BlockSpec(blk, lambda b, n: (0, b * n_chunks + n, 0)),
                   pl.BlockSpec((chunk, V7X_LANES), lambda b, n: (b * n_chunks + n, 0))),
        scratch_shapes=[pltpu.VMEM((d, gs, QB, V7X_LANES), BF16), pltpu.VMEM((d, gs, QB, V7X_LANES), BF16)],
        compiler_params=_cparams(("parallel", "arbitrary")),
        name=f"dilated_prompt_g{g}",
    )(qkv, qkv, qkv, bias2)


def _split_dot(x, w_ref):
    hi = x.astype(BF16)
    lo = (x - hi.astype(F32)).astype(BF16)
    w = w_ref[...]
    return jnp.dot(hi, w, preferred_element_type=F32) + jnp.dot(lo, w, preferred_element_type=F32)


def _dil_sample_kernel(q_ref, k_ref, v_ref, cache_ref, bias_ref, biasn_ref, o_ref, l_ref, cout_ref, *,
                       t_valid, t_pad):
    n_slab = q_ref.shape[0]
    nb = cache_ref.shape[0]
    w = cache_ref.shape[-1]
    lane = lax.broadcasted_iota(I32, (t_pad, V7X_LANES), 1)
    is_new = lax.broadcasted_iota(I32, (DH, V7X_LANES), 1) >= V7X_LANES - t_valid

    def as_last_columns(x):
        shifted = pltpu.roll(x, t_pad - t_valid, 0)
        tile = jnp.concatenate([jnp.zeros((V7X_LANES - t_pad, x.shape[1]), F32), shifted], axis=0)
        return tile.T

    def sequence(b, _):
        rows = pl.ds(pl.multiple_of(b * t_pad, t_pad), t_pad)
        unslab = lambda ref: jnp.concatenate([ref[c, rows, :] for c in range(n_slab)], axis=-1)
        qn = unslab(q_ref)
        kn = unslab(k_ref)
        vn = unslab(v_ref)
        new_cols = (as_last_columns(kn), as_last_columns(vn))
        qb, kb, vb = qn.astype(BF16), kn.astype(BF16), vn.astype(BF16)
        lse = jnp.zeros((t_pad, V7X_LANES), F32)
        outs = []
        for h in range(HG):
            sl = slice(h * DH, (h + 1) * DH)
            k_t = cache_ref[b, 0, h]
            v_t = cache_ref[b, 1, h]
            s = jnp.dot(qb[:, sl], k_t.astype(BF16), preferred_element_type=F32) * (DH ** -0.5) + bias_ref[h]
            sn = lax.dot_general(qb[:, sl], kb[:, sl], (((1,), (1,)), ((), ())),
                                 preferred_element_type=F32) * (DH ** -0.5) + biasn_ref[h]
            m = jnp.maximum(jnp.max(s, axis=-1, keepdims=True), jnp.max(sn, axis=-1, keepdims=True))
            p = jnp.exp(s - m)
            pn = jnp.exp(sn - m)
            den = jnp.sum(p, axis=-1, keepdims=True) + jnp.sum(pn, axis=-1, keepdims=True)
            pv = lax.dot_general(p.astype(BF16), v_t.astype(BF16), (((1,), (1,)), ((), ())),
                                 preferred_element_type=F32)
            outs.append((pv + jnp.dot(pn.astype(BF16), vb[:, sl], preferred_element_type=F32)) / den)
            lse = jnp.where(lane == h, m + jnp.log(den), lse)
            for kv, old in enumerate((k_t, v_t)):
                moved = pltpu.roll(old, w - t_valid, 1)
                cout_ref[b, kv, h] = moved
                cout_ref[b, kv, h, :, w - V7X_LANES:] = jnp.where(is_new, new_cols[kv][sl, :],
                                                                  moved[:, w - V7X_LANES:])
        o = jnp.concatenate(outs, axis=-1)
        for c in range(n_slab):
            o_ref[c, rows, :] = o[:, c * V7X_LANES:(c + 1) * V7X_LANES]
        l_ref[rows, :] = lse
        return 0

    if nb == 1:
        sequence(0, 0)
    else:
        lax.fori_loop(0, nb, sequence, 0)


def _dilated_sample(qkv, cache, tab, g, d, t_valid):
    n_slabs, m_rows, _ = qkv.shape
    db, wb = cache.shape[0], cache.shape[-1]
    t_pad = m_rows // db
    gw = HG * DH
    gs = gw // V7X_LANES
    assert wb == N_STRIDE * d and t_valid <= V7X_LANES and wb % V7X_LANES == 0
    k_col = n_slabs // gs // 3

    def bias_of(j, ok):
        onehot = (np.arange(REL_BUCKETS)[:, None] == _rel_bucket_np(d * j).reshape(1, -1)) & ok.reshape(1, -1)
        looked_up = jnp.dot(tab.T.astype(F32), jnp.asarray(onehot.astype(np.float32)),
                            precision=lax.Precision.HIGHEST).reshape((HG,) + j.shape)
        return jnp.where(ok[None], looked_up, NEG)

    tq = np.arange(t_pad)[:, None]
    pos = np.arange(wb)[None, :]
    jc = (wb + tq - pos) // d
    bias = bias_of(jc, ((wb + tq - pos) % d == 0) & (jc >= 1) & (jc <= N_STRIDE))
    tn = np.arange(t_pad)[None, :]
    jn = (tq - tn) // d
    bias_new = bias_of(jn, (tn <= tq) & ((tq - tn) % d == 0) & (jn <= N_STRIDE) & (tn < t_valid))

    kern = functools.partial(_dil_sample_kernel, t_valid=t_valid, t_pad=t_pad)
    cache_bytes = 4 * math.prod(cache.shape[1:])
    nb = max(1, min(db, (4 * 1024 * 1024) // cache_bytes))
    assert db % nb == 0
    blk = (gs, nb * t_pad, V7X_LANES)
    cblk = pl.BlockSpec((nb,) + cache.shape[1:], lambda b: (b, 0, 0, 0, 0))
    return pl.pallas_call(
        kern,
        out_shape=(jax.ShapeDtypeStruct((gs, m_rows, V7X_LANES), F32),
                   jax.ShapeDtypeStruct((m_rows, V7X_LANES), F32),
                   jax.ShapeDtypeStruct(cache.shape, F32)),
        grid=(db // nb,),
        in_specs=[
            pl.BlockSpec(blk, lambda b: (g, b, 0)),
            pl.BlockSpec(blk, lambda b: (k_col + g, b, 0)),
            pl.BlockSpec(blk, lambda b: (2 * k_col + g, b, 0)),
            cblk, _const_spec(bias.shape), _const_spec(bias_new.shape),
        ],
        out_specs=(pl.BlockSpec(blk, lambda b: (0, b, 0)),
                   pl.BlockSpec((nb * t_pad, V7X_LANES), lambda b: (b, 0)),
                   cblk),
        compiler_params=_cparams(("parallel",)),
        name=f"dilated_sample_g{g}",
    )(qkv, qkv, qkv, cache, bias, bias_new)


def _merge_wo_kernel(o0_ref, o1_ref, o2_ref, l0_ref, l1_ref, l2_ref, x_ref, hexp_ref, w_ref, out_ref):
    l0, l1, l2 = l0_ref[...], l1_ref[...], l2_ref[...]
    m = jnp.maximum(jnp.maximum(l0, l1), l2)
    e0, e1, e2 = jnp.exp(l0 - m), jnp.exp(l1 - m), jnp.exp(l2 - m)
    inv = 1.0 / (e0 + e1 + e2)
    unslab = lambda ref: jnp.concatenate([ref[c] for c in range(ref.shape[0])], axis=-1)
    o = (_split_dot(e0 * inv, hexp_ref) * unslab(o0_ref) + _split_dot(e1 * inv, hexp_ref) * unslab(o1_ref)
         + _split_dot(e2 * inv, hexp_ref) * unslab(o2_ref))
    out_ref[...] = x_ref[...] + jnp.dot(o.astype(BF16), w_ref[...], preferred_element_type=F32)


def _merge_wo(outs, lses, x, w, *, name):
    m, d = x.shape
    gw = w.shape[0]
    tm = min(m, 512)
    assert m % tm == 0
    head = np.arange(gw) // DH
    hexp = jnp.asarray((np.arange(V7X_LANES)[:, None] == head[None, :]).astype(np.float32), dtype=BF16)
    row = lambda width: pl.BlockSpec((tm, width), lambda i: (i, 0))
    slab = pl.BlockSpec((gw // V7X_LANES, tm, V7X_LANES), lambda i: (0, i, 0))
    return pl.pallas_call(
        _merge_wo_kernel,
        out_shape=jax.ShapeDtypeStruct((m, d), F32),
        grid=(m // tm,),
        in_specs=[slab] * 3 + [row(V7X_LANES)] * 3 + [row(d), _const_spec(hexp.shape), _const_spec(w.shape)],
        out_specs=row(d),
        compiler_params=_cparams(("parallel",)),
        name=name,
    )(*outs, *lses, x, hexp, w)


def _router_kernel(x_ref, g_ref, wr_ref, br_ref, cnt0_ref, tri_ref,
                   hn_ref, eid_ref, gate_ref, rank_ref, cnt_ref, run_ref):
    @pl.when(pl.program_id(0) == 0)
    def _():
        run_ref[...] = cnt0_ref[...]

    hn = _rms(x_ref[...], g_ref[...])
    hn_ref[...] = hn
    wr = wr_ref[...]
    h_hi = hn.astype(BF16)
    h_lo = (hn - h_hi.astype(F32)).astype(BF16)
    w_hi = wr.astype(BF16)
    w_lo = (wr - w_hi.astype(F32)).astype(BF16)
    logits = (jnp.dot(h_hi, w_hi, preferred_element_type=F32) + jnp.dot(h_hi, w_lo, preferred_element_type=F32)
              + jnp.dot(h_lo, w_hi, preferred_element_type=F32)) + br_ref[...]
    ne = logits.shape[1]
    lane = lax.broadcasted_iota(I32, logits.shape, 1)
    m1 = jnp.max(logits, axis=-1, keepdims=True)
    i1 = jnp.min(jnp.where(logits == m1, lane, ne), axis=-1, keepdims=True)
    rest = jnp.where(lane == i1, -jnp.inf, logits)
    m2 = jnp.max(rest, axis=-1, keepdims=True)
    i2 = jnp.min(jnp.where(rest == m2, lane, ne), axis=-1, keepdims=True)
    e2 = jnp.exp(m2 - m1)
    g1 = 1.0 / (1.0 + e2)
    g2 = e2 / (1.0 + e2)
    eid_ref[...] = jnp.where(lane == 0, i1, jnp.where(lane == 1, i2, 0))
    gate_ref[...] = jnp.where(lane == 0, g1, jnp.where(lane == 1, g2, 0.0))
    chosen = jnp.logical_or(lane == i1, lane == i2)
    before = jnp.dot(tri_ref[...], jnp.where(chosen, 1.0, 0.0).astype(BF16),
                     preferred_element_type=F32) + run_ref[...]
    r1 = jnp.sum(jnp.where(lane == i1, before, 0.0), axis=-1, keepdims=True)
    r2 = jnp.sum(jnp.where(lane == i2, before, 0.0), axis=-1, keepdims=True)
    rank_ref[...] = jnp.where(lane == 0, r1, jnp.where(lane == 1, r2, 0.0)).astype(I32)
    run_ref[...] += jnp.sum(jnp.where(chosen, 1.0, 0.0), axis=0, keepdims=True)
    cnt_ref[...] = run_ref[...]


def _router(x, g, wr, br, cnt0, *, name):
    m, d = x.shape
    ne = wr.shape[1]
    tm = min(m, 512)
    assert m % tm == 0
    tri = jnp.asarray(np.tril(np.ones((tm, tm), np.float32), -1), dtype=BF16)
    row = lambda width: pl.BlockSpec((tm, width), lambda i: (i, 0))
    return pl.pallas_call(
        _router_kernel,
        out_shape=(jax.ShapeDtypeStruct((m, d), F32),
                   jax.ShapeDtypeStruct((m, ne), I32),
                   jax.ShapeDtypeStruct((m, ne), F32),
                   jax.ShapeDtypeStruct((m, ne), I32),
                   jax.ShapeDtypeStruct((1, ne), F32)),
        grid=(m // tm,),
        in_specs=[row(d), _const_spec((1, d)), _const_spec(wr.shape), _const_spec((1, ne)),
                  _const_spec((1, ne)), _const_spec((tm, tm))],
        out_specs=(row(d), row(ne), row(ne), row(ne), _const_spec((1, ne))),
        scratch_shapes=[pltpu.VMEM((1, ne), F32)],
        compiler_params=_cparams(("arbitrary",)),
        name=name,
    )(x, g.reshape(1, d), wr, br.reshape(1, ne), cnt0, tri)


def _dispatch_kernel(pos_ref, hn_ref, xs_in_ref, xs_ref, stage_ref, sem, *, tmd):
    del xs_in_ref
    i = pl.program_id(0)
    n = pl.num_programs(0)
    slot = i % 2

    def wait_slot(s):
        pltpu.make_async_copy(xs_ref.at[pl.ds(0, TOP_K * tmd)], xs_ref.at[pl.ds(0, TOP_K * tmd)],
                              sem.at[s]).wait()

    @pl.when(i >= 2)
    def _():
        wait_slot(slot)

    stage_ref[slot] = hn_ref[...]

    def start(r, _):
        for k in range(TOP_K):
            pltpu.make_async_copy(stage_ref.at[slot, pl.ds(r, 1)],
                                  xs_ref.at[pl.ds(pos_ref[0, 0, TOP_K * r + k], 1)], sem.at[slot]).start()
        return 0

    lax.fori_loop(0, tmd, start, 0, unroll=8)

    @pl.when(i == n - 1)
    def _():
        wait_slot(slot)

        @pl.when(n >= 2)
        def _():
            wait_slot(1 - slot)


def _dispatch(hn, pos, xs, *, name):
    m, d = hn.shape
    tmd = min(m, 256)
    assert m % tmd == 0
    kern = functools.partial(_dispatch_kernel, tmd=tmd)
    return pl.pallas_call(
        kern,
        out_shape=jax.ShapeDtypeStruct(xs.shape, xs.dtype),
        grid=(m // tmd,),
        in_specs=[
            pl.BlockSpec((1, 1, TOP_K * tmd), lambda i: (i, 0, 0), memory_space=pltpu.SMEM),
            pl.BlockSpec((tmd, d), lambda i: (i, 0)),
            pl.BlockSpec(memory_space=pl.ANY),
        ],
        out_specs=pl.BlockSpec(memory_space=pl.ANY),
        scratch_shapes=[pltpu.VMEM((2, tmd, d), F32), pltpu.SemaphoreType.DMA((2,))],
        input_output_aliases={2: 0},
        compiler_params=_cparams(("arbitrary",)),
        name=name,
    )(pos.reshape(m // tmd, 1, TOP_K * tmd), hn, xs)


def _moe_ffn_kernel(te_ref, nused_ref, x_ref, wg_ref, wu_ref, wd_ref, o_ref, xb_ref, acc_ref):
    t = pl.program_id(0)
    j = pl.program_id(1)
    last = pl.num_programs(1) - 1
    used = t < nused_ref[0]

    @pl.when(used)
    def _():
        @pl.when(j == 0)
        def _():
            xb_ref[...] = x_ref[...].astype(BF16)
            acc_ref[...] = jnp.zeros_like(acc_ref)

        xb = xb_ref[...]
        gate = jnp.dot(xb, wg_ref[...].astype(BF16), preferred_element_type=F32)
        up = jnp.dot(xb, wu_ref[...].astype(BF16), preferred_element_type=F32)
        act = (jax.nn.silu(gate) * up).astype(BF16)
        acc_ref[...] += jnp.dot(act, wd_ref[...].astype(BF16), preferred_element_type=F32)

        @pl.when(j == last)
        def _():
            o_ref[...] = acc_ref[...]

    @pl.when(jnp.logical_and(jnp.logical_not(used), j == last))
    def _():
        o_ref[...] = jnp.zeros_like(o_ref)


def _moe_ffn(xs, te, n_used, w_gu, w_d):
    n_rows, d = xs.shape
    hid = w_d.shape[1]
    tm, th = MOE_TM, MOE_TH
    nj = hid // th
    nt = n_rows // tm
    assert hid % th == 0 and n_rows % tm == 0

    def jj(t, j, nu):
        return jnp.where(t < nu[0], j, nj - 1)

    grid_spec = pltpu.PrefetchScalarGridSpec(
        num_scalar_prefetch=2,
        grid=(nt, nj),
        in_specs=[
            pl.BlockSpec((tm, d), lambda t, j, te, nu: (jnp.maximum(jnp.minimum(t, nu[0] - 1), 0), 0)),
            pl.BlockSpec((None, d, th), lambda t, j, te, nu: (te[t], 0, jj(t, j, nu))),
            pl.BlockSpec((None, d, th), lambda t, j, te, nu: (te[t], 0, jj(t, j, nu) + nj)),
            pl.BlockSpec((None, th, d), lambda t, j, te, nu: (te[t], jj(t, j, nu), 0)),
        ],
        out_specs=pl.BlockSpec((tm, d), lambda t, j, te, nu: (t, 0)),
        scratch_shapes=[pltpu.VMEM((tm, d), BF16), pltpu.VMEM((tm, d), F32)],
    )
    return pl.pallas_call(
        _moe_ffn_kernel,
        out_shape=jax.ShapeDtypeStruct((n_rows, d), F32),
        grid_spec=grid_spec,
        compiler_params=_cparams(("arbitrary", "arbitrary")),
        name="moe_experts",
    )(te, n_used, xs, w_gu, w_gu, w_d)


def _combine_kernel(pos_ref, pos_next_ref, x_ref, gate_ref, ys_ref, o_ref, buf_ref, sem, *, tmc):
    i = pl.program_id(0)
    n = pl.num_programs(0)
    slot = i % 2

    def gather(p_ref, s):
        def start(r, _):
            for k in range(TOP_K):
                pltpu.make_async_copy(ys_ref.at[pl.ds(p_ref[0, 0, TOP_K * r + k], 1)],
                                      buf_ref.at[s, k, pl.ds(r, 1)], sem.at[s]).start()
            return 0

        lax.fori_loop(0, tmc, start, 0, unroll=8)

    @pl.when(i == 0)
    def _():
        gather(pos_ref, slot)

    @pl.when(i + 1 < n)
    def _():
        gather(pos_next_ref, 1 - slot)

    pltpu.make_async_copy(ys_ref.at[pl.ds(0, TOP_K * tmc)], ys_ref.at[pl.ds(0, TOP_K * tmc)], sem.at[slot]).wait()
    g = gate_ref[...]
    acc = x_ref[...]
    for k in range(TOP_K):
        acc = acc + g[:, k:k + 1] * buf_ref[slot, k]
    o_ref[...] = acc


def _combine(x, gate, pos, ys, *, name):
    m, d = x.shape
    ne = gate.shape[1]
    tmc = min(m, 256)
    assert m % tmc == 0
    kern = functools.partial(_combine_kernel, tmc=tmc)
    n_tiles = m // tmc
    pos3 = pos.reshape(n_tiles, 1, TOP_K * tmc)
    return pl.pallas_call(
        kern,
        out_shape=jax.ShapeDtypeStruct((m, d), F32),
        grid=(n_tiles,),
        in_specs=[
            pl.BlockSpec((1, 1, TOP_K * tmc), lambda i: (i, 0, 0), memory_space=pltpu.SMEM),
            pl.BlockSpec((1, 1, TOP_K * tmc), lambda i: (jnp.minimum(i + 1, n_tiles - 1), 0, 0),
                         memory_space=pltpu.SMEM),
            pl.BlockSpec((tmc, d), lambda i: (i, 0)),
            pl.BlockSpec((tmc, ne), lambda i: (i, 0)),
            pl.BlockSpec(memory_space=pl.ANY),
        ],
        out_specs=pl.BlockSpec((tmc, d), lambda i: (i, 0)),
        scratch_shapes=[pltpu.VMEM((2, TOP_K, tmc, d), F32), pltpu.SemaphoreType.DMA((2,))],
        compiler_params=_cparams(("arbitrary",)),
        name=name,
    )(pos3, pos3, x, gate, ys)


def _moe_plan(counts, eids, ranks, tm, nt):
    padded = ((counts + tm - 1) // tm) * tm
    ends = jnp.cumsum(padded)
    starts = ends - padded

    def positions(eid, rank):
        start = jnp.zeros_like(rank)
        for e in range(N_EXPERTS):
            start = start + jnp.where(eid == e, starts[e], 0)
        return start + rank

    pos = [positions(e[:, :TOP_K], r[:, :TOP_K]) for e, r in zip(eids, ranks)]
    n_used = (ends[-1] // tm).astype(I32)
    tile = jnp.arange(nt, dtype=I32)
    first_row = jnp.minimum(tile, n_used - 1) * tm
    te = jnp.minimum(jnp.sum((first_row[:, None] >= ends[None, :]).astype(I32), axis=1), N_EXPERTS - 1)
    return pos, te, n_used.reshape(1)


def _block_diag(w):
    n, a, b = w.shape
    return jnp.einsum("nij,nm->nimj", w, jnp.eye(n, dtype=w.dtype)).reshape(n * a, n * b)


def kernel(x_prompt, x_sample, state_conv, state_lru, state_s5_re, state_s5_im, cache_swa0_kv, cache_swa1_kv, cache_swa2_kv, cache_mem_k, cache_mem_v, mem_prompt, norm_mix, norm_xa, norm_ffn, norm_mem, w_in_even, conv_w, conv_b, lru_wa, lru_ba, lru_wx, lru_bx, lru_lam, s5_lam_re, s5_lam_im, s5_log_dt, s5_b_re, s5_b_im, s5_c_re, s5_c_im, s5_d, s5_w_glu, s5_b_glu, w_out_even, w_qkv_odd, q_norm_odd, k_norm_odd, w_o_odd, rel_bias, xa_wq, xa_wkv, xa_qn, xa_kn, xa_wo, ffn_w_gu, ffn_w_down, moe_router_w, moe_router_b, moe_w_gu, moe_w_down):
    bp, s_len, d = x_prompt.shape
    db, t_dec, _ = x_sample.shape
    n_mem = mem_prompt.shape[1]
    tp = SAMPLE_PAD_T
    d_lru = conv_w.shape[-1]
    s5_g, s5_p, s5_h = s5_b_re.shape[1:]
    n_state = s5_g * s5_p
    d_s5 = s5_g * s5_h
    xa_dh = d // XA_HEADS
    caches = (cache_swa0_kv, cache_swa1_kv, cache_swa2_kv)
    bf = lambda w: w.astype(BF16)

    yp = x_prompt
    ys = jnp.pad(x_sample, ((0, 0), (0, tp - t_dec), (0, 0)))

    w_in = bf(w_in_even[0])
    bbr, bbi, apow = _s5_prep(s5_lam_re[0], s5_lam_im[0], s5_log_dt[0], s5_b_re[0], s5_b_im[0])
    eye_g = jnp.eye(s5_g, dtype=F32)
    mix_w = dict(
        cw=conv_w[0], cb=conv_b[0].reshape(1, d_lru),
        wa=bf(_block_diag(lru_wa[0])), ba=lru_ba[0].reshape(1, d_lru),
        wx=bf(_block_diag(lru_wx[0])), bx=lru_bx[0].reshape(1, d_lru),
        lam=lru_lam[0].reshape(1, d_lru),
        bbr=bbr, bbi=bbi, apow=apow,
        ccr=bf(jnp.einsum("ghp,gk->gpkh", s5_c_re[0], eye_g).reshape(n_state, d_s5)),
        cci=bf(jnp.einsum("ghp,gk->gpkh", s5_c_im[0], eye_g).reshape(n_state, d_s5)),
        d=s5_d[0].reshape(1, d_s5), wglu=bf(s5_w_glu[0]), bglu=s5_b_glu[0].reshape(1, d_s5),
        wout=bf(w_out_even[0]),
    )
    z_p = _norm_matmul(yp.reshape(bp * s_len, d), norm_mix[0], w_in, name="in_proj_p").reshape(bp, s_len, -1)
    yp, p_lru, p_s5r, p_s5i = _mixer0(
        z_p, yp, jnp.zeros((bp, V7X_SUBLANES, d_lru), F32), jnp.zeros((bp, 1, d_lru), F32),
        jnp.zeros((bp, 1, n_state), F32), jnp.zeros((bp, 1, n_state), F32), mix_w,
        tc=MIX_TC, last_row=MIX_TC - 1)
    z_s = _norm_matmul(ys.reshape(db * tp, d), norm_mix[0], w_in, name="in_proj_s").reshape(db, tp, -1)
    conv_init = jnp.pad(state_conv[0], ((0, 0), (V7X_SUBLANES - (CONV_W - 1), 0), (0, 0)))
    ys, s_lru, s_s5r, s_s5i = _mixer0(
        z_s, ys, conv_init, state_lru[0].reshape(db, 1, d_lru),
        state_s5_re[0].reshape(db, 1, n_state), state_s5_im[0].reshape(db, 1, n_state), mix_w,
        tc=tp, last_row=t_dec - 1)
    p_state_conv = z_p[:, s_len - (CONV_W - 1):, :d_lru][None]
    s_state_conv = z_s[:, t_dec - (CONV_W - 1):t_dec, :d_lru][None]

    p_mk, p_mv = [], []

    def cross_attention(layer, yp, ys):
        kn_gain = jnp.concatenate([jnp.tile(xa_kn[layer], XA_HEADS), jnp.ones((d,), F32)]).reshape(1, 2 * d)
        kv = _norm_matmul(mem_prompt.reshape(bp * n_mem, d), norm_mem[layer], bf(xa_wkv[layer]),
                          head_gain=kn_gain, n_norm_cols=d, dh=xa_dh, name=f"mem_kv{layer}")
        kv = kv.reshape(bp, n_mem, 2 * d)
        mk, mv = kv[:, :, :d], kv[:, :, d:]
        p_mk.append(mk.reshape(bp, n_mem, XA_HEADS, xa_dh))
        p_mv.append(mv.reshape(bp, n_mem, XA_HEADS, xa_dh))
        wq, wo = bf(xa_wq[layer]), bf(xa_wo[layer])
        yp = _xattn(yp, norm_xa[layer], wq, xa_qn[layer], mk, mv, wo, tm=512, name=f"xattn_p{layer}")
        ys = _xattn(ys, norm_xa[layer], wq, xa_qn[layer], cache_mem_k, cache_mem_v, wo,
                    tm=tp, nb=4, name=f"xattn_s{layer}", layer=layer)
        return yp, ys

    yp, ys = cross_attention(0, yp, ys)
    w_gu0, w_d0 = bf(ffn_w_gu[0]), bf(ffn_w_down[0])
    yp = _ffn(yp.reshape(bp * s_len, d), norm_ffn[0], w_gu0, w_d0, name="ffn_p").reshape(bp, s_len, d)
    ys = _ffn(ys.reshape(db * tp, d), norm_ffn[0], w_gu0, w_d0, name="ffn_s").reshape(db, tp, d)

    n_heads = len(WINDOWS) * HG
    d_c = n_heads * DH
    gw = HG * DH
    w_qkv = bf(w_qkv_odd[0])
    qk_gain = jnp.concatenate([jnp.tile(q_norm_odd[0], n_heads), jnp.tile(k_norm_odd[0], n_heads),
                               jnp.ones((d_c,), F32)]).reshape(1, 3 * d_c)
    qkv_p = _norm_matmul(yp.reshape(bp * s_len, d), norm_mix[1], w_qkv, head_gain=qk_gain,
                         n_norm_cols=2 * d_c, dh=DH, slabs=True, name="qkv_p")
    qkv_s = _norm_matmul(ys.reshape(db * tp, d), norm_mix[1], w_qkv, head_gain=qk_gain,
                         n_norm_cols=2 * d_c, dh=DH, slabs=True, name="qkv_s")
    gs = gw // V7X_LANES
    w_o = bf(w_o_odd[0])
    outs_p, lses_p, outs_s, lses_s, p_swa, s_swa = [], [], [], [], [], []

    def kv_rows(qkv, n_batch, t_len, g, lo, hi):
        ks = (len(WINDOWS) + g) * gs
        vs = (2 * len(WINDOWS) + g) * gs
        slabs = qkv.reshape(-1, n_batch, t_len, V7X_LANES)
        kv = jnp.stack([slabs[ks:ks + gs, :, lo:hi], slabs[vs:vs + gs, :, lo:hi]])
        return jnp.transpose(kv, (2, 3, 0, 1, 4)).reshape(n_batch, hi - lo, 2, HG, DH)

    for g, dil in enumerate(DILATIONS):
        tab = rel_bias[:, g * HG:(g + 1) * HG]
        o, l = _dilated_prompt(qkv_p, tab, g, dil, bp)
        outs_p.append(o)
        lses_p.append(l)
        cache_t = jnp.transpose(caches[g][0], (0, 2, 3, 4, 1))
        o, l, cache_t = _dilated_sample(qkv_s, cache_t, tab, g, dil, t_dec)
        outs_s.append(o)
        lses_s.append(l)
        s_swa.append(jnp.transpose(cache_t, (0, 4, 1, 2, 3))[None])
        win = min(WINDOWS[g], s_len)
        p_swa.append(kv_rows(qkv_p, bp, s_len, g, s_len - win, s_len)[None])
    yp = _merge_wo(outs_p, lses_p, yp.reshape(bp * s_len, d), w_o, name="merge_wo_p").reshape(bp, s_len, d)
    ys = _merge_wo(outs_s, lses_s, ys.reshape(db * tp, d), w_o, name="merge_wo_s").reshape(db, tp, d)

    yp, ys = cross_attention(1, yp, ys)

    yp2 = yp.reshape(bp * s_len, d)
    ys2 = ys[:, :t_dec].reshape(db * t_dec, d)
    zero_cnt = jnp.zeros((1, N_EXPERTS), F32)
    hn_p, eid_p, gate_p, rank_p, cnt_p = _router(yp2, norm_ffn[1], moe_router_w[0], moe_router_b[0], zero_cnt,
                                                 name="router_p")
    hn_s, eid_s, gate_s, rank_s, cnt_s = _router(ys2, norm_ffn[1], moe_router_w[0], moe_router_b[0], cnt_p,
                                                 name="router_s")
    n_tok = yp2.shape[0] + ys2.shape[0]
    nt = -(-(n_tok * TOP_K) // MOE_TM) + N_EXPERTS
    (pos_p, pos_s), te, n_used = _moe_plan(cnt_s[0].astype(I32), (eid_p, eid_s), (rank_p, rank_s), MOE_TM, nt)
    xs = jnp.zeros((nt * MOE_TM, d), F32)
    xs = _dispatch(hn_p, pos_p, xs, name="dispatch_p")
    xs = _dispatch(hn_s, pos_s, xs, name="dispatch_s")
    ysort = _moe_ffn(xs, te, n_used, moe_w_gu[0], moe_w_down[0])
    yp = _combine(yp2, gate_p, pos_p, ysort, name="moe_combine_p").reshape(bp, s_len, d)
    ys = _combine(ys2, gate_s, pos_s, ysort, name="moe_combine_s").reshape(db, t_dec, d)

    return (yp, ys,
            p_state_conv, p_lru.reshape(1, bp, d_lru),
            p_s5r.reshape(1, bp, s5_g, s5_p), p_s5i.reshape(1, bp, s5_g, s5_p),
            p_swa[0], p_swa[1], p_swa[2], jnp.stack(p_mk), jnp.stack(p_mv),
            s_state_conv, s_lru.reshape(1, db, d_lru),
            s_s5r.reshape(1, db, s5_g, s5_p), s_s5i.reshape(1, db, s5_g, s5_p),
            s_swa[0], s_swa[1], s_swa[2])
```

```python
import functools
import math

import jax
import jax.numpy as jnp
import numpy as np
from jax import lax
from jax.experimental import pallas as pl
from jax.experimental.pallas import tpu as pltpu

F32 = jnp.float32
BF16 = jnp.bfloat16
I32 = jnp.int32

EPS = 1e-6
NEG = -1e30

V7X_SUBLANES = 8
V7X_LANES = 128
V7X_VMEM_BYTES = 64 * 1024 * 1024
VMEM_LIMIT = V7X_VMEM_BYTES - 8 * 1024 * 1024

LRU_C = 8.0
CONV_W = 4
WINDOWS = (128, 512, 2048)
DILATIONS = (1, 4, 16)
HG = 8
DH = 64
N_STRIDE = 128
QB = 128
REL_BUCKETS = 32
REL_MAX_DIST = WINDOWS[-1]
XA_HEADS = 4
N_EXPERTS = 8
TOP_K = 2
SAMPLE_PAD_T = 16

MIX_TC = 256
MOE_TM = 1024
MOE_TH = 512
MOE_TOKEN_TILE = 256
FFN_TH = 512


def _cparams(sem):
    return pltpu.CompilerParams(dimension_semantics=sem, vmem_limit_bytes=VMEM_LIMIT)


def _const_spec(shape):
    nd = len(shape)
    return pl.BlockSpec(shape, lambda *_: (0,) * nd)


def _rms(x, g):
    return x * lax.rsqrt(jnp.mean(x * x, axis=-1, keepdims=True) + EPS) * g


def _norm_matmul_kernel(x_ref, g_ref, w_ref, hg_ref, hs_ref, o_ref, hn_ref, *, n_norm_tiles, dh, slabs,
                        row_split):
    j = pl.program_id(1)

    @pl.when(j == 0)
    def _():
        hn_ref[...] = _rms(x_ref[...], g_ref[...]).astype(BF16)

    tm = hn_ref.shape[0]
    rs = tm // row_split

    def emit(r, val):
        rows = slice(r * rs, (r + 1) * rs)
        if slabs:
            for c in range(o_ref.shape[0]):
                o_ref[c, rows, :] = val[:, c * V7X_LANES:(c + 1) * V7X_LANES]
        else:
            o_ref[rows, :] = val

    def tile(r, normed):
        y = jnp.dot(hn_ref[r * rs:(r + 1) * rs, :], w_ref[...], preferred_element_type=F32)
        if normed:
            ssq = jnp.dot((y * y).astype(BF16), hs_ref[...], preferred_element_type=F32)
            y = y * lax.rsqrt(ssq * (1.0 / dh) + EPS) * hg_ref[...]
        emit(r, y)

    if n_norm_tiles == 0:
        for r in range(row_split):
            tile(r, False)
    else:
        @pl.when(j < n_norm_tiles)
        def _():
            for r in range(row_split):
                tile(r, True)

        @pl.when(j >= n_norm_tiles)
        def _():
            for r in range(row_split):
                tile(r, False)


def _head_sum_matrix(tn, dh):
    idx = np.arange(tn) // dh
    return jnp.asarray((idx[:, None] == idx[None, :]).astype(np.float32), dtype=BF16)


def _norm_matmul(x, g, w, *, tn=512, head_gain=None, n_norm_cols=0, dh=1, slabs=False, name):
    m, d = x.shape
    n = w.shape[1]
    tm = min(m, 1024)
    assert m % tm == 0 and n % tn == 0 and n_norm_cols % tn == 0
    if head_gain is None:
        head_gain = jnp.ones((1, n), F32)
    hs = _head_sum_matrix(tn, dh)
    row_split = 2 if tm % 512 == 0 else 1
    kern = functools.partial(_norm_matmul_kernel, n_norm_tiles=n_norm_cols // tn, dh=dh, slabs=slabs,
                             row_split=row_split)
    if slabs:
        out_shape = jax.ShapeDtypeStruct((n // V7X_LANES, m, V7X_LANES), F32)
        out_spec = pl.BlockSpec((tn // V7X_LANES, tm, V7X_LANES), lambda i, j: (j, i, 0))
    else:
        out_shape = jax.ShapeDtypeStruct((m, n), F32)
        out_spec = pl.BlockSpec((tm, tn), lambda i, j: (i, j))
    return pl.pallas_call(
        kern,
        out_shape=out_shape,
        grid=(m // tm, n // tn),
        in_specs=[
            pl.BlockSpec((tm, d), lambda i, j: (i, 0)),
            pl.BlockSpec((1, d), lambda i, j: (0, 0)),
            pl.BlockSpec((d, tn), lambda i, j: (0, j)),
            pl.BlockSpec((1, tn), lambda i, j: (0, j)),
            pl.BlockSpec((tn, tn), lambda i, j: (0, 0)),
        ],
        out_specs=out_spec,
        scratch_shapes=[pltpu.VMEM((tm, d), BF16)],
        compiler_params=_cparams(("parallel", "arbitrary")),
        name=name,
    )(x, g.reshape(1, d), w, head_gain, hs)


def _s5_prep_kernel(lre_ref, lim_ref, ldt_ref, bre_ref, bim_ref, bbr_ref, bbi_ref, apow_ref):
    lr = lre_ref[...]
    li = lim_ref[...]
    dt = jnp.exp(ldt_ref[...])
    mag = jnp.exp(lr * dt)
    ab_r = mag * jnp.cos(li * dt)
    ab_i = mag * jnp.sin(li * dt)
    den = lr * lr + li * li
    nr = ab_r - 1.0
    cr = (nr * lr + ab_i * li) / den
    ci = (ab_i * lr - nr * li) / den
    b_r = bre_ref[...]
    b_i = bim_ref[...]
    bbr_ref[...] = (cr * b_r - ci * b_i).astype(BF16)
    bbi_ref[...] = (cr * b_i + ci * b_r).astype(BF16)

    n = lr.shape[1]
    row = lax.broadcasted_iota(I32, (V7X_SUBLANES, n), 0)

    def power(kf):
        mg = jnp.exp(kf * (lr * dt))
        return mg * jnp.cos(kf * (li * dt)), mg * jnp.sin(kf * (li * dt))

    for i, s in enumerate((1, 2, 4)):
        pr, pi = power(jnp.full((V7X_SUBLANES, n), s, F32))
        keep = row >= s
        apow_ref[2 * i] = jnp.where(keep, pr, 0.0)
        apow_ref[2 * i + 1] = jnp.where(keep, pi, 0.0)
    pr, pi = power((row + 1).astype(F32))
    apow_ref[6] = pr
    apow_ref[7] = pi


def _s5_prep(lam_re, lam_im, log_dt, b_re, b_im):
    g, p, h = b_re.shape
    n = g * p
    eye = jnp.eye(g, dtype=F32)
    bre_bd = jnp.einsum("gph,gk->khgp", b_re, eye).reshape(g * h, n)
    bim_bd = jnp.einsum("gph,gk->khgp", b_im, eye).reshape(g * h, n)
    ldt = jnp.broadcast_to(log_dt[:, None], (g, p)).reshape(1, n)
    return pl.pallas_call(
        _s5_prep_kernel,
        out_shape=(
            jax.ShapeDtypeStruct((g * h, n), BF16),
            jax.ShapeDtypeStruct((g * h, n), BF16),
            jax.ShapeDtypeStruct((8, V7X_SUBLANES, n), F32),
        ),
        compiler_params=pltpu.CompilerParams(vmem_limit_bytes=VMEM_LIMIT),
        name="s5_prep",
    )(lam_re.reshape(1, n), lam_im.reshape(1, n), ldt, bre_bd, bim_bd)


def _mixer0_kernel(z_ref, x_ref, convi_ref, h0_ref, sr0_ref, si0_ref,
                   cw_ref, cb_ref, wa_ref, ba_ref, wx_ref, bx_ref, lam_ref,
                   bbr_ref, bbi_ref, apow_ref, ccr_ref, cci_ref, d_ref, wglu_ref, bglu_ref, wout_ref,
                   y_ref, hl_ref, srl_ref, sil_ref,
                   ext_ref, hc_ref, src_ref, sic_ref, xr_ref, xi_ref, ha_ref, hb_ref,
                   *, tc, last_row):
    c = pl.program_id(1)
    d_lru = cw_ref.shape[1]
    d_s5 = d_ref.shape[1]
    n_tiles = tc // V7X_SUBLANES

    @pl.when(c == 0)
    def _():
        ext_ref[...] = convi_ref[...]
        hc_ref[...] = h0_ref[...]
        src_ref[...] = sr0_ref[...]
        sic_ref[...] = si0_ref[...]

    z = z_ref[...]
    xa = z[:, :d_lru]
    ga = z[:, d_lru:2 * d_lru]
    u = z[:, 2 * d_lru:]

    ext = jnp.concatenate([ext_ref[...], xa], axis=0)
    xc = cb_ref[...] + xa * cw_ref[CONV_W - 1:CONV_W, :]
    for s in range(1, CONV_W):
        xc = xc + pltpu.roll(ext, s, 0)[V7X_SUBLANES:, :] * cw_ref[CONV_W - 1 - s:CONV_W - s, :]
    ext_ref[...] = xa[tc - V7X_SUBLANES:, :]

    xcb = xc.astype(BF16)
    r = jax.nn.sigmoid(jnp.dot(xcb, wa_ref[...], preferred_element_type=F32) + ba_ref[...])
    ig = jax.nn.sigmoid(jnp.dot(xcb, wx_ref[...], preferred_element_type=F32) + bx_ref[...])
    lam = lam_ref[...]
    softplus_neg = jnp.maximum(-lam, 0.0) + jnp.log1p(jnp.exp(-jnp.abs(lam)))
    log_a = -LRU_C * r * softplus_neg
    a = jnp.exp(log_a)
    bt = jnp.sqrt(-jnp.tanh(log_a) * (a * a + 1.0)) * ig * xc

    a3 = a.reshape(n_tiles, V7X_SUBLANES, d_lru)
    b3 = bt.reshape(n_tiles, V7X_SUBLANES, d_lru)
    row = lax.broadcasted_iota(I32, (1, V7X_SUBLANES, d_lru), 1)
    for s in (1, 2, 4):
        keep = row >= s
        ar = pltpu.roll(a3, s, 1)
        br = pltpu.roll(b3, s, 1)
        b3 = jnp.where(keep, a3 * br + b3, b3)
        a3 = jnp.where(keep, a3 * ar, a3)
    ha_ref[...] = a3.reshape(tc, d_lru)
    hb_ref[...] = b3.reshape(tc, d_lru)

    def lru_tile(i, carry):
        r0 = pl.multiple_of(i * V7X_SUBLANES, V7X_SUBLANES)
        h = ha_ref[pl.ds(r0, V7X_SUBLANES), :] * carry + hb_ref[pl.ds(r0, V7X_SUBLANES), :]
        hb_ref[pl.ds(r0, V7X_SUBLANES), :] = h
        return h[V7X_SUBLANES - 1:, :]

    hc_ref[...] = lax.fori_loop(0, n_tiles, lru_tile, hc_ref[...])
    hs = hb_ref[...]
    hl_ref[...] = hb_ref[last_row:last_row + 1, :]
    ya = hs * jax.nn.gelu(ga)

    ub = u.astype(BF16)
    n_state = bbr_ref.shape[1]
    n_slab = d_s5 // V7X_LANES
    sw = n_state // n_slab

    def input_dot(w_ref):
        return jnp.concatenate(
            [jnp.dot(ub[:, c * V7X_LANES:(c + 1) * V7X_LANES],
                     w_ref[c * V7X_LANES:(c + 1) * V7X_LANES, c * sw:(c + 1) * sw],
                     preferred_element_type=F32) for c in range(n_slab)], axis=-1)

    xr3 = input_dot(bbr_ref).reshape(n_tiles, V7X_SUBLANES, n_state)
    xi3 = input_dot(bbi_ref).reshape(n_tiles, V7X_SUBLANES, n_state)
    for i, s in enumerate((1, 2, 4)):
        cr = apow_ref[2 * i][None]
        ci = apow_ref[2 * i + 1][None]
        rr = pltpu.roll(xr3, s, 1)
        ri = pltpu.roll(xi3, s, 1)
        xr3, xi3 = xr3 + cr * rr - ci * ri, xi3 + cr * ri + ci * rr
    xr_ref[...] = xr3.reshape(tc, n_state)
    xi_ref[...] = xi3.reshape(tc, n_state)

    def s5_tile(i, carry):
        cr_, ci_ = carry
        r0 = pl.multiple_of(i * V7X_SUBLANES, V7X_SUBLANES)
        pr = apow_ref[6]
        pi = apow_ref[7]
        nr_ = xr_ref[pl.ds(r0, V7X_SUBLANES), :] + pr * cr_ - pi * ci_
        ni_ = xi_ref[pl.ds(r0, V7X_SUBLANES), :] + pr * ci_ + pi * cr_
        xr_ref[pl.ds(r0, V7X_SUBLANES), :] = nr_
        xi_ref[pl.ds(r0, V7X_SUBLANES), :] = ni_
        return nr_[V7X_SUBLANES - 1:, :], ni_[V7X_SUBLANES - 1:, :]

    cr_f, ci_f = lax.fori_loop(0, n_tiles, s5_tile, (src_ref[...], sic_ref[...]))
    src_ref[...] = cr_f
    sic_ref[...] = ci_f
    srl_ref[...] = xr_ref[last_row:last_row + 1, :]
    sil_ref[...] = xi_ref[last_row:last_row + 1, :]

    def output_dot(x_ref_, w_ref):
        return jnp.concatenate(
            [jnp.dot(x_ref_[:, c * sw:(c + 1) * sw].astype(BF16),
                     w_ref[c * sw:(c + 1) * sw, c * V7X_LANES:(c + 1) * V7X_LANES],
                     preferred_element_type=F32) for c in range(n_slab)], axis=-1)

    ys = output_dot(xr_ref, ccr_ref) - output_dot(xi_ref, cci_ref)
    ys = ys + d_ref[...] * u
    gs = jax.nn.gelu(ys)
    yb = gs * jax.nn.sigmoid(jnp.dot(gs.astype(BF16), wglu_ref[...], preferred_element_type=F32)
                             + bglu_ref[...])

    y_ref[...] = (x_ref[...]
                  + jnp.dot(ya.astype(BF16), wout_ref[:d_lru, :], preferred_element_type=F32)
                  + jnp.dot(yb.astype(BF16), wout_ref[d_lru:, :], preferred_element_type=F32))


def _mixer0(z, x, conv_init, h0, sr0, si0, wts, *, tc, last_row):
    bn, t_len, d_in = z.shape
    d = x.shape[2]
    d_lru = wts["cw"].shape[1]
    n_state = wts["bbr"].shape[1]
    assert t_len % tc == 0
    kern = functools.partial(_mixer0_kernel, tc=tc, last_row=last_row)
    wnames = ("cw", "cb", "wa", "ba", "wx", "bx", "lam", "bbr", "bbi", "apow",
              "ccr", "cci", "d", "wglu", "bglu", "wout")
    wlist = [wts[k] for k in wnames]
    per_b = lambda shape: pl.BlockSpec((None,) + shape, lambda b, c: (b,) + (0,) * len(shape))
    in_specs = [
        pl.BlockSpec((None, tc, d_in), lambda b, c: (b, c, 0)),
        pl.BlockSpec((None, tc, d), lambda b, c: (b, c, 0)),
        per_b((V7X_SUBLANES, d_lru)), per_b((1, d_lru)), per_b((1, n_state)), per_b((1, n_state)),
    ] + [_const_spec(w.shape) for w in wlist]
    return pl.pallas_call(
        kern,
        out_shape=(
            jax.ShapeDtypeStruct((bn, t_len, d), F32),
            jax.ShapeDtypeStruct((bn, 1, d_lru), F32),
            jax.ShapeDtypeStruct((bn, 1, n_state), F32),
            jax.ShapeDtypeStruct((bn, 1, n_state), F32),
        ),
        grid=(bn, t_len // tc),
        in_specs=in_specs,
        out_specs=(
            pl.BlockSpec((None, tc, d), lambda b, c: (b, c, 0)),
            per_b((1, d_lru)), per_b((1, n_state)), per_b((1, n_state)),
        ),
        scratch_shapes=[
            pltpu.VMEM((V7X_SUBLANES, d_lru), F32),
            pltpu.VMEM((1, d_lru), F32),
            pltpu.VMEM((1, n_state), F32),
            pltpu.VMEM((1, n_state), F32),
            pltpu.VMEM((tc, n_state), F32),
            pltpu.VMEM((tc, n_state), F32),
            pltpu.VMEM((tc, d_lru), F32),
            pltpu.VMEM((tc, d_lru), F32),
        ],
        compiler_params=_cparams(("parallel", "arbitrary")),
        name="mixer0",
    )(z, x, conv_init, h0, sr0, si0, *wlist)


def _xattn_kernel(x_ref, g_ref, wq_ref, qg_ref, mk_ref, mv_ref, wo_ref, o_ref):
    nb, tm, d = x_ref.shape
    x = x_ref[...].reshape(nb * tm, d)
    q = jnp.dot(_rms(x, g_ref[...]).astype(BF16), wq_ref[...], preferred_element_type=F32)
    dh = qg_ref.shape[1]
    head_major = len(mk_ref.shape) == 4
    rows = []
    for b in range(nb):
        if head_major:
            kb = pltpu.einshape("nhd->hnd", mk_ref[b])
            vb = pltpu.einshape("nhd->hnd", mv_ref[b])
            head_of = lambda arr, h: arr[h]
        else:
            kb, vb = mk_ref[b], mv_ref[b]
            head_of = lambda arr, h: arr[:, h * dh:(h + 1) * dh]
        outs = []
        for h in range(XA_HEADS):
            qn = _rms(q[b * tm:(b + 1) * tm, h * dh:(h + 1) * dh], qg_ref[...]).astype(BF16)
            s = lax.dot_general(qn, head_of(kb, h).astype(BF16), (((1,), (1,)), ((), ())),
                                preferred_element_type=F32) * (dh ** -0.5)
            m = jnp.max(s, axis=-1, keepdims=True)
            p = jnp.exp(s - m)
            den = jnp.sum(p, axis=-1, keepdims=True)
            oh = jnp.dot(p.astype(BF16), head_of(vb, h).astype(BF16), preferred_element_type=F32) / den
            outs.append(oh.astype(BF16))
        rows.append(jnp.concatenate(outs, axis=-1))
    o = rows[0] if nb == 1 else jnp.concatenate(rows, axis=0)
    o_ref[...] = (x + jnp.dot(o, wo_ref[...], preferred_element_type=F32)).reshape(nb, tm, d)


def _xattn(x, g, wq, qg, mk, mv, wo, *, tm, nb=1, name, layer=None):
    bn, t_len, d = x.shape
    assert t_len % tm == 0 and bn % nb == 0
    if layer is None:
        mem_spec = pl.BlockSpec((nb,) + mk.shape[1:], lambda b, i: (b, 0, 0))
    else:
        mem_spec = pl.BlockSpec((None, nb) + mk.shape[2:], lambda b, i: (layer, b, 0, 0, 0))
    return pl.pallas_call(
        _xattn_kernel,
        out_shape=jax.ShapeDtypeStruct(x.shape, F32),
        grid=(bn // nb, t_len // tm),
        in_specs=[
            pl.BlockSpec((nb, tm, d), lambda b, i: (b, i, 0)),
            _const_spec((1, d)),
            _const_spec(wq.shape),
            _const_spec((1, qg.shape[-1])),
            mem_spec,
            mem_spec,
            _const_spec(wo.shape),
        ],
        out_specs=pl.BlockSpec((nb, tm, d), lambda b, i: (b, i, 0)),
        compiler_params=_cparams(("parallel", "arbitrary")),
        name=name,
    )(x, g.reshape(1, d), wq, qg.reshape(1, -1), mk, mv, wo)


def _ffn_kernel(x_ref, g_ref, wg_ref, wu_ref, wd_ref, o_ref, hn_ref, acc_ref):
    j = pl.program_id(1)

    @pl.when(j == 0)
    def _():
        hn_ref[...] = _rms(x_ref[...], g_ref[...]).astype(BF16)
        acc_ref[...] = jnp.zeros_like(acc_ref)

    hn = hn_ref[...]
    gate = jnp.dot(hn, wg_ref[...], preferred_element_type=F32)
    up = jnp.dot(hn, wu_ref[...], preferred_element_type=F32)
    act = (jax.nn.silu(gate) * up).astype(BF16)
    acc_ref[...] += jnp.dot(act, wd_ref[...], preferred_element_type=F32)

    @pl.when(j == pl.num_programs(1) - 1)
    def _():
        o_ref[...] = x_ref[...] + acc_ref[...]


def _ffn(x, g, w_gu, w_d, *, name):
    m, d = x.shape
    hid = w_d.shape[0]
    th = FFN_TH
    tm = min(m, 1024)
    nj = hid // th
    assert m % tm == 0 and hid % th == 0
    return pl.pallas_call(
        _ffn_kernel,
        out_shape=jax.ShapeDtypeStruct((m, d), F32),
        grid=(m // tm, nj),
        in_specs=[
            pl.BlockSpec((tm, d), lambda i, j: (i, 0)),
            pl.BlockSpec((1, d), lambda i, j: (0, 0)),
            pl.BlockSpec((d, th), lambda i, j: (0, j)),
            pl.BlockSpec((d, th), lambda i, j: (0, j + nj)),
            pl.BlockSpec((th, d), lambda i, j: (j, 0)),
        ],
        out_specs=pl.BlockSpec((tm, d), lambda i, j: (i, 0)),
        scratch_shapes=[pltpu.VMEM((tm, d), BF16), pltpu.VMEM((tm, d), F32)],
        compiler_params=_cparams(("parallel", "arbitrary")),
        name=name,
    )(x, g.reshape(1, d), w_gu, w_gu, w_d)


def _rel_bucket_np(dist):
    dist = np.clip(np.asarray(dist), 0, None)
    max_exact = REL_BUCKETS // 2
    safe = np.maximum(dist, max_exact).astype(np.float32)
    large = max_exact + np.floor(np.log(safe / max_exact) / math.log(REL_MAX_DIST / max_exact)
                                 * (REL_BUCKETS - max_exact)).astype(np.int32)
    large = np.minimum(large, REL_BUCKETS - 1)
    return np.where(dist < max_exact, dist, large).astype(np.int32)


def _dil_kernel(q_ref, k_ref, v_ref, bias_ref, o_ref, l_ref, kprev_ref, vprev_ref, *, d, nblk):
    n = pl.program_id(1)

    @pl.when(n == 0)
    def _():
        kprev_ref[...] = jnp.zeros_like(kprev_ref)
        vprev_ref[...] = jnp.zeros_like(vprev_ref)

    first_sel = jnp.minimum(n, 1)
    lane = lax.broadcasted_iota(I32, (QB, V7X_LANES), 1)
    n_slab = q_ref.shape[0]
    heads_per_slab = V7X_LANES // DH

    def residue(r, _):
        prev_k = [kprev_ref[r, c] for c in range(n_slab)]
        prev_v = [vprev_ref[r, c] for c in range(n_slab)]
        for blk in range(nblk):
            rows = pl.ds(blk * QB * d + r, QB, stride=d)
            bsel = first_sel if blk == 0 else 1
            lse = jnp.zeros((QB, V7X_LANES), F32)
            for c in range(n_slab):
                q = (q_ref[c, rows, :] * (DH ** -0.5)).astype(BF16)
                kc = k_ref[c, rows, :].astype(BF16)
                vc = v_ref[c, rows, :].astype(BF16)
                k = jnp.concatenate([prev_k[c], kc], axis=0)
                v = jnp.concatenate([prev_v[c], vc], axis=0)
                outs = []
                for hh in range(heads_per_slab):
                    h = c * heads_per_slab + hh
                    sl = slice(hh * DH, (hh + 1) * DH)
                    s = lax.dot_general(q[:, sl], k[:, sl], (((1,), (1,)), ((), ())),
                                        preferred_element_type=F32) + bias_ref[bsel, h]
                    m = jnp.max(s, axis=-1, keepdims=True)
                    p = jnp.exp(s - m)
                    den = jnp.sum(p, axis=-1, keepdims=True)
                    outs.append(jnp.dot(p.astype(BF16), v[:, sl], preferred_element_type=F32) / den)
                    lse = jnp.where(lane == h, m + jnp.log(den), lse)
                o_ref[c, rows, :] = jnp.concatenate(outs, axis=-1)
                prev_k[c] = kc
                prev_v[c] = vc
            l_ref[rows, :] = lse
        for c in range(n_slab):
            kprev_ref[r, c] = prev_k[c]
            vprev_ref[r, c] = prev_v[c]
        return 0

    lax.fori_loop(0, d, residue, 0, unroll=min(d, 2))


def _dilated_prompt(qkv, tab, g, d, bn):
    n_slabs, m_rows, _ = qkv.shape
    s_len = m_rows // bn
    gw = HG * DH
    gs = gw // V7X_LANES
    nblk = 2 if d == 1 else 1
    chunk = nblk * QB * d
    n_chunks = s_len // chunk
    assert s_len % chunk == 0
    k_col = n_slabs // gs // 3

    qi = np.arange(QB)[:, None]
    ki = np.arange(2 * QB)[None, :]
    dist = qi + QB - ki
    band = (dist >= 0) & (dist <= N_STRIDE)
    first = band & (ki >= QB)
    onehot = (np.arange(REL_BUCKETS)[:, None] == _rel_bucket_np(dist * d).reshape(1, -1)).astype(np.float32)
    bias = jnp.dot(tab.T.astype(F32), jnp.asarray(onehot),
                   precision=lax.Precision.HIGHEST).reshape(HG, QB, 2 * QB)
    bias2 = jnp.stack([jnp.where(first[None], bias, NEG), jnp.where(band[None], bias, NEG)])

    col = lambda section: (lambda b, n: (section * k_col + g, b * n_chunks + n, 0))
    blk = (gs, chunk, V7X_LANES)
    return pl.pallas_call(
        functools.partial(_dil_kernel, d=d, nblk=nblk),
        out_shape=(jax.ShapeDtypeStruct((gs, m_rows, V7X_LANES), F32),
                   jax.ShapeDtypeStruct((m_rows, V7X_LANES), F32)),
        grid=(bn, n_chunks),
        in_specs=[
            pl.BlockSpec(blk, col(0)), pl.BlockSpec(blk, col(1)), pl.BlockSpec(blk, col(2)),
            _const_spec(bias2.shape),
        ],
        out_specs=(pl.BlockSpec(blk, lambda b, n: (0, b * n_chunks + n, 0)),
                   pl.BlockSpec((chunk, V7X_LANES), lambda b, n: (b * n_chunks + n, 0))),
        scratch_shapes=[pltpu.VMEM((d, gs, QB, V7X_LANES), BF16), pltpu.VMEM((d, gs, QB, V7X_LANES), BF16)],
        compiler_params=_cparams(("parallel", "arbitrary")),
        name=f"dilated_prompt_g{g}",
    )(qkv, qkv, qkv, bias2)


def _split_dot(x, w_ref):
    hi = x.astype(BF16)
    lo = (x - hi.astype(F32)).astype(BF16)
    w = w_ref[...]
    return jnp.dot(hi, w, preferred_element_type=F32) + jnp.dot(lo, w, preferred_element_type=F32)


def _dil_sample_kernel(q_ref, k_ref, v_ref, cache_ref, bias_ref, biasn_ref, o_ref, l_ref, cout_ref, *,
                       t_valid, t_pad):
    n_slab = q_ref.shape[0]
    nb = cache_ref.shape[0]
    w = cache_ref.shape[-1]
    lane = lax.broadcasted_iota(I32, (t_pad, V7X_LANES), 1)
    is_new = lax.broadcasted_iota(I32, (DH, V7X_LANES), 1) >= V7X_LANES - t_valid

    def as_last_columns(x):
        shifted = pltpu.roll(x, t_pad - t_valid, 0)
        tile = jnp.concatenate([jnp.zeros((V7X_LANES - t_pad, x.shape[1]), F32), shifted], axis=0)
        return tile.T

    def sequence(b, _):
        rows = pl.ds(pl.multiple_of(b * t_pad, t_pad), t_pad)
        unslab = lambda ref: jnp.concatenate([ref[c, rows, :] for c in range(n_slab)], axis=-1)
        qn = unslab(q_ref)
        kn = unslab(k_ref)
        vn = unslab(v_ref)
        new_cols = (as_last_columns(kn), as_last_columns(vn))
        qb, kb, vb = qn.astype(BF16), kn.astype(BF16), vn.astype(BF16)
        lse = jnp.zeros((t_pad, V7X_LANES), F32)
        outs = []
        for h in range(HG):
            sl = slice(h * DH, (h + 1) * DH)
            k_t = cache_ref[b, 0, h]
            v_t = cache_ref[b, 1, h]
            s = jnp.dot(qb[:, sl], k_t.astype(BF16), preferred_element_type=F32) * (DH ** -0.5) + bias_ref[h]
            sn = lax.dot_general(qb[:, sl], kb[:, sl], (((1,), (1,)), ((), ())),
                                 preferred_element_type=F32) * (DH ** -0.5) + biasn_ref[h]
            m = jnp.maximum(jnp.max(s, axis=-1, keepdims=True), jnp.max(sn, axis=-1, keepdims=True))
            p = jnp.exp(s - m)
            pn = jnp.exp(sn - m)
            den = jnp.sum(p, axis=-1, keepdims=True) + jnp.sum(pn, axis=-1, keepdims=True)
            pv = lax.dot_general(p.astype(BF16), v_t.astype(BF16), (((1,), (1,)), ((), ())),
                                 preferred_element_type=F32)
            outs.append((pv + jnp.dot(pn.astype(BF16), vb[:, sl], preferred_element_type=F32)) / den)
            lse = jnp.where(lane == h, m + jnp.log(den), lse)
            for kv, old in enumerate((k_t, v_t)):
                moved = pltpu.roll(old, w - t_valid, 1)
                cout_ref[b, kv, h] = moved
                cout_ref[b, kv, h, :, w - V7X_LANES:] = jnp.where(is_new, new_cols[kv][sl, :],
                                                                  moved[:, w - V7X_LANES:])
        o = jnp.concatenate(outs, axis=-1)
        for c in range(n_slab):
            o_ref[c, rows, :] = o[:, c * V7X_LANES:(c + 1) * V7X_LANES]
        l_ref[rows, :] = lse
        return 0

    if nb == 1:
        sequence(0, 0)
    else:
        lax.fori_loop(0, nb, sequence, 0)


def _dilated_sample(qkv, cache, tab, g, d, t_valid):
    n_slabs, m_rows, _ = qkv.shape
    db, wb = cache.shape[0], cache.shape[-1]
    t_pad = m_rows // db
    gw = HG * DH
    gs = gw // V7X_LANES
    assert wb == N_STRIDE * d and t_valid <= V7X_LANES and wb % V7X_LANES == 0
    k_col = n_slabs // gs // 3

    def bias_of(j, ok):
        onehot = (np.arange(REL_BUCKETS)[:, None] == _rel_bucket_np(d * j).reshape(1, -1)) & ok.reshape(1, -1)
        looked_up = jnp.dot(tab.T.astype(F32), jnp.asarray(onehot.astype(np.float32)),
                            precision=lax.Precision.HIGHEST).reshape((HG,) + j.shape)
        return jnp.where(ok[None], looked_up, NEG)

    tq = np.arange(t_pad)[:, None]
    pos = np.arange(wb)[None, :]
    jc = (wb + tq - pos) // d
    bias = bias_of(jc, ((wb + tq - pos) % d == 0) & (jc >= 1) & (jc <= N_STRIDE))
    tn = np.arange(t_pad)[None, :]
    jn = (tq - tn) // d
    bias_new = bias_of(jn, (tn <= tq) & ((tq - tn) % d == 0) & (jn <= N_STRIDE) & (tn < t_valid))

    kern = functools.partial(_dil_sample_kernel, t_valid=t_valid, t_pad=t_pad)
    cache_bytes = 4 * math.prod(cache.shape[1:])
    nb = max(1, min(db, (4 * 1024 * 1024) // cache_bytes))
    assert db % nb == 0
    blk = (gs, nb * t_pad, V7X_LANES)
    cblk = pl.BlockSpec((nb,) + cache.shape[1:], lambda b: (b, 0, 0, 0, 0))
    return pl.pallas_call(
        kern,
        out_shape=(jax.ShapeDtypeStruct((gs, m_rows, V7X_LANES), F32),
                   jax.ShapeDtypeStruct((m_rows, V7X_LANES), F32),
                   jax.ShapeDtypeStruct(cache.shape, F32)),
        grid=(db // nb,),
        in_specs=[
            pl.BlockSpec(blk, lambda b: (g, b, 0)),
            pl.BlockSpec(blk, lambda b: (k_col + g, b, 0)),
            pl.BlockSpec(blk, lambda b: (2 * k_col + g, b, 0)),
            cblk, _const_spec(bias.shape), _const_spec(bias_new.shape),
        ],
        out_specs=(pl.BlockSpec(blk, lambda b: (0, b, 0)),
                   pl.BlockSpec((nb * t_pad, V7X_LANES), lambda b: (b, 0)),
                   cblk),
        compiler_params=_cparams(("parallel",)),
        name=f"dilated_sample_g{g}",
    )(qkv, qkv, qkv, cache, bias, bias_new)


def _merge_wo_kernel(o0_ref, o1_ref, o2_ref, l0_ref, l1_ref, l2_ref, x_ref, hexp_ref, w_ref, out_ref):
    l0, l1, l2 = l0_ref[...], l1_ref[...], l2_ref[...]
    m = jnp.maximum(jnp.maximum(l0, l1), l2)
    e0, e1, e2 = jnp.exp(l0 - m), jnp.exp(l1 - m), jnp.exp(l2 - m)
    inv = 1.0 / (e0 + e1 + e2)
    unslab = lambda ref: jnp.concatenate([ref[c] for c in range(ref.shape[0])], axis=-1)
    o = (_split_dot(e0 * inv, hexp_ref) * unslab(o0_ref) + _split_dot(e1 * inv, hexp_ref) * unslab(o1_ref)
         + _split_dot(e2 * inv, hexp_ref) * unslab(o2_ref))
    out_ref[...] = x_ref[...] + jnp.dot(o.astype(BF16), w_ref[...], preferred_element_type=F32)


def _merge_wo(outs, lses, x, w, *, name):
    m, d = x.shape
    gw = w.shape[0]
    tm = min(m, 512)
    assert m % tm == 0
    head = np.arange(gw) // DH
    hexp = jnp.asarray((np.arange(V7X_LANES)[:, None] == head[None, :]).astype(np.float32), dtype=BF16)
    row = lambda width: pl.BlockSpec((tm, width), lambda i: (i, 0))
    slab = pl.BlockSpec((gw // V7X_LANES, tm, V7X_LANES), lambda i: (0, i, 0))
    return pl.pallas_call(
        _merge_wo_kernel,
        out_shape=jax.ShapeDtypeStruct((m, d), F32),
        grid=(m // tm,),
        in_specs=[slab] * 3 + [row(V7X_LANES)] * 3 + [row(d), _const_spec(hexp.shape), _const_spec(w.shape)],
        out_specs=row(d),
        compiler_params=_cparams(("parallel",)),
        name=name,
    )(*outs, *lses, x, hexp, w)


def _router_kernel(x_ref, g_ref, wr_ref, br_ref, cnt0_ref, tri_ref,
                   hn_ref, eid_ref, gate_ref, lp_ref, tbase_ref, tcnt_ref, cnt_ref, run_ref):
    @pl.when(pl.program_id(0) == 0)
    def _():
        run_ref[...] = cnt0_ref[...]

    hn = _rms(x_ref[...], g_ref[...])
    hn_ref[...] = hn
    wr = wr_ref[...]
    h_hi = hn.astype(BF16)
    h_lo = (hn - h_hi.astype(F32)).astype(BF16)
    w_hi = wr.astype(BF16)
    w_lo = (wr - w_hi.astype(F32)).astype(BF16)
    logits = (jnp.dot(h_hi, w_hi, preferred_element_type=F32) + jnp.dot(h_hi, w_lo, preferred_element_type=F32)
              + jnp.dot(h_lo, w_hi, preferred_element_type=F32)) + br_ref[...]
    ne = logits.shape[1]
    lane = lax.broadcasted_iota(I32, logits.shape, 1)
    m1 = jnp.max(logits, axis=-1, keepdims=True)
    i1 = jnp.min(jnp.where(logits == m1, lane, ne), axis=-1, keepdims=True)
    rest = jnp.where(lane == i1, -jnp.inf, logits)
    m2 = jnp.max(rest, axis=-1, keepdims=True)
    i2 = jnp.min(jnp.where(rest == m2, lane, ne), axis=-1, keepdims=True)
    e2 = jnp.exp(m2 - m1)
    g1 = 1.0 / (1.0 + e2)
    g2 = e2 / (1.0 + e2)
    eid_ref[...] = jnp.where(lane == 0, i1, jnp.where(lane == 1, i2, 0))
    gate_ref[...] = jnp.where(lane == 0, g1, jnp.where(lane == 1, g2, 0.0))
    chosen = jnp.where(jnp.logical_or(lane == i1, lane == i2), 1.0, 0.0)
    before = jnp.dot(tri_ref[...], chosen.astype(BF16), preferred_element_type=F32)
    tile_cnt = jnp.ceil(jnp.sum(chosen, axis=0, keepdims=True) * (1.0 / V7X_SUBLANES)) * V7X_SUBLANES
    lane_row = lane[:1]
    lower = jnp.zeros_like(tile_cnt)
    for e in range(ne - 1):
        lower = lower + jnp.where(lane_row > e, tile_cnt[:, e:e + 1], 0.0)
    local = before + lower
    l1 = jnp.sum(jnp.where(lane == i1, local, 0.0), axis=-1, keepdims=True)
    l2 = jnp.sum(jnp.where(lane == i2, local, 0.0), axis=-1, keepdims=True)
    lp_ref[...] = jnp.where(lane == 0, l1, jnp.where(lane == 1, l2, 0.0)).astype(I32)
    tbase_ref[0] = run_ref[...]
    tcnt_ref[0] = tile_cnt
    run_ref[...] += tile_cnt
    cnt_ref[...] = run_ref[...]


def _router(x, g, wr, br, cnt0, *, name):
    m, d = x.shape
    ne = wr.shape[1]
    tm = min(m, MOE_TOKEN_TILE)
    assert m % tm == 0
    n_tiles = m // tm
    tri = jnp.asarray(np.tril(np.ones((tm, tm), np.float32), -1), dtype=BF16)
    row = lambda width: pl.BlockSpec((tm, width), lambda i: (i, 0))
    per_tile = pl.BlockSpec((1, 1, ne), lambda i: (i, 0, 0))
    return pl.pallas_call(
        _router_kernel,
        out_shape=(jax.ShapeDtypeStruct((m, d), F32),
                   jax.ShapeDtypeStruct((m, ne), I32),
                   jax.ShapeDtypeStruct((m, ne), F32),
                   jax.ShapeDtypeStruct((m, ne), I32),
                   jax.ShapeDtypeStruct((n_tiles, 1, ne), F32),
                   jax.ShapeDtypeStruct((n_tiles, 1, ne), F32),
                   jax.ShapeDtypeStruct((1, ne), F32)),
        grid=(n_tiles,),
        in_specs=[row(d), _const_spec((1, d)), _const_spec(wr.shape), _const_spec((1, ne)),
                  _const_spec((1, ne)), _const_spec((tm, tm))],
        out_specs=(row(d), row(ne), row(ne), row(ne), per_tile, per_tile, _const_spec((1, ne))),
        scratch_shapes=[pltpu.VMEM((1, ne), F32)],
        compiler_params=_cparams(("arbitrary",)),
        name=name,
    )(x, g.reshape(1, d), wr, br.reshape(1, ne), cnt0, tri)


def _staging_rows(n_tokens):
    return TOP_K * n_tokens + V7X_SUBLANES * N_EXPERTS


def _expert_run_copies(tile, dst_ref, cnt_ref, n_tokens, make_copy, wait):
    local = jnp.int32(0)
    for e in range(N_EXPERTS):
        count = cnt_ref[tile * N_EXPERTS + e]
        sorted_row = dst_ref[tile * N_EXPERTS + e]
        size = pl.next_power_of_2(n_tokens)
        while size >= V7X_SUBLANES:
            has = (count & size) != 0

            @pl.when(has)
            def _(local=local, sorted_row=sorted_row, size=size):
                copy = make_copy(pl.multiple_of(local, V7X_SUBLANES), pl.multiple_of(sorted_row, V7X_SUBLANES), size)
                if wait:
                    copy.wait()
                else:
                    copy.start()

            step = jnp.where(has, size, 0)
            local = local + step
            sorted_row = sorted_row + step
            size //= 2


def _dispatch_kernel(dst_ref, cnt_ref, hn_ref, lp_ref, xs_in_ref, xs_ref, stage_ref, sem, *, tmd):
    del xs_in_ref
    i = pl.program_id(0)
    n = pl.num_programs(0)
    slot = i % 2

    def copies(tile, s, wait):
        _expert_run_copies(tile, dst_ref, cnt_ref, tmd, lambda local, sorted_row, size: pltpu.make_async_copy(
            stage_ref.at[s, pl.ds(local, size)], xs_ref.at[pl.ds(sorted_row, size)], sem.at[s]), wait)

    @pl.when(i >= 2)
    def _():
        copies(i - 2, slot, True)

    lp = lp_ref[...]
    col = lax.broadcasted_iota(I32, (tmd, stage_ref.shape[1]), 1)
    place = (jnp.where(col == lp[:, 0:1], 1.0, 0.0) + jnp.where(col == lp[:, 1:2], 1.0, 0.0)).astype(BF16)
    stage_ref[slot] = lax.dot_general(place, hn_ref[...].astype(BF16), (((0,), (0,)), ((), ())),
                                      preferred_element_type=F32)
    copies(i, slot, False)

    @pl.when(i == n - 1)
    def _():
        copies(i, slot, True)

        @pl.when(n >= 2)
        def _():
            copies(i - 1, 1 - slot, True)


def _dispatch(hn, lp, dst, cnt, xs, *, name):
    m, d = hn.shape
    ne = lp.shape[1]
    tmd = min(m, MOE_TOKEN_TILE)
    assert m % tmd == 0
    kern = functools.partial(_dispatch_kernel, tmd=tmd)
    grid_spec = pltpu.PrefetchScalarGridSpec(
        num_scalar_prefetch=2,
        grid=(m // tmd,),
        in_specs=[
            pl.BlockSpec((tmd, d), lambda i, dst, cnt: (i, 0)),
            pl.BlockSpec((tmd, ne), lambda i, dst, cnt: (i, 0)),
            pl.BlockSpec(memory_space=pl.ANY),
        ],
        out_specs=pl.BlockSpec(memory_space=pl.ANY),
        scratch_shapes=[pltpu.VMEM((2, _staging_rows(tmd), d), F32), pltpu.SemaphoreType.DMA((2,))],
    )
    return pl.pallas_call(
        kern,
        out_shape=jax.ShapeDtypeStruct(xs.shape, xs.dtype),
        grid_spec=grid_spec,
        input_output_aliases={4: 0},
        compiler_params=_cparams(("arbitrary",)),
        name=name,
    )(dst, cnt, hn, lp, xs)


def _moe_ffn_kernel(te_ref, nused_ref, x_ref, wg_ref, wu_ref, wd_ref, o_ref, xb_ref, acc_ref):
    t = pl.program_id(0)
    j = pl.program_id(1)
    last = pl.num_programs(1) - 1
    used = t < nused_ref[0]

    @pl.when(used)
    def _():
        @pl.when(j == 0)
        def _():
            xb_ref[...] = x_ref[...].astype(BF16)
            acc_ref[...] = jnp.zeros_like(acc_ref)

        xb = xb_ref[...]
        gate = jnp.dot(xb, wg_ref[...].astype(BF16), preferred_element_type=F32)
        up = jnp.dot(xb, wu_ref[...].astype(BF16), preferred_element_type=F32)
        act = (jax.nn.silu(gate) * up).astype(BF16)
        acc_ref[...] += jnp.dot(act, wd_ref[...].astype(BF16), preferred_element_type=F32)

        @pl.when(j == last)
        def _():
            o_ref[...] = acc_ref[...]

    @pl.when(jnp.logical_and(jnp.logical_not(used), j == last))
    def _():
        o_ref[...] = jnp.zeros_like(o_ref)


def _moe_ffn(xs, te, n_used, w_gu, w_d):
    n_rows, d = xs.shape
    hid = w_d.shape[1]
    tm, th = MOE_TM, MOE_TH
    nj = hid // th
    nt = n_rows // tm
    assert hid % th == 0 and n_rows % tm == 0

    def jj(t, j, nu):
        return jnp.where(t < nu[0], j, nj - 1)

    grid_spec = pltpu.PrefetchScalarGridSpec(
        num_scalar_prefetch=2,
        grid=(nt, nj),
        in_specs=[
            pl.BlockSpec((tm, d), lambda t, j, te, nu: (jnp.maximum(jnp.minimum(t, nu[0] - 1), 0), 0)),
            pl.BlockSpec((None, d, th), lambda t, j, te, nu: (te[t], 0, jj(t, j, nu))),
            pl.BlockSpec((None, d, th), lambda t, j, te, nu: (te[t], 0, jj(t, j, nu) + nj)),
            pl.BlockSpec((None, th, d), lambda t, j, te, nu: (te[t], jj(t, j, nu), 0)),
        ],
        out_specs=pl.BlockSpec((tm, d), lambda t, j, te, nu: (t, 0)),
        scratch_shapes=[pltpu.VMEM((tm, d), BF16), pltpu.VMEM((tm, d), F32)],
    )
    return pl.pallas_call(
        _moe_ffn_kernel,
        out_shape=jax.ShapeDtypeStruct((n_rows, d), F32),
        grid_spec=grid_spec,
        compiler_params=_cparams(("arbitrary", "arbitrary")),
        name="moe_experts",
    )(te, n_used, xs, w_gu, w_gu, w_d)


def _combine_kernel(dst_ref, cnt_ref, x_ref, gate_ref, lp_ref, ys_ref, o_ref, buf_ref, sem, *, tmc):
    i = pl.program_id(0)
    n = pl.num_programs(0)
    slot = i % 2

    def gather(tile, s, wait):
        _expert_run_copies(tile, dst_ref, cnt_ref, tmc, lambda local, sorted_row, size: pltpu.make_async_copy(
            ys_ref.at[pl.ds(sorted_row, size)], buf_ref.at[s, pl.ds(local, size)], sem.at[s]), wait)

    @pl.when(i == 0)
    def _():
        buf_ref[...] = jnp.zeros_like(buf_ref)
        gather(i, slot, False)

    @pl.when(i + 1 < n)
    def _():
        gather(i + 1, 1 - slot, False)

    gather(i, slot, True)
    g = gate_ref[...]
    lp = lp_ref[...]
    col = lax.broadcasted_iota(I32, (tmc, buf_ref.shape[1]), 1)
    w = jnp.where(col == lp[:, 0:1], g[:, 0:1], 0.0) + jnp.where(col == lp[:, 1:2], g[:, 1:2], 0.0)
    rows = buf_ref[slot]
    w_hi = w.astype(BF16)
    w_lo = (w - w_hi.astype(F32)).astype(BF16)
    r_hi = rows.astype(BF16)
    r_lo = (rows - r_hi.astype(F32)).astype(BF16)
    o_ref[...] = (x_ref[...] + jnp.dot(w_hi, r_hi, preferred_element_type=F32)
                  + jnp.dot(w_hi, r_lo, preferred_element_type=F32)
                  + jnp.dot(w_lo, r_hi, preferred_element_type=F32))


def _combine(x, gate, lp, dst, cnt, ys, *, name):
    m, d = x.shape
    ne = gate.shape[1]
    tmc = min(m, MOE_TOKEN_TILE)
    assert m % tmc == 0
    kern = functools.partial(_combine_kernel, tmc=tmc)
    row = lambda width: pl.BlockSpec((tmc, width), lambda i, dst, cnt: (i, 0))
    grid_spec = pltpu.PrefetchScalarGridSpec(
        num_scalar_prefetch=2,
        grid=(m // tmc,),
        in_specs=[row(d), row(ne), row(ne), pl.BlockSpec(memory_space=pl.ANY)],
        out_specs=row(d),
        scratch_shapes=[pltpu.VMEM((2, _staging_rows(tmc), d), F32), pltpu.SemaphoreType.DMA((2,))],
    )
    return pl.pallas_call(
        kern,
        out_shape=jax.ShapeDtypeStruct((m, d), F32),
        grid_spec=grid_spec,
        compiler_params=_cparams(("arbitrary",)),
        name=name,
    )(dst, cnt, x, gate, lp, ys)


def _moe_plan(counts, tile_bases, tile_counts, tm, nt):
    padded = ((counts + tm - 1) // tm) * tm
    ends = jnp.cumsum(padded)
    starts = ends - padded
    dst = [(starts[None, :] + b[:, 0, :].astype(I32)).reshape(-1) for b in tile_bases]
    cnt = [c[:, 0, :].astype(I32).reshape(-1) for c in tile_counts]
    n_used = (ends[-1] // tm).astype(I32)
    tile = jnp.arange(nt, dtype=I32)
    first_row = jnp.minimum(tile, n_used - 1) * tm
    te = jnp.minimum(jnp.sum((first_row[:, None] >= ends[None, :]).astype(I32), axis=1), N_EXPERTS - 1)
    return dst, cnt, te, n_used.reshape(1)


def _block_diag(w):
    n, a, b = w.shape
    return jnp.einsum("nij,nm->nimj", w, jnp.eye(n, dtype=w.dtype)).reshape(n * a, n * b)


def kernel(x_prompt, x_sample, state_conv, state_lru, state_s5_re, state_s5_im, cache_swa0_kv, cache_swa1_kv, cache_swa2_kv, cache_mem_k, cache_mem_v, mem_prompt, norm_mix, norm_xa, norm_ffn, norm_mem, w_in_even, conv_w, conv_b, lru_wa, lru_ba, lru_wx, lru_bx, lru_lam, s5_lam_re, s5_lam_im, s5_log_dt, s5_b_re, s5_b_im, s5_c_re, s5_c_im, s5_d, s5_w_glu, s5_b_glu, w_out_even, w_qkv_odd, q_norm_odd, k_norm_odd, w_o_odd, rel_bias, xa_wq, xa_wkv, xa_qn, xa_kn, xa_wo, ffn_w_gu, ffn_w_down, moe_router_w, moe_router_b, moe_w_gu, moe_w_down):
    bp, s_len, d = x_prompt.shape
    db, t_dec, _ = x_sample.shape
    n_mem = mem_prompt.shape[1]
    tp = SAMPLE_PAD_T
    d_lru = conv_w.shape[-1]
    s5_g, s5_p, s5_h = s5_b_re.shape[1:]
    n_state = s5_g * s5_p
    d_s5 = s5_g * s5_h
    xa_dh = d // XA_HEADS
    caches = (cache_swa0_kv, cache_swa1_kv, cache_swa2_kv)
    bf = lambda w: w.astype(BF16)

    yp = x_prompt
    ys = jnp.pad(x_sample, ((0, 0), (0, tp - t_dec), (0, 0)))

    w_in = bf(w_in_even[0])
    bbr, bbi, apow = _s5_prep(s5_lam_re[0], s5_lam_im[0], s5_log_dt[0], s5_b_re[0], s5_b_im[0])
    eye_g = jnp.eye(s5_g, dtype=F32)
    mix_w = dict(
        cw=conv_w[0], cb=conv_b[0].reshape(1, d_lru),
        wa=bf(_block_diag(lru_wa[0])), ba=lru_ba[0].reshape(1, d_lru),
        wx=bf(_block_diag(lru_wx[0])), bx=lru_bx[0].reshape(1, d_lru),
        lam=lru_lam[0].reshape(1, d_lru),
        bbr=bbr, bbi=bbi, apow=apow,
        ccr=bf(jnp.einsum("ghp,gk->gpkh", s5_c_re[0], eye_g).reshape(n_state, d_s5)),
        cci=bf(jnp.einsum("ghp,gk->gpkh", s5_c_im[0], eye_g).reshape(n_state, d_s5)),
        d=s5_d[0].reshape(1, d_s5), wglu=bf(s5_w_glu[0]), bglu=s5_b_glu[0].reshape(1, d_s5),
        wout=bf(w_out_even[0]),
    )
    z_p = _norm_matmul(yp.reshape(bp * s_len, d), norm_mix[0], w_in, name="in_proj_p").reshape(bp, s_len, -1)
    yp, p_lru, p_s5r, p_s5i = _mixer0(
        z_p, yp, jnp.zeros((bp, V7X_SUBLANES, d_lru), F32), jnp.zeros((bp, 1, d_lru), F32),
        jnp.zeros((bp, 1, n_state), F32), jnp.zeros((bp, 1, n_state), F32), mix_w,
        tc=MIX_TC, last_row=MIX_TC - 1)
    z_s = _norm_matmul(ys.reshape(db * tp, d), norm_mix[0], w_in, name="in_proj_s").reshape(db, tp, -1)
    conv_init = jnp.pad(state_conv[0], ((0, 0), (V7X_SUBLANES - (CONV_W - 1), 0), (0, 0)))
    ys, s_lru, s_s5r, s_s5i = _mixer0(
        z_s, ys, conv_init, state_lru[0].reshape(db, 1, d_lru),
        state_s5_re[0].reshape(db, 1, n_state), state_s5_im[0].reshape(db, 1, n_state), mix_w,
        tc=tp, last_row=t_dec - 1)
    p_state_conv = z_p[:, s_len - (CONV_W - 1):, :d_lru][None]
    s_state_conv = z_s[:, t_dec - (CONV_W - 1):t_dec, :d_lru][None]

    p_mk, p_mv = [], []

    def cross_attention(layer, yp, ys):
        kn_gain = jnp.concatenate([jnp.tile(xa_kn[layer], XA_HEADS), jnp.ones((d,), F32)]).reshape(1, 2 * d)
        kv = _norm_matmul(mem_prompt.reshape(bp * n_mem, d), norm_mem[layer], bf(xa_wkv[layer]),
                          head_gain=kn_gain, n_norm_cols=d, dh=xa_dh, name=f"mem_kv{layer}")
        kv = kv.reshape(bp, n_mem, 2 * d)
        mk, mv = kv[:, :, :d], kv[:, :, d:]
        p_mk.append(mk.reshape(bp, n_mem, XA_HEADS, xa_dh))
        p_mv.append(mv.reshape(bp, n_mem, XA_HEADS, xa_dh))
        wq, wo = bf(xa_wq[layer]), bf(xa_wo[layer])
        yp = _xattn(yp, norm_xa[layer], wq, xa_qn[layer], mk, mv, wo, tm=512, name=f"xattn_p{layer}")
        ys = _xattn(ys, norm_xa[layer], wq, xa_qn[layer], cache_mem_k, cache_mem_v, wo,
                    tm=tp, nb=4, name=f"xattn_s{layer}", layer=layer)
        return yp, ys

    yp, ys = cross_attention(0, yp, ys)
    w_gu0, w_d0 = bf(ffn_w_gu[0]), bf(ffn_w_down[0])
    yp = _ffn(yp.reshape(bp * s_len, d), norm_ffn[0], w_gu0, w_d0, name="ffn_p").reshape(bp, s_len, d)
    ys = _ffn(ys.reshape(db * tp, d), norm_ffn[0], w_gu0, w_d0, name="ffn_s").reshape(db, tp, d)

    n_heads = len(WINDOWS) * HG
    d_c = n_heads * DH
    gw = HG * DH
    w_qkv = bf(w_qkv_odd[0])
    qk_gain = jnp.concatenate([jnp.tile(q_norm_odd[0], n_heads), jnp.tile(k_norm_odd[0], n_heads),
                               jnp.ones((d_c,), F32)]).reshape(1, 3 * d_c)
    qkv_p = _norm_matmul(yp.reshape(bp * s_len, d), norm_mix[1], w_qkv, head_gain=qk_gain,
                         n_norm_cols=2 * d_c, dh=DH, slabs=True, name="qkv_p")
    qkv_s = _norm_matmul(ys.reshape(db * tp, d), norm_mix[1], w_qkv, head_gain=qk_gain,
                         n_norm_cols=2 * d_c, dh=DH, slabs=True, name="qkv_s")
    gs = gw // V7X_LANES
    w_o = bf(w_o_odd[0])
    outs_p, lses_p, outs_s, lses_s, p_swa, s_swa = [], [], [], [], [], []

    def kv_rows(qkv, n_batch, t_len, g, lo, hi):
        ks = (len(WINDOWS) + g) * gs
        vs = (2 * len(WINDOWS) + g) * gs
        slabs = qkv.reshape(-1, n_batch, t_len, V7X_LANES)
        kv = jnp.stack([slabs[ks:ks + gs, :, lo:hi], slabs[vs:vs + gs, :, lo:hi]])
        return jnp.transpose(kv, (2, 3, 0, 1, 4)).reshape(n_batch, hi - lo, 2, HG, DH)

    for g, dil in enumerate(DILATIONS):
        tab = rel_bias[:, g * HG:(g + 1) * HG]
        o, l = _dilated_prompt(qkv_p, tab, g, dil, bp)
        outs_p.append(o)
        lses_p.append(l)
        cache_t = jnp.transpose(caches[g][0], (0, 2, 3, 4, 1))
        o, l, cache_t = _dilated_sample(qkv_s, cache_t, tab, g, dil, t_dec)
        outs_s.append(o)
        lses_s.append(l)
        s_swa.append(jnp.transpose(cache_t, (0, 4, 1, 2, 3))[None])
        win = min(WINDOWS[g], s_len)
        p_swa.append(kv_rows(qkv_p, bp, s_len, g, s_len - win, s_len)[None])
    yp = _merge_wo(outs_p, lses_p, yp.reshape(bp * s_len, d), w_o, name="merge_wo_p").reshape(bp, s_len, d)
    ys = _merge_wo(outs_s, lses_s, ys.reshape(db * tp, d), w_o, name="merge_wo_s").reshape(db, tp, d)

    yp, ys = cross_attention(1, yp, ys)

    yp2 = yp.reshape(bp * s_len, d)
    ys2 = ys[:, :t_dec].reshape(db * t_dec, d)
    zero_cnt = jnp.zeros((1, N_EXPERTS), F32)
    hn_p, _, gate_p, lp_p, base_p, tcnt_p, cnt_p = _router(
        yp2, norm_ffn[1], moe_router_w[0], moe_router_b[0], zero_cnt, name="router_p")
    hn_s, _, gate_s, lp_s, base_s, tcnt_s, cnt_s = _router(
        ys2, norm_ffn[1], moe_router_w[0], moe_router_b[0], cnt_p, name="router_s")
    n_tok = yp2.shape[0] + ys2.shape[0]
    n_runs = N_EXPERTS * (base_p.shape[0] + base_s.shape[0])
    nt = -(-(n_tok * TOP_K + (V7X_SUBLANES - 1) * n_runs) // MOE_TM) + N_EXPERTS
    (dst_p, dst_s), (rc_p, rc_s), te, n_used = _moe_plan(
        cnt_s[0].astype(I32), (base_p, base_s), (tcnt_p, tcnt_s), MOE_TM, nt)
    xs = jnp.zeros((nt * MOE_TM, d), F32)
    xs = _dispatch(hn_p, lp_p, dst_p, rc_p, xs, name="dispatch_p")
    xs = _dispatch(hn_s, lp_s, dst_s, rc_s, xs, name="dispatch_s")
    ysort = _moe_ffn(xs, te, n_used, moe_w_gu[0], moe_w_down[0])
    yp = _combine(yp2, gate_p, lp_p, dst_p, rc_p, ysort, name="moe_combine_p").reshape(bp, s_len, d)
    ys = _combine(ys2, gate_s, lp_s, dst_s, rc_s, ysort, name="moe_combine_s").reshape(db, t_dec, d)

    return (yp, ys,
            p_state_conv, p_lru.reshape(1, bp, d_lru),
            p_s5r.reshape(1, bp, s5_g, s5_p), p_s5i.reshape(1, bp, s5_g, s5_p),
            p_swa[0], p_swa[1], p_swa[2], jnp.stack(p_mk), jnp.stack(p_mv),
            s_state_conv, s_lru.reshape(1, db, d_lru),
            s_s5r.reshape(1, db, s5_g, s5_p), s_s5i.reshape(1, db, s5_g, s5_p),
            s_swa[0], s_swa[1], s_swa[2])
```

```python
import functools
import math

import jax
import jax.numpy as jnp
import numpy as np
from jax import lax
from jax.experimental import pallas as pl
from jax.experimental.pallas import tpu as pltpu

F32 = jnp.float32
BF16 = jnp.bfloat16
I32 = jnp.int32

EPS = 1e-6
NEG = -1e30

V7X_SUBLANES = 8
V7X_LANES = 128
V7X_VMEM_BYTES = 64 * 1024 * 1024
VMEM_LIMIT = V7X_VMEM_BYTES - 8 * 1024 * 1024

LRU_C = 8.0
CONV_W = 4
WINDOWS = (128, 512, 2048)
DILATIONS = (1, 4, 16)
HG = 8
DH = 64
N_STRIDE = 128
QB = 128
REL_BUCKETS = 32
REL_MAX_DIST = WINDOWS[-1]
XA_HEADS = 4
N_EXPERTS = 8
TOP_K = 2
SAMPLE_PAD_T = 16

MIX_TC = 256
MOE_TM = 1024
MOE_TH = 512
MOE_TOKEN_TILE = 256
FFN_TH = 512


def _cparams(sem):
    return pltpu.CompilerParams(dimension_semantics=sem, vmem_limit_bytes=VMEM_LIMIT)


def _const_spec(shape):
    nd = len(shape)
    return pl.BlockSpec(shape, lambda *_: (0,) * nd)


def _rms(x, g):
    return x * lax.rsqrt(jnp.mean(x * x, axis=-1, keepdims=True) + EPS) * g


def _norm_matmul_kernel(x_ref, g_ref, w_ref, hg_ref, hs_ref, o_ref, hn_ref, *, n_norm_tiles, dh, slabs,
                        row_split):
    j = pl.program_id(1)

    @pl.when(j == 0)
    def _():
        hn_ref[...] = _rms(x_ref[...], g_ref[...]).astype(BF16)

    tm = hn_ref.shape[0]
    rs = tm // row_split

    def emit(r, val):
        rows = slice(r * rs, (r + 1) * rs)
        if slabs:
            for c in range(o_ref.shape[0]):
                o_ref[c, rows, :] = val[:, c * V7X_LANES:(c + 1) * V7X_LANES]
        else:
            o_ref[rows, :] = val

    def tile(r, normed):
        y = jnp.dot(hn_ref[r * rs:(r + 1) * rs, :], w_ref[...], preferred_element_type=F32)
        if normed:
            ssq = jnp.dot((y * y).astype(BF16), hs_ref[...], preferred_element_type=F32)
            y = y * lax.rsqrt(ssq * (1.0 / dh) + EPS) * hg_ref[...]
        emit(r, y)

    if n_norm_tiles == 0:
        for r in range(row_split):
            tile(r, False)
    else:
        @pl.when(j < n_norm_tiles)
        def _():
            for r in range(row_split):
                tile(r, True)

        @pl.when(j >= n_norm_tiles)
        def _():
            for r in range(row_split):
                tile(r, False)


def _head_sum_matrix(tn, dh):
    idx = np.arange(tn) // dh
    return jnp.asarray((idx[:, None] == idx[None, :]).astype(np.float32), dtype=BF16)


def _norm_matmul(x, g, w, *, tn=512, head_gain=None, n_norm_cols=0, dh=1, slabs=False, name):
    m, d = x.shape
    n = w.shape[1]
    tm = min(m, 1024)
    assert m % tm == 0 and n % tn == 0 and n_norm_cols % tn == 0
    if head_gain is None:
        head_gain = jnp.ones((1, n), F32)
    hs = _head_sum_matrix(tn, dh)
    row_split = 2 if tm % 512 == 0 else 1
    kern = functools.partial(_norm_matmul_kernel, n_norm_tiles=n_norm_cols // tn, dh=dh, slabs=slabs,
                             row_split=row_split)
    if slabs:
        out_shape = jax.ShapeDtypeStruct((n // V7X_LANES, m, V7X_LANES), F32)
        out_spec = pl.BlockSpec((tn // V7X_LANES, tm, V7X_LANES), lambda i, j: (j, i, 0))
    else:
        out_shape = jax.ShapeDtypeStruct((m, n), F32)
        out_spec = pl.BlockSpec((tm, tn), lambda i, j: (i, j))
    return pl.pallas_call(
        kern,
        out_shape=out_shape,
        grid=(m // tm, n // tn),
        in_specs=[
            pl.BlockSpec((tm, d), lambda i, j: (i, 0)),
            pl.BlockSpec((1, d), lambda i, j: (0, 0)),
            pl.BlockSpec((d, tn), lambda i, j: (0, j)),
            pl.BlockSpec((1, tn), lambda i, j: (0, j)),
            pl.BlockSpec((tn, tn), lambda i, j: (0, 0)),
        ],
        out_specs=out_spec,
        scratch_shapes=[pltpu.VMEM((tm, d), BF16)],
        compiler_params=_cparams(("parallel", "arbitrary")),
        name=name,
    )(x, g.reshape(1, d), w, head_gain, hs)


def _s5_prep_kernel(lre_ref, lim_ref, ldt_ref, bre_ref, bim_ref, bbr_ref, bbi_ref, apow_ref):
    lr = lre_ref[...]
    li = lim_ref[...]
    dt = jnp.exp(ldt_ref[...])
    mag = jnp.exp(lr * dt)
    ab_r = mag * jnp.cos(li * dt)
    ab_i = mag * jnp.sin(li * dt)
    den = lr * lr + li * li
    nr = ab_r - 1.0
    cr = (nr * lr + ab_i * li) / den
    ci = (ab_i * lr - nr * li) / den
    n_slab, _, sw = bre_ref.shape
    for c in range(n_slab):
        cr_c = cr[:, c * sw:(c + 1) * sw]
        ci_c = ci[:, c * sw:(c + 1) * sw]
        b_r = bre_ref[c]
        b_i = bim_ref[c]
        bbr_ref[c] = (cr_c * b_r - ci_c * b_i).astype(BF16)
        bbi_ref[c] = (cr_c * b_i + ci_c * b_r).astype(BF16)

    n = lr.shape[1]
    row = lax.broadcasted_iota(I32, (V7X_SUBLANES, n), 0)

    def power(kf):
        mg = jnp.exp(kf * (lr * dt))
        return mg * jnp.cos(kf * (li * dt)), mg * jnp.sin(kf * (li * dt))

    for i, s in enumerate((1, 2, 4)):
        pr, pi = power(jnp.full((V7X_SUBLANES, n), s, F32))
        keep = row >= s
        apow_ref[2 * i] = jnp.where(keep, pr, 0.0)
        apow_ref[2 * i + 1] = jnp.where(keep, pi, 0.0)
    pr, pi = power((row + 1).astype(F32))
    apow_ref[6] = pr
    apow_ref[7] = pi


def _s5_prep(lam_re, lam_im, log_dt, b_re, b_im):
    g, p, h = b_re.shape
    n = g * p
    n_slab = g * h // V7X_LANES
    gl = g // n_slab
    eye = jnp.eye(gl, dtype=F32)

    def slab_blocks(b):
        return jnp.einsum("cgph,gk->ckhgp", b.reshape(n_slab, gl, p, h), eye).reshape(n_slab, V7X_LANES, gl * p)

    bre_bd = slab_blocks(b_re)
    bim_bd = slab_blocks(b_im)
    ldt = jnp.broadcast_to(log_dt[:, None], (g, p)).reshape(1, n)
    return pl.pallas_call(
        _s5_prep_kernel,
        out_shape=(
            jax.ShapeDtypeStruct(bre_bd.shape, BF16),
            jax.ShapeDtypeStruct(bre_bd.shape, BF16),
            jax.ShapeDtypeStruct((8, V7X_SUBLANES, n), F32),
        ),
        compiler_params=pltpu.CompilerParams(vmem_limit_bytes=VMEM_LIMIT),
        name="s5_prep",
    )(lam_re.reshape(1, n), lam_im.reshape(1, n), ldt, bre_bd, bim_bd)


def _mixer0_kernel(z_ref, x_ref, convi_ref, h0_ref, sr0_ref, si0_ref,
                   cw_ref, cb_ref, wa_ref, ba_ref, wx_ref, bx_ref, lam_ref,
                   bbr_ref, bbi_ref, apow_ref, ccr_ref, cci_ref, d_ref, wglu_ref, bglu_ref, wout_ref,
                   y_ref, hl_ref, srl_ref, sil_ref,
                   ext_ref, hc_ref, src_ref, sic_ref, xr_ref, xi_ref, ha_ref, hb_ref,
                   *, tc, last_row):
    c = pl.program_id(1)
    d_lru = cw_ref.shape[1]
    d_s5 = d_ref.shape[1]
    n_tiles = tc // V7X_SUBLANES

    @pl.when(c == 0)
    def _():
        ext_ref[...] = convi_ref[...]
        hc_ref[...] = h0_ref[...]
        src_ref[...] = sr0_ref[...]
        sic_ref[...] = si0_ref[...]

    z = z_ref[...]
    xa = z[:, :d_lru]
    ga = z[:, d_lru:2 * d_lru]
    u = z[:, 2 * d_lru:]

    ext = jnp.concatenate([ext_ref[...], xa], axis=0)
    xc = cb_ref[...] + xa * cw_ref[CONV_W - 1:CONV_W, :]
    for s in range(1, CONV_W):
        xc = xc + pltpu.roll(ext, s, 0)[V7X_SUBLANES:, :] * cw_ref[CONV_W - 1 - s:CONV_W - s, :]
    ext_ref[...] = xa[tc - V7X_SUBLANES:, :]

    xcb = xc.astype(BF16)
    r = jax.nn.sigmoid(jnp.dot(xcb, wa_ref[...], preferred_element_type=F32) + ba_ref[...])
    ig = jax.nn.sigmoid(jnp.dot(xcb, wx_ref[...], preferred_element_type=F32) + bx_ref[...])
    lam = lam_ref[...]
    softplus_neg = jnp.maximum(-lam, 0.0) + jnp.log1p(jnp.exp(-jnp.abs(lam)))
    log_a = -LRU_C * r * softplus_neg
    a = jnp.exp(log_a)
    bt = jnp.sqrt(-jnp.tanh(log_a) * (a * a + 1.0)) * ig * xc

    a3 = a.reshape(n_tiles, V7X_SUBLANES, d_lru)
    b3 = bt.reshape(n_tiles, V7X_SUBLANES, d_lru)
    row = lax.broadcasted_iota(I32, (1, V7X_SUBLANES, d_lru), 1)
    for s in (1, 2, 4):
        keep = row >= s
        ar = pltpu.roll(a3, s, 1)
        br = pltpu.roll(b3, s, 1)
        b3 = jnp.where(keep, a3 * br + b3, b3)
        a3 = jnp.where(keep, a3 * ar, a3)
    ha_ref[...] = a3.reshape(tc, d_lru)
    hb_ref[...] = b3.reshape(tc, d_lru)

    def lru_tile(i, carry):
        r0 = pl.multiple_of(i * V7X_SUBLANES, V7X_SUBLANES)
        h = ha_ref[pl.ds(r0, V7X_SUBLANES), :] * carry + hb_ref[pl.ds(r0, V7X_SUBLANES), :]
        hb_ref[pl.ds(r0, V7X_SUBLANES), :] = h
        return h[V7X_SUBLANES - 1:, :]

    hc_ref[...] = lax.fori_loop(0, n_tiles, lru_tile, hc_ref[...])
    hs = hb_ref[...]
    hl_ref[...] = hb_ref[last_row:last_row + 1, :]
    ya = hs * jax.nn.gelu(ga)

    ub = u.astype(BF16)
    n_slab, _, sw = bbr_ref.shape
    n_state = n_slab * sw

    def input_dot(w_ref):
        return jnp.concatenate(
            [jnp.dot(ub[:, c * V7X_LANES:(c + 1) * V7X_LANES], w_ref[c], preferred_element_type=F32)
             for c in range(n_slab)], axis=-1)

    xr3 = input_dot(bbr_ref).reshape(n_tiles, V7X_SUBLANES, n_state)
    xi3 = input_dot(bbi_ref).reshape(n_tiles, V7X_SUBLANES, n_state)
    for i, s in enumerate((1, 2, 4)):
        cr = apow_ref[2 * i][None]
        ci = apow_ref[2 * i + 1][None]
        rr = pltpu.roll(xr3, s, 1)
        ri = pltpu.roll(xi3, s, 1)
        xr3, xi3 = xr3 + cr * rr - ci * ri, xi3 + cr * ri + ci * rr
    xr_ref[...] = xr3.reshape(tc, n_state)
    xi_ref[...] = xi3.reshape(tc, n_state)

    def s5_tile(i, carry):
        cr_, ci_ = carry
        r0 = pl.multiple_of(i * V7X_SUBLANES, V7X_SUBLANES)
        pr = apow_ref[6]
        pi = apow_ref[7]
        nr_ = xr_ref[pl.ds(r0, V7X_SUBLANES), :] + pr * cr_ - pi * ci_
        ni_ = xi_ref[pl.ds(r0, V7X_SUBLANES), :] + pr * ci_ + pi * cr_
        xr_ref[pl.ds(r0, V7X_SUBLANES), :] = nr_
        xi_ref[pl.ds(r0, V7X_SUBLANES), :] = ni_
        return nr_[V7X_SUBLANES - 1:, :], ni_[V7X_SUBLANES - 1:, :]

    cr_f, ci_f = lax.fori_loop(0, n_tiles, s5_tile, (src_ref[...], sic_ref[...]))
    src_ref[...] = cr_f
    sic_ref[...] = ci_f
    srl_ref[...] = xr_ref[last_row:last_row + 1, :]
    sil_ref[...] = xi_ref[last_row:last_row + 1, :]

    def output_dot(x_ref_, w_ref):
        return jnp.concatenate(
            [jnp.dot(x_ref_[:, c * sw:(c + 1) * sw].astype(BF16), w_ref[c], preferred_element_type=F32)
             for c in range(n_slab)], axis=-1)

    ys = output_dot(xr_ref, ccr_ref) - output_dot(xi_ref, cci_ref)
    ys = ys + d_ref[...] * u
    gs = jax.nn.gelu(ys)
    yb = gs * jax.nn.sigmoid(jnp.dot(gs.astype(BF16), wglu_ref[...], preferred_element_type=F32)
                             + bglu_ref[...])

    y_ref[...] = (x_ref[...]
                  + jnp.dot(ya.astype(BF16), wout_ref[:d_lru, :], preferred_element_type=F32)
                  + jnp.dot(yb.astype(BF16), wout_ref[d_lru:, :], preferred_element_type=F32))


def _mixer0(z, x, conv_init, h0, sr0, si0, wts, *, tc, last_row):
    bn, t_len, d_in = z.shape
    d = x.shape[2]
    d_lru = wts["cw"].shape[1]
    n_state = wts["apow"].shape[-1]
    assert t_len % tc == 0
    kern = functools.partial(_mixer0_kernel, tc=tc, last_row=last_row)
    wnames = ("cw", "cb", "wa", "ba", "wx", "bx", "lam", "bbr", "bbi", "apow",
              "ccr", "cci", "d", "wglu", "bglu", "wout")
    wlist = [wts[k] for k in wnames]
    per_b = lambda shape: pl.BlockSpec((None,) + shape, lambda b, c: (b,) + (0,) * len(shape))
    in_specs = [
        pl.BlockSpec((None, tc, d_in), lambda b, c: (b, c, 0)),
        pl.BlockSpec((None, tc, d), lambda b, c: (b, c, 0)),
        per_b((V7X_SUBLANES, d_lru)), per_b((1, d_lru)), per_b((1, n_state)), per_b((1, n_state)),
    ] + [_const_spec(w.shape) for w in wlist]
    return pl.pallas_call(
        kern,
        out_shape=(
            jax.ShapeDtypeStruct((bn, t_len, d), F32),
            jax.ShapeDtypeStruct((bn, 1, d_lru), F32),
            jax.ShapeDtypeStruct((bn, 1, n_state), F32),
            jax.ShapeDtypeStruct((bn, 1, n_state), F32),
        ),
        grid=(bn, t_len // tc),
        in_specs=in_specs,
        out_specs=(
            pl.BlockSpec((None, tc, d), lambda b, c: (b, c, 0)),
            per_b((1, d_lru)), per_b((1, n_state)), per_b((1, n_state)),
        ),
        scratch_shapes=[
            pltpu.VMEM((V7X_SUBLANES, d_lru), F32),
            pltpu.VMEM((1, d_lru), F32),
            pltpu.VMEM((1, n_state), F32),
            pltpu.VMEM((1, n_state), F32),
            pltpu.VMEM((tc, n_state), F32),
            pltpu.VMEM((tc, n_state), F32),
            pltpu.VMEM((tc, d_lru), F32),
            pltpu.VMEM((tc, d_lru), F32),
        ],
        compiler_params=_cparams(("parallel", "arbitrary")),
        name="mixer0",
    )(z, x, conv_init, h0, sr0, si0, *wlist)


def _xattn_kernel(x_ref, g_ref, wq_ref, qg_ref, mk_ref, mv_ref, wo_ref, o_ref):
    nb, tm, d = x_ref.shape
    x = x_ref[...].reshape(nb * tm, d)
    q = jnp.dot(_rms(x, g_ref[...]).astype(BF16), wq_ref[...], preferred_element_type=F32)
    dh = qg_ref.shape[1]
    head_major = len(mk_ref.shape) == 4
    rows = []
    for b in range(nb):
        if head_major:
            kb = pltpu.einshape("nhd->hnd", mk_ref[b])
            vb = pltpu.einshape("nhd->hnd", mv_ref[b])
            head_of = lambda arr, h: arr[h]
        else:
            kb, vb = mk_ref[b], mv_ref[b]
            head_of = lambda arr, h: arr[:, h * dh:(h + 1) * dh]
        outs = []
        for h in range(XA_HEADS):
            qn = _rms(q[b * tm:(b + 1) * tm, h * dh:(h + 1) * dh], qg_ref[...]).astype(BF16)
            s = lax.dot_general(qn, head_of(kb, h).astype(BF16), (((1,), (1,)), ((), ())),
                                preferred_element_type=F32) * (dh ** -0.5)
            m = jnp.max(s, axis=-1, keepdims=True)
            p = jnp.exp(s - m)
            den = jnp.sum(p, axis=-1, keepdims=True)
            oh = jnp.dot(p.astype(BF16), head_of(vb, h).astype(BF16), preferred_element_type=F32) / den
            outs.append(oh.astype(BF16))
        rows.append(jnp.concatenate(outs, axis=-1))
    o = rows[0] if nb == 1 else jnp.concatenate(rows, axis=0)
    o_ref[...] = (x + jnp.dot(o, wo_ref[...], preferred_element_type=F32)).reshape(nb, tm, d)


def _xattn(x, g, wq, qg, mk, mv, wo, *, tm, nb=1, name, layer=None):
    bn, t_len, d = x.shape
    assert t_len % tm == 0 and bn % nb == 0
    if layer is None:
        k_spec = pl.BlockSpec((nb, mk.shape[1], d), lambda b, i: (b, 0, 0))
        v_spec = pl.BlockSpec((nb, mv.shape[1], d), lambda b, i: (b, 0, 1))
    else:
        k_spec = v_spec = pl.BlockSpec((None, nb) + mk.shape[2:], lambda b, i: (layer, b, 0, 0, 0))
    return pl.pallas_call(
        _xattn_kernel,
        out_shape=jax.ShapeDtypeStruct(x.shape, F32),
        grid=(bn // nb, t_len // tm),
        in_specs=[
            pl.BlockSpec((nb, tm, d), lambda b, i: (b, i, 0)),
            _const_spec((1, d)),
            _const_spec(wq.shape),
            _const_spec((1, qg.shape[-1])),
            k_spec,
            v_spec,
            _const_spec(wo.shape),
        ],
        out_specs=pl.BlockSpec((nb, tm, d), lambda b, i: (b, i, 0)),
        compiler_params=_cparams(("parallel", "arbitrary")),
        name=name,
    )(x, g.reshape(1, d), wq, qg.reshape(1, -1), mk, mv, wo)


def _ffn_kernel(x_ref, g_ref, wg_ref, wu_ref, wd_ref, o_ref, hn_ref, acc_ref):
    j = pl.program_id(1)

    @pl.when(j == 0)
    def _():
        hn_ref[...] = _rms(x_ref[...], g_ref[...]).astype(BF16)
        acc_ref[...] = jnp.zeros_like(acc_ref)

    hn = hn_ref[...]
    gate = jnp.dot(hn, wg_ref[...], preferred_element_type=F32)
    up = jnp.dot(hn, wu_ref[...], preferred_element_type=F32)
    act = (jax.nn.silu(gate) * up).astype(BF16)
    acc_ref[...] += jnp.dot(act, wd_ref[...], preferred_element_type=F32)

    @pl.when(j == pl.num_programs(1) - 1)
    def _():
        o_ref[...] = x_ref[...] + acc_ref[...]


def _ffn(x, g, w_gu, w_d, *, name):
    m, d = x.shape
    hid = w_d.shape[0]
    th = FFN_TH
    tm = min(m, 1024)
    nj = hid // th
    assert m % tm == 0 and hid % th == 0
    return pl.pallas_call(
        _ffn_kernel,
        out_shape=jax.ShapeDtypeStruct((m, d), F32),
        grid=(m // tm, nj),
        in_specs=[
            pl.BlockSpec((tm, d), lambda i, j: (i, 0)),
            pl.BlockSpec((1, d), lambda i, j: (0, 0)),
            pl.BlockSpec((d, th), lambda i, j: (0, j)),
            pl.BlockSpec((d, th), lambda i, j: (0, j + nj)),
            pl.BlockSpec((th, d), lambda i, j: (j, 0)),
        ],
        out_specs=pl.BlockSpec((tm, d), lambda i, j: (i, 0)),
        scratch_shapes=[pltpu.VMEM((tm, d), BF16), pltpu.VMEM((tm, d), F32)],
        compiler_params=_cparams(("parallel", "arbitrary")),
        name=name,
    )(x, g.reshape(1, d), w_gu, w_gu, w_d)


def _rel_bucket_np(dist):
    dist = np.clip(np.asarray(dist), 0, None)
    max_exact = REL_BUCKETS // 2
    safe = np.maximum(dist, max_exact).astype(np.float32)
    large = max_exact + np.floor(np.log(safe / max_exact) / math.log(REL_MAX_DIST / max_exact)
                                 * (REL_BUCKETS - max_exact)).astype(np.int32)
    large = np.minimum(large, REL_BUCKETS - 1)
    return np.where(dist < max_exact, dist, large).astype(np.int32)


def _dil_kernel(q_ref, k_ref, v_ref, bias_ref, o_ref, l_ref, kprev_ref, vprev_ref, *, d, nblk):
    n = pl.program_id(1)

    @pl.when(n == 0)
    def _():
        kprev_ref[...] = jnp.zeros_like(kprev_ref)
        vprev_ref[...] = jnp.zeros_like(vprev_ref)

    first_sel = jnp.minimum(n, 1)
    lane = lax.broadcasted_iota(I32, (QB, V7X_LANES), 1)
    n_slab = q_ref.shape[0]
    heads_per_slab = V7X_LANES // DH

    def residue(r, _):
        prev_k = [kprev_ref[r, c] for c in range(n_slab)]
        prev_v = [vprev_ref[r, c] for c in range(n_slab)]
        for blk in range(nblk):
            rows = pl.ds(blk * QB * d + r, QB, stride=d)
            bsel = first_sel if blk == 0 else 1
            lse = jnp.zeros((QB, V7X_LANES), F32)
            for c in range(n_slab):
                q = (q_ref[c, rows, :] * (DH ** -0.5)).astype(BF16)
                kc = k_ref[c, rows, :].astype(BF16)
                vc = v_ref[c, rows, :].astype(BF16)
                k = jnp.concatenate([prev_k[c], kc], axis=0)
                v = jnp.concatenate([prev_v[c], vc], axis=0)
                outs = []
                for hh in range(heads_per_slab):
                    h = c * heads_per_slab + hh
                    sl = slice(hh * DH, (hh + 1) * DH)
                    s = lax.dot_general(q[:, sl], k[:, sl], (((1,), (1,)), ((), ())),
                                        preferred_element_type=F32) + bias_ref[bsel, h]
                    m = jnp.max(s, axis=-1, keepdims=True)
                    p = jnp.exp(s - m)
                    den = jnp.sum(p, axis=-1, keepdims=True)
                    outs.append(jnp.dot(p.astype(BF16), v[:, sl], preferred_element_type=F32) / den)
                    lse = jnp.where(lane == h, m + jnp.log(den), lse)
                o_ref[c, rows, :] = jnp.concatenate(outs, axis=-1)
                prev_k[c] = kc
                prev_v[c] = vc
            l_ref[rows, :] = lse
        for c in range(n_slab):
            kprev_ref[r, c] = prev_k[c]
            vprev_ref[r, c] = prev_v[c]
        return 0

    lax.fori_loop(0, d, residue, 0, unroll=min(d, 2))


def _dilated_prompt(qkv, tab, g, d, bn):
    n_slabs, m_rows, _ = qkv.shape
    s_len = m_rows // bn
    gw = HG * DH
    gs = gw // V7X_LANES
    nblk = 2 if d == 1 else 1
    chunk = nblk * QB * d
    n_chunks = s_len // chunk
    assert s_len % chunk == 0
    k_col = n_slabs // gs // 3

    qi = np.arange(QB)[:, None]
    ki = np.arange(2 * QB)[None, :]
    dist = qi + QB - ki
    band = (dist >= 0) & (dist <= N_STRIDE)
    first = band & (ki >= QB)
    onehot = (np.arange(REL_BUCKETS)[:, None] == _rel_bucket_np(dist * d).reshape(1, -1)).astype(np.float32)
    bias = jnp.dot(tab.T.astype(F32), jnp.asarray(onehot),
                   precision=lax.Precision.HIGHEST).reshape(HG, QB, 2 * QB)
    bias2 = jnp.stack([jnp.where(first[None], bias, NEG), jnp.where(band[None], bias, NEG)])

    col = lambda section: (lambda b, n: (section * k_col + g, b * n_chunks + n, 0))
    blk = (gs, chunk, V7X_LANES)
    return pl.pallas_call(
        functools.partial(_dil_kernel, d=d, nblk=nblk),
        out_shape=(jax.ShapeDtypeStruct((gs, m_rows, V7X_LANES), F32),
                   jax.ShapeDtypeStruct((m_rows, V7X_LANES), F32)),
        grid=(bn, n_chunks),
        in_specs=[
            pl.BlockSpec(blk, col(0)), pl.BlockSpec(blk, col(1)), pl.BlockSpec(blk, col(2)),
            _const_spec(bias2.shape),
        ],
        out_specs=(pl.BlockSpec(blk, lambda b, n: (0, b * n_chunks + n, 0)),
                   pl.BlockSpec((chunk, V7X_LANES), lambda b, n: (b * n_chunks + n, 0))),
        scratch_shapes=[pltpu.VMEM((d, gs, QB, V7X_LANES), BF16), pltpu.VMEM((d, gs, QB, V7X_LANES), BF16)],
        compiler_params=_cparams(("parallel", "arbitrary")),
        name=f"dilated_prompt_g{g}",
    )(qkv, qkv, qkv, bias2)


def _split_dot(x, w_ref):
    hi = x.astype(BF16)
    lo = (x - hi.astype(F32)).astype(BF16)
    w = w_ref[...]
    return jnp.dot(hi, w, preferred_element_type=F32) + jnp.dot(lo, w, preferred_element_type=F32)


def _dil_sample_kernel(q_ref, k_ref, v_ref, cache_ref, bias_ref, biasn_ref, o_ref, l_ref, cout_ref, *,
                       t_valid, t_pad):
    n_slab = q_ref.shape[0]
    nb = cache_ref.shape[0]
    w = cache_ref.shape[-1]
    lane = lax.broadcasted_iota(I32, (t_pad, V7X_LANES), 1)
    is_new = lax.broadcasted_iota(I32, (DH, V7X_LANES), 1) >= V7X_LANES - t_valid

    def as_last_columns(x):
        shifted = pltpu.roll(x, t_pad - t_valid, 0)
        tile = jnp.concatenate([jnp.zeros((V7X_LANES - t_pad, x.shape[1]), F32), shifted], axis=0)
        return tile.T

    def sequence(b, _):
        rows = pl.ds(pl.multiple_of(b * t_pad, t_pad), t_pad)
        unslab = lambda ref: jnp.concatenate([ref[c, rows, :] for c in range(n_slab)], axis=-1)
        qn = unslab(q_ref)
        kn = unslab(k_ref)
        vn = unslab(v_ref)
        new_cols = (as_last_columns(kn), as_last_columns(vn))
        qb, kb, vb = qn.astype(BF16), kn.astype(BF16), vn.astype(BF16)
        lse = jnp.zeros((t_pad, V7X_LANES), F32)
        outs = []
        for h in range(HG):
            sl = slice(h * DH, (h + 1) * DH)
            k_t = cache_ref[b, 0, h]
            v_t = cache_ref[b, 1, h]
            s = jnp.dot(qb[:, sl], k_t.astype(BF16), preferred_element_type=F32) * (DH ** -0.5) + bias_ref[h]
            sn = lax.dot_general(qb[:, sl], kb[:, sl], (((1,), (1,)), ((), ())),
                                 preferred_element_type=F32) * (DH ** -0.5) + biasn_ref[h]
            m = jnp.maximum(jnp.max(s, axis=-1, keepdims=True), jnp.max(sn, axis=-1, keepdims=True))
            p = jnp.exp(s - m)
            pn = jnp.exp(sn - m)
            den = jnp.sum(p, axis=-1, keepdims=True) + jnp.sum(pn, axis=-1, keepdims=True)
            pv = lax.dot_general(p.astype(BF16), v_t.astype(BF16), (((1,), (1,)), ((), ())),
                                 preferred_element_type=F32)
            outs.append((pv + jnp.dot(pn.astype(BF16), vb[:, sl], preferred_element_type=F32)) / den)
            lse = jnp.where(lane == h, m + jnp.log(den), lse)
            for kv, old in enumerate((k_t, v_t)):
                moved = pltpu.roll(old, w - t_valid, 1)
                cout_ref[b, kv, h] = moved
                cout_ref[b, kv, h, :, w - V7X_LANES:] = jnp.where(is_new, new_cols[kv][sl, :],
                                                                  moved[:, w - V7X_LANES:])
        o = jnp.concatenate(outs, axis=-1)
        for c in range(n_slab):
            o_ref[c, rows, :] = o[:, c * V7X_LANES:(c + 1) * V7X_LANES]
        l_ref[rows, :] = lse
        return 0

    if nb == 1:
        sequence(0, 0)
    else:
        lax.fori_loop(0, nb, sequence, 0)


def _dilated_sample(qkv, cache, tab, g, d, t_valid):
    n_slabs, m_rows, _ = qkv.shape
    db, wb = cache.shape[0], cache.shape[-1]
    t_pad = m_rows // db
    gw = HG * DH
    gs = gw // V7X_LANES
    assert wb == N_STRIDE * d and t_valid <= V7X_LANES and wb % V7X_LANES == 0
    k_col = n_slabs // gs // 3

    def bias_of(j, ok):
        onehot = (np.arange(REL_BUCKETS)[:, None] == _rel_bucket_np(d * j).reshape(1, -1)) & ok.reshape(1, -1)
        looked_up = jnp.dot(tab.T.astype(F32), jnp.asarray(onehot.astype(np.float32)),
                            precision=lax.Precision.HIGHEST).reshape((HG,) + j.shape)
        return jnp.where(ok[None], looked_up, NEG)

    tq = np.arange(t_pad)[:, None]
    pos = np.arange(wb)[None, :]
    jc = (wb + tq - pos) // d
    bias = bias_of(jc, ((wb + tq - pos) % d == 0) & (jc >= 1) & (jc <= N_STRIDE))
    tn = np.arange(t_pad)[None, :]
    jn = (tq - tn) // d
    bias_new = bias_of(jn, (tn <= tq) & ((tq - tn) % d == 0) & (jn <= N_STRIDE) & (tn < t_valid))

    kern = functools.partial(_dil_sample_kernel, t_valid=t_valid, t_pad=t_pad)
    cache_bytes = 4 * math.prod(cache.shape[1:])
    nb = max(1, min(db, (4 * 1024 * 1024) // cache_bytes))
    assert db % nb == 0
    blk = (gs, nb * t_pad, V7X_LANES)
    cblk = pl.BlockSpec((nb,) + cache.shape[1:], lambda b: (b, 0, 0, 0, 0))
    return pl.pallas_call(
        kern,
        out_shape=(jax.ShapeDtypeStruct((gs, m_rows, V7X_LANES), F32),
                   jax.ShapeDtypeStruct((m_rows, V7X_LANES), F32),
                   jax.ShapeDtypeStruct(cache.shape, F32)),
        grid=(db // nb,),
        in_specs=[
            pl.BlockSpec(blk, lambda b: (g, b, 0)),
            pl.BlockSpec(blk, lambda b: (k_col + g, b, 0)),
            pl.BlockSpec(blk, lambda b: (2 * k_col + g, b, 0)),
            cblk, _const_spec(bias.shape), _const_spec(bias_new.shape),
        ],
        out_specs=(pl.BlockSpec(blk, lambda b: (0, b, 0)),
                   pl.BlockSpec((nb * t_pad, V7X_LANES), lambda b: (b, 0)),
                   cblk),
        compiler_params=_cparams(("parallel",)),
        name=f"dilated_sample_g{g}",
    )(qkv, qkv, qkv, cache, bias, bias_new)


def _merge_wo_kernel(o0_ref, o1_ref, o2_ref, l0_ref, l1_ref, l2_ref, x_ref, hexp_ref, w_ref, out_ref):
    l0, l1, l2 = l0_ref[...], l1_ref[...], l2_ref[...]
    m = jnp.maximum(jnp.maximum(l0, l1), l2)
    e0, e1, e2 = jnp.exp(l0 - m), jnp.exp(l1 - m), jnp.exp(l2 - m)
    inv = 1.0 / (e0 + e1 + e2)
    unslab = lambda ref: jnp.concatenate([ref[c] for c in range(ref.shape[0])], axis=-1)
    o = (_split_dot(e0 * inv, hexp_ref) * unslab(o0_ref) + _split_dot(e1 * inv, hexp_ref) * unslab(o1_ref)
         + _split_dot(e2 * inv, hexp_ref) * unslab(o2_ref))
    out_ref[...] = x_ref[...] + jnp.dot(o.astype(BF16), w_ref[...], preferred_element_type=F32)


def _merge_wo(outs, lses, x, w, *, name):
    m, d = x.shape
    gw = w.shape[0]
    tm = min(m, 512)
    assert m % tm == 0
    head = np.arange(gw) // DH
    hexp = jnp.asarray((np.arange(V7X_LANES)[:, None] == head[None, :]).astype(np.float32), dtype=BF16)
    row = lambda width: pl.BlockSpec((tm, width), lambda i: (i, 0))
    slab = pl.BlockSpec((gw // V7X_LANES, tm, V7X_LANES), lambda i: (0, i, 0))
    return pl.pallas_call(
        _merge_wo_kernel,
        out_shape=jax.ShapeDtypeStruct((m, d), F32),
        grid=(m // tm,),
        in_specs=[slab] * 3 + [row(V7X_LANES)] * 3 + [row(d), _const_spec(hexp.shape), _const_spec(w.shape)],
        out_specs=row(d),
        compiler_params=_cparams(("parallel",)),
        name=name,
    )(*outs, *lses, x, hexp, w)


def _router_kernel(x_ref, g_ref, wr_ref, br_ref, cnt0_ref, tri_ref,
                   hn_ref, gate_ref, lp_ref, tbase_ref, tcnt_ref, cnt_ref, run_ref):
    @pl.when(pl.program_id(0) == 0)
    def _():
        run_ref[...] = cnt0_ref[...]

    hn = _rms(x_ref[...], g_ref[...])
    hn_ref[...] = hn
    wr = wr_ref[...]
    h_hi = hn.astype(BF16)
    h_lo = (hn - h_hi.astype(F32)).astype(BF16)
    w_hi = wr.astype(BF16)
    w_lo = (wr - w_hi.astype(F32)).astype(BF16)
    nt_dot = lambda a, b: lax.dot_general(a, b, (((1,), (1,)), ((), ())), preferred_element_type=F32)
    logits = nt_dot(w_hi, h_hi) + nt_dot(w_lo, h_hi) + nt_dot(w_hi, h_lo) + br_ref[...]
    ne = logits.shape[0]
    eidx = lax.broadcasted_iota(I32, logits.shape, 0)
    m1 = jnp.max(logits, axis=0, keepdims=True)
    i1 = jnp.min(jnp.where(logits == m1, eidx, ne), axis=0, keepdims=True)
    rest = jnp.where(eidx == i1, -jnp.inf, logits)
    m2 = jnp.max(rest, axis=0, keepdims=True)
    i2 = jnp.min(jnp.where(rest == m2, eidx, ne), axis=0, keepdims=True)
    e2 = jnp.exp(m2 - m1)
    g1 = 1.0 / (1.0 + e2)
    g2 = e2 / (1.0 + e2)
    gate_ref[...] = jnp.where(eidx == 0, g1, jnp.where(eidx == 1, g2, 0.0))
    chosen = jnp.where(jnp.logical_or(eidx == i1, eidx == i2), 1.0, 0.0)
    before = jnp.dot(chosen.astype(BF16), tri_ref[...], preferred_element_type=F32)
    tile_cnt = jnp.ceil(jnp.sum(chosen, axis=1, keepdims=True) * (1.0 / V7X_SUBLANES)) * V7X_SUBLANES
    ecol = eidx[:, :1]
    lower = jnp.zeros_like(tile_cnt)
    for e in range(ne - 1):
        lower = lower + jnp.where(ecol > e, tile_cnt[e:e + 1, :], 0.0)
    local = before + lower
    l1 = jnp.sum(jnp.where(eidx == i1, local, 0.0), axis=0, keepdims=True)
    l2 = jnp.sum(jnp.where(eidx == i2, local, 0.0), axis=0, keepdims=True)
    lp_ref[...] = jnp.where(eidx == 0, l1, jnp.where(eidx == 1, l2, 0.0)).astype(I32)
    tbase_ref[0] = run_ref[...]
    tcnt_ref[0] = tile_cnt
    run_ref[...] += tile_cnt
    cnt_ref[...] = run_ref[...]


def _router(x, g, wr, br, cnt0, *, name):
    m, d = x.shape
    ne = wr.shape[1]
    tm = min(m, MOE_TOKEN_TILE)
    assert m % tm == 0
    n_tiles = m // tm
    tri = jnp.asarray(np.triu(np.ones((tm, tm), np.float32), 1), dtype=BF16)
    cols = pl.BlockSpec((ne, tm), lambda i: (0, i))
    per_tile = pl.BlockSpec((1, ne, 1), lambda i: (i, 0, 0))
    return pl.pallas_call(
        _router_kernel,
        out_shape=(jax.ShapeDtypeStruct((m, d), F32),
                   jax.ShapeDtypeStruct((ne, m), F32),
                   jax.ShapeDtypeStruct((ne, m), I32),
                   jax.ShapeDtypeStruct((n_tiles, ne, 1), F32),
                   jax.ShapeDtypeStruct((n_tiles, ne, 1), F32),
                   jax.ShapeDtypeStruct((ne, 1), F32)),
        grid=(n_tiles,),
        in_specs=[pl.BlockSpec((tm, d), lambda i: (i, 0)), _const_spec((1, d)), _const_spec((ne, d)),
                  _const_spec((ne, 1)), _const_spec((ne, 1)), _const_spec((tm, tm))],
        out_specs=(pl.BlockSpec((tm, d), lambda i: (i, 0)), cols, cols, per_tile, per_tile,
                   _const_spec((ne, 1))),
        scratch_shapes=[pltpu.VMEM((ne, 1), F32)],
        compiler_params=_cparams(("arbitrary",)),
        name=name,
    )(x, g.reshape(1, d), wr.T, br.reshape(ne, 1), cnt0, tri)


def _staging_rows(n_tokens):
    return TOP_K * n_tokens + V7X_SUBLANES * N_EXPERTS


def _expert_run_copies(tile, dst_ref, cnt_ref, n_tokens, make_copy, wait):
    local = jnp.int32(0)
    for e in range(N_EXPERTS):
        count = cnt_ref[tile * N_EXPERTS + e]
        sorted_row = dst_ref[tile * N_EXPERTS + e]
        size = pl.next_power_of_2(n_tokens)
        while size >= V7X_SUBLANES:
            has = (count & size) != 0

            @pl.when(has)
            def _(local=local, sorted_row=sorted_row, size=size):
                copy = make_copy(pl.multiple_of(local, V7X_SUBLANES), pl.multiple_of(sorted_row, V7X_SUBLANES), size)
                if wait:
                    copy.wait()
                else:
                    copy.start()

            step = jnp.where(has, size, 0)
            local = local + step
            sorted_row = sorted_row + step
            size //= 2


def _dispatch_kernel(dst_ref, cnt_ref, hn_ref, lp_ref, xs_in_ref, xs_ref, stage_ref, sem, *, tmd):
    del xs_in_ref
    i = pl.program_id(0)
    n = pl.num_programs(0)
    slot = i % 2

    def copies(tile, s, wait):
        _expert_run_copies(tile, dst_ref, cnt_ref, tmd, lambda local, sorted_row, size: pltpu.make_async_copy(
            stage_ref.at[s, pl.ds(local, size)], xs_ref.at[pl.ds(sorted_row, size)], sem.at[s]), wait)

    @pl.when(i >= 2)
    def _():
        copies(i - 2, slot, True)

    lp = lp_ref[...]
    row = lax.broadcasted_iota(I32, (stage_ref.shape[1], tmd), 0)
    place = (jnp.where(row == lp[0:1, :], 1.0, 0.0) + jnp.where(row == lp[1:2, :], 1.0, 0.0)).astype(BF16)
    stage_ref[slot] = jnp.dot(place, hn_ref[...].astype(BF16), preferred_element_type=F32)
    copies(i, slot, False)

    @pl.when(i == n - 1)
    def _():
        copies(i, slot, True)

        @pl.when(n >= 2)
        def _():
            copies(i - 1, 1 - slot, True)


def _dispatch(hn, lp, dst, cnt, xs, *, name):
    m, d = hn.shape
    ne = lp.shape[0]
    tmd = min(m, MOE_TOKEN_TILE)
    assert m % tmd == 0
    kern = functools.partial(_dispatch_kernel, tmd=tmd)
    grid_spec = pltpu.PrefetchScalarGridSpec(
        num_scalar_prefetch=2,
        grid=(m // tmd,),
        in_specs=[
            pl.BlockSpec((tmd, d), lambda i, dst, cnt: (i, 0)),
            pl.BlockSpec((ne, tmd), lambda i, dst, cnt: (0, i)),
            pl.BlockSpec(memory_space=pl.ANY),
        ],
        out_specs=pl.BlockSpec(memory_space=pl.ANY),
        scratch_shapes=[pltpu.VMEM((2, _staging_rows(tmd), d), F32), pltpu.SemaphoreType.DMA((2,))],
    )
    return pl.pallas_call(
        kern,
        out_shape=jax.ShapeDtypeStruct(xs.shape, xs.dtype),
        grid_spec=grid_spec,
        input_output_aliases={4: 0},
        compiler_params=_cparams(("arbitrary",)),
        name=name,
    )(dst, cnt, hn, lp, xs)


def _moe_ffn_kernel(te_ref, nused_ref, x_ref, wg_ref, wu_ref, wd_ref, o_ref, xb_ref, acc_ref):
    t = pl.program_id(0)
    j = pl.program_id(1)
    last = pl.num_programs(1) - 1
    used = t < nused_ref[0]

    @pl.when(used)
    def _():
        @pl.when(j == 0)
        def _():
            xb_ref[...] = x_ref[...].astype(BF16)
            acc_ref[...] = jnp.zeros_like(acc_ref)

        xb = xb_ref[...]
        gate = jnp.dot(xb, wg_ref[...].astype(BF16), preferred_element_type=F32)
        up = jnp.dot(xb, wu_ref[...].astype(BF16), preferred_element_type=F32)
        act = (jax.nn.silu(gate) * up).astype(BF16)
        acc_ref[...] += jnp.dot(act, wd_ref[...].astype(BF16), preferred_element_type=F32)

        @pl.when(j == last)
        def _():
            o_ref[...] = acc_ref[...]

    @pl.when(jnp.logical_and(jnp.logical_not(used), j == last))
    def _():
        o_ref[...] = jnp.zeros_like(o_ref)


def _moe_ffn(xs, te, n_used, w_gu, w_d):
    n_rows, d = xs.shape
    hid = w_d.shape[1]
    tm, th = MOE_TM, MOE_TH
    nj = hid // th
    nt = n_rows // tm
    assert hid % th == 0 and n_rows % tm == 0

    def jj(t, j, nu):
        return jnp.where(t < nu[0], j, nj - 1)

    grid_spec = pltpu.PrefetchScalarGridSpec(
        num_scalar_prefetch=2,
        grid=(nt, nj),
        in_specs=[
            pl.BlockSpec((tm, d), lambda t, j, te, nu: (jnp.maximum(jnp.minimum(t, nu[0] - 1), 0), 0)),
            pl.BlockSpec((None, d, th), lambda t, j, te, nu: (te[t], 0, jj(t, j, nu))),
            pl.BlockSpec((None, d, th), lambda t, j, te, nu: (te[t], 0, jj(t, j, nu) + nj)),
            pl.BlockSpec((None, th, d), lambda t, j, te, nu: (te[t], jj(t, j, nu), 0)),
        ],
        out_specs=pl.BlockSpec((tm, d), lambda t, j, te, nu: (t, 0)),
        scratch_shapes=[pltpu.VMEM((tm, d), BF16), pltpu.VMEM((tm, d), F32)],
    )
    return pl.pallas_call(
        _moe_ffn_kernel,
        out_shape=jax.ShapeDtypeStruct((n_rows, d), F32),
        grid_spec=grid_spec,
        compiler_params=_cparams(("arbitrary", "arbitrary")),
        name="moe_experts",
    )(te, n_used, xs, w_gu, w_gu, w_d)


def _combine_kernel(dst_ref, cnt_ref, x_ref, gate_ref, lp_ref, ys_ref, o_ref, buf_ref, sem, *, tmc):
    i = pl.program_id(0)
    n = pl.num_programs(0)
    slot = i % 2

    def gather(tile, s, wait):
        _expert_run_copies(tile, dst_ref, cnt_ref, tmc, lambda local, sorted_row, size: pltpu.make_async_copy(
            ys_ref.at[pl.ds(sorted_row, size)], buf_ref.at[s, pl.ds(local, size)], sem.at[s]), wait)

    @pl.when(i == 0)
    def _():
        buf_ref[...] = jnp.zeros_like(buf_ref)
        gather(i, slot, False)

    @pl.when(i + 1 < n)
    def _():
        gather(i + 1, 1 - slot, False)

    gather(i, slot, True)
    g = gate_ref[...]
    lp = lp_ref[...]
    row = lax.broadcasted_iota(I32, (buf_ref.shape[1], tmc), 0)
    w = jnp.where(row == lp[0:1, :], g[0:1, :], 0.0) + jnp.where(row == lp[1:2, :], g[1:2, :], 0.0)
    rows = buf_ref[slot]
    w_hi = w.astype(BF16)
    w_lo = (w - w_hi.astype(F32)).astype(BF16)
    r_hi = rows.astype(BF16)
    r_lo = (rows - r_hi.astype(F32)).astype(BF16)
    tn_dot = lambda a, b: lax.dot_general(a, b, (((0,), (0,)), ((), ())), preferred_element_type=F32)
    o_ref[...] = x_ref[...] + tn_dot(w_hi, r_hi) + tn_dot(w_hi, r_lo) + tn_dot(w_lo, r_hi)


def _combine(x, gate, lp, dst, cnt, ys, *, name):
    m, d = x.shape
    ne = gate.shape[0]
    tmc = min(m, MOE_TOKEN_TILE)
    assert m % tmc == 0
    kern = functools.partial(_combine_kernel, tmc=tmc)
    row = lambda width: pl.BlockSpec((tmc, width), lambda i, dst, cnt: (i, 0))
    cols = pl.BlockSpec((ne, tmc), lambda i, dst, cnt: (0, i))
    grid_spec = pltpu.PrefetchScalarGridSpec(
        num_scalar_prefetch=2,
        grid=(m // tmc,),
        in_specs=[row(d), cols, cols, pl.BlockSpec(memory_space=pl.ANY)],
        out_specs=row(d),
        scratch_shapes=[pltpu.VMEM((2, _staging_rows(tmc), d), F32), pltpu.SemaphoreType.DMA((2,))],
    )
    return pl.pallas_call(
        kern,
        out_shape=jax.ShapeDtypeStruct((m, d), F32),
        grid_spec=grid_spec,
        compiler_params=_cparams(("arbitrary",)),
        name=name,
    )(dst, cnt, x, gate, lp, ys)


def _moe_plan(counts, tile_bases, tile_counts, tm, nt):
    padded = ((counts + tm - 1) // tm) * tm
    ends = jnp.cumsum(padded)
    starts = ends - padded
    dst = [(starts[None, :] + b[:, :, 0].astype(I32)).reshape(-1) for b in tile_bases]
    cnt = [c[:, :, 0].astype(I32).reshape(-1) for c in tile_counts]
    n_used = (ends[-1] // tm).astype(I32)
    tile = jnp.arange(nt, dtype=I32)
    first_row = jnp.minimum(tile, n_used - 1) * tm
    te = jnp.minimum(jnp.sum((first_row[:, None] >= ends[None, :]).astype(I32), axis=1), N_EXPERTS - 1)
    return dst, cnt, te, n_used.reshape(1)


def _block_diag(w):
    n, a, b = w.shape
    return jnp.einsum("nij,nm->nimj", w, jnp.eye(n, dtype=w.dtype)).reshape(n * a, n * b)


def kernel(x_prompt, x_sample, state_conv, state_lru, state_s5_re, state_s5_im, cache_swa0_kv, cache_swa1_kv, cache_swa2_kv, cache_mem_k, cache_mem_v, mem_prompt, norm_mix, norm_xa, norm_ffn, norm_mem, w_in_even, conv_w, conv_b, lru_wa, lru_ba, lru_wx, lru_bx, lru_lam, s5_lam_re, s5_lam_im, s5_log_dt, s5_b_re, s5_b_im, s5_c_re, s5_c_im, s5_d, s5_w_glu, s5_b_glu, w_out_even, w_qkv_odd, q_norm_odd, k_norm_odd, w_o_odd, rel_bias, xa_wq, xa_wkv, xa_qn, xa_kn, xa_wo, ffn_w_gu, ffn_w_down, moe_router_w, moe_router_b, moe_w_gu, moe_w_down):
    bp, s_len, d = x_prompt.shape
    db, t_dec, _ = x_sample.shape
    n_mem = mem_prompt.shape[1]
    tp = SAMPLE_PAD_T
    d_lru = conv_w.shape[-1]
    s5_g, s5_p, s5_h = s5_b_re.shape[1:]
    n_state = s5_g * s5_p
    d_s5 = s5_g * s5_h
    xa_dh = d // XA_HEADS
    caches = (cache_swa0_kv, cache_swa1_kv, cache_swa2_kv)
    bf = lambda w: w.astype(BF16)

    yp = x_prompt
    ys = jnp.pad(x_sample, ((0, 0), (0, tp - t_dec), (0, 0)))

    w_in = bf(w_in_even[0])
    bbr, bbi, apow = _s5_prep(s5_lam_re[0], s5_lam_im[0], s5_log_dt[0], s5_b_re[0], s5_b_im[0])
    n_slab = d_s5 // V7X_LANES
    gl = s5_g // n_slab
    eye_g = jnp.eye(gl, dtype=F32)

    def c_blocks(c_mat):
        blocks = jnp.einsum("cghp,gk->cgpkh", c_mat.reshape(n_slab, gl, s5_h, s5_p), eye_g)
        return bf(blocks.reshape(n_slab, gl * s5_p, V7X_LANES))

    mix_w = dict(
        cw=conv_w[0], cb=conv_b[0].reshape(1, d_lru),
        wa=bf(_block_diag(lru_wa[0])), ba=lru_ba[0].reshape(1, d_lru),
        wx=bf(_block_diag(lru_wx[0])), bx=lru_bx[0].reshape(1, d_lru),
        lam=lru_lam[0].reshape(1, d_lru),
        bbr=bbr, bbi=bbi, apow=apow,
        ccr=c_blocks(s5_c_re[0]), cci=c_blocks(s5_c_im[0]),
        d=s5_d[0].reshape(1, d_s5), wglu=bf(s5_w_glu[0]), bglu=s5_b_glu[0].reshape(1, d_s5),
        wout=bf(w_out_even[0]),
    )
    z_p = _norm_matmul(yp.reshape(bp * s_len, d), norm_mix[0], w_in, name="in_proj_p").reshape(bp, s_len, -1)
    yp, p_lru, p_s5r, p_s5i = _mixer0(
        z_p, yp, jnp.zeros((bp, V7X_SUBLANES, d_lru), F32), jnp.zeros((bp, 1, d_lru), F32),
        jnp.zeros((bp, 1, n_state), F32), jnp.zeros((bp, 1, n_state), F32), mix_w,
        tc=MIX_TC, last_row=MIX_TC - 1)
    z_s = _norm_matmul(ys.reshape(db * tp, d), norm_mix[0], w_in, name="in_proj_s").reshape(db, tp, -1)
    conv_init = jnp.pad(state_conv[0], ((0, 0), (V7X_SUBLANES - (CONV_W - 1), 0), (0, 0)))
    ys, s_lru, s_s5r, s_s5i = _mixer0(
        z_s, ys, conv_init, state_lru[0].reshape(db, 1, d_lru),
        state_s5_re[0].reshape(db, 1, n_state), state_s5_im[0].reshape(db, 1, n_state), mix_w,
        tc=tp, last_row=t_dec - 1)
    p_state_conv = z_p[:, s_len - (CONV_W - 1):, :d_lru][None]
    s_state_conv = z_s[:, t_dec - (CONV_W - 1):t_dec, :d_lru][None]

    p_mk, p_mv = [], []

    def cross_attention(layer, yp, ys):
        kn_gain = jnp.concatenate([jnp.tile(xa_kn[layer], XA_HEADS), jnp.ones((d,), F32)]).reshape(1, 2 * d)
        kv = _norm_matmul(mem_prompt.reshape(bp * n_mem, d), norm_mem[layer], bf(xa_wkv[layer]),
                          head_gain=kn_gain, n_norm_cols=d, dh=xa_dh, name=f"mem_kv{layer}")
        kv = kv.reshape(bp, n_mem, 2 * d)
        mk, mv = kv[:, :, :d], kv[:, :, d:]
        p_mk.append(mk.reshape(bp, n_mem, XA_HEADS, xa_dh))
        p_mv.append(mv.reshape(bp, n_mem, XA_HEADS, xa_dh))
        wq, wo = bf(xa_wq[layer]), bf(xa_wo[layer])
        yp = _xattn(yp, norm_xa[layer], wq, xa_qn[layer], kv, kv, wo, tm=512, name=f"xattn_p{layer}")
        ys = _xattn(ys, norm_xa[layer], wq, xa_qn[layer], cache_mem_k, cache_mem_v, wo,
                    tm=tp, nb=4, name=f"xattn_s{layer}", layer=layer)
        return yp, ys

    yp, ys = cross_attention(0, yp, ys)
    w_gu0, w_d0 = bf(ffn_w_gu[0]), bf(ffn_w_down[0])
    yp = _ffn(yp.reshape(bp * s_len, d), norm_ffn[0], w_gu0, w_d0, name="ffn_p").reshape(bp, s_len, d)
    ys = _ffn(ys.reshape(db * tp, d), norm_ffn[0], w_gu0, w_d0, name="ffn_s").reshape(db, tp, d)

    n_heads = len(WINDOWS) * HG
    d_c = n_heads * DH
    gw = HG * DH
    w_qkv = bf(w_qkv_odd[0])
    qk_gain = jnp.concatenate([jnp.tile(q_norm_odd[0], n_heads), jnp.tile(k_norm_odd[0], n_heads),
                               jnp.ones((d_c,), F32)]).reshape(1, 3 * d_c)
    qkv_p = _norm_matmul(yp.reshape(bp * s_len, d), norm_mix[1], w_qkv, head_gain=qk_gain,
                         n_norm_cols=2 * d_c, dh=DH, slabs=True, name="qkv_p")
    qkv_s = _norm_matmul(ys.reshape(db * tp, d), norm_mix[1], w_qkv, head_gain=qk_gain,
                         n_norm_cols=2 * d_c, dh=DH, slabs=True, name="qkv_s")
    gs = gw // V7X_LANES
    w_o = bf(w_o_odd[0])
    outs_p, lses_p, outs_s, lses_s, p_swa, s_swa = [], [], [], [], [], []

    def kv_rows(qkv, n_batch, t_len, g, lo, hi):
        ks = (len(WINDOWS) + g) * gs
        vs = (2 * len(WINDOWS) + g) * gs
        slabs = qkv.reshape(-1, n_batch, t_len, V7X_LANES)
        kv = jnp.stack([slabs[ks:ks + gs, :, lo:hi], slabs[vs:vs + gs, :, lo:hi]])
        return jnp.transpose(kv, (2, 3, 0, 1, 4)).reshape(n_batch, hi - lo, 2, HG, DH)

    for g, dil in enumerate(DILATIONS):
        tab = rel_bias[:, g * HG:(g + 1) * HG]
        o, l = _dilated_prompt(qkv_p, tab, g, dil, bp)
        outs_p.append(o)
        lses_p.append(l)
        cache_t = jnp.transpose(caches[g][0], (0, 2, 3, 4, 1))
        o, l, cache_t = _dilated_sample(qkv_s, cache_t, tab, g, dil, t_dec)
        outs_s.append(o)
        lses_s.append(l)
        s_swa.append(jnp.transpose(cache_t, (0, 4, 1, 2, 3))[None])
        win = min(WINDOWS[g], s_len)
        p_swa.append(kv_rows(qkv_p, bp, s_len, g, s_len - win, s_len)[None])
    yp = _merge_wo(outs_p, lses_p, yp.reshape(bp * s_len, d), w_o, name="merge_wo_p").reshape(bp, s_len, d)
    ys = _merge_wo(outs_s, lses_s, ys.reshape(db * tp, d), w_o, name="merge_wo_s").reshape(db, tp, d)

    yp, ys = cross_attention(1, yp, ys)

    yp2 = yp.reshape(bp * s_len, d)
    ys2 = ys[:, :t_dec].reshape(db * t_dec, d)
    zero_cnt = jnp.zeros((N_EXPERTS, 1), F32)
    hn_p, gate_p, lp_p, base_p, tcnt_p, cnt_p = _router(
        yp2, norm_ffn[1], moe_router_w[0], moe_router_b[0], zero_cnt, name="router_p")
    hn_s, gate_s, lp_s, base_s, tcnt_s, cnt_s = _router(
        ys2, norm_ffn[1], moe_router_w[0], moe_router_b[0], cnt_p, name="router_s")
    n_tok = yp2.shape[0] + ys2.shape[0]
    n_runs = N_EXPERTS * (base_p.shape[0] + base_s.shape[0])
    nt = -(-(n_tok * TOP_K + (V7X_SUBLANES - 1) * n_runs) // MOE_TM) + N_EXPERTS
    (dst_p, dst_s), (rc_p, rc_s), te, n_used = _moe_plan(
        cnt_s[:, 0].astype(I32), (base_p, base_s), (tcnt_p, tcnt_s), MOE_TM, nt)
    xs = jnp.zeros((nt * MOE_TM, d), F32)
    xs = _dispatch(hn_p, lp_p, dst_p, rc_p, xs, name="dispatch_p")
    xs = _dispatch(hn_s, lp_s, dst_s, rc_s, xs, name="dispatch_s")
    ysort = _moe_ffn(xs, te, n_used, moe_w_gu[0], moe_w_down[0])
    yp = _combine(yp2, gate_p, lp_p, dst_p, rc_p, ysort, name="moe_combine_p").reshape(bp, s_len, d)
    ys = _combine(ys2, gate_s, lp_s, dst_s, rc_s, ysort, name="moe_combine_s").reshape(db, t_dec, d)

    return (yp, ys,
            p_state_conv, p_lru.reshape(1, bp, d_lru),
            p_s5r.reshape(1, bp, s5_g, s5_p), p_s5i.reshape(1, bp, s5_g, s5_p),
            p_swa[0], p_swa[1], p_swa[2], jnp.stack(p_mk), jnp.stack(p_mv),
            s_state_conv, s_lru.reshape(1, db, d_lru),
            s_s5r.reshape(1, db, s5_g, s5_p), s_s5i.reshape(1, db, s5_g, s5_p),
            s_swa[0], s_swa[1], s_swa[2])
```

```python
import functools
import math

import jax
import jax.numpy as jnp
import numpy as np
from jax import lax
from jax.experimental import pallas as pl
from jax.experimental.pallas import tpu as pltpu

F32 = jnp.float32
BF16 = jnp.bfloat16
I32 = jnp.int32

EPS = 1e-6
NEG = -1e30

V7X_SUBLANES = 8
V7X_LANES = 128
V7X_VMEM_BYTES = 64 * 1024 * 1024
VMEM_LIMIT = V7X_VMEM_BYTES - 8 * 1024 * 1024

LRU_C = 8.0
CONV_W = 4
WINDOWS = (128, 512, 2048)
DILATIONS = (1, 4, 16)
HG = 8
DH = 64
N_STRIDE = 128
QB = 128
REL_BUCKETS = 32
REL_MAX_DIST = WINDOWS[-1]
XA_HEADS = 4
N_EXPERTS = 8
TOP_K = 2
SAMPLE_PAD_T = 16

MIX_TC = 256
MOE_TM = 1024
MOE_TH = 512
MOE_TOKEN_TILE = 256
FFN_TH = 512


def _cparams(sem):
    return pltpu.CompilerParams(dimension_semantics=sem, vmem_limit_bytes=VMEM_LIMIT)


def _const_spec(shape):
    nd = len(shape)
    return pl.BlockSpec(shape, lambda *_: (0,) * nd)


def _rms(x, g):
    return x * lax.rsqrt(jnp.mean(x * x, axis=-1, keepdims=True) + EPS) * g


def _norm_matmul_kernel(x_ref, g_ref, w_ref, hg_ref, hs_ref, o_ref, hn_ref, *, n_norm_tiles, dh, slabs,
                        row_split):
    j = pl.program_id(1)

    @pl.when(j == 0)
    def _():
        hn_ref[...] = _rms(x_ref[...], g_ref[...]).astype(BF16)

    tm = hn_ref.shape[0]
    rs = tm // row_split

    def emit(r, val):
        rows = slice(r * rs, (r + 1) * rs)
        if slabs:
            for c in range(o_ref.shape[0]):
                o_ref[c, rows, :] = val[:, c * V7X_LANES:(c + 1) * V7X_LANES]
        else:
            o_ref[rows, :] = val

    def tile(r, normed):
        y = jnp.dot(hn_ref[r * rs:(r + 1) * rs, :], w_ref[...], preferred_element_type=F32)
        if normed:
            sw = hs_ref.shape[0]
            parts = []
            for s in range(y.shape[1] // sw):
                part = y[:, s * sw:(s + 1) * sw]
                ssq = jnp.dot((part * part).astype(BF16), hs_ref[...], preferred_element_type=F32)
                parts.append(part * lax.rsqrt(ssq * (1.0 / dh) + EPS) * hg_ref[:, s * sw:(s + 1) * sw])
            y = parts[0] if len(parts) == 1 else jnp.concatenate(parts, axis=-1)
        emit(r, y)

    if n_norm_tiles == 0:
        for r in range(row_split):
            tile(r, False)
    else:
        @pl.when(j < n_norm_tiles)
        def _():
            for r in range(row_split):
                tile(r, True)

        @pl.when(j >= n_norm_tiles)
        def _():
            for r in range(row_split):
                tile(r, False)


def _head_sum_matrix(tn, dh):
    idx = np.arange(tn) // dh
    return jnp.asarray((idx[:, None] == idx[None, :]).astype(np.float32), dtype=BF16)


def _norm_matmul(x, g, w, *, tn=512, head_gain=None, n_norm_cols=0, dh=1, slabs=False, name):
    m, d = x.shape
    n = w.shape[1]
    tm = min(m, 1024)
    assert m % tm == 0 and n % tn == 0 and n_norm_cols % tn == 0
    if head_gain is None:
        head_gain = jnp.ones((1, n), F32)
    hs = _head_sum_matrix(min(tn, 512), dh) if n_norm_cols else jnp.zeros((V7X_SUBLANES, V7X_LANES), BF16)
    row_split = 2 if tm % 512 == 0 else 1
    kern = functools.partial(_norm_matmul_kernel, n_norm_tiles=n_norm_cols // tn, dh=dh, slabs=slabs,
                             row_split=row_split)
    if slabs:
        out_shape = jax.ShapeDtypeStruct((n // V7X_LANES, m, V7X_LANES), F32)
        out_spec = pl.BlockSpec((tn // V7X_LANES, tm, V7X_LANES), lambda i, j: (j, i, 0))
    else:
        out_shape = jax.ShapeDtypeStruct((m, n), F32)
        out_spec = pl.BlockSpec((tm, tn), lambda i, j: (i, j))
    return pl.pallas_call(
        kern,
        out_shape=out_shape,
        grid=(m // tm, n // tn),
        in_specs=[
            pl.BlockSpec((tm, d), lambda i, j: (i, 0)),
            pl.BlockSpec((1, d), lambda i, j: (0, 0)),
            pl.BlockSpec((d, tn), lambda i, j: (0, j)),
            pl.BlockSpec((1, tn), lambda i, j: (0, j)),
            _const_spec(hs.shape),
        ],
        out_specs=out_spec,
        scratch_shapes=[pltpu.VMEM((tm, d), BF16)],
        compiler_params=_cparams(("parallel", "arbitrary")),
        name=name,
    )(x, g.reshape(1, d), w, head_gain, hs)


def _s5_prep_kernel(lre_ref, lim_ref, ldt_ref, bre_ref, bim_ref, bbr_ref, bbi_ref, apow_ref):
    lr = lre_ref[...]
    li = lim_ref[...]
    dt = jnp.exp(ldt_ref[...])
    mag = jnp.exp(lr * dt)
    ab_r = mag * jnp.cos(li * dt)
    ab_i = mag * jnp.sin(li * dt)
    den = lr * lr + li * li
    nr = ab_r - 1.0
    cr = (nr * lr + ab_i * li) / den
    ci = (ab_i * lr - nr * li) / den
    n_slab, _, sw = bre_ref.shape
    for c in range(n_slab):
        cr_c = cr[:, c * sw:(c + 1) * sw]
        ci_c = ci[:, c * sw:(c + 1) * sw]
        b_r = bre_ref[c]
        b_i = bim_ref[c]
        bbr_ref[c] = (cr_c * b_r - ci_c * b_i).astype(BF16)
        bbi_ref[c] = (cr_c * b_i + ci_c * b_r).astype(BF16)

    n = lr.shape[1]
    row = lax.broadcasted_iota(I32, (V7X_SUBLANES, n), 0)

    def power(kf):
        mg = jnp.exp(kf * (lr * dt))
        return mg * jnp.cos(kf * (li * dt)), mg * jnp.sin(kf * (li * dt))

    for i, s in enumerate((1, 2, 4)):
        pr, pi = power(jnp.full((V7X_SUBLANES, n), s, F32))
        keep = row >= s
        apow_ref[2 * i] = jnp.where(keep, pr, 0.0)
        apow_ref[2 * i + 1] = jnp.where(keep, pi, 0.0)
    pr, pi = power((row + 1).astype(F32))
    apow_ref[6] = pr
    apow_ref[7] = pi


def _s5_prep(lam_re, lam_im, log_dt, b_re, b_im):
    g, p, h = b_re.shape
    n = g * p
    n_slab = g * h // V7X_LANES
    gl = g // n_slab
    eye = jnp.eye(gl, dtype=F32)

    def slab_blocks(b):
        return jnp.einsum("cgph,gk->ckhgp", b.reshape(n_slab, gl, p, h), eye).reshape(n_slab, V7X_LANES, gl * p)

    bre_bd = slab_blocks(b_re)
    bim_bd = slab_blocks(b_im)
    ldt = jnp.broadcast_to(log_dt[:, None], (g, p)).reshape(1, n)
    return pl.pallas_call(
        _s5_prep_kernel,
        out_shape=(
            jax.ShapeDtypeStruct(bre_bd.shape, BF16),
            jax.ShapeDtypeStruct(bre_bd.shape, BF16),
            jax.ShapeDtypeStruct((8, V7X_SUBLANES, n), F32),
        ),
        compiler_params=pltpu.CompilerParams(vmem_limit_bytes=VMEM_LIMIT),
        name="s5_prep",
    )(lam_re.reshape(1, n), lam_im.reshape(1, n), ldt, bre_bd, bim_bd)


def _mixer0_kernel(z_ref, x_ref, convi_ref, h0_ref, sr0_ref, si0_ref,
                   cw_ref, cb_ref, wa_ref, ba_ref, wx_ref, bx_ref, lam_ref,
                   bbr_ref, bbi_ref, apow_ref, ccr_ref, cci_ref, d_ref, wglu_ref, bglu_ref, wout_ref,
                   y_ref, hl_ref, srl_ref, sil_ref,
                   ext_ref, hc_ref, src_ref, sic_ref, xr_ref, xi_ref, ha_ref, hb_ref,
                   *, tc, last_row):
    c = pl.program_id(1)
    d_lru = cw_ref.shape[1]
    d_s5 = d_ref.shape[1]
    n_tiles = tc // V7X_SUBLANES

    @pl.when(c == 0)
    def _():
        ext_ref[...] = convi_ref[...]
        hc_ref[...] = h0_ref[...]
        src_ref[...] = sr0_ref[...]
        sic_ref[...] = si0_ref[...]

    z = z_ref[...]
    xa = z[:, :d_lru]
    ga = z[:, d_lru:2 * d_lru]
    u = z[:, 2 * d_lru:]

    ext = jnp.concatenate([ext_ref[...], xa], axis=0)
    xc = cb_ref[...] + xa * cw_ref[CONV_W - 1:CONV_W, :]
    for s in range(1, CONV_W):
        xc = xc + pltpu.roll(ext, s, 0)[V7X_SUBLANES:, :] * cw_ref[CONV_W - 1 - s:CONV_W - s, :]
    ext_ref[...] = xa[tc - V7X_SUBLANES:, :]

    xcb = xc.astype(BF16)
    r = jax.nn.sigmoid(jnp.dot(xcb, wa_ref[...], preferred_element_type=F32) + ba_ref[...])
    ig = jax.nn.sigmoid(jnp.dot(xcb, wx_ref[...], preferred_element_type=F32) + bx_ref[...])
    lam = lam_ref[...]
    softplus_neg = jnp.maximum(-lam, 0.0) + jnp.log1p(jnp.exp(-jnp.abs(lam)))
    log_a = -LRU_C * r * softplus_neg
    a = jnp.exp(log_a)
    bt = jnp.sqrt(-jnp.tanh(log_a) * (a * a + 1.0)) * ig * xc

    a3 = a.reshape(n_tiles, V7X_SUBLANES, d_lru)
    b3 = bt.reshape(n_tiles, V7X_SUBLANES, d_lru)
    row = lax.broadcasted_iota(I32, (1, V7X_SUBLANES, d_lru), 1)
    for s in (1, 2, 4):
        keep = row >= s
        ar = pltpu.roll(a3, s, 1)
        br = pltpu.roll(b3, s, 1)
        b3 = jnp.where(keep, a3 * br + b3, b3)
        a3 = jnp.where(keep, a3 * ar, a3)
    ha_ref[...] = a3.reshape(tc, d_lru)
    hb_ref[...] = b3.reshape(tc, d_lru)

    def lru_tile(i, carry):
        r0 = pl.multiple_of(i * V7X_SUBLANES, V7X_SUBLANES)
        h = ha_ref[pl.ds(r0, V7X_SUBLANES), :] * carry + hb_ref[pl.ds(r0, V7X_SUBLANES), :]
        hb_ref[pl.ds(r0, V7X_SUBLANES), :] = h
        return h[V7X_SUBLANES - 1:, :]

    hc_ref[...] = lax.fori_loop(0, n_tiles, lru_tile, hc_ref[...])
    hs = hb_ref[...]
    hl_ref[...] = hb_ref[last_row:last_row + 1, :]
    ya = hs * jax.nn.gelu(ga)

    ub = u.astype(BF16)
    n_slab, _, sw = bbr_ref.shape
    n_state = n_slab * sw

    def input_dot(w_ref):
        return jnp.concatenate(
            [jnp.dot(ub[:, c * V7X_LANES:(c + 1) * V7X_LANES], w_ref[c], preferred_element_type=F32)
             for c in range(n_slab)], axis=-1)

    xr3 = input_dot(bbr_ref).reshape(n_tiles, V7X_SUBLANES, n_state)
    xi3 = input_dot(bbi_ref).reshape(n_tiles, V7X_SUBLANES, n_state)
    for i, s in enumerate((1, 2, 4)):
        cr = apow_ref[2 * i][None]
        ci = apow_ref[2 * i + 1][None]
        rr = pltpu.roll(xr3, s, 1)
        ri = pltpu.roll(xi3, s, 1)
        xr3, xi3 = xr3 + cr * rr - ci * ri, xi3 + cr * ri + ci * rr
    xr_ref[...] = xr3.reshape(tc, n_state)
    xi_ref[...] = xi3.reshape(tc, n_state)

    def s5_tile(i, carry):
        cr_, ci_ = carry
        r0 = pl.multiple_of(i * V7X_SUBLANES, V7X_SUBLANES)
        pr = apow_ref[6]
        pi = apow_ref[7]
        nr_ = xr_ref[pl.ds(r0, V7X_SUBLANES), :] + pr * cr_ - pi * ci_
        ni_ = xi_ref[pl.ds(r0, V7X_SUBLANES), :] + pr * ci_ + pi * cr_
        xr_ref[pl.ds(r0, V7X_SUBLANES), :] = nr_
        xi_ref[pl.ds(r0, V7X_SUBLANES), :] = ni_
        return nr_[V7X_SUBLANES - 1:, :], ni_[V7X_SUBLANES - 1:, :]

    cr_f, ci_f = lax.fori_loop(0, n_tiles, s5_tile, (src_ref[...], sic_ref[...]))
    src_ref[...] = cr_f
    sic_ref[...] = ci_f
    srl_ref[...] = xr_ref[last_row:last_row + 1, :]
    sil_ref[...] = xi_ref[last_row:last_row + 1, :]

    def output_dot(x_ref_, w_ref):
        return jnp.concatenate(
            [jnp.dot(x_ref_[:, c * sw:(c + 1) * sw].astype(BF16), w_ref[c], preferred_element_type=F32)
             for c in range(n_slab)], axis=-1)

    ys = output_dot(xr_ref, ccr_ref) - output_dot(xi_ref, cci_ref)
    ys = ys + d_ref[...] * u
    gs = jax.nn.gelu(ys)
    yb = gs * jax.nn.sigmoid(jnp.dot(gs.astype(BF16), wglu_ref[...], preferred_element_type=F32)
                             + bglu_ref[...])

    y_ref[...] = (x_ref[...]
                  + jnp.dot(ya.astype(BF16), wout_ref[:d_lru, :], preferred_element_type=F32)
                  + jnp.dot(yb.astype(BF16), wout_ref[d_lru:, :], preferred_element_type=F32))


def _mixer0(z, x, conv_init, h0, sr0, si0, wts, *, tc, last_row):
    bn, t_len, d_in = z.shape
    d = x.shape[2]
    d_lru = wts["cw"].shape[1]
    n_state = wts["apow"].shape[-1]
    assert t_len % tc == 0
    kern = functools.partial(_mixer0_kernel, tc=tc, last_row=last_row)
    wnames = ("cw", "cb", "wa", "ba", "wx", "bx", "lam", "bbr", "bbi", "apow",
              "ccr", "cci", "d", "wglu", "bglu", "wout")
    wlist = [wts[k] for k in wnames]
    per_b = lambda shape: pl.BlockSpec((None,) + shape, lambda b, c: (b,) + (0,) * len(shape))
    in_specs = [
        pl.BlockSpec((None, tc, d_in), lambda b, c: (b, c, 0)),
        pl.BlockSpec((None, tc, d), lambda b, c: (b, c, 0)),
        per_b((V7X_SUBLANES, d_lru)), per_b((1, d_lru)), per_b((1, n_state)), per_b((1, n_state)),
    ] + [_const_spec(w.shape) for w in wlist]
    return pl.pallas_call(
        kern,
        out_shape=(
            jax.ShapeDtypeStruct((bn, t_len, d), F32),
            jax.ShapeDtypeStruct((bn, 1, d_lru), F32),
            jax.ShapeDtypeStruct((bn, 1, n_state), F32),
            jax.ShapeDtypeStruct((bn, 1, n_state), F32),
        ),
        grid=(bn, t_len // tc),
        in_specs=in_specs,
        out_specs=(
            pl.BlockSpec((None, tc, d), lambda b, c: (b, c, 0)),
            per_b((1, d_lru)), per_b((1, n_state)), per_b((1, n_state)),
        ),
        scratch_shapes=[
            pltpu.VMEM((V7X_SUBLANES, d_lru), F32),
            pltpu.VMEM((1, d_lru), F32),
            pltpu.VMEM((1, n_state), F32),
            pltpu.VMEM((1, n_state), F32),
            pltpu.VMEM((tc, n_state), F32),
            pltpu.VMEM((tc, n_state), F32),
            pltpu.VMEM((tc, d_lru), F32),
            pltpu.VMEM((tc, d_lru), F32),
        ],
        compiler_params=_cparams(("parallel", "arbitrary")),
        name="mixer0",
    )(z, x, conv_init, h0, sr0, si0, *wlist)


def _xattn_kernel(x_ref, g_ref, wq_ref, qg_ref, mk_ref, mv_ref, wo_ref, o_ref):
    nb, tm, d = x_ref.shape
    x = x_ref[...].reshape(nb * tm, d)
    q = jnp.dot(_rms(x, g_ref[...]).astype(BF16), wq_ref[...], preferred_element_type=F32)
    dh = qg_ref.shape[1]
    head_major = len(mk_ref.shape) == 4
    rows = []
    for b in range(nb):
        if head_major:
            kb = pltpu.einshape("nhd->hnd", mk_ref[b])
            vb = pltpu.einshape("nhd->hnd", mv_ref[b])
            head_of = lambda arr, h: arr[h]
        else:
            kb, vb = mk_ref[b], mv_ref[b]
            head_of = lambda arr, h: arr[:, h * dh:(h + 1) * dh]
        outs = []
        for h in range(XA_HEADS):
            qn = _rms(q[b * tm:(b + 1) * tm, h * dh:(h + 1) * dh], qg_ref[...]).astype(BF16)
            s = lax.dot_general(qn, head_of(kb, h).astype(BF16), (((1,), (1,)), ((), ())),
                                preferred_element_type=F32) * (dh ** -0.5)
            m = jnp.max(s, axis=-1, keepdims=True)
            p = jnp.exp(s - m)
            den = jnp.sum(p, axis=-1, keepdims=True)
            oh = jnp.dot(p.astype(BF16), head_of(vb, h).astype(BF16), preferred_element_type=F32) / den
            outs.append(oh.astype(BF16))
        rows.append(jnp.concatenate(outs, axis=-1))
    o = rows[0] if nb == 1 else jnp.concatenate(rows, axis=0)
    o_ref[...] = (x + jnp.dot(o, wo_ref[...], preferred_element_type=F32)).reshape(nb, tm, d)


def _xattn(x, g, wq, qg, mk, mv, wo, *, tm, nb=1, name, layer=None):
    bn, t_len, d = x.shape
    assert t_len % tm == 0 and bn % nb == 0
    if layer is None:
        k_spec = pl.BlockSpec((nb, mk.shape[1], d), lambda b, i: (b, 0, 0))
        v_spec = pl.BlockSpec((nb, mv.shape[1], d), lambda b, i: (b, 0, 1))
    else:
        k_spec = v_spec = pl.BlockSpec((None, nb) + mk.shape[2:], lambda b, i: (layer, b, 0, 0, 0))
    return pl.pallas_call(
        _xattn_kernel,
        out_shape=jax.ShapeDtypeStruct(x.shape, F32),
        grid=(bn // nb, t_len // tm),
        in_specs=[
            pl.BlockSpec((nb, tm, d), lambda b, i: (b, i, 0)),
            _const_spec((1, d)),
            _const_spec(wq.shape),
            _const_spec((1, qg.shape[-1])),
            k_spec,
            v_spec,
            _const_spec(wo.shape),
        ],
        out_specs=pl.BlockSpec((nb, tm, d), lambda b, i: (b, i, 0)),
        compiler_params=_cparams(("parallel", "arbitrary")),
        name=name,
    )(x, g.reshape(1, d), wq, qg.reshape(1, -1), mk, mv, wo)


def _ffn_kernel(x_ref, g_ref, wg_ref, wu_ref, wd_ref, o_ref, hn_ref, acc_ref):
    j = pl.program_id(1)

    @pl.when(j == 0)
    def _():
        hn_ref[...] = _rms(x_ref[...], g_ref[...]).astype(BF16)
        acc_ref[...] = jnp.zeros_like(acc_ref)

    hn = hn_ref[...]
    gate = jnp.dot(hn, wg_ref[...], preferred_element_type=F32)
    up = jnp.dot(hn, wu_ref[...], preferred_element_type=F32)
    act = (jax.nn.silu(gate) * up).astype(BF16)
    acc_ref[...] += jnp.dot(act, wd_ref[...], preferred_element_type=F32)

    @pl.when(j == pl.num_programs(1) - 1)
    def _():
        o_ref[...] = x_ref[...] + acc_ref[...]


def _ffn(x, g, w_gu, w_d, *, name):
    m, d = x.shape
    hid = w_d.shape[0]
    th = FFN_TH
    tm = min(m, 1024)
    nj = hid // th
    assert m % tm == 0 and hid % th == 0
    return pl.pallas_call(
        _ffn_kernel,
        out_shape=jax.ShapeDtypeStruct((m, d), F32),
        grid=(m // tm, nj),
        in_specs=[
            pl.BlockSpec((tm, d), lambda i, j: (i, 0)),
            pl.BlockSpec((1, d), lambda i, j: (0, 0)),
            pl.BlockSpec((d, th), lambda i, j: (0, j)),
            pl.BlockSpec((d, th), lambda i, j: (0, j + nj)),
            pl.BlockSpec((th, d), lambda i, j: (j, 0)),
        ],
        out_specs=pl.BlockSpec((tm, d), lambda i, j: (i, 0)),
        scratch_shapes=[pltpu.VMEM((tm, d), BF16), pltpu.VMEM((tm, d), F32)],
        compiler_params=_cparams(("parallel", "arbitrary")),
        name=name,
    )(x, g.reshape(1, d), w_gu, w_gu, w_d)


def _rel_bucket_np(dist):
    dist = np.clip(np.asarray(dist), 0, None)
    max_exact = REL_BUCKETS // 2
    safe = np.maximum(dist, max_exact).astype(np.float32)
    large = max_exact + np.floor(np.log(safe / max_exact) / math.log(REL_MAX_DIST / max_exact)
                                 * (REL_BUCKETS - max_exact)).astype(np.int32)
    large = np.minimum(large, REL_BUCKETS - 1)
    return np.where(dist < max_exact, dist, large).astype(np.int32)


def _dil_kernel(q_ref, k_ref, v_ref, bias_ref, o_ref, l_ref, kprev_ref, vprev_ref, *, d, nblk):
    n = pl.program_id(1)

    @pl.when(n == 0)
    def _():
        kprev_ref[...] = jnp.zeros_like(kprev_ref)
        vprev_ref[...] = jnp.zeros_like(vprev_ref)

    first_sel = jnp.minimum(n, 1)
    lane = lax.broadcasted_iota(I32, (QB, V7X_LANES), 1)
    n_slab = q_ref.shape[0]
    heads_per_slab = V7X_LANES // DH

    def residue(r, _):
        prev_k = [kprev_ref[r, c] for c in range(n_slab)]
        prev_v = [vprev_ref[r, c] for c in range(n_slab)]
        for blk in range(nblk):
            rows = pl.ds(blk * QB * d + r, QB, stride=d)
            bsel = first_sel if blk == 0 else 1
            lse = jnp.zeros((QB, V7X_LANES), F32)
            for c in range(n_slab):
                q = (q_ref[c, rows, :] * (DH ** -0.5)).astype(BF16)
                kc = k_ref[c, rows, :].astype(BF16)
                vc = v_ref[c, rows, :].astype(BF16)
                k = jnp.concatenate([prev_k[c], kc], axis=0)
                v = jnp.concatenate([prev_v[c], vc], axis=0)
                outs = []
                for hh in range(heads_per_slab):
                    h = c * heads_per_slab + hh
                    sl = slice(hh * DH, (hh + 1) * DH)
                    s = lax.dot_general(q[:, sl], k[:, sl], (((1,), (1,)), ((), ())),
                                        preferred_element_type=F32) + bias_ref[bsel, h]
                    m = jnp.max(s, axis=-1, keepdims=True)
                    p = jnp.exp(s - m)
                    den = jnp.sum(p, axis=-1, keepdims=True)
                    outs.append(jnp.dot(p.astype(BF16), v[:, sl], preferred_element_type=F32) / den)
                    lse = jnp.where(lane == h, m + jnp.log(den), lse)
                o_ref[c, rows, :] = jnp.concatenate(outs, axis=-1)
                prev_k[c] = kc
                prev_v[c] = vc
            l_ref[rows, :] = lse
        for c in range(n_slab):
            kprev_ref[r, c] = prev_k[c]
            vprev_ref[r, c] = prev_v[c]
        return 0

    lax.fori_loop(0, d, residue, 0, unroll=min(d, 2))


def _dilated_prompt(qkv, tab, g, d, bn):
    n_slabs, m_rows, _ = qkv.shape
    s_len = m_rows // bn
    gw = HG * DH
    gs = gw // V7X_LANES
    nblk = 2 if d == 1 else 1
    chunk = nblk * QB * d
    n_chunks = s_len // chunk
    assert s_len % chunk == 0
    k_col = n_slabs // gs // 3

    qi = np.arange(QB)[:, None]
    ki = np.arange(2 * QB)[None, :]
    dist = qi + QB - ki
    band = (dist >= 0) & (dist <= N_STRIDE)
    first = band & (ki >= QB)
    onehot = (np.arange(REL_BUCKETS)[:, None] == _rel_bucket_np(dist * d).reshape(1, -1)).astype(np.float32)
    bias = jnp.dot(tab.T.astype(F32), jnp.asarray(onehot),
                   precision=lax.Precision.HIGHEST).reshape(HG, QB, 2 * QB)
    bias2 = jnp.stack([jnp.where(first[None], bias, NEG), jnp.where(band[None], bias, NEG)])

    col = lambda section: (lambda b, n: (section * k_col + g, b * n_chunks + n, 0))
    blk = (gs, chunk, V7X_LANES)
    return pl.pallas_call(
        functools.partial(_dil_kernel, d=d, nblk=nblk),
        out_shape=(jax.ShapeDtypeStruct((gs, m_rows, V7X_LANES), F32),
                   jax.ShapeDtypeStruct((m_rows, V7X_LANES), F32)),
        grid=(bn, n_chunks),
        in_specs=[
            pl.BlockSpec(blk, col(0)), pl.BlockSpec(blk, col(1)), pl.BlockSpec(blk, col(2)),
            _const_spec(bias2.shape),
        ],
        out_specs=(pl.BlockSpec(blk, lambda b, n: (0, b * n_chunks + n, 0)),
                   pl.BlockSpec((chunk, V7X_LANES), lambda b, n: (b * n_chunks + n, 0))),
        scratch_shapes=[pltpu.VMEM((d, gs, QB, V7X_LANES), BF16), pltpu.VMEM((d, gs, QB, V7X_LANES), BF16)],
        compiler_params=_cparams(("parallel", "arbitrary")),
        name=f"dilated_prompt_g{g}",
    )(qkv, qkv, qkv, bias2)


def _split_dot(x, w_ref):
    hi = x.astype(BF16)
    lo = (x - hi.astype(F32)).astype(BF16)
    w = w_ref[...]
    return jnp.dot(hi, w, preferred_element_type=F32) + jnp.dot(lo, w, preferred_element_type=F32)


def _dil_sample_kernel(q_ref, k_ref, v_ref, cache_ref, bias_ref, biasn_ref, o_ref, l_ref, cout_ref, *,
                       t_valid, t_pad):
    n_slab = q_ref.shape[0]
    nb = cache_ref.shape[0]
    w = cache_ref.shape[-1]
    lane = lax.broadcasted_iota(I32, (t_pad, V7X_LANES), 1)
    is_new = lax.broadcasted_iota(I32, (DH, V7X_LANES), 1) >= V7X_LANES - t_valid

    def as_last_columns(x):
        shifted = pltpu.roll(x, t_pad - t_valid, 0)
        tile = jnp.concatenate([jnp.zeros((V7X_LANES - t_pad, x.shape[1]), F32), shifted], axis=0)
        return tile.T

    def sequence(b, _):
        rows = pl.ds(pl.multiple_of(b * t_pad, t_pad), t_pad)
        unslab = lambda ref: jnp.concatenate([ref[c, rows, :] for c in range(n_slab)], axis=-1)
        qn = unslab(q_ref)
        kn = unslab(k_ref)
        vn = unslab(v_ref)
        new_cols = (as_last_columns(kn), as_last_columns(vn))
        qb, kb, vb = qn.astype(BF16), kn.astype(BF16), vn.astype(BF16)
        lse = jnp.zeros((t_pad, V7X_LANES), F32)
        outs = []
        for h in range(HG):
            sl = slice(h * DH, (h + 1) * DH)
            k_t = cache_ref[b, 0, h]
            v_t = cache_ref[b, 1, h]
            s = jnp.dot(qb[:, sl], k_t.astype(BF16), preferred_element_type=F32) * (DH ** -0.5) + bias_ref[h]
            sn = lax.dot_general(qb[:, sl], kb[:, sl], (((1,), (1,)), ((), ())),
                                 preferred_element_type=F32) * (DH ** -0.5) + biasn_ref[h]
            m = jnp.maximum(jnp.max(s, axis=-1, keepdims=True), jnp.max(sn, axis=-1, keepdims=True))
            p = jnp.exp(s - m)
            pn = jnp.exp(sn - m)
            den = jnp.sum(p, axis=-1, keepdims=True) + jnp.sum(pn, axis=-1, keepdims=True)
            pv = lax.dot_general(p.astype(BF16), v_t.astype(BF16), (((1,), (1,)), ((), ())),
                                 preferred_element_type=F32)
            outs.append((pv + jnp.dot(pn.astype(BF16), vb[:, sl], preferred_element_type=F32)) / den)
            lse = jnp.where(lane == h, m + jnp.log(den), lse)
            for kv, old in enumerate((k_t, v_t)):
                moved = pltpu.roll(old, w - t_valid, 1)
                cout_ref[b, kv, h] = moved
                cout_ref[b, kv, h, :, w - V7X_LANES:] = jnp.where(is_new, new_cols[kv][sl, :],
                                                                  moved[:, w - V7X_LANES:])
        o = jnp.concatenate(outs, axis=-1)
        for c in range(n_slab):
            o_ref[c, rows, :] = o[:, c * V7X_LANES:(c + 1) * V7X_LANES]
        l_ref[rows, :] = lse
        return 0

    if nb == 1:
        sequence(0, 0)
    else:
        lax.fori_loop(0, nb, sequence, 0)


def _dilated_sample(qkv, cache, tab, g, d, t_valid):
    n_slabs, m_rows, _ = qkv.shape
    db, wb = cache.shape[0], cache.shape[-1]
    t_pad = m_rows // db
    gw = HG * DH
    gs = gw // V7X_LANES
    assert wb == N_STRIDE * d and t_valid <= V7X_LANES and wb % V7X_LANES == 0
    k_col = n_slabs // gs // 3

    def bias_of(j, ok):
        onehot = (np.arange(REL_BUCKETS)[:, None] == _rel_bucket_np(d * j).reshape(1, -1)) & ok.reshape(1, -1)
        looked_up = jnp.dot(tab.T.astype(F32), jnp.asarray(onehot.astype(np.float32)),
                            precision=lax.Precision.HIGHEST).reshape((HG,) + j.shape)
        return jnp.where(ok[None], looked_up, NEG)

    tq = np.arange(t_pad)[:, None]
    pos = np.arange(wb)[None, :]
    jc = (wb + tq - pos) // d
    bias = bias_of(jc, ((wb + tq - pos) % d == 0) & (jc >= 1) & (jc <= N_STRIDE))
    tn = np.arange(t_pad)[None, :]
    jn = (tq - tn) // d
    bias_new = bias_of(jn, (tn <= tq) & ((tq - tn) % d == 0) & (jn <= N_STRIDE) & (tn < t_valid))

    kern = functools.partial(_dil_sample_kernel, t_valid=t_valid, t_pad=t_pad)
    cache_bytes = 4 * math.prod(cache.shape[1:])
    nb = max(1, min(db, (4 * 1024 * 1024) // cache_bytes))
    assert db % nb == 0
    blk = (gs, nb * t_pad, V7X_LANES)
    cblk = pl.BlockSpec((nb,) + cache.shape[1:], lambda b: (b, 0, 0, 0, 0))
    return pl.pallas_call(
        kern,
        out_shape=(jax.ShapeDtypeStruct((gs, m_rows, V7X_LANES), F32),
                   jax.ShapeDtypeStruct((m_rows, V7X_LANES), F32),
                   jax.ShapeDtypeStruct(cache.shape, F32)),
        grid=(db // nb,),
        in_specs=[
            pl.BlockSpec(blk, lambda b: (g, b, 0)),
            pl.BlockSpec(blk, lambda b: (k_col + g, b, 0)),
            pl.BlockSpec(blk, lambda b: (2 * k_col + g, b, 0)),
            cblk, _const_spec(bias.shape), _const_spec(bias_new.shape),
        ],
        out_specs=(pl.BlockSpec(blk, lambda b: (0, b, 0)),
                   pl.BlockSpec((nb * t_pad, V7X_LANES), lambda b: (b, 0)),
                   cblk),
        compiler_params=_cparams(("parallel",)),
        name=f"dilated_sample_g{g}",
    )(qkv, qkv, qkv, cache, bias, bias_new)


def _merge_wo_kernel(o0_ref, o1_ref, o2_ref, l0_ref, l1_ref, l2_ref, x_ref, hexp_ref, w_ref, out_ref):
    l0, l1, l2 = l0_ref[...], l1_ref[...], l2_ref[...]
    m = jnp.maximum(jnp.maximum(l0, l1), l2)
    e0, e1, e2 = jnp.exp(l0 - m), jnp.exp(l1 - m), jnp.exp(l2 - m)
    inv = 1.0 / (e0 + e1 + e2)
    unslab = lambda ref: jnp.concatenate([ref[c] for c in range(ref.shape[0])], axis=-1)
    o = (_split_dot(e0 * inv, hexp_ref) * unslab(o0_ref) + _split_dot(e1 * inv, hexp_ref) * unslab(o1_ref)
         + _split_dot(e2 * inv, hexp_ref) * unslab(o2_ref))
    out_ref[...] = x_ref[...] + jnp.dot(o.astype(BF16), w_ref[...], preferred_element_type=F32)


def _merge_wo(outs, lses, x, w, *, name):
    m, d = x.shape
    gw = w.shape[0]
    tm = min(m, 512)
    assert m % tm == 0
    head = np.arange(gw) // DH
    hexp = jnp.asarray((np.arange(V7X_LANES)[:, None] == head[None, :]).astype(np.float32), dtype=BF16)
    row = lambda width: pl.BlockSpec((tm, width), lambda i: (i, 0))
    slab = pl.BlockSpec((gw // V7X_LANES, tm, V7X_LANES), lambda i: (0, i, 0))
    return pl.pallas_call(
        _merge_wo_kernel,
        out_shape=jax.ShapeDtypeStruct((m, d), F32),
        grid=(m // tm,),
        in_specs=[slab] * 3 + [row(V7X_LANES)] * 3 + [row(d), _const_spec(hexp.shape), _const_spec(w.shape)],
        out_specs=row(d),
        compiler_params=_cparams(("parallel",)),
        name=name,
    )(*outs, *lses, x, hexp, w)


def _router_kernel(x_ref, g_ref, wr_ref, br_ref, cnt0_ref, tri_ref,
                   hn_ref, gate_ref, lp_ref, tbase_ref, tcnt_ref, cnt_ref, run_ref):
    @pl.when(pl.program_id(0) == 0)
    def _():
        run_ref[...] = cnt0_ref[...]

    hn = _rms(x_ref[...], g_ref[...])
    hn_ref[...] = hn
    wr = wr_ref[...]
    h_hi = hn.astype(BF16)
    h_lo = (hn - h_hi.astype(F32)).astype(BF16)
    w_hi = wr.astype(BF16)
    w_lo = (wr - w_hi.astype(F32)).astype(BF16)
    nt_dot = lambda a, b: lax.dot_general(a, b, (((1,), (1,)), ((), ())), preferred_element_type=F32)
    logits = nt_dot(w_hi, h_hi) + nt_dot(w_lo, h_hi) + nt_dot(w_hi, h_lo) + br_ref[...]
    ne = logits.shape[0]
    eidx = lax.broadcasted_iota(I32, logits.shape, 0)
    m1 = jnp.max(logits, axis=0, keepdims=True)
    i1 = jnp.min(jnp.where(logits == m1, eidx, ne), axis=0, keepdims=True)
    rest = jnp.where(eidx == i1, -jnp.inf, logits)
    m2 = jnp.max(rest, axis=0, keepdims=True)
    i2 = jnp.min(jnp.where(rest == m2, eidx, ne), axis=0, keepdims=True)
    e2 = jnp.exp(m2 - m1)
    g1 = 1.0 / (1.0 + e2)
    g2 = e2 / (1.0 + e2)
    gate_ref[...] = jnp.where(eidx == 0, g1, jnp.where(eidx == 1, g2, 0.0))
    chosen = jnp.where(jnp.logical_or(eidx == i1, eidx == i2), 1.0, 0.0)
    before = jnp.dot(chosen.astype(BF16), tri_ref[...], preferred_element_type=F32)
    tile_cnt = jnp.ceil(jnp.sum(chosen, axis=1, keepdims=True) * (1.0 / V7X_SUBLANES)) * V7X_SUBLANES
    ecol = eidx[:, :1]
    lower = jnp.zeros_like(tile_cnt)
    for e in range(ne - 1):
        lower = lower + jnp.where(ecol > e, tile_cnt[e:e + 1, :], 0.0)
    local = before + lower
    l1 = jnp.sum(jnp.where(eidx == i1, local, 0.0), axis=0, keepdims=True)
    l2 = jnp.sum(jnp.where(eidx == i2, local, 0.0), axis=0, keepdims=True)
    lp_ref[...] = jnp.where(eidx == 0, l1, jnp.where(eidx == 1, l2, 0.0)).astype(I32)
    tbase_ref[0] = run_ref[...]
    tcnt_ref[0] = tile_cnt
    run_ref[...] += tile_cnt
    cnt_ref[...] = run_ref[...]


def _router(x, g, wr, br, cnt0, *, name):
    m, d = x.shape
    ne = wr.shape[1]
    tm = min(m, MOE_TOKEN_TILE)
    assert m % tm == 0
    n_tiles = m // tm
    tri = jnp.asarray(np.triu(np.ones((tm, tm), np.float32), 1), dtype=BF16)
    cols = pl.BlockSpec((ne, tm), lambda i: (0, i))
    per_tile = pl.BlockSpec((1, ne, 1), lambda i: (i, 0, 0))
    return pl.pallas_call(
        _router_kernel,
        out_shape=(jax.ShapeDtypeStruct((m, d), F32),
                   jax.ShapeDtypeStruct((ne, m), F32),
                   jax.ShapeDtypeStruct((ne, m), I32),
                   jax.ShapeDtypeStruct((n_tiles, ne, 1), F32),
                   jax.ShapeDtypeStruct((n_tiles, ne, 1), F32),
                   jax.ShapeDtypeStruct((ne, 1), F32)),
        grid=(n_tiles,),
        in_specs=[pl.BlockSpec((tm, d), lambda i: (i, 0)), _const_spec((1, d)), _const_spec((ne, d)),
                  _const_spec((ne, 1)), _const_spec((ne, 1)), _const_spec((tm, tm))],
        out_specs=(pl.BlockSpec((tm, d), lambda i: (i, 0)), cols, cols, per_tile, per_tile,
                   _const_spec((ne, 1))),
        scratch_shapes=[pltpu.VMEM((ne, 1), F32)],
        compiler_params=_cparams(("arbitrary",)),
        name=name,
    )(x, g.reshape(1, d), wr.T, br.reshape(ne, 1), cnt0, tri)


def _staging_rows(n_tokens):
    return TOP_K * n_tokens + V7X_SUBLANES * N_EXPERTS


def _expert_run_copies(tile, dst_ref, cnt_ref, n_tokens, make_copy, wait):
    local = jnp.int32(0)
    for e in range(N_EXPERTS):
        count = cnt_ref[tile * N_EXPERTS + e]
        sorted_row = dst_ref[tile * N_EXPERTS + e]
        size = pl.next_power_of_2(n_tokens)
        while size >= V7X_SUBLANES:
            has = (count & size) != 0

            @pl.when(has)
            def _(local=local, sorted_row=sorted_row, size=size):
                copy = make_copy(pl.multiple_of(local, V7X_SUBLANES), pl.multiple_of(sorted_row, V7X_SUBLANES), size)
                if wait:
                    copy.wait()
                else:
                    copy.start()

            step = jnp.where(has, size, 0)
            local = local + step
            sorted_row = sorted_row + step
            size //= 2


def _dispatch_kernel(dst_ref, cnt_ref, hn_ref, lp_ref, xs_in_ref, xs_ref, stage_ref, sem, *, tmd):
    del xs_in_ref
    i = pl.program_id(0)
    n = pl.num_programs(0)
    slot = i % 2

    def copies(tile, s, wait):
        _expert_run_copies(tile, dst_ref, cnt_ref, tmd, lambda local, sorted_row, size: pltpu.make_async_copy(
            stage_ref.at[s, pl.ds(local, size)], xs_ref.at[pl.ds(sorted_row, size)], sem.at[s]), wait)

    @pl.when(i >= 2)
    def _():
        copies(i - 2, slot, True)

    lp = lp_ref[...]
    row = lax.broadcasted_iota(I32, (stage_ref.shape[1], tmd), 0)
    place = (jnp.where(row == lp[0:1, :], 1.0, 0.0) + jnp.where(row == lp[1:2, :], 1.0, 0.0)).astype(BF16)
    stage_ref[slot] = jnp.dot(place, hn_ref[...].astype(BF16), preferred_element_type=F32)
    copies(i, slot, False)

    @pl.when(i == n - 1)
    def _():
        copies(i, slot, True)

        @pl.when(n >= 2)
        def _():
            copies(i - 1, 1 - slot, True)


def _dispatch(hn, lp, dst, cnt, xs, *, name):
    m, d = hn.shape
    ne = lp.shape[0]
    tmd = min(m, MOE_TOKEN_TILE)
    assert m % tmd == 0
    kern = functools.partial(_dispatch_kernel, tmd=tmd)
    grid_spec = pltpu.PrefetchScalarGridSpec(
        num_scalar_prefetch=2,
        grid=(m // tmd,),
        in_specs=[
            pl.BlockSpec((tmd, d), lambda i, dst, cnt: (i, 0)),
            pl.BlockSpec((ne, tmd), lambda i, dst, cnt: (0, i)),
            pl.BlockSpec(memory_space=pl.ANY),
        ],
        out_specs=pl.BlockSpec(memory_space=pl.ANY),
        scratch_shapes=[pltpu.VMEM((2, _staging_rows(tmd), d), F32), pltpu.SemaphoreType.DMA((2,))],
    )
    return pl.pallas_call(
        kern,
        out_shape=jax.ShapeDtypeStruct(xs.shape, xs.dtype),
        grid_spec=grid_spec,
        input_output_aliases={4: 0},
        compiler_params=_cparams(("arbitrary",)),
        name=name,
    )(dst, cnt, hn, lp, xs)


def _moe_ffn_kernel(te_ref, nused_ref, x_ref, wg_ref, wu_ref, wd_ref, o_ref, xb_ref, acc_ref):
    t = pl.program_id(0)
    j = pl.program_id(1)
    last = pl.num_programs(1) - 1
    used = t < nused_ref[0]

    @pl.when(used)
    def _():
        @pl.when(j == 0)
        def _():
            xb_ref[...] = x_ref[...].astype(BF16)
            acc_ref[...] = jnp.zeros_like(acc_ref)

        xb = xb_ref[...]
        gate = jnp.dot(xb, wg_ref[...].astype(BF16), preferred_element_type=F32)
        up = jnp.dot(xb, wu_ref[...].astype(BF16), preferred_element_type=F32)
        act = (jax.nn.silu(gate) * up).astype(BF16)
        acc_ref[...] += jnp.dot(act, wd_ref[...].astype(BF16), preferred_element_type=F32)

        @pl.when(j == last)
        def _():
            o_ref[...] = acc_ref[...]

    @pl.when(jnp.logical_and(jnp.logical_not(used), j == last))
    def _():
        o_ref[...] = jnp.zeros_like(o_ref)


def _moe_ffn(xs, te, n_used, w_gu, w_d):
    n_rows, d = xs.shape
    hid = w_d.shape[1]
    tm, th = MOE_TM, MOE_TH
    nj = hid // th
    nt = n_rows // tm
    assert hid % th == 0 and n_rows % tm == 0

    def jj(t, j, nu):
        return jnp.where(t < nu[0], j, nj - 1)

    grid_spec = pltpu.PrefetchScalarGridSpec(
        num_scalar_prefetch=2,
        grid=(nt, nj),
        in_specs=[
            pl.BlockSpec((tm, d), lambda t, j, te, nu: (jnp.maximum(jnp.minimum(t, nu[0] - 1), 0), 0)),
            pl.BlockSpec((None, d, th), lambda t, j, te, nu: (te[t], 0, jj(t, j, nu))),
            pl.BlockSpec((None, d, th), lambda t, j, te, nu: (te[t], 0, jj(t, j, nu) + nj)),
            pl.BlockSpec((None, th, d), lambda t, j, te, nu: (te[t], jj(t, j, nu), 0)),
        ],
        out_specs=pl.BlockSpec((tm, d), lambda t, j, te, nu: (t, 0)),
        scratch_shapes=[pltpu.VMEM((tm, d), BF16), pltpu.VMEM((tm, d), F32)],
    )
    return pl.pallas_call(
        _moe_ffn_kernel,
        out_shape=jax.ShapeDtypeStruct((n_rows, d), F32),
        grid_spec=grid_spec,
        compiler_params=_cparams(("arbitrary", "arbitrary")),
        name="moe_experts",
    )(te, n_used, xs, w_gu, w_gu, w_d)


def _combine_kernel(dst_ref, cnt_ref, x_ref, gate_ref, lp_ref, ys_ref, o_ref, buf_ref, sem, *, tmc):
    i = pl.program_id(0)
    n = pl.num_programs(0)
    slot = i % 2

    def gather(tile, s, wait):
        _expert_run_copies(tile, dst_ref, cnt_ref, tmc, lambda local, sorted_row, size: pltpu.make_async_copy(
            ys_ref.at[pl.ds(sorted_row, size)], buf_ref.at[s, pl.ds(local, size)], sem.at[s]), wait)

    @pl.when(i == 0)
    def _():
        buf_ref[...] = jnp.zeros_like(buf_ref)
        gather(i, slot, False)

    @pl.when(i + 1 < n)
    def _():
        gather(i + 1, 1 - slot, False)

    gather(i, slot, True)
    g = gate_ref[...]
    lp = lp_ref[...]
    row = lax.broadcasted_iota(I32, (buf_ref.shape[1], tmc), 0)
    w = jnp.where(row == lp[0:1, :], g[0:1, :], 0.0) + jnp.where(row == lp[1:2, :], g[1:2, :], 0.0)
    rows = buf_ref[slot]
    w_hi = w.astype(BF16)
    w_lo = (w - w_hi.astype(F32)).astype(BF16)
    r_hi = rows.astype(BF16)
    r_lo = (rows - r_hi.astype(F32)).astype(BF16)
    tn_dot = lambda a, b: lax.dot_general(a, b, (((0,), (0,)), ((), ())), preferred_element_type=F32)
    o_ref[...] = x_ref[...] + tn_dot(w_hi, r_hi) + tn_dot(w_hi, r_lo) + tn_dot(w_lo, r_hi)


def _combine(x, gate, lp, dst, cnt, ys, *, name):
    m, d = x.shape
    ne = gate.shape[0]
    tmc = min(m, MOE_TOKEN_TILE)
    assert m % tmc == 0
    kern = functools.partial(_combine_kernel, tmc=tmc)
    row = lambda width: pl.BlockSpec((tmc, width), lambda i, dst, cnt: (i, 0))
    cols = pl.BlockSpec((ne, tmc), lambda i, dst, cnt: (0, i))
    grid_spec = pltpu.PrefetchScalarGridSpec(
        num_scalar_prefetch=2,
        grid=(m // tmc,),
        in_specs=[row(d), cols, cols, pl.BlockSpec(memory_space=pl.ANY)],
        out_specs=row(d),
        scratch_shapes=[pltpu.VMEM((2, _staging_rows(tmc), d), F32), pltpu.SemaphoreType.DMA((2,))],
    )
    return pl.pallas_call(
        kern,
        out_shape=jax.ShapeDtypeStruct((m, d), F32),
        grid_spec=grid_spec,
        compiler_params=_cparams(("arbitrary",)),
        name=name,
    )(dst, cnt, x, gate, lp, ys)


def _moe_plan(counts, tile_bases, tile_counts, tm, nt):
    padded = ((counts + tm - 1) // tm) * tm
    ends = jnp.cumsum(padded)
    starts = ends - padded
    dst = [(starts[None, :] + b[:, :, 0].astype(I32)).reshape(-1) for b in tile_bases]
    cnt = [c[:, :, 0].astype(I32).reshape(-1) for c in tile_counts]
    n_used = (ends[-1] // tm).astype(I32)
    tile = jnp.arange(nt, dtype=I32)
    first_row = jnp.minimum(tile, n_used - 1) * tm
    te = jnp.minimum(jnp.sum((first_row[:, None] >= ends[None, :]).astype(I32), axis=1), N_EXPERTS - 1)
    return dst, cnt, te, n_used.reshape(1)


def _block_diag(w):
    n, a, b = w.shape
    return jnp.einsum("nij,nm->nimj", w, jnp.eye(n, dtype=w.dtype)).reshape(n * a, n * b)


def kernel(x_prompt, x_sample, state_conv, state_lru, state_s5_re, state_s5_im, cache_swa0_kv, cache_swa1_kv, cache_swa2_kv, cache_mem_k, cache_mem_v, mem_prompt, norm_mix, norm_xa, norm_ffn, norm_mem, w_in_even, conv_w, conv_b, lru_wa, lru_ba, lru_wx, lru_bx, lru_lam, s5_lam_re, s5_lam_im, s5_log_dt, s5_b_re, s5_b_im, s5_c_re, s5_c_im, s5_d, s5_w_glu, s5_b_glu, w_out_even, w_qkv_odd, q_norm_odd, k_norm_odd, w_o_odd, rel_bias, xa_wq, xa_wkv, xa_qn, xa_kn, xa_wo, ffn_w_gu, ffn_w_down, moe_router_w, moe_router_b, moe_w_gu, moe_w_down):
    bp, s_len, d = x_prompt.shape
    db, t_dec, _ = x_sample.shape
    n_mem = mem_prompt.shape[1]
    tp = SAMPLE_PAD_T
    d_lru = conv_w.shape[-1]
    s5_g, s5_p, s5_h = s5_b_re.shape[1:]
    n_state = s5_g * s5_p
    d_s5 = s5_g * s5_h
    xa_dh = d // XA_HEADS
    caches = (cache_swa0_kv, cache_swa1_kv, cache_swa2_kv)
    bf = lambda w: w.astype(BF16)

    yp = x_prompt
    ys = jnp.pad(x_sample, ((0, 0), (0, tp - t_dec), (0, 0)))

    w_in = bf(w_in_even[0])
    bbr, bbi, apow = _s5_prep(s5_lam_re[0], s5_lam_im[0], s5_log_dt[0], s5_b_re[0], s5_b_im[0])
    n_slab = d_s5 // V7X_LANES
    gl = s5_g // n_slab
    eye_g = jnp.eye(gl, dtype=F32)

    def c_blocks(c_mat):
        blocks = jnp.einsum("cghp,gk->cgpkh", c_mat.reshape(n_slab, gl, s5_h, s5_p), eye_g)
        return bf(blocks.reshape(n_slab, gl * s5_p, V7X_LANES))

    mix_w = dict(
        cw=conv_w[0], cb=conv_b[0].reshape(1, d_lru),
        wa=bf(_block_diag(lru_wa[0])), ba=lru_ba[0].reshape(1, d_lru),
        wx=bf(_block_diag(lru_wx[0])), bx=lru_bx[0].reshape(1, d_lru),
        lam=lru_lam[0].reshape(1, d_lru),
        bbr=bbr, bbi=bbi, apow=apow,
        ccr=c_blocks(s5_c_re[0]), cci=c_blocks(s5_c_im[0]),
        d=s5_d[0].reshape(1, d_s5), wglu=bf(s5_w_glu[0]), bglu=s5_b_glu[0].reshape(1, d_s5),
        wout=bf(w_out_even[0]),
    )
    z_p = _norm_matmul(yp.reshape(bp * s_len, d), norm_mix[0], w_in, tn=w_in.shape[1],
                       name="in_proj_p").reshape(bp, s_len, -1)
    yp, p_lru, p_s5r, p_s5i = _mixer0(
        z_p, yp, jnp.zeros((bp, V7X_SUBLANES, d_lru), F32), jnp.zeros((bp, 1, d_lru), F32),
        jnp.zeros((bp, 1, n_state), F32), jnp.zeros((bp, 1, n_state), F32), mix_w,
        tc=MIX_TC, last_row=MIX_TC - 1)
    z_s = _norm_matmul(ys.reshape(db * tp, d), norm_mix[0], w_in, name="in_proj_s").reshape(db, tp, -1)
    conv_init = jnp.pad(state_conv[0], ((0, 0), (V7X_SUBLANES - (CONV_W - 1), 0), (0, 0)))
    ys, s_lru, s_s5r, s_s5i = _mixer0(
        z_s, ys, conv_init, state_lru[0].reshape(db, 1, d_lru),
        state_s5_re[0].reshape(db, 1, n_state), state_s5_im[0].reshape(db, 1, n_state), mix_w,
        tc=tp, last_row=t_dec - 1)
    p_state_conv = z_p[:, s_len - (CONV_W - 1):, :d_lru][None]
    s_state_conv = z_s[:, t_dec - (CONV_W - 1):t_dec, :d_lru][None]

    p_mk, p_mv = [], []

    def cross_attention(layer, yp, ys):
        kn_gain = jnp.concatenate([jnp.tile(xa_kn[layer], XA_HEADS), jnp.ones((d,), F32)]).reshape(1, 2 * d)
        kv = _norm_matmul(mem_prompt.reshape(bp * n_mem, d), norm_mem[layer], bf(xa_wkv[layer]),
                          head_gain=kn_gain, n_norm_cols=d, dh=xa_dh, name=f"mem_kv{layer}")
        kv = kv.reshape(bp, n_mem, 2 * d)
        mk, mv = kv[:, :, :d], kv[:, :, d:]
        p_mk.append(mk.reshape(bp, n_mem, XA_HEADS, xa_dh))
        p_mv.append(mv.reshape(bp, n_mem, XA_HEADS, xa_dh))
        wq, wo = bf(xa_wq[layer]), bf(xa_wo[layer])
        yp = _xattn(yp, norm_xa[layer], wq, xa_qn[layer], kv, kv, wo, tm=512, name=f"xattn_p{layer}")
        ys = _xattn(ys, norm_xa[layer], wq, xa_qn[layer], cache_mem_k, cache_mem_v, wo,
                    tm=tp, nb=4, name=f"xattn_s{layer}", layer=layer)
        return yp, ys

    yp, ys = cross_attention(0, yp, ys)
    w_gu0, w_d0 = bf(ffn_w_gu[0]), bf(ffn_w_down[0])
    yp = _ffn(yp.reshape(bp * s_len, d), norm_ffn[0], w_gu0, w_d0, name="ffn_p").reshape(bp, s_len, d)
    ys = _ffn(ys.reshape(db * tp, d), norm_ffn[0], w_gu0, w_d0, name="ffn_s").reshape(db, tp, d)

    n_heads = len(WINDOWS) * HG
    d_c = n_heads * DH
    gw = HG * DH
    w_qkv = bf(w_qkv_odd[0])
    qk_gain = jnp.concatenate([jnp.tile(q_norm_odd[0], n_heads), jnp.tile(k_norm_odd[0], n_heads),
                               jnp.ones((d_c,), F32)]).reshape(1, 3 * d_c)
    qkv_p = _norm_matmul(yp.reshape(bp * s_len, d), norm_mix[1], w_qkv, head_gain=qk_gain,
                         n_norm_cols=2 * d_c, dh=DH, slabs=True, tn=d_c, name="qkv_p")
    qkv_s = _norm_matmul(ys.reshape(db * tp, d), norm_mix[1], w_qkv, head_gain=qk_gain,
                         n_norm_cols=2 * d_c, dh=DH, slabs=True, name="qkv_s")
    gs = gw // V7X_LANES
    w_o = bf(w_o_odd[0])
    outs_p, lses_p, outs_s, lses_s, p_swa, s_swa = [], [], [], [], [], []

    def kv_rows(qkv, n_batch, t_len, g, lo, hi):
        ks = (len(WINDOWS) + g) * gs
        vs = (2 * len(WINDOWS) + g) * gs
        slabs = qkv.reshape(-1, n_batch, t_len, V7X_LANES)
        kv = jnp.stack([slabs[ks:ks + gs, :, lo:hi], slabs[vs:vs + gs, :, lo:hi]])
        return jnp.transpose(kv, (2, 3, 0, 1, 4)).reshape(n_batch, hi - lo, 2, HG, DH)

    for g, dil in enumerate(DILATIONS):
        tab = rel_bias[:, g * HG:(g + 1) * HG]
        o, l = _dilated_prompt(qkv_p, tab, g, dil, bp)
        outs_p.append(o)
        lses_p.append(l)
        cache_t = jnp.transpose(caches[g][0], (0, 2, 3, 4, 1))
        o, l, cache_t = _dilated_sample(qkv_s, cache_t, tab, g, dil, t_dec)
        outs_s.append(o)
        lses_s.append(l)
        s_swa.append(jnp.transpose(cache_t, (0, 4, 1, 2, 3))[None])
        win = min(WINDOWS[g], s_len)
        p_swa.append(kv_rows(qkv_p, bp, s_len, g, s_len - win, s_len)[None])
    yp = _merge_wo(outs_p, lses_p, yp.reshape(bp * s_len, d), w_o, name="merge_wo_p").reshape(bp, s_len, d)
    ys = _merge_wo(outs_s, lses_s, ys.reshape(db * tp, d), w_o, name="merge_wo_s").reshape(db, tp, d)

    yp, ys = cross_attention(1, yp, ys)

    yp2 = yp.reshape(bp * s_len, d)
    ys2 = ys[:, :t_dec].reshape(db * t_dec, d)
    zero_cnt = jnp.zeros((N_EXPERTS, 1), F32)
    hn_p, gate_p, lp_p, base_p, tcnt_p, cnt_p = _router(
        yp2, norm_ffn[1], moe_router_w[0], moe_router_b[0], zero_cnt, name="router_p")
    hn_s, gate_s, lp_s, base_s, tcnt_s, cnt_s = _router(
        ys2, norm_ffn[1], moe_router_w[0], moe_router_b[0], cnt_p, name="router_s")
    n_tok = yp2.shape[0] + ys2.shape[0]
    n_runs = N_EXPERTS * (base_p.shape[0] + base_s.shape[0])
    nt = -(-(n_tok * TOP_K + (V7X_SUBLANES - 1) * n_runs) // MOE_TM) + N_EXPERTS
    (dst_p, dst_s), (rc_p, rc_s), te, n_used = _moe_plan(
        cnt_s[:, 0].astype(I32), (base_p, base_s), (tcnt_p, tcnt_s), MOE_TM, nt)
    xs = jnp.zeros((nt * MOE_TM, d), F32)
    xs = _dispatch(hn_p, lp_p, dst_p, rc_p, xs, name="dispatch_p")
    xs = _dispatch(hn_s, lp_s, dst_s, rc_s, xs, name="dispatch_s")
    ysort = _moe_ffn(xs, te, n_used, moe_w_gu[0], moe_w_down[0])
    yp = _combine(yp2, gate_p, lp_p, dst_p, rc_p, ysort, name="moe_combine_p").reshape(bp, s_len, d)
    ys = _combine(ys2, gate_s, lp_s, dst_s, rc_s, ysort, name="moe_combine_s").reshape(db, t_dec, d)

    return (yp, ys,
            p_state_conv, p_lru.reshape(1, bp, d_lru),
            p_s5r.reshape(1, bp, s5_g, s5_p), p_s5i.reshape(1, bp, s5_g, s5_p),
            p_swa[0], p_swa[1], p_swa[2], jnp.stack(p_mk), jnp.stack(p_mv),
            s_state_conv, s_lru.reshape(1, db, d_lru),
            s_s5r.reshape(1, db, s5_g, s5_p), s_s5i.reshape(1, db, s5_g, s5_p),
            s_swa[0], s_swa[1], s_swa[2])
```

```python
import functools
import math

import jax
import jax.numpy as jnp
import numpy as np
from jax import lax
from jax.experimental import pallas as pl
from jax.experimental.pallas import tpu as pltpu

F32 = jnp.float32
BF16 = jnp.bfloat16
I32 = jnp.int32

EPS = 1e-6
NEG = -1e30

V7X_SUBLANES = 8
V7X_LANES = 128
V7X_VMEM_BYTES = 64 * 1024 * 1024
VMEM_LIMIT = V7X_VMEM_BYTES - 8 * 1024 * 1024

LRU_C = 8.0
CONV_W = 4
WINDOWS = (128, 512, 2048)
DILATIONS = (1, 4, 16)
HG = 8
DH = 64
N_STRIDE = 128
QB = 128
REL_BUCKETS = 32
REL_MAX_DIST = WINDOWS[-1]
XA_HEADS = 4
N_EXPERTS = 8
TOP_K = 2
SAMPLE_PAD_T = 16

MIX_TC = 256
MOE_TM = 1024
MOE_TH = 512
MOE_TOKEN_TILE = 256
FFN_TH = 512


def _cparams(sem):
    return pltpu.CompilerParams(dimension_semantics=sem, vmem_limit_bytes=VMEM_LIMIT)


def _const_spec(shape):
    nd = len(shape)
    return pl.BlockSpec(shape, lambda *_: (0,) * nd)


def _rms(x, g):
    return x * lax.rsqrt(jnp.mean(x * x, axis=-1, keepdims=True) + EPS) * g


def _norm_matmul_kernel(x_ref, g_ref, w_ref, hg_ref, hs_ref, o_ref, hn_ref, *, n_norm_tiles, dh, slabs,
                        row_split):
    j = pl.program_id(1)

    @pl.when(j == 0)
    def _():
        hn_ref[...] = _rms(x_ref[...], g_ref[...]).astype(BF16)

    tm = hn_ref.shape[0]
    rs = tm // row_split

    def emit(r, val):
        rows = slice(r * rs, (r + 1) * rs)
        if slabs:
            for c in range(o_ref.shape[0]):
                o_ref[c, rows, :] = val[:, c * V7X_LANES:(c + 1) * V7X_LANES]
        else:
            o_ref[rows, :] = val

    def tile(r, normed):
        y = jnp.dot(hn_ref[r * rs:(r + 1) * rs, :], w_ref[...], preferred_element_type=F32)
        if normed:
            sw = hs_ref.shape[0]
            parts = []
            for s in range(y.shape[1] // sw):
                part = y[:, s * sw:(s + 1) * sw]
                ssq = jnp.dot((part * part).astype(BF16), hs_ref[...], preferred_element_type=F32)
                parts.append(part * lax.rsqrt(ssq * (1.0 / dh) + EPS) * hg_ref[:, s * sw:(s + 1) * sw])
            y = parts[0] if len(parts) == 1 else jnp.concatenate(parts, axis=-1)
        emit(r, y)

    if n_norm_tiles == 0:
        for r in range(row_split):
            tile(r, False)
    else:
        @pl.when(j < n_norm_tiles)
        def _():
            for r in range(row_split):
                tile(r, True)

        @pl.when(j >= n_norm_tiles)
        def _():
            for r in range(row_split):
                tile(r, False)


def _head_sum_matrix(tn, dh):
    idx = np.arange(tn) // dh
    return jnp.asarray((idx[:, None] == idx[None, :]).astype(np.float32), dtype=BF16)


def _norm_matmul(x, g, w, *, tn=512, head_gain=None, n_norm_cols=0, dh=1, slabs=False, name):
    m, d = x.shape
    n = w.shape[1]
    tm = min(m, 1024)
    assert m % tm == 0 and n % tn == 0 and n_norm_cols % tn == 0
    if head_gain is None:
        head_gain = jnp.ones((1, n), F32)
    hs = _head_sum_matrix(min(tn, 512), dh) if n_norm_cols else jnp.zeros((V7X_SUBLANES, V7X_LANES), BF16)
    row_split = 2 if tm % 512 == 0 else 1
    kern = functools.partial(_norm_matmul_kernel, n_norm_tiles=n_norm_cols // tn, dh=dh, slabs=slabs,
                             row_split=row_split)
    if slabs:
        out_shape = jax.ShapeDtypeStruct((n // V7X_LANES, m, V7X_LANES), F32)
        out_spec = pl.BlockSpec((tn // V7X_LANES, tm, V7X_LANES), lambda i, j: (j, i, 0))
    else:
        out_shape = jax.ShapeDtypeStruct((m, n), F32)
        out_spec = pl.BlockSpec((tm, tn), lambda i, j: (i, j))
    return pl.pallas_call(
        kern,
        out_shape=out_shape,
        grid=(m // tm, n // tn),
        in_specs=[
            pl.BlockSpec((tm, d), lambda i, j: (i, 0)),
            pl.BlockSpec((1, d), lambda i, j: (0, 0)),
            pl.BlockSpec((d, tn), lambda i, j: (0, j)),
            pl.BlockSpec((1, tn), lambda i, j: (0, j)),
            _const_spec(hs.shape),
        ],
        out_specs=out_spec,
        scratch_shapes=[pltpu.VMEM((tm, d), BF16)],
        compiler_params=_cparams(("parallel", "arbitrary")),
        name=name,
    )(x, g.reshape(1, d), w, head_gain, hs)


def _s5_prep_kernel(lre_ref, lim_ref, ldt_ref, bre_ref, bim_ref, bbr_ref, bbi_ref, apow_ref):
    lr = lre_ref[...]
    li = lim_ref[...]
    dt = jnp.exp(ldt_ref[...])
    mag = jnp.exp(lr * dt)
    ab_r = mag * jnp.cos(li * dt)
    ab_i = mag * jnp.sin(li * dt)
    den = lr * lr + li * li
    nr = ab_r - 1.0
    cr = (nr * lr + ab_i * li) / den
    ci = (ab_i * lr - nr * li) / den
    n_slab, _, sw = bre_ref.shape
    for c in range(n_slab):
        cr_c = cr[:, c * sw:(c + 1) * sw]
        ci_c = ci[:, c * sw:(c + 1) * sw]
        b_r = bre_ref[c]
        b_i = bim_ref[c]
        bbr_ref[c] = (cr_c * b_r - ci_c * b_i).astype(BF16)
        bbi_ref[c] = (cr_c * b_i + ci_c * b_r).astype(BF16)

    n = lr.shape[1]
    row = lax.broadcasted_iota(I32, (V7X_SUBLANES, n), 0)

    def power(kf):
        mg = jnp.exp(kf * (lr * dt))
        return mg * jnp.cos(kf * (li * dt)), mg * jnp.sin(kf * (li * dt))

    for i, s in enumerate((1, 2, 4)):
        pr, pi = power(jnp.full((V7X_SUBLANES, n), s, F32))
        keep = row >= s
        apow_ref[2 * i] = jnp.where(keep, pr, 0.0)
        apow_ref[2 * i + 1] = jnp.where(keep, pi, 0.0)
    pr, pi = power((row + 1).astype(F32))
    apow_ref[6] = pr
    apow_ref[7] = pi


def _s5_prep(lam_re, lam_im, log_dt, b_re, b_im):
    g, p, h = b_re.shape
    n = g * p
    n_slab = g * h // V7X_LANES
    gl = g // n_slab
    eye = jnp.eye(gl, dtype=F32)

    def slab_blocks(b):
        return jnp.einsum("cgph,gk->ckhgp", b.reshape(n_slab, gl, p, h), eye).reshape(n_slab, V7X_LANES, gl * p)

    bre_bd = slab_blocks(b_re)
    bim_bd = slab_blocks(b_im)
    ldt = jnp.broadcast_to(log_dt[:, None], (g, p)).reshape(1, n)
    return pl.pallas_call(
        _s5_prep_kernel,
        out_shape=(
            jax.ShapeDtypeStruct(bre_bd.shape, BF16),
            jax.ShapeDtypeStruct(bre_bd.shape, BF16),
            jax.ShapeDtypeStruct((8, V7X_SUBLANES, n), F32),
        ),
        compiler_params=pltpu.CompilerParams(vmem_limit_bytes=VMEM_LIMIT),
        name="s5_prep",
    )(lam_re.reshape(1, n), lam_im.reshape(1, n), ldt, bre_bd, bim_bd)


def _mixer0_kernel(z_ref, x_ref, convi_ref, h0_ref, sr0_ref, si0_ref,
                   cw_ref, cb_ref, wa_ref, ba_ref, wx_ref, bx_ref, lam_ref,
                   bbr_ref, bbi_ref, apow_ref, ccr_ref, cci_ref, d_ref, wglu_ref, bglu_ref, wout_ref,
                   y_ref, hl_ref, srl_ref, sil_ref,
                   ext_ref, hc_ref, src_ref, sic_ref, xr_ref, xi_ref, ha_ref, hb_ref,
                   *, tc, last_row):
    c = pl.program_id(1)
    d_lru = cw_ref.shape[1]
    d_s5 = d_ref.shape[1]
    n_tiles = tc // V7X_SUBLANES

    @pl.when(c == 0)
    def _():
        ext_ref[...] = convi_ref[...]
        hc_ref[...] = h0_ref[...]
        src_ref[...] = sr0_ref[...]
        sic_ref[...] = si0_ref[...]

    z = z_ref[...]
    xa = z[:, :d_lru]
    ga = z[:, d_lru:2 * d_lru]
    u = z[:, 2 * d_lru:]

    ext = jnp.concatenate([ext_ref[...], xa], axis=0)
    xc = cb_ref[...] + xa * cw_ref[CONV_W - 1:CONV_W, :]
    for s in range(1, CONV_W):
        xc = xc + pltpu.roll(ext, s, 0)[V7X_SUBLANES:, :] * cw_ref[CONV_W - 1 - s:CONV_W - s, :]
    ext_ref[...] = xa[tc - V7X_SUBLANES:, :]

    xcb = xc.astype(BF16)
    r = jax.nn.sigmoid(jnp.dot(xcb, wa_ref[...], preferred_element_type=F32) + ba_ref[...])
    ig = jax.nn.sigmoid(jnp.dot(xcb, wx_ref[...], preferred_element_type=F32) + bx_ref[...])
    lam = lam_ref[...]
    softplus_neg = jnp.maximum(-lam, 0.0) + jnp.log1p(jnp.exp(-jnp.abs(lam)))
    log_a = -LRU_C * r * softplus_neg
    a = jnp.exp(log_a)
    bt = jnp.sqrt(-jnp.tanh(log_a) * (a * a + 1.0)) * ig * xc

    a3 = a.reshape(n_tiles, V7X_SUBLANES, d_lru)
    b3 = bt.reshape(n_tiles, V7X_SUBLANES, d_lru)
    row = lax.broadcasted_iota(I32, (1, V7X_SUBLANES, d_lru), 1)
    for s in (1, 2, 4):
        keep = row >= s
        ar = pltpu.roll(a3, s, 1)
        br = pltpu.roll(b3, s, 1)
        b3 = jnp.where(keep, a3 * br + b3, b3)
        a3 = jnp.where(keep, a3 * ar, a3)
    ha_ref[...] = a3.reshape(tc, d_lru)
    hb_ref[...] = b3.reshape(tc, d_lru)

    def lru_tile(i, carry):
        r0 = pl.multiple_of(i * V7X_SUBLANES, V7X_SUBLANES)
        h = ha_ref[pl.ds(r0, V7X_SUBLANES), :] * carry + hb_ref[pl.ds(r0, V7X_SUBLANES), :]
        hb_ref[pl.ds(r0, V7X_SUBLANES), :] = h
        return h[V7X_SUBLANES - 1:, :]

    hc_ref[...] = lax.fori_loop(0, n_tiles, lru_tile, hc_ref[...])
    hs = hb_ref[...]
    hl_ref[...] = hb_ref[last_row:last_row + 1, :]
    ya = hs * jax.nn.gelu(ga)

    ub = u.astype(BF16)
    n_slab, _, sw = bbr_ref.shape
    n_state = n_slab * sw

    def input_dot(w_ref):
        return jnp.concatenate(
            [jnp.dot(ub[:, c * V7X_LANES:(c + 1) * V7X_LANES], w_ref[c], preferred_element_type=F32)
             for c in range(n_slab)], axis=-1)

    xr3 = input_dot(bbr_ref).reshape(n_tiles, V7X_SUBLANES, n_state)
    xi3 = input_dot(bbi_ref).reshape(n_tiles, V7X_SUBLANES, n_state)
    for i, s in enumerate((1, 2, 4)):
        cr = apow_ref[2 * i][None]
        ci = apow_ref[2 * i + 1][None]
        rr = pltpu.roll(xr3, s, 1)
        ri = pltpu.roll(xi3, s, 1)
        xr3, xi3 = xr3 + cr * rr - ci * ri, xi3 + cr * ri + ci * rr
    xr_ref[...] = xr3.reshape(tc, n_state)
    xi_ref[...] = xi3.reshape(tc, n_state)

    def s5_tile(i, carry):
        cr_, ci_ = carry
        r0 = pl.multiple_of(i * V7X_SUBLANES, V7X_SUBLANES)
        pr = apow_ref[6]
        pi = apow_ref[7]
        nr_ = xr_ref[pl.ds(r0, V7X_SUBLANES), :] + pr * cr_ - pi * ci_
        ni_ = xi_ref[pl.ds(r0, V7X_SUBLANES), :] + pr * ci_ + pi * cr_
        xr_ref[pl.ds(r0, V7X_SUBLANES), :] = nr_
        xi_ref[pl.ds(r0, V7X_SUBLANES), :] = ni_
        return nr_[V7X_SUBLANES - 1:, :], ni_[V7X_SUBLANES - 1:, :]

    cr_f, ci_f = lax.fori_loop(0, n_tiles, s5_tile, (src_ref[...], sic_ref[...]))
    src_ref[...] = cr_f
    sic_ref[...] = ci_f
    srl_ref[...] = xr_ref[last_row:last_row + 1, :]
    sil_ref[...] = xi_ref[last_row:last_row + 1, :]

    def output_dot(x_ref_, w_ref):
        return jnp.concatenate(
            [jnp.dot(x_ref_[:, c * sw:(c + 1) * sw].astype(BF16), w_ref[c], preferred_element_type=F32)
             for c in range(n_slab)], axis=-1)

    ys = output_dot(xr_ref, ccr_ref) - output_dot(xi_ref, cci_ref)
    ys = ys + d_ref[...] * u
    gs = jax.nn.gelu(ys)
    yb = gs * jax.nn.sigmoid(jnp.dot(gs.astype(BF16), wglu_ref[...], preferred_element_type=F32)
                             + bglu_ref[...])

    y_ref[...] = (x_ref[...]
                  + jnp.dot(ya.astype(BF16), wout_ref[:d_lru, :], preferred_element_type=F32)
                  + jnp.dot(yb.astype(BF16), wout_ref[d_lru:, :], preferred_element_type=F32))


def _mixer0(z, x, conv_init, h0, sr0, si0, wts, *, tc, last_row):
    bn, t_len, d_in = z.shape
    d = x.shape[2]
    d_lru = wts["cw"].shape[1]
    n_state = wts["apow"].shape[-1]
    assert t_len % tc == 0
    kern = functools.partial(_mixer0_kernel, tc=tc, last_row=last_row)
    wnames = ("cw", "cb", "wa", "ba", "wx", "bx", "lam", "bbr", "bbi", "apow",
              "ccr", "cci", "d", "wglu", "bglu", "wout")
    wlist = [wts[k] for k in wnames]
    per_b = lambda shape: pl.BlockSpec((None,) + shape, lambda b, c: (b,) + (0,) * len(shape))
    in_specs = [
        pl.BlockSpec((None, tc, d_in), lambda b, c: (b, c, 0)),
        pl.BlockSpec((None, tc, d), lambda b, c: (b, c, 0)),
        per_b((V7X_SUBLANES, d_lru)), per_b((1, d_lru)), per_b((1, n_state)), per_b((1, n_state)),
    ] + [_const_spec(w.shape) for w in wlist]
    return pl.pallas_call(
        kern,
        out_shape=(
            jax.ShapeDtypeStruct((bn, t_len, d), F32),
            jax.ShapeDtypeStruct((bn, 1, d_lru), F32),
            jax.ShapeDtypeStruct((bn, 1, n_state), F32),
            jax.ShapeDtypeStruct((bn, 1, n_state), F32),
        ),
        grid=(bn, t_len // tc),
        in_specs=in_specs,
        out_specs=(
            pl.BlockSpec((None, tc, d), lambda b, c: (b, c, 0)),
            per_b((1, d_lru)), per_b((1, n_state)), per_b((1, n_state)),
        ),
        scratch_shapes=[
            pltpu.VMEM((V7X_SUBLANES, d_lru), F32),
            pltpu.VMEM((1, d_lru), F32),
            pltpu.VMEM((1, n_state), F32),
            pltpu.VMEM((1, n_state), F32),
            pltpu.VMEM((tc, n_state), F32),
            pltpu.VMEM((tc, n_state), F32),
            pltpu.VMEM((tc, d_lru), F32),
            pltpu.VMEM((tc, d_lru), F32),
        ],
        compiler_params=_cparams(("parallel", "arbitrary")),
        name="mixer0",
    )(z, x, conv_init, h0, sr0, si0, *wlist)


def _xattn_kernel(x_ref, g_ref, wq_ref, qg_ref, mk_ref, mv_ref, wo_ref, o_ref):
    nb, tm, d = x_ref.shape
    x = x_ref[...].reshape(nb * tm, d)
    q = jnp.dot(_rms(x, g_ref[...]).astype(BF16), wq_ref[...], preferred_element_type=F32)
    dh = qg_ref.shape[1]
    head_major = len(mk_ref.shape) == 4
    rows = []
    for b in range(nb):
        if head_major:
            kb = pltpu.einshape("nhd->hnd", mk_ref[b])
            vb = pltpu.einshape("nhd->hnd", mv_ref[b])
            head_of = lambda arr, h: arr[h]
        else:
            kb, vb = mk_ref[b], mv_ref[b]
            head_of = lambda arr, h: arr[:, h * dh:(h + 1) * dh]
        outs = []
        for h in range(XA_HEADS):
            qn = _rms(q[b * tm:(b + 1) * tm, h * dh:(h + 1) * dh], qg_ref[...]).astype(BF16)
            s = lax.dot_general(qn, head_of(kb, h).astype(BF16), (((1,), (1,)), ((), ())),
                                preferred_element_type=F32) * (dh ** -0.5)
            m = jnp.max(s, axis=-1, keepdims=True)
            p = jnp.exp(s - m)
            den = jnp.sum(p, axis=-1, keepdims=True)
            oh = jnp.dot(p.astype(BF16), head_of(vb, h).astype(BF16), preferred_element_type=F32) / den
            outs.append(oh.astype(BF16))
        rows.append(jnp.concatenate(outs, axis=-1))
    o = rows[0] if nb == 1 else jnp.concatenate(rows, axis=0)
    o_ref[...] = (x + jnp.dot(o, wo_ref[...], preferred_element_type=F32)).reshape(nb, tm, d)


def _xattn(x, g, wq, qg, mk, mv, wo, *, tm, nb=1, name, layer=None):
    bn, t_len, d = x.shape
    assert t_len % tm == 0 and bn % nb == 0
    if layer is None:
        k_spec = pl.BlockSpec((nb, mk.shape[1], d), lambda b, i: (b, 0, 0))
        v_spec = pl.BlockSpec((nb, mv.shape[1], d), lambda b, i: (b, 0, 1))
    else:
        k_spec = v_spec = pl.BlockSpec((None, nb) + mk.shape[2:], lambda b, i: (layer, b, 0, 0, 0))
    return pl.pallas_call(
        _xattn_kernel,
        out_shape=jax.ShapeDtypeStruct(x.shape, F32),
        grid=(bn // nb, t_len // tm),
        in_specs=[
            pl.BlockSpec((nb, tm, d), lambda b, i: (b, i, 0)),
            _const_spec((1, d)),
            _const_spec(wq.shape),
            _const_spec((1, qg.shape[-1])),
            k_spec,
            v_spec,
            _const_spec(wo.shape),
        ],
        out_specs=pl.BlockSpec((nb, tm, d), lambda b, i: (b, i, 0)),
        compiler_params=_cparams(("parallel", "arbitrary")),
        name=name,
    )(x, g.reshape(1, d), wq, qg.reshape(1, -1), mk, mv, wo)


def _ffn_kernel(x_ref, g_ref, wg_ref, wu_ref, wd_ref, o_ref, hn_ref, acc_ref):
    j = pl.program_id(1)

    @pl.when(j == 0)
    def _():
        hn_ref[...] = _rms(x_ref[...], g_ref[...]).astype(BF16)
        acc_ref[...] = jnp.zeros_like(acc_ref)

    hn = hn_ref[...]
    gate = jnp.dot(hn, wg_ref[...], preferred_element_type=F32)
    up = jnp.dot(hn, wu_ref[...], preferred_element_type=F32)
    act = (jax.nn.silu(gate) * up).astype(BF16)
    acc_ref[...] += jnp.dot(act, wd_ref[...], preferred_element_type=F32)

    @pl.when(j == pl.num_programs(1) - 1)
    def _():
        o_ref[...] = x_ref[...] + acc_ref[...]


def _ffn(x, g, w_gu, w_d, *, name):
    m, d = x.shape
    hid = w_d.shape[0]
    th = FFN_TH
    tm = min(m, 1024)
    nj = hid // th
    assert m % tm == 0 and hid % th == 0
    return pl.pallas_call(
        _ffn_kernel,
        out_shape=jax.ShapeDtypeStruct((m, d), F32),
        grid=(m // tm, nj),
        in_specs=[
            pl.BlockSpec((tm, d), lambda i, j: (i, 0)),
            pl.BlockSpec((1, d), lambda i, j: (0, 0)),
            pl.BlockSpec((d, th), lambda i, j: (0, j)),
            pl.BlockSpec((d, th), lambda i, j: (0, j + nj)),
            pl.BlockSpec((th, d), lambda i, j: (j, 0)),
        ],
        out_specs=pl.BlockSpec((tm, d), lambda i, j: (i, 0)),
        scratch_shapes=[pltpu.VMEM((tm, d), BF16), pltpu.VMEM((tm, d), F32)],
        compiler_params=_cparams(("parallel", "arbitrary")),
        name=name,
    )(x, g.reshape(1, d), w_gu, w_gu, w_d)


def _rel_bucket_np(dist):
    dist = np.clip(np.asarray(dist), 0, None)
    max_exact = REL_BUCKETS // 2
    safe = np.maximum(dist, max_exact).astype(np.float32)
    large = max_exact + np.floor(np.log(safe / max_exact) / math.log(REL_MAX_DIST / max_exact)
                                 * (REL_BUCKETS - max_exact)).astype(np.int32)
    large = np.minimum(large, REL_BUCKETS - 1)
    return np.where(dist < max_exact, dist, large).astype(np.int32)


def _dil_kernel(q_ref, k_ref, v_ref, bias_ref, o_ref, l_ref, kprev_ref, vprev_ref, *, d, nblk):
    n = pl.program_id(1)

    @pl.when(n == 0)
    def _():
        kprev_ref[...] = jnp.zeros_like(kprev_ref)
        vprev_ref[...] = jnp.zeros_like(vprev_ref)

    first_sel = jnp.minimum(n, 1)
    lane = lax.broadcasted_iota(I32, (QB, V7X_LANES), 1)
    n_slab = q_ref.shape[0]
    heads_per_slab = V7X_LANES // DH
    kv_head = lax.broadcasted_iota(I32, (2 * QB, V7X_LANES), 1) // DH

    def residue(r, _):
        prev_k = [kprev_ref[r, c] for c in range(n_slab)]
        prev_v = [vprev_ref[r, c] for c in range(n_slab)]
        for blk in range(nblk):
            rows = pl.ds(blk * QB * d + r, QB, stride=d)
            bsel = first_sel if blk == 0 else 1
            lse = jnp.zeros((QB, V7X_LANES), F32)
            for c in range(n_slab):
                q = (q_ref[c, rows, :] * (DH ** -0.5)).astype(BF16)
                kc = k_ref[c, rows, :].astype(BF16)
                vc = v_ref[c, rows, :].astype(BF16)
                k = jnp.concatenate([prev_k[c], kc], axis=0)
                v = jnp.concatenate([prev_v[c], vc], axis=0)
                k_heads = jnp.concatenate([jnp.where(kv_head == hh, k, 0) for hh in range(heads_per_slab)], axis=0)
                v_heads = jnp.concatenate([jnp.where(kv_head == hh, v, 0) for hh in range(heads_per_slab)], axis=0)
                s_all = lax.dot_general(q, k_heads, (((1,), (1,)), ((), ())), preferred_element_type=F32)
                probs = []
                den_lanes = jnp.zeros((QB, V7X_LANES), F32)
                for hh in range(heads_per_slab):
                    h = c * heads_per_slab + hh
                    s = s_all[:, hh * 2 * QB:(hh + 1) * 2 * QB] + bias_ref[bsel, h]
                    m = jnp.max(s, axis=-1, keepdims=True)
                    p = jnp.exp(s - m)
                    den = jnp.sum(p, axis=-1, keepdims=True)
                    probs.append(p.astype(BF16))
                    den_lanes = jnp.where(lane // DH == hh, den, den_lanes)
                    lse = jnp.where(lane == h, m + jnp.log(den), lse)
                o_ref[c, rows, :] = jnp.dot(jnp.concatenate(probs, axis=-1), v_heads,
                                            preferred_element_type=F32) / den_lanes
                prev_k[c] = kc
                prev_v[c] = vc
            l_ref[rows, :] = lse
        for c in range(n_slab):
            kprev_ref[r, c] = prev_k[c]
            vprev_ref[r, c] = prev_v[c]
        return 0

    lax.fori_loop(0, d, residue, 0, unroll=min(d, 2))


def _dilated_prompt(qkv, tab, g, d, bn):
    n_slabs, m_rows, _ = qkv.shape
    s_len = m_rows // bn
    gw = HG * DH
    gs = gw // V7X_LANES
    nblk = 2 if d == 1 else 1
    chunk = nblk * QB * d
    n_chunks = s_len // chunk
    assert s_len % chunk == 0
    k_col = n_slabs // gs // 3

    qi = np.arange(QB)[:, None]
    ki = np.arange(2 * QB)[None, :]
    dist = qi + QB - ki
    band = (dist >= 0) & (dist <= N_STRIDE)
    first = band & (ki >= QB)
    onehot = (np.arange(REL_BUCKETS)[:, None] == _rel_bucket_np(dist * d).reshape(1, -1)).astype(np.float32)
    bias = jnp.dot(tab.T.astype(F32), jnp.asarray(onehot),
                   precision=lax.Precision.HIGHEST).reshape(HG, QB, 2 * QB)
    bias2 = jnp.stack([jnp.where(first[None], bias, NEG), jnp.where(band[None], bias, NEG)])

    col = lambda section: (lambda b, n: (section * k_col + g, b * n_chunks + n, 0))
    blk = (gs, chunk, V7X_LANES)
    return pl.pallas_call(
        functools.partial(_dil_kernel, d=d, nblk=nblk),
        out_shape=(jax.ShapeDtypeStruct((gs, m_rows, V7X_LANES), F32),
                   jax.ShapeDtypeStruct((m_rows, V7X_LANES), F32)),
        grid=(bn, n_chunks),
        in_specs=[
            pl.BlockSpec(blk, col(0)), pl.BlockSpec(blk, col(1)), pl.BlockSpec(blk, col(2)),
            _const_spec(bias2.shape),
        ],
        out_specs=(pl.BlockSpec(blk, lambda b, n: (0, b * n_chunks + n, 0)),
                   pl.BlockSpec((chunk, V7X_LANES), lambda b, n: (b * n_chunks + n, 0))),
        scratch_shapes=[pltpu.VMEM((d, gs, QB, V7X_LANES), BF16), pltpu.VMEM((d, gs, QB, V7X_LANES), BF16)],
        compiler_params=_cparams(("parallel", "arbitrary")),
        name=f"dilated_prompt_g{g}",
    )(qkv, qkv, qkv, bias2)


def _split_dot(x, w_ref):
    hi = x.astype(BF16)
    lo = (x - hi.astype(F32)).astype(BF16)
    w = w_ref[...]
    return jnp.dot(hi, w, preferred_element_type=F32) + jnp.dot(lo, w, preferred_element_type=F32)


def _dil_sample_kernel(q_ref, k_ref, v_ref, cache_ref, bias_ref, biasn_ref, o_ref, l_ref, cout_ref, *,
                       t_valid, t_pad):
    n_slab = q_ref.shape[0]
    nb = cache_ref.shape[0]
    w = cache_ref.shape[-1]
    lane = lax.broadcasted_iota(I32, (t_pad, V7X_LANES), 1)
    is_new = lax.broadcasted_iota(I32, (DH, V7X_LANES), 1) >= V7X_LANES - t_valid

    def as_last_columns(x):
        shifted = pltpu.roll(x, t_pad - t_valid, 0)
        tile = jnp.concatenate([jnp.zeros((V7X_LANES - t_pad, x.shape[1]), F32), shifted], axis=0)
        return tile.T

    def sequence(b, _):
        rows = pl.ds(pl.multiple_of(b * t_pad, t_pad), t_pad)
        unslab = lambda ref: jnp.concatenate([ref[c, rows, :] for c in range(n_slab)], axis=-1)
        qn = unslab(q_ref)
        kn = unslab(k_ref)
        vn = unslab(v_ref)
        new_cols = (as_last_columns(kn), as_last_columns(vn))
        qb, kb, vb = qn.astype(BF16), kn.astype(BF16), vn.astype(BF16)
        lse = jnp.zeros((t_pad, V7X_LANES), F32)
        outs = []
        for h in range(HG):
            sl = slice(h * DH, (h + 1) * DH)
            k_t = cache_ref[b, 0, h]
            v_t = cache_ref[b, 1, h]
            s = jnp.dot(qb[:, sl], k_t.astype(BF16), preferred_element_type=F32) * (DH ** -0.5) + bias_ref[h]
            sn = lax.dot_general(qb[:, sl], kb[:, sl], (((1,), (1,)), ((), ())),
                                 preferred_element_type=F32) * (DH ** -0.5) + biasn_ref[h]
            m = jnp.maximum(jnp.max(s, axis=-1, keepdims=True), jnp.max(sn, axis=-1, keepdims=True))
            p = jnp.exp(s - m)
            pn = jnp.exp(sn - m)
            den = jnp.sum(p, axis=-1, keepdims=True) + jnp.sum(pn, axis=-1, keepdims=True)
            pv = lax.dot_general(p.astype(BF16), v_t.astype(BF16), (((1,), (1,)), ((), ())),
                                 preferred_element_type=F32)
            outs.append((pv + jnp.dot(pn.astype(BF16), vb[:, sl], preferred_element_type=F32)) / den)
            lse = jnp.where(lane == h, m + jnp.log(den), lse)
            for kv, old in enumerate((k_t, v_t)):
                moved = pltpu.roll(old, w - t_valid, 1)
                cout_ref[b, kv, h] = moved
                cout_ref[b, kv, h, :, w - V7X_LANES:] = jnp.where(is_new, new_cols[kv][sl, :],
                                                                  moved[:, w - V7X_LANES:])
        o = jnp.concatenate(outs, axis=-1)
        for c in range(n_slab):
            o_ref[c, rows, :] = o[:, c * V7X_LANES:(c + 1) * V7X_LANES]
        l_ref[rows, :] = lse
        return 0

    if nb == 1:
        sequence(0, 0)
    else:
        lax.fori_loop(0, nb, sequence, 0)


def _dilated_sample(qkv, cache, tab, g, d, t_valid):
    n_slabs, m_rows, _ = qkv.shape
    db, wb = cache.shape[0], cache.shape[-1]
    t_pad = m_rows // db
    gw = HG * DH
    gs = gw // V7X_LANES
    assert wb == N_STRIDE * d and t_valid <= V7X_LANES and wb % V7X_LANES == 0
    k_col = n_slabs // gs // 3

    def bias_of(j, ok):
        onehot = (np.arange(REL_BUCKETS)[:, None] == _rel_bucket_np(d * j).reshape(1, -1)) & ok.reshape(1, -1)
        looked_up = jnp.dot(tab.T.astype(F32), jnp.asarray(onehot.astype(np.float32)),
                            precision=lax.Precision.HIGHEST).reshape((HG,) + j.shape)
        return jnp.where(ok[None], looked_up, NEG)

    tq = np.arange(t_pad)[:, None]
    pos = np.arange(wb)[None, :]
    jc = (wb + tq - pos) // d
    bias = bias_of(jc, ((wb + tq - pos) % d == 0) & (jc >= 1) & (jc <= N_STRIDE))
    tn = np.arange(t_pad)[None, :]
    jn = (tq - tn) // d
    bias_new = bias_of(jn, (tn <= tq) & ((tq - tn) % d == 0) & (jn <= N_STRIDE) & (tn < t_valid))

    kern = functools.partial(_dil_sample_kernel, t_valid=t_valid, t_pad=t_pad)
    cache_bytes = 4 * math.prod(cache.shape[1:])
    nb = max(1, min(db, (4 * 1024 * 1024) // cache_bytes))
    assert db % nb == 0
    blk = (gs, nb * t_pad, V7X_LANES)
    cblk = pl.BlockSpec((nb,) + cache.shape[1:], lambda b: (b, 0, 0, 0, 0))
    return pl.pallas_call(
        kern,
        out_shape=(jax.ShapeDtypeStruct((gs, m_rows, V7X_LANES), F32),
                   jax.ShapeDtypeStruct((m_rows, V7X_LANES), F32),
                   jax.ShapeDtypeStruct(cache.shape, F32)),
        grid=(db // nb,),
        in_specs=[
            pl.BlockSpec(blk, lambda b: (g, b, 0)),
            pl.BlockSpec(blk, lambda b: (k_col + g, b, 0)),
            pl.BlockSpec(blk, lambda b: (2 * k_col + g, b, 0)),
            cblk, _const_spec(bias.shape), _const_spec(bias_new.shape),
        ],
        out_specs=(pl.BlockSpec(blk, lambda b: (0, b, 0)),
                   pl.BlockSpec((nb * t_pad, V7X_LANES), lambda b: (b, 0)),
                   cblk),
        compiler_params=_cparams(("parallel",)),
        name=f"dilated_sample_g{g}",
    )(qkv, qkv, qkv, cache, bias, bias_new)


def _merge_wo_kernel(o0_ref, o1_ref, o2_ref, l0_ref, l1_ref, l2_ref, x_ref, hexp_ref, w_ref, out_ref):
    l0, l1, l2 = l0_ref[...], l1_ref[...], l2_ref[...]
    m = jnp.maximum(jnp.maximum(l0, l1), l2)
    e0, e1, e2 = jnp.exp(l0 - m), jnp.exp(l1 - m), jnp.exp(l2 - m)
    inv = 1.0 / (e0 + e1 + e2)
    unslab = lambda ref: jnp.concatenate([ref[c] for c in range(ref.shape[0])], axis=-1)
    o = (_split_dot(e0 * inv, hexp_ref) * unslab(o0_ref) + _split_dot(e1 * inv, hexp_ref) * unslab(o1_ref)
         + _split_dot(e2 * inv, hexp_ref) * unslab(o2_ref))
    out_ref[...] = x_ref[...] + jnp.dot(o.astype(BF16), w_ref[...], preferred_element_type=F32)


def _merge_wo(outs, lses, x, w, *, name):
    m, d = x.shape
    gw = w.shape[0]
    tm = min(m, 512)
    assert m % tm == 0
    head = np.arange(gw) // DH
    hexp = jnp.asarray((np.arange(V7X_LANES)[:, None] == head[None, :]).astype(np.float32), dtype=BF16)
    row = lambda width: pl.BlockSpec((tm, width), lambda i: (i, 0))
    slab = pl.BlockSpec((gw // V7X_LANES, tm, V7X_LANES), lambda i: (0, i, 0))
    return pl.pallas_call(
        _merge_wo_kernel,
        out_shape=jax.ShapeDtypeStruct((m, d), F32),
        grid=(m // tm,),
        in_specs=[slab] * 3 + [row(V7X_LANES)] * 3 + [row(d), _const_spec(hexp.shape), _const_spec(w.shape)],
        out_specs=row(d),
        compiler_params=_cparams(("parallel",)),
        name=name,
    )(*outs, *lses, x, hexp, w)


def _router_kernel(x_ref, g_ref, wr_ref, br_ref, cnt0_ref, tri_ref,
                   hn_ref, gate_ref, lp_ref, tbase_ref, tcnt_ref, cnt_ref, run_ref):
    @pl.when(pl.program_id(0) == 0)
    def _():
        run_ref[...] = cnt0_ref[...]

    hn = _rms(x_ref[...], g_ref[...])
    hn_ref[...] = hn
    wr = wr_ref[...]
    h_hi = hn.astype(BF16)
    h_lo = (hn - h_hi.astype(F32)).astype(BF16)
    w_hi = wr.astype(BF16)
    w_lo = (wr - w_hi.astype(F32)).astype(BF16)
    nt_dot = lambda a, b: lax.dot_general(a, b, (((1,), (1,)), ((), ())), preferred_element_type=F32)
    logits = nt_dot(w_hi, h_hi) + nt_dot(w_lo, h_hi) + nt_dot(w_hi, h_lo) + br_ref[...]
    ne = logits.shape[0]
    eidx = lax.broadcasted_iota(I32, logits.shape, 0)
    m1 = jnp.max(logits, axis=0, keepdims=True)
    i1 = jnp.min(jnp.where(logits == m1, eidx, ne), axis=0, keepdims=True)
    rest = jnp.where(eidx == i1, -jnp.inf, logits)
    m2 = jnp.max(rest, axis=0, keepdims=True)
    i2 = jnp.min(jnp.where(rest == m2, eidx, ne), axis=0, keepdims=True)
    e2 = jnp.exp(m2 - m1)
    g1 = 1.0 / (1.0 + e2)
    g2 = e2 / (1.0 + e2)
    gate_ref[...] = jnp.where(eidx == 0, g1, jnp.where(eidx == 1, g2, 0.0))
    chosen = jnp.where(jnp.logical_or(eidx == i1, eidx == i2), 1.0, 0.0)
    before = jnp.dot(chosen.astype(BF16), tri_ref[...], preferred_element_type=F32)
    tile_cnt = jnp.ceil(jnp.sum(chosen, axis=1, keepdims=True) * (1.0 / V7X_SUBLANES)) * V7X_SUBLANES
    ecol = eidx[:, :1]
    lower = jnp.zeros_like(tile_cnt)
    for e in range(ne - 1):
        lower = lower + jnp.where(ecol > e, tile_cnt[e:e + 1, :], 0.0)
    local = before + lower
    l1 = jnp.sum(jnp.where(eidx == i1, local, 0.0), axis=0, keepdims=True)
    l2 = jnp.sum(jnp.where(eidx == i2, local, 0.0), axis=0, keepdims=True)
    lp_ref[...] = jnp.where(eidx == 0, l1, jnp.where(eidx == 1, l2, 0.0)).astype(I32)
    tbase_ref[0] = run_ref[...]
    tcnt_ref[0] = tile_cnt
    run_ref[...] += tile_cnt
    cnt_ref[...] = run_ref[...]


def _router(x, g, wr, br, cnt0, *, name):
    m, d = x.shape
    ne = wr.shape[1]
    tm = min(m, MOE_TOKEN_TILE)
    assert m % tm == 0
    n_tiles = m // tm
    tri = jnp.asarray(np.triu(np.ones((tm, tm), np.float32), 1), dtype=BF16)
    cols = pl.BlockSpec((ne, tm), lambda i: (0, i))
    per_tile = pl.BlockSpec((1, ne, 1), lambda i: (i, 0, 0))
    return pl.pallas_call(
        _router_kernel,
        out_shape=(jax.ShapeDtypeStruct((m, d), F32),
                   jax.ShapeDtypeStruct((ne, m), F32),
                   jax.ShapeDtypeStruct((ne, m), I32),
                   jax.ShapeDtypeStruct((n_tiles, ne, 1), F32),
                   jax.ShapeDtypeStruct((n_tiles, ne, 1), F32),
                   jax.ShapeDtypeStruct((ne, 1), F32)),
        grid=(n_tiles,),
        in_specs=[pl.BlockSpec((tm, d), lambda i: (i, 0)), _const_spec((1, d)), _const_spec((ne, d)),
                  _const_spec((ne, 1)), _const_spec((ne, 1)), _const_spec((tm, tm))],
        out_specs=(pl.BlockSpec((tm, d), lambda i: (i, 0)), cols, cols, per_tile, per_tile,
                   _const_spec((ne, 1))),
        scratch_shapes=[pltpu.VMEM((ne, 1), F32)],
        compiler_params=_cparams(("arbitrary",)),
        name=name,
    )(x, g.reshape(1, d), wr.T, br.reshape(ne, 1), cnt0, tri)


def _staging_rows(n_tokens):
    return TOP_K * n_tokens + V7X_SUBLANES * N_EXPERTS


def _expert_run_copies(tile, dst_ref, cnt_ref, n_tokens, make_copy, wait):
    local = jnp.int32(0)
    for e in range(N_EXPERTS):
        count = cnt_ref[tile * N_EXPERTS + e]
        sorted_row = dst_ref[tile * N_EXPERTS + e]
        size = pl.next_power_of_2(n_tokens)
        while size >= V7X_SUBLANES:
            has = (count & size) != 0

            @pl.when(has)
            def _(local=local, sorted_row=sorted_row, size=size):
                copy = make_copy(pl.multiple_of(local, V7X_SUBLANES), pl.multiple_of(sorted_row, V7X_SUBLANES), size)
                if wait:
                    copy.wait()
                else:
                    copy.start()

            step = jnp.where(has, size, 0)
            local = local + step
            sorted_row = sorted_row + step
            size //= 2


def _dispatch_kernel(dst_ref, cnt_ref, hn_ref, lp_ref, xs_in_ref, xs_ref, stage_ref, sem, *, tmd):
    del xs_in_ref
    i = pl.program_id(0)
    n = pl.num_programs(0)
    slot = i % 2

    def copies(tile, s, wait):
        _expert_run_copies(tile, dst_ref, cnt_ref, tmd, lambda local, sorted_row, size: pltpu.make_async_copy(
            stage_ref.at[s, pl.ds(local, size)], xs_ref.at[pl.ds(sorted_row, size)], sem.at[s]), wait)

    @pl.when(i >= 2)
    def _():
        copies(i - 2, slot, True)

    lp = lp_ref[...]
    row = lax.broadcasted_iota(I32, (stage_ref.shape[1], tmd), 0)
    place = (jnp.where(row == lp[0:1, :], 1.0, 0.0) + jnp.where(row == lp[1:2, :], 1.0, 0.0)).astype(BF16)
    stage_ref[slot] = jnp.dot(place, hn_ref[...].astype(BF16), preferred_element_type=F32)
    copies(i, slot, False)

    @pl.when(i == n - 1)
    def _():
        copies(i, slot, True)

        @pl.when(n >= 2)
        def _():
            copies(i - 1, 1 - slot, True)


def _dispatch(hn, lp, dst, cnt, xs, *, name):
    m, d = hn.shape
    ne = lp.shape[0]
    tmd = min(m, MOE_TOKEN_TILE)
    assert m % tmd == 0
    kern = functools.partial(_dispatch_kernel, tmd=tmd)
    grid_spec = pltpu.PrefetchScalarGridSpec(
        num_scalar_prefetch=2,
        grid=(m // tmd,),
        in_specs=[
            pl.BlockSpec((tmd, d), lambda i, dst, cnt: (i, 0)),
            pl.BlockSpec((ne, tmd), lambda i, dst, cnt: (0, i)),
            pl.BlockSpec(memory_space=pl.ANY),
        ],
        out_specs=pl.BlockSpec(memory_space=pl.ANY),
        scratch_shapes=[pltpu.VMEM((2, _staging_rows(tmd), d), F32), pltpu.SemaphoreType.DMA((2,))],
    )
    return pl.pallas_call(
        kern,
        out_shape=jax.ShapeDtypeStruct(xs.shape, xs.dtype),
        grid_spec=grid_spec,
        input_output_aliases={4: 0},
        compiler_params=_cparams(("arbitrary",)),
        name=name,
    )(dst, cnt, hn, lp, xs)


def _moe_ffn_kernel(te_ref, nused_ref, x_ref, wg_ref, wu_ref, wd_ref, o_ref, xb_ref, acc_ref):
    t = pl.program_id(0)
    j = pl.program_id(1)
    last = pl.num_programs(1) - 1
    used = t < nused_ref[0]

    @pl.when(used)
    def _():
        @pl.when(j == 0)
        def _():
            xb_ref[...] = x_ref[...].astype(BF16)
            acc_ref[...] = jnp.zeros_like(acc_ref)

        xb = xb_ref[...]
        gate = jnp.dot(xb, wg_ref[...].astype(BF16), preferred_element_type=F32)
        up = jnp.dot(xb, wu_ref[...].astype(BF16), preferred_element_type=F32)
        act = (jax.nn.silu(gate) * up).astype(BF16)
        acc_ref[...] += jnp.dot(act, wd_ref[...].astype(BF16), preferred_element_type=F32)

        @pl.when(j == last)
        def _():
            o_ref[...] = acc_ref[...]

    @pl.when(jnp.logical_and(jnp.logical_not(used), j == last))
    def _():
        o_ref[...] = jnp.zeros_like(o_ref)


def _moe_ffn(xs, te, n_used, w_gu, w_d):
    n_rows, d = xs.shape
    hid = w_d.shape[1]
    tm, th = MOE_TM, MOE_TH
    nj = hid // th
    nt = n_rows // tm
    assert hid % th == 0 and n_rows % tm == 0

    def jj(t, j, nu):
        return jnp.where(t < nu[0], j, nj - 1)

    grid_spec = pltpu.PrefetchScalarGridSpec(
        num_scalar_prefetch=2,
        grid=(nt, nj),
        in_specs=[
            pl.BlockSpec((tm, d), lambda t, j, te, nu: (jnp.maximum(jnp.minimum(t, nu[0] - 1), 0), 0)),
            pl.BlockSpec((None, d, th), lambda t, j, te, nu: (te[t], 0, jj(t, j, nu))),
            pl.BlockSpec((None, d, th), lambda t, j, te, nu: (te[t], 0, jj(t, j, nu) + nj)),
            pl.BlockSpec((None, th, d), lambda t, j, te, nu: (te[t], jj(t, j, nu), 0)),
        ],
        out_specs=pl.BlockSpec((tm, d), lambda t, j, te, nu: (t, 0)),
        scratch_shapes=[pltpu.VMEM((tm, d), BF16), pltpu.VMEM((tm, d), F32)],
    )
    return pl.pallas_call(
        _moe_ffn_kernel,
        out_shape=jax.ShapeDtypeStruct((n_rows, d), F32),
        grid_spec=grid_spec,
        compiler_params=_cparams(("arbitrary", "arbitrary")),
        name="moe_experts",
    )(te, n_used, xs, w_gu, w_gu, w_d)


def _combine_kernel(dst_ref, cnt_ref, x_ref, gate_ref, lp_ref, ys_ref, o_ref, buf_ref, sem, *, tmc):
    i = pl.program_id(0)
    n = pl.num_programs(0)
    slot = i % 2

    def gather(tile, s, wait):
        _expert_run_copies(tile, dst_ref, cnt_ref, tmc, lambda local, sorted_row, size: pltpu.make_async_copy(
            ys_ref.at[pl.ds(sorted_row, size)], buf_ref.at[s, pl.ds(local, size)], sem.at[s]), wait)

    @pl.when(i == 0)
    def _():
        buf_ref[...] = jnp.zeros_like(buf_ref)
        gather(i, slot, False)

    @pl.when(i + 1 < n)
    def _():
        gather(i + 1, 1 - slot, False)

    gather(i, slot, True)
    g = gate_ref[...]
    lp = lp_ref[...]
    row = lax.broadcasted_iota(I32, (buf_ref.shape[1], tmc), 0)
    w = jnp.where(row == lp[0:1, :], g[0:1, :], 0.0) + jnp.where(row == lp[1:2, :], g[1:2, :], 0.0)
    rows = buf_ref[slot]
    w_hi = w.astype(BF16)
    w_lo = (w - w_hi.astype(F32)).astype(BF16)
    r_hi = rows.astype(BF16)
    r_lo = (rows - r_hi.astype(F32)).astype(BF16)
    tn_dot = lambda a, b: lax.dot_general(a, b, (((0,), (0,)), ((), ())), preferred_element_type=F32)
    o_ref[...] = x_ref[...] + tn_dot(w_hi, r_hi) + tn_dot(w_hi, r_lo) + tn_dot(w_lo, r_hi)


def _combine(x, gate, lp, dst, cnt, ys, *, name):
    m, d = x.shape
    ne = gate.shape[0]
    tmc = min(m, MOE_TOKEN_TILE)
    assert m % tmc == 0
    kern = functools.partial(_combine_kernel, tmc=tmc)
    row = lambda width: pl.BlockSpec((tmc, width), lambda i, dst, cnt: (i, 0))
    cols = pl.BlockSpec((ne, tmc), lambda i, dst, cnt: (0, i))
    grid_spec = pltpu.PrefetchScalarGridSpec(
        num_scalar_prefetch=2,
        grid=(m // tmc,),
        in_specs=[row(d), cols, cols, pl.BlockSpec(memory_space=pl.ANY)],
        out_specs=row(d),
        scratch_shapes=[pltpu.VMEM((2, _staging_rows(tmc), d), F32), pltpu.SemaphoreType.DMA((2,))],
    )
    return pl.pallas_call(
        kern,
        out_shape=jax.ShapeDtypeStruct((m, d), F32),
        grid_spec=grid_spec,
        compiler_params=_cparams(("arbitrary",)),
        name=name,
    )(dst, cnt, x, gate, lp, ys)


def _moe_plan(counts, tile_bases, tile_counts, tm, nt):
    padded = ((counts + tm - 1) // tm) * tm
    ends = jnp.cumsum(padded)
    starts = ends - padded
    dst = [(starts[None, :] + b[:, :, 0].astype(I32)).reshape(-1) for b in tile_bases]
    cnt = [c[:, :, 0].astype(I32).reshape(-1) for c in tile_counts]
    n_used = (ends[-1] // tm).astype(I32)
    tile = jnp.arange(nt, dtype=I32)
    first_row = jnp.minimum(tile, n_used - 1) * tm
    te = jnp.minimum(jnp.sum((first_row[:, None] >= ends[None, :]).astype(I32), axis=1), N_EXPERTS - 1)
    return dst, cnt, te, n_used.reshape(1)


def _block_diag(w):
    n, a, b = w.shape
    return jnp.einsum("nij,nm->nimj", w, jnp.eye(n, dtype=w.dtype)).reshape(n * a, n * b)


def kernel(x_prompt, x_sample, state_conv, state_lru, state_s5_re, state_s5_im, cache_swa0_kv, cache_swa1_kv, cache_swa2_kv, cache_mem_k, cache_mem_v, mem_prompt, norm_mix, norm_xa, norm_ffn, norm_mem, w_in_even, conv_w, conv_b, lru_wa, lru_ba, lru_wx, lru_bx, lru_lam, s5_lam_re, s5_lam_im, s5_log_dt, s5_b_re, s5_b_im, s5_c_re, s5_c_im, s5_d, s5_w_glu, s5_b_glu, w_out_even, w_qkv_odd, q_norm_odd, k_norm_odd, w_o_odd, rel_bias, xa_wq, xa_wkv, xa_qn, xa_kn, xa_wo, ffn_w_gu, ffn_w_down, moe_router_w, moe_router_b, moe_w_gu, moe_w_down):
    bp, s_len, d = x_prompt.shape
    db, t_dec, _ = x_sample.shape
    n_mem = mem_prompt.shape[1]
    tp = SAMPLE_PAD_T
    d_lru = conv_w.shape[-1]
    s5_g, s5_p, s5_h = s5_b_re.shape[1:]
    n_state = s5_g * s5_p
    d_s5 = s5_g * s5_h
    xa_dh = d // XA_HEADS
    caches = (cache_swa0_kv, cache_swa1_kv, cache_swa2_kv)
    bf = lambda w: w.astype(BF16)

    yp = x_prompt
    ys = jnp.pad(x_sample, ((0, 0), (0, tp - t_dec), (0, 0)))

    w_in = bf(w_in_even[0])
    bbr, bbi, apow = _s5_prep(s5_lam_re[0], s5_lam_im[0], s5_log_dt[0], s5_b_re[0], s5_b_im[0])
    n_slab = d_s5 // V7X_LANES
    gl = s5_g // n_slab
    eye_g = jnp.eye(gl, dtype=F32)

    def c_blocks(c_mat):
        blocks = jnp.einsum("cghp,gk->cgpkh", c_mat.reshape(n_slab, gl, s5_h, s5_p), eye_g)
        return bf(blocks.reshape(n_slab, gl * s5_p, V7X_LANES))

    mix_w = dict(
        cw=conv_w[0], cb=conv_b[0].reshape(1, d_lru),
        wa=bf(_block_diag(lru_wa[0])), ba=lru_ba[0].reshape(1, d_lru),
        wx=bf(_block_diag(lru_wx[0])), bx=lru_bx[0].reshape(1, d_lru),
        lam=lru_lam[0].reshape(1, d_lru),
        bbr=bbr, bbi=bbi, apow=apow,
        ccr=c_blocks(s5_c_re[0]), cci=c_blocks(s5_c_im[0]),
        d=s5_d[0].reshape(1, d_s5), wglu=bf(s5_w_glu[0]), bglu=s5_b_glu[0].reshape(1, d_s5),
        wout=bf(w_out_even[0]),
    )
    z_p = _norm_matmul(yp.reshape(bp * s_len, d), norm_mix[0], w_in, tn=w_in.shape[1],
                       name="in_proj_p").reshape(bp, s_len, -1)
    yp, p_lru, p_s5r, p_s5i = _mixer0(
        z_p, yp, jnp.zeros((bp, V7X_SUBLANES, d_lru), F32), jnp.zeros((bp, 1, d_lru), F32),
        jnp.zeros((bp, 1, n_state), F32), jnp.zeros((bp, 1, n_state), F32), mix_w,
        tc=MIX_TC, last_row=MIX_TC - 1)
    z_s = _norm_matmul(ys.reshape(db * tp, d), norm_mix[0], w_in, name="in_proj_s").reshape(db, tp, -1)
    conv_init = jnp.pad(state_conv[0], ((0, 0), (V7X_SUBLANES - (CONV_W - 1), 0), (0, 0)))
    ys, s_lru, s_s5r, s_s5i = _mixer0(
        z_s, ys, conv_init, state_lru[0].reshape(db, 1, d_lru),
        state_s5_re[0].reshape(db, 1, n_state), state_s5_im[0].reshape(db, 1, n_state), mix_w,
        tc=tp, last_row=t_dec - 1)
    p_state_conv = z_p[:, s_len - (CONV_W - 1):, :d_lru][None]
    s_state_conv = z_s[:, t_dec - (CONV_W - 1):t_dec, :d_lru][None]

    p_mk, p_mv = [], []

    def cross_attention(layer, yp, ys):
        kn_gain = jnp.concatenate([jnp.tile(xa_kn[layer], XA_HEADS), jnp.ones((d,), F32)]).reshape(1, 2 * d)
        kv = _norm_matmul(mem_prompt.reshape(bp * n_mem, d), norm_mem[layer], bf(xa_wkv[layer]),
                          head_gain=kn_gain, n_norm_cols=d, dh=xa_dh, name=f"mem_kv{layer}")
        kv = kv.reshape(bp, n_mem, 2 * d)
        mk, mv = kv[:, :, :d], kv[:, :, d:]
        p_mk.append(mk.reshape(bp, n_mem, XA_HEADS, xa_dh))
        p_mv.append(mv.reshape(bp, n_mem, XA_HEADS, xa_dh))
        wq, wo = bf(xa_wq[layer]), bf(xa_wo[layer])
        yp = _xattn(yp, norm_xa[layer], wq, xa_qn[layer], kv, kv, wo, tm=512, name=f"xattn_p{layer}")
        ys = _xattn(ys, norm_xa[layer], wq, xa_qn[layer], cache_mem_k, cache_mem_v, wo,
                    tm=tp, nb=4, name=f"xattn_s{layer}", layer=layer)
        return yp, ys

    yp, ys = cross_attention(0, yp, ys)
    w_gu0, w_d0 = bf(ffn_w_gu[0]), bf(ffn_w_down[0])
    yp = _ffn(yp.reshape(bp * s_len, d), norm_ffn[0], w_gu0, w_d0, name="ffn_p").reshape(bp, s_len, d)
    ys = _ffn(ys.reshape(db * tp, d), norm_ffn[0], w_gu0, w_d0, name="ffn_s").reshape(db, tp, d)

    n_heads = len(WINDOWS) * HG
    d_c = n_heads * DH
    gw = HG * DH
    w_qkv = bf(w_qkv_odd[0])
    qk_gain = jnp.concatenate([jnp.tile(q_norm_odd[0], n_heads), jnp.tile(k_norm_odd[0], n_heads),
                               jnp.ones((d_c,), F32)]).reshape(1, 3 * d_c)
    qkv_p = _norm_matmul(yp.reshape(bp * s_len, d), norm_mix[1], w_qkv, head_gain=qk_gain,
                         n_norm_cols=2 * d_c, dh=DH, slabs=True, tn=d_c, name="qkv_p")
    qkv_s = _norm_matmul(ys.reshape(db * tp, d), norm_mix[1], w_qkv, head_gain=qk_gain,
                         n_norm_cols=2 * d_c, dh=DH, slabs=True, name="qkv_s")
    gs = gw // V7X_LANES
    w_o = bf(w_o_odd[0])
    outs_p, lses_p, outs_s, lses_s, p_swa, s_swa = [], [], [], [], [], []

    def kv_rows(qkv, n_batch, t_len, g, lo, hi):
        ks = (len(WINDOWS) + g) * gs
        vs = (2 * len(WINDOWS) + g) * gs
        slabs = qkv.reshape(-1, n_batch, t_len, V7X_LANES)
        kv = jnp.stack([slabs[ks:ks + gs, :, lo:hi], slabs[vs:vs + gs, :, lo:hi]])
        return jnp.transpose(kv, (2, 3, 0, 1, 4)).reshape(n_batch, hi - lo, 2, HG, DH)

    for g, dil in enumerate(DILATIONS):
        tab = rel_bias[:, g * HG:(g + 1) * HG]
        o, l = _dilated_prompt(qkv_p, tab, g, dil, bp)
        outs_p.append(o)
        lses_p.append(l)
        cache_t = jnp.transpose(caches[g][0], (0, 2, 3, 4, 1))
        o, l, cache_t = _dilated_sample(qkv_s, cache_t, tab, g, dil, t_dec)
        outs_s.append(o)
        lses_s.append(l)
        s_swa.append(jnp.transpose(cache_t, (0, 4, 1, 2, 3))[None])
        win = min(WINDOWS[g], s_len)
        p_swa.append(kv_rows(qkv_p, bp, s_len, g, s_len - win, s_len)[None])
    yp = _merge_wo(outs_p, lses_p, yp.reshape(bp * s_len, d), w_o, name="merge_wo_p").reshape(bp, s_len, d)
    ys = _merge_wo(outs_s, lses_s, ys.reshape(db * tp, d), w_o, name="merge_wo_s").reshape(db, tp, d)

    yp, ys = cross_attention(1, yp, ys)

    yp2 = yp.reshape(bp * s_len, d)
    ys2 = ys[:, :t_dec].reshape(db * t_dec, d)
    zero_cnt = jnp.zeros((N_EXPERTS, 1), F32)
    hn_p, gate_p, lp_p, base_p, tcnt_p, cnt_p = _router(
        yp2, norm_ffn[1], moe_router_w[0], moe_router_b[0], zero_cnt, name="router_p")
    hn_s, gate_s, lp_s, base_s, tcnt_s, cnt_s = _router(
        ys2, norm_ffn[1], moe_router_w[0], moe_router_b[0], cnt_p, name="router_s")
    n_tok = yp2.shape[0] + ys2.shape[0]
    n_runs = N_EXPERTS * (base_p.shape[0] + base_s.shape[0])
    nt = -(-(n_tok * TOP_K + (V7X_SUBLANES - 1) * n_runs) // MOE_TM) + N_EXPERTS
    (dst_p, dst_s), (rc_p, rc_s), te, n_used = _moe_plan(
        cnt_s[:, 0].astype(I32), (base_p, base_s), (tcnt_p, tcnt_s), MOE_TM, nt)
    xs = jnp.zeros((nt * MOE_TM, d), F32)
    xs = _dispatch(hn_p, lp_p, dst_p, rc_p, xs, name="dispatch_p")
    xs = _dispatch(hn_s, lp_s, dst_s, rc_s, xs, name="dispatch_s")
    ysort = _moe_ffn(xs, te, n_used, moe_w_gu[0], moe_w_down[0])
    yp = _combine(yp2, gate_p, lp_p, dst_p, rc_p, ysort, name="moe_combine_p").reshape(bp, s_len, d)
    ys = _combine(ys2, gate_s, lp_s, dst_s, rc_s, ysort, name="moe_combine_s").reshape(db, t_dec, d)

    return (yp, ys,
            p_state_conv, p_lru.reshape(1, bp, d_lru),
            p_s5r.reshape(1, bp, s5_g, s5_p), p_s5i.reshape(1, bp, s5_g, s5_p),
            p_swa[0], p_swa[1], p_swa[2], jnp.stack(p_mk), jnp.stack(p_mv),
            s_state_conv, s_lru.reshape(1, db, d_lru),
            s_s5r.reshape(1, db, s5_g, s5_p), s_s5i.reshape(1, db, s5_g, s5_p),
            s_swa[0], s_swa[1], s_swa[2])
```

```python
import functools
import math

import jax
import jax.numpy as jnp
import numpy as np
from jax import lax
from jax.experimental import pallas as pl
from jax.experimental.pallas import tpu as pltpu

F32 = jnp.float32
BF16 = jnp.bfloat16
I32 = jnp.int32

EPS = 1e-6
NEG = -1e30

V7X_SUBLANES = 8
V7X_LANES = 128
V7X_VMEM_BYTES = 64 * 1024 * 1024
VMEM_LIMIT = V7X_VMEM_BYTES - 8 * 1024 * 1024

LRU_C = 8.0
CONV_W = 4
WINDOWS = (128, 512, 2048)
DILATIONS = (1, 4, 16)
HG = 8
DH = 64
N_STRIDE = 128
QB = 128
REL_BUCKETS = 32
REL_MAX_DIST = WINDOWS[-1]
XA_HEADS = 4
N_EXPERTS = 8
TOP_K = 2
SAMPLE_PAD_T = 16

MIX_TC = 256
MOE_TM = 1024
MOE_TH = 512
MOE_TOKEN_TILE = 256
SLABS_PER_DOT = 2
FFN_TH = 512


def _cparams(sem):
    return pltpu.CompilerParams(dimension_semantics=sem, vmem_limit_bytes=VMEM_LIMIT)


def _const_spec(shape):
    nd = len(shape)
    return pl.BlockSpec(shape, lambda *_: (0,) * nd)


def _rms(x, g):
    return x * lax.rsqrt(jnp.mean(x * x, axis=-1, keepdims=True) + EPS) * g


def _norm_matmul_kernel(x_ref, g_ref, w_ref, hg_ref, hs_ref, o_ref, hn_ref, *, n_norm_tiles, dh, slabs,
                        row_split):
    j = pl.program_id(1)

    @pl.when(j == 0)
    def _():
        hn_ref[...] = _rms(x_ref[...], g_ref[...]).astype(BF16)

    tm = hn_ref.shape[0]
    rs = tm // row_split

    def emit(r, val):
        rows = slice(r * rs, (r + 1) * rs)
        if slabs:
            for c in range(o_ref.shape[0]):
                o_ref[c, rows, :] = val[:, c * V7X_LANES:(c + 1) * V7X_LANES]
        else:
            o_ref[rows, :] = val

    def tile(r, normed):
        y = jnp.dot(hn_ref[r * rs:(r + 1) * rs, :], w_ref[...], preferred_element_type=F32)
        if normed:
            sw = hs_ref.shape[0]
            parts = []
            for s in range(y.shape[1] // sw):
                part = y[:, s * sw:(s + 1) * sw]
                ssq = jnp.dot((part * part).astype(BF16), hs_ref[...], preferred_element_type=F32)
                parts.append(part * lax.rsqrt(ssq * (1.0 / dh) + EPS) * hg_ref[:, s * sw:(s + 1) * sw])
            y = parts[0] if len(parts) == 1 else jnp.concatenate(parts, axis=-1)
        emit(r, y)

    if n_norm_tiles == 0:
        for r in range(row_split):
            tile(r, False)
    else:
        @pl.when(j < n_norm_tiles)
        def _():
            for r in range(row_split):
                tile(r, True)

        @pl.when(j >= n_norm_tiles)
        def _():
            for r in range(row_split):
                tile(r, False)


def _head_sum_matrix(tn, dh):
    idx = np.arange(tn) // dh
    return jnp.asarray((idx[:, None] == idx[None, :]).astype(np.float32), dtype=BF16)


def _norm_matmul(x, g, w, *, tn=512, head_gain=None, n_norm_cols=0, dh=1, slabs=False, name):
    m, d = x.shape
    n = w.shape[1]
    tm = min(m, 1024)
    assert m % tm == 0 and n % tn == 0 and n_norm_cols % tn == 0
    if head_gain is None:
        head_gain = jnp.ones((1, n), F32)
    hs = _head_sum_matrix(min(tn, 512), dh) if n_norm_cols else jnp.zeros((V7X_SUBLANES, V7X_LANES), BF16)
    row_split = 2 if tm % 512 == 0 else 1
    kern = functools.partial(_norm_matmul_kernel, n_norm_tiles=n_norm_cols // tn, dh=dh, slabs=slabs,
                             row_split=row_split)
    if slabs:
        out_shape = jax.ShapeDtypeStruct((n // V7X_LANES, m, V7X_LANES), F32)
        out_spec = pl.BlockSpec((tn // V7X_LANES, tm, V7X_LANES), lambda i, j: (j, i, 0))
    else:
        out_shape = jax.ShapeDtypeStruct((m, n), F32)
        out_spec = pl.BlockSpec((tm, tn), lambda i, j: (i, j))
    return pl.pallas_call(
        kern,
        out_shape=out_shape,
        grid=(m // tm, n // tn),
        in_specs=[
            pl.BlockSpec((tm, d), lambda i, j: (i, 0)),
            pl.BlockSpec((1, d), lambda i, j: (0, 0)),
            pl.BlockSpec((d, tn), lambda i, j: (0, j)),
            pl.BlockSpec((1, tn), lambda i, j: (0, j)),
            _const_spec(hs.shape),
        ],
        out_specs=out_spec,
        scratch_shapes=[pltpu.VMEM((tm, d), BF16)],
        compiler_params=_cparams(("parallel", "arbitrary")),
        name=name,
    )(x, g.reshape(1, d), w, head_gain, hs)


def _s5_prep_kernel(lre_ref, lim_ref, ldt_ref, bre_ref, bim_ref, bbr_ref, bbi_ref, apow_ref):
    lr = lre_ref[...]
    li = lim_ref[...]
    dt = jnp.exp(ldt_ref[...])
    mag = jnp.exp(lr * dt)
    ab_r = mag * jnp.cos(li * dt)
    ab_i = mag * jnp.sin(li * dt)
    den = lr * lr + li * li
    nr = ab_r - 1.0
    cr = (nr * lr + ab_i * li) / den
    ci = (ab_i * lr - nr * li) / den
    n_slab, _, sw = bre_ref.shape
    for c in range(n_slab):
        cr_c = cr[:, c * sw:(c + 1) * sw]
        ci_c = ci[:, c * sw:(c + 1) * sw]
        b_r = bre_ref[c]
        b_i = bim_ref[c]
        bbr_ref[c] = (cr_c * b_r - ci_c * b_i).astype(BF16)
        bbi_ref[c] = (cr_c * b_i + ci_c * b_r).astype(BF16)

    n = lr.shape[1]
    row = lax.broadcasted_iota(I32, (V7X_SUBLANES, n), 0)

    def power(kf):
        mg = jnp.exp(kf * (lr * dt))
        return mg * jnp.cos(kf * (li * dt)), mg * jnp.sin(kf * (li * dt))

    for i, s in enumerate((1, 2, 4)):
        pr, pi = power(jnp.full((V7X_SUBLANES, n), s, F32))
        keep = row >= s
        apow_ref[2 * i] = jnp.where(keep, pr, 0.0)
        apow_ref[2 * i + 1] = jnp.where(keep, pi, 0.0)
    pr, pi = power((row + 1).astype(F32))
    apow_ref[6] = pr
    apow_ref[7] = pi


def _s5_prep(lam_re, lam_im, log_dt, b_re, b_im):
    g, p, h = b_re.shape
    n = g * p
    n_slab = g * h // V7X_LANES
    gl = g // n_slab
    eye = jnp.eye(gl, dtype=F32)

    def slab_blocks(b):
        return jnp.einsum("cgph,gk->ckhgp", b.reshape(n_slab, gl, p, h), eye).reshape(n_slab, V7X_LANES, gl * p)

    bre_bd = slab_blocks(b_re)
    bim_bd = slab_blocks(b_im)
    ldt = jnp.broadcast_to(log_dt[:, None], (g, p)).reshape(1, n)
    return pl.pallas_call(
        _s5_prep_kernel,
        out_shape=(
            jax.ShapeDtypeStruct(bre_bd.shape, BF16),
            jax.ShapeDtypeStruct(bre_bd.shape, BF16),
            jax.ShapeDtypeStruct((8, V7X_SUBLANES, n), F32),
        ),
        compiler_params=pltpu.CompilerParams(vmem_limit_bytes=VMEM_LIMIT),
        name="s5_prep",
    )(lam_re.reshape(1, n), lam_im.reshape(1, n), ldt, bre_bd, bim_bd)


def _mixer0_kernel(z_ref, x_ref, convi_ref, h0_ref, sr0_ref, si0_ref,
                   cw_ref, cb_ref, wa_ref, ba_ref, wx_ref, bx_ref, lam_ref,
                   bbr_ref, bbi_ref, apow_ref, ccr_ref, cci_ref, d_ref, wglu_ref, bglu_ref, wout_ref,
                   y_ref, hl_ref, srl_ref, sil_ref,
                   ext_ref, hc_ref, src_ref, sic_ref, xr_ref, xi_ref, ha_ref, hb_ref,
                   *, tc, last_row):
    c = pl.program_id(1)
    d_lru = cw_ref.shape[1]
    d_s5 = d_ref.shape[1]
    n_tiles = tc // V7X_SUBLANES

    @pl.when(c == 0)
    def _():
        ext_ref[...] = convi_ref[...]
        hc_ref[...] = h0_ref[...]
        src_ref[...] = sr0_ref[...]
        sic_ref[...] = si0_ref[...]

    z = z_ref[...]
    xa = z[:, :d_lru]
    ga = z[:, d_lru:2 * d_lru]
    u = z[:, 2 * d_lru:]

    ext = jnp.concatenate([ext_ref[...], xa], axis=0)
    xc = cb_ref[...] + xa * cw_ref[CONV_W - 1:CONV_W, :]
    for s in range(1, CONV_W):
        xc = xc + pltpu.roll(ext, s, 0)[V7X_SUBLANES:, :] * cw_ref[CONV_W - 1 - s:CONV_W - s, :]
    ext_ref[...] = xa[tc - V7X_SUBLANES:, :]

    xcb = xc.astype(BF16)
    r = jax.nn.sigmoid(jnp.dot(xcb, wa_ref[...], preferred_element_type=F32) + ba_ref[...])
    ig = jax.nn.sigmoid(jnp.dot(xcb, wx_ref[...], preferred_element_type=F32) + bx_ref[...])
    lam = lam_ref[...]
    softplus_neg = jnp.maximum(-lam, 0.0) + jnp.log1p(jnp.exp(-jnp.abs(lam)))
    log_a = -LRU_C * r * softplus_neg
    a = jnp.exp(log_a)
    bt = jnp.sqrt(-jnp.tanh(log_a) * (a * a + 1.0)) * ig * xc

    a3 = a.reshape(n_tiles, V7X_SUBLANES, d_lru)
    b3 = bt.reshape(n_tiles, V7X_SUBLANES, d_lru)
    row = lax.broadcasted_iota(I32, (1, V7X_SUBLANES, d_lru), 1)
    for s in (1, 2, 4):
        keep = row >= s
        ar = pltpu.roll(a3, s, 1)
        br = pltpu.roll(b3, s, 1)
        b3 = jnp.where(keep, a3 * br + b3, b3)
        a3 = jnp.where(keep, a3 * ar, a3)
    ha_ref[...] = a3.reshape(tc, d_lru)
    hb_ref[...] = b3.reshape(tc, d_lru)

    def lru_tile(i, carry):
        r0 = pl.multiple_of(i * V7X_SUBLANES, V7X_SUBLANES)
        h = ha_ref[pl.ds(r0, V7X_SUBLANES), :] * carry + hb_ref[pl.ds(r0, V7X_SUBLANES), :]
        hb_ref[pl.ds(r0, V7X_SUBLANES), :] = h
        return h[V7X_SUBLANES - 1:, :]

    hc_ref[...] = lax.fori_loop(0, n_tiles, lru_tile, hc_ref[...])
    hs = hb_ref[...]
    hl_ref[...] = hb_ref[last_row:last_row + 1, :]
    ya = hs * jax.nn.gelu(ga)

    ub = u.astype(BF16)
    n_slab, _, sw = bbr_ref.shape
    n_state = n_slab * sw

    def input_dot(w_ref):
        return jnp.concatenate(
            [jnp.dot(ub[:, c * V7X_LANES:(c + 1) * V7X_LANES], w_ref[c], preferred_element_type=F32)
             for c in range(n_slab)], axis=-1)

    xr3 = input_dot(bbr_ref).reshape(n_tiles, V7X_SUBLANES, n_state)
    xi3 = input_dot(bbi_ref).reshape(n_tiles, V7X_SUBLANES, n_state)
    for i, s in enumerate((1, 2, 4)):
        cr = apow_ref[2 * i][None]
        ci = apow_ref[2 * i + 1][None]
        rr = pltpu.roll(xr3, s, 1)
        ri = pltpu.roll(xi3, s, 1)
        xr3, xi3 = xr3 + cr * rr - ci * ri, xi3 + cr * ri + ci * rr
    xr_ref[...] = xr3.reshape(tc, n_state)
    xi_ref[...] = xi3.reshape(tc, n_state)

    def s5_tile(i, carry):
        cr_, ci_ = carry
        r0 = pl.multiple_of(i * V7X_SUBLANES, V7X_SUBLANES)
        pr = apow_ref[6]
        pi = apow_ref[7]
        nr_ = xr_ref[pl.ds(r0, V7X_SUBLANES), :] + pr * cr_ - pi * ci_
        ni_ = xi_ref[pl.ds(r0, V7X_SUBLANES), :] + pr * ci_ + pi * cr_
        xr_ref[pl.ds(r0, V7X_SUBLANES), :] = nr_
        xi_ref[pl.ds(r0, V7X_SUBLANES), :] = ni_
        return nr_[V7X_SUBLANES - 1:, :], ni_[V7X_SUBLANES - 1:, :]

    cr_f, ci_f = lax.fori_loop(0, n_tiles, s5_tile, (src_ref[...], sic_ref[...]))
    src_ref[...] = cr_f
    sic_ref[...] = ci_f
    srl_ref[...] = xr_ref[last_row:last_row + 1, :]
    sil_ref[...] = xi_ref[last_row:last_row + 1, :]

    def output_dot(x_ref_, w_ref):
        return jnp.concatenate(
            [jnp.dot(x_ref_[:, c * sw:(c + 1) * sw].astype(BF16), w_ref[c], preferred_element_type=F32)
             for c in range(n_slab)], axis=-1)

    ys = output_dot(xr_ref, ccr_ref) - output_dot(xi_ref, cci_ref)
    ys = ys + d_ref[...] * u
    gs = jax.nn.gelu(ys)
    yb = gs * jax.nn.sigmoid(jnp.dot(gs.astype(BF16), wglu_ref[...], preferred_element_type=F32)
                             + bglu_ref[...])

    y_ref[...] = (x_ref[...]
                  + jnp.dot(ya.astype(BF16), wout_ref[:d_lru, :], preferred_element_type=F32)
                  + jnp.dot(yb.astype(BF16), wout_ref[d_lru:, :], preferred_element_type=F32))


def _mixer0(z, x, conv_init, h0, sr0, si0, wts, *, tc, last_row):
    bn, t_len, d_in = z.shape
    d = x.shape[2]
    d_lru = wts["cw"].shape[1]
    n_state = wts["apow"].shape[-1]
    assert t_len % tc == 0
    kern = functools.partial(_mixer0_kernel, tc=tc, last_row=last_row)
    wnames = ("cw", "cb", "wa", "ba", "wx", "bx", "lam", "bbr", "bbi", "apow",
              "ccr", "cci", "d", "wglu", "bglu", "wout")
    wlist = [wts[k] for k in wnames]
    per_b = lambda shape: pl.BlockSpec((None,) + shape, lambda b, c: (b,) + (0,) * len(shape))
    in_specs = [
        pl.BlockSpec((None, tc, d_in), lambda b, c: (b, c, 0)),
        pl.BlockSpec((None, tc, d), lambda b, c: (b, c, 0)),
        per_b((V7X_SUBLANES, d_lru)), per_b((1, d_lru)), per_b((1, n_state)), per_b((1, n_state)),
    ] + [_const_spec(w.shape) for w in wlist]
    return pl.pallas_call(
        kern,
        out_shape=(
            jax.ShapeDtypeStruct((bn, t_len, d), F32),
            jax.ShapeDtypeStruct((bn, 1, d_lru), F32),
            jax.ShapeDtypeStruct((bn, 1, n_state), F32),
            jax.ShapeDtypeStruct((bn, 1, n_state), F32),
        ),
        grid=(bn, t_len // tc),
        in_specs=in_specs,
        out_specs=(
            pl.BlockSpec((None, tc, d), lambda b, c: (b, c, 0)),
            per_b((1, d_lru)), per_b((1, n_state)), per_b((1, n_state)),
        ),
        scratch_shapes=[
            pltpu.VMEM((V7X_SUBLANES, d_lru), F32),
            pltpu.VMEM((1, d_lru), F32),
            pltpu.VMEM((1, n_state), F32),
            pltpu.VMEM((1, n_state), F32),
            pltpu.VMEM((tc, n_state), F32),
            pltpu.VMEM((tc, n_state), F32),
            pltpu.VMEM((tc, d_lru), F32),
            pltpu.VMEM((tc, d_lru), F32),
        ],
        compiler_params=_cparams(("parallel", "arbitrary")),
        name="mixer0",
    )(z, x, conv_init, h0, sr0, si0, *wlist)


def _xattn_kernel(x_ref, g_ref, wq_ref, qg_ref, mk_ref, mv_ref, wo_ref, o_ref):
    nb, tm, d = x_ref.shape
    x = x_ref[...].reshape(nb * tm, d)
    q = jnp.dot(_rms(x, g_ref[...]).astype(BF16), wq_ref[...], preferred_element_type=F32)
    dh = qg_ref.shape[1]
    head_major = len(mk_ref.shape) == 4
    rows = []
    for b in range(nb):
        if head_major:
            kb = pltpu.einshape("nhd->hnd", mk_ref[b])
            vb = pltpu.einshape("nhd->hnd", mv_ref[b])
            head_of = lambda arr, h: arr[h]
        else:
            kb, vb = mk_ref[b], mv_ref[b]
            head_of = lambda arr, h: arr[:, h * dh:(h + 1) * dh]
        outs = []
        for h in range(XA_HEADS):
            qn = _rms(q[b * tm:(b + 1) * tm, h * dh:(h + 1) * dh], qg_ref[...]).astype(BF16)
            s = lax.dot_general(qn, head_of(kb, h).astype(BF16), (((1,), (1,)), ((), ())),
                                preferred_element_type=F32) * (dh ** -0.5)
            m = jnp.max(s, axis=-1, keepdims=True)
            p = jnp.exp(s - m)
            den = jnp.sum(p, axis=-1, keepdims=True)
            oh = jnp.dot(p.astype(BF16), head_of(vb, h).astype(BF16), preferred_element_type=F32) / den
            outs.append(oh.astype(BF16))
        rows.append(jnp.concatenate(outs, axis=-1))
    o = rows[0] if nb == 1 else jnp.concatenate(rows, axis=0)
    o_ref[...] = (x + jnp.dot(o, wo_ref[...], preferred_element_type=F32)).reshape(nb, tm, d)


def _xattn(x, g, wq, qg, mk, mv, wo, *, tm, nb=1, name, layer=None):
    bn, t_len, d = x.shape
    assert t_len % tm == 0 and bn % nb == 0
    if layer is None:
        k_spec = pl.BlockSpec((nb, mk.shape[1], d), lambda b, i: (b, 0, 0))
        v_spec = pl.BlockSpec((nb, mv.shape[1], d), lambda b, i: (b, 0, 1))
    else:
        k_spec = v_spec = pl.BlockSpec((None, nb) + mk.shape[2:], lambda b, i: (layer, b, 0, 0, 0))
    return pl.pallas_call(
        _xattn_kernel,
        out_shape=jax.ShapeDtypeStruct(x.shape, F32),
        grid=(bn // nb, t_len // tm),
        in_specs=[
            pl.BlockSpec((nb, tm, d), lambda b, i: (b, i, 0)),
            _const_spec((1, d)),
            _const_spec(wq.shape),
            _const_spec((1, qg.shape[-1])),
            k_spec,
            v_spec,
            _const_spec(wo.shape),
        ],
        out_specs=pl.BlockSpec((nb, tm, d), lambda b, i: (b, i, 0)),
        compiler_params=_cparams(("parallel", "arbitrary")),
        name=name,
    )(x, g.reshape(1, d), wq, qg.reshape(1, -1), mk, mv, wo)


def _ffn_kernel(x_ref, g_ref, wg_ref, wu_ref, wd_ref, o_ref, hn_ref, acc_ref):
    j = pl.program_id(1)

    @pl.when(j == 0)
    def _():
        hn_ref[...] = _rms(x_ref[...], g_ref[...]).astype(BF16)
        acc_ref[...] = jnp.zeros_like(acc_ref)

    hn = hn_ref[...]
    gate = jnp.dot(hn, wg_ref[...], preferred_element_type=F32)
    up = jnp.dot(hn, wu_ref[...], preferred_element_type=F32)
    act = (jax.nn.silu(gate) * up).astype(BF16)
    acc_ref[...] += jnp.dot(act, wd_ref[...], preferred_element_type=F32)

    @pl.when(j == pl.num_programs(1) - 1)
    def _():
        o_ref[...] = x_ref[...] + acc_ref[...]


def _ffn(x, g, w_gu, w_d, *, name):
    m, d = x.shape
    hid = w_d.shape[0]
    th = FFN_TH
    tm = min(m, 1024)
    nj = hid // th
    assert m % tm == 0 and hid % th == 0
    return pl.pallas_call(
        _ffn_kernel,
        out_shape=jax.ShapeDtypeStruct((m, d), F32),
        grid=(m // tm, nj),
        in_specs=[
            pl.BlockSpec((tm, d), lambda i, j: (i, 0)),
            pl.BlockSpec((1, d), lambda i, j: (0, 0)),
            pl.BlockSpec((d, th), lambda i, j: (0, j)),
            pl.BlockSpec((d, th), lambda i, j: (0, j + nj)),
            pl.BlockSpec((th, d), lambda i, j: (j, 0)),
        ],
        out_specs=pl.BlockSpec((tm, d), lambda i, j: (i, 0)),
        scratch_shapes=[pltpu.VMEM((tm, d), BF16), pltpu.VMEM((tm, d), F32)],
        compiler_params=_cparams(("parallel", "arbitrary")),
        name=name,
    )(x, g.reshape(1, d), w_gu, w_gu, w_d)


def _rel_bucket_np(dist):
    dist = np.clip(np.asarray(dist), 0, None)
    max_exact = REL_BUCKETS // 2
    safe = np.maximum(dist, max_exact).astype(np.float32)
    large = max_exact + np.floor(np.log(safe / max_exact) / math.log(REL_MAX_DIST / max_exact)
                                 * (REL_BUCKETS - max_exact)).astype(np.int32)
    large = np.minimum(large, REL_BUCKETS - 1)
    return np.where(dist < max_exact, dist, large).astype(np.int32)


def _dil_kernel(q_ref, k_ref, v_ref, bias_ref, o_ref, l_ref, kprev_ref, vprev_ref, *, d, nblk):
    n = pl.program_id(1)

    @pl.when(n == 0)
    def _():
        kprev_ref[...] = jnp.zeros_like(kprev_ref)
        vprev_ref[...] = jnp.zeros_like(vprev_ref)

    first_sel = jnp.minimum(n, 1)
    lane = lax.broadcasted_iota(I32, (QB, V7X_LANES), 1)
    n_slab = q_ref.shape[0]
    heads_per_slab = V7X_LANES // DH
    kv_head = lax.broadcasted_iota(I32, (2 * QB, SLABS_PER_DOT * V7X_LANES), 1) // DH
    q_head = lax.broadcasted_iota(I32, (QB, SLABS_PER_DOT * V7X_LANES), 1) // DH

    def residue(r, _):
        prev_k = [kprev_ref[r, c] for c in range(n_slab)]
        prev_v = [vprev_ref[r, c] for c in range(n_slab)]
        for blk in range(nblk):
            rows = pl.ds(blk * QB * d + r, QB, stride=d)
            bsel = first_sel if blk == 0 else 1
            lse = jnp.zeros((QB, V7X_LANES), F32)
            for c0 in range(0, n_slab, SLABS_PER_DOT):
                slabs = range(c0, c0 + SLABS_PER_DOT)
                wide = lambda parts: jnp.concatenate(parts, axis=-1)
                q = wide([(q_ref[c, rows, :] * (DH ** -0.5)).astype(BF16) for c in slabs])
                kcs = [k_ref[c, rows, :].astype(BF16) for c in slabs]
                vcs = [v_ref[c, rows, :].astype(BF16) for c in slabs]
                k = jnp.concatenate([wide([prev_k[c] for c in slabs]), wide(kcs)], axis=0)
                v = jnp.concatenate([wide([prev_v[c] for c in slabs]), wide(vcs)], axis=0)
                n_heads = SLABS_PER_DOT * heads_per_slab
                k_heads = jnp.concatenate([jnp.where(kv_head == hh, k, 0) for hh in range(n_heads)], axis=0)
                v_heads = jnp.concatenate([jnp.where(kv_head == hh, v, 0) for hh in range(n_heads)], axis=0)
                s_all = lax.dot_general(q, k_heads, (((1,), (1,)), ((), ())), preferred_element_type=F32)
                probs = []
                den_lanes = jnp.zeros((QB, SLABS_PER_DOT * V7X_LANES), F32)
                for hh in range(n_heads):
                    h = c0 * heads_per_slab + hh
                    s = s_all[:, hh * 2 * QB:(hh + 1) * 2 * QB] + bias_ref[bsel, h]
                    m = jnp.max(s, axis=-1, keepdims=True)
                    p = jnp.exp(s - m)
                    den = jnp.sum(p, axis=-1, keepdims=True)
                    probs.append(p.astype(BF16))
                    den_lanes = jnp.where(q_head == hh, den, den_lanes)
                    lse = jnp.where(lane == h, m + jnp.log(den), lse)
                o = jnp.dot(wide(probs), v_heads, preferred_element_type=F32) / den_lanes
                for i, c in enumerate(slabs):
                    o_ref[c, rows, :] = o[:, i * V7X_LANES:(i + 1) * V7X_LANES]
                    prev_k[c] = kcs[i]
                    prev_v[c] = vcs[i]
            l_ref[rows, :] = lse
        for c in range(n_slab):
            kprev_ref[r, c] = prev_k[c]
            vprev_ref[r, c] = prev_v[c]
        return 0

    lax.fori_loop(0, d, residue, 0, unroll=min(d, 2))


def _dilated_prompt(qkv, tab, g, d, bn):
    n_slabs, m_rows, _ = qkv.shape
    s_len = m_rows // bn
    gw = HG * DH
    gs = gw // V7X_LANES
    nblk = 2 if d == 1 else 1
    chunk = nblk * QB * d
    n_chunks = s_len // chunk
    assert s_len % chunk == 0
    k_col = n_slabs // gs // 3

    qi = np.arange(QB)[:, None]
    ki = np.arange(2 * QB)[None, :]
    dist = qi + QB - ki
    band = (dist >= 0) & (dist <= N_STRIDE)
    first = band & (ki >= QB)
    onehot = (np.arange(REL_BUCKETS)[:, None] == _rel_bucket_np(dist * d).reshape(1, -1)).astype(np.float32)
    bias = jnp.dot(tab.T.astype(F32), jnp.asarray(onehot),
                   precision=lax.Precision.HIGHEST).reshape(HG, QB, 2 * QB)
    bias2 = jnp.stack([jnp.where(first[None], bias, NEG), jnp.where(band[None], bias, NEG)])

    col = lambda section: (lambda b, n: (section * k_col + g, b * n_chunks + n, 0))
    blk = (gs, chunk, V7X_LANES)
    return pl.pallas_call(
        functools.partial(_dil_kernel, d=d, nblk=nblk),
        out_shape=(jax.ShapeDtypeStruct((gs, m_rows, V7X_LANES), F32),
                   jax.ShapeDtypeStruct((m_rows, V7X_LANES), F32)),
        grid=(bn, n_chunks),
        in_specs=[
            pl.BlockSpec(blk, col(0)), pl.BlockSpec(blk, col(1)), pl.BlockSpec(blk, col(2)),
            _const_spec(bias2.shape),
        ],
        out_specs=(pl.BlockSpec(blk, lambda b, n: (0, b * n_chunks + n, 0)),
                   pl.BlockSpec((chunk, V7X_LANES), lambda b, n: (b * n_chunks + n, 0))),
        scratch_shapes=[pltpu.VMEM((d, gs, QB, V7X_LANES), BF16), pltpu.VMEM((d, gs, QB, V7X_LANES), BF16)],
        compiler_params=_cparams(("parallel", "arbitrary")),
        name=f"dilated_prompt_g{g}",
    )(qkv, qkv, qkv, bias2)


def _split_dot(x, w_ref):
    hi = x.astype(BF16)
    lo = (x - hi.astype(F32)).astype(BF16)
    w = w_ref[...]
    return jnp.dot(hi, w, preferred_element_type=F32) + jnp.dot(lo, w, preferred_element_type=F32)


def _dil_sample_kernel(q_ref, k_ref, v_ref, cache_ref, bias_ref, biasn_ref, o_ref, l_ref, cout_ref, *,
                       t_valid, t_pad):
    n_slab = q_ref.shape[0]
    nb = cache_ref.shape[0]
    w = cache_ref.shape[-1]
    lane = lax.broadcasted_iota(I32, (t_pad, V7X_LANES), 1)
    is_new = lax.broadcasted_iota(I32, (DH, V7X_LANES), 1) >= V7X_LANES - t_valid

    def as_last_columns(x):
        shifted = pltpu.roll(x, t_pad - t_valid, 0)
        tile = jnp.concatenate([jnp.zeros((V7X_LANES - t_pad, x.shape[1]), F32), shifted], axis=0)
        return tile.T

    def sequence(b, _):
        rows = pl.ds(pl.multiple_of(b * t_pad, t_pad), t_pad)
        unslab = lambda ref: jnp.concatenate([ref[c, rows, :] for c in range(n_slab)], axis=-1)
        qn = unslab(q_ref)
        kn = unslab(k_ref)
        vn = unslab(v_ref)
        new_cols = (as_last_columns(kn), as_last_columns(vn))
        qb, kb, vb = qn.astype(BF16), kn.astype(BF16), vn.astype(BF16)
        lse = jnp.zeros((t_pad, V7X_LANES), F32)
        outs = []
        for h in range(HG):
            sl = slice(h * DH, (h + 1) * DH)
            k_t = cache_ref[b, 0, h]
            v_t = cache_ref[b, 1, h]
            s = jnp.dot(qb[:, sl], k_t.astype(BF16), preferred_element_type=F32) * (DH ** -0.5) + bias_ref[h]
            sn = lax.dot_general(qb[:, sl], kb[:, sl], (((1,), (1,)), ((), ())),
                                 preferred_element_type=F32) * (DH ** -0.5) + biasn_ref[h]
            m = jnp.maximum(jnp.max(s, axis=-1, keepdims=True), jnp.max(sn, axis=-1, keepdims=True))
            p = jnp.exp(s - m)
            pn = jnp.exp(sn - m)
            den = jnp.sum(p, axis=-1, keepdims=True) + jnp.sum(pn, axis=-1, keepdims=True)
            pv = lax.dot_general(p.astype(BF16), v_t.astype(BF16), (((1,), (1,)), ((), ())),
                                 preferred_element_type=F32)
            outs.append((pv + jnp.dot(pn.astype(BF16), vb[:, sl], preferred_element_type=F32)) / den)
            lse = jnp.where(lane == h, m + jnp.log(den), lse)
            for kv, old in enumerate((k_t, v_t)):
                moved = pltpu.roll(old, w - t_valid, 1)
                cout_ref[b, kv, h] = moved
                cout_ref[b, kv, h, :, w - V7X_LANES:] = jnp.where(is_new, new_cols[kv][sl, :],
                                                                  moved[:, w - V7X_LANES:])
        o = jnp.concatenate(outs, axis=-1)
        for c in range(n_slab):
            o_ref[c, rows, :] = o[:, c * V7X_LANES:(c + 1) * V7X_LANES]
        l_ref[rows, :] = lse
        return 0

    if nb == 1:
        sequence(0, 0)
    else:
        lax.fori_loop(0, nb, sequence, 0)


def _dilated_sample(qkv, cache, tab, g, d, t_valid):
    n_slabs, m_rows, _ = qkv.shape
    db, wb = cache.shape[0], cache.shape[-1]
    t_pad = m_rows // db
    gw = HG * DH
    gs = gw // V7X_LANES
    assert wb == N_STRIDE * d and t_valid <= V7X_LANES and wb % V7X_LANES == 0
    k_col = n_slabs // gs // 3

    def bias_of(j, ok):
        onehot = (np.arange(REL_BUCKETS)[:, None] == _rel_bucket_np(d * j).reshape(1, -1)) & ok.reshape(1, -1)
        looked_up = jnp.dot(tab.T.astype(F32), jnp.asarray(onehot.astype(np.float32)),
                            precision=lax.Precision.HIGHEST).reshape((HG,) + j.shape)
        return jnp.where(ok[None], looked_up, NEG)

    tq = np.arange(t_pad)[:, None]
    pos = np.arange(wb)[None, :]
    jc = (wb + tq - pos) // d
    bias = bias_of(jc, ((wb + tq - pos) % d == 0) & (jc >= 1) & (jc <= N_STRIDE))
    tn = np.arange(t_pad)[None, :]
    jn = (tq - tn) // d
    bias_new = bias_of(jn, (tn <= tq) & ((tq - tn) % d == 0) & (jn <= N_STRIDE) & (tn < t_valid))

    kern = functools.partial(_dil_sample_kernel, t_valid=t_valid, t_pad=t_pad)
    cache_bytes = 4 * math.prod(cache.shape[1:])
    nb = max(1, min(db, (4 * 1024 * 1024) // cache_bytes))
    assert db % nb == 0
    blk = (gs, nb * t_pad, V7X_LANES)
    cblk = pl.BlockSpec((nb,) + cache.shape[1:], lambda b: (b, 0, 0, 0, 0))
    return pl.pallas_call(
        kern,
        out_shape=(jax.ShapeDtypeStruct((gs, m_rows, V7X_LANES), F32),
                   jax.ShapeDtypeStruct((m_rows, V7X_LANES), F32),
                   jax.ShapeDtypeStruct(cache.shape, F32)),
        grid=(db // nb,),
        in_specs=[
            pl.BlockSpec(blk, lambda b: (g, b, 0)),
            pl.BlockSpec(blk, lambda b: (k_col + g, b, 0)),
            pl.BlockSpec(blk, lambda b: (2 * k_col + g, b, 0)),
            cblk, _const_spec(bias.shape), _const_spec(bias_new.shape),
        ],
        out_specs=(pl.BlockSpec(blk, lambda b: (0, b, 0)),
                   pl.BlockSpec((nb * t_pad, V7X_LANES), lambda b: (b, 0)),
                   cblk),
        compiler_params=_cparams(("parallel",)),
        name=f"dilated_sample_g{g}",
    )(qkv, qkv, qkv, cache, bias, bias_new)


def _merge_wo_kernel(o0_ref, o1_ref, o2_ref, l0_ref, l1_ref, l2_ref, x_ref, hexp_ref, w_ref, out_ref):
    l0, l1, l2 = l0_ref[...], l1_ref[...], l2_ref[...]
    m = jnp.maximum(jnp.maximum(l0, l1), l2)
    e0, e1, e2 = jnp.exp(l0 - m), jnp.exp(l1 - m), jnp.exp(l2 - m)
    inv = 1.0 / (e0 + e1 + e2)
    unslab = lambda ref: jnp.concatenate([ref[c] for c in range(ref.shape[0])], axis=-1)
    o = (_split_dot(e0 * inv, hexp_ref) * unslab(o0_ref) + _split_dot(e1 * inv, hexp_ref) * unslab(o1_ref)
         + _split_dot(e2 * inv, hexp_ref) * unslab(o2_ref))
    out_ref[...] = x_ref[...] + jnp.dot(o.astype(BF16), w_ref[...], preferred_element_type=F32)


def _merge_wo(outs, lses, x, w, *, name):
    m, d = x.shape
    gw = w.shape[0]
    tm = min(m, 512)
    assert m % tm == 0
    head = np.arange(gw) // DH
    hexp = jnp.asarray((np.arange(V7X_LANES)[:, None] == head[None, :]).astype(np.float32), dtype=BF16)
    row = lambda width: pl.BlockSpec((tm, width), lambda i: (i, 0))
    slab = pl.BlockSpec((gw // V7X_LANES, tm, V7X_LANES), lambda i: (0, i, 0))
    return pl.pallas_call(
        _merge_wo_kernel,
        out_shape=jax.ShapeDtypeStruct((m, d), F32),
        grid=(m // tm,),
        in_specs=[slab] * 3 + [row(V7X_LANES)] * 3 + [row(d), _const_spec(hexp.shape), _const_spec(w.shape)],
        out_specs=row(d),
        compiler_params=_cparams(("parallel",)),
        name=name,
    )(*outs, *lses, x, hexp, w)


def _router_kernel(x_ref, g_ref, wr_ref, br_ref, cnt0_ref, tri_ref,
                   hn_ref, gate_ref, lp_ref, tbase_ref, tcnt_ref, cnt_ref, run_ref):
    @pl.when(pl.program_id(0) == 0)
    def _():
        run_ref[...] = cnt0_ref[...]

    hn = _rms(x_ref[...], g_ref[...])
    hn_ref[...] = hn
    wr = wr_ref[...]
    h_hi = hn.astype(BF16)
    h_lo = (hn - h_hi.astype(F32)).astype(BF16)
    w_hi = wr.astype(BF16)
    w_lo = (wr - w_hi.astype(F32)).astype(BF16)
    nt_dot = lambda a, b: lax.dot_general(a, b, (((1,), (1,)), ((), ())), preferred_element_type=F32)
    logits = nt_dot(w_hi, h_hi) + nt_dot(w_lo, h_hi) + nt_dot(w_hi, h_lo) + br_ref[...]
    ne = logits.shape[0]
    eidx = lax.broadcasted_iota(I32, logits.shape, 0)
    m1 = jnp.max(logits, axis=0, keepdims=True)
    i1 = jnp.min(jnp.where(logits == m1, eidx, ne), axis=0, keepdims=True)
    rest = jnp.where(eidx == i1, -jnp.inf, logits)
    m2 = jnp.max(rest, axis=0, keepdims=True)
    i2 = jnp.min(jnp.where(rest == m2, eidx, ne), axis=0, keepdims=True)
    e2 = jnp.exp(m2 - m1)
    g1 = 1.0 / (1.0 + e2)
    g2 = e2 / (1.0 + e2)
    gate_ref[...] = jnp.where(eidx == 0, g1, jnp.where(eidx == 1, g2, 0.0))
    chosen = jnp.where(jnp.logical_or(eidx == i1, eidx == i2), 1.0, 0.0)
    before = jnp.dot(chosen.astype(BF16), tri_ref[...], preferred_element_type=F32)
    tile_cnt = jnp.ceil(jnp.sum(chosen, axis=1, keepdims=True) * (1.0 / V7X_SUBLANES)) * V7X_SUBLANES
    ecol = eidx[:, :1]
    lower = jnp.zeros_like(tile_cnt)
    for e in range(ne - 1):
        lower = lower + jnp.where(ecol > e, tile_cnt[e:e + 1, :], 0.0)
    local = before + lower
    l1 = jnp.sum(jnp.where(eidx == i1, local, 0.0), axis=0, keepdims=True)
    l2 = jnp.sum(jnp.where(eidx == i2, local, 0.0), axis=0, keepdims=True)
    lp_ref[...] = jnp.where(eidx == 0, l1, jnp.where(eidx == 1, l2, 0.0)).astype(I32)
    tbase_ref[0] = run_ref[...]
    tcnt_ref[0] = tile_cnt
    run_ref[...] += tile_cnt
    cnt_ref[...] = run_ref[...]


def _router(x, g, wr, br, cnt0, *, name):
    m, d = x.shape
    ne = wr.shape[1]
    tm = min(m, MOE_TOKEN_TILE)
    assert m % tm == 0
    n_tiles = m // tm
    tri = jnp.asarray(np.triu(np.ones((tm, tm), np.float32), 1), dtype=BF16)
    cols = pl.BlockSpec((ne, tm), lambda i: (0, i))
    per_tile = pl.BlockSpec((1, ne, 1), lambda i: (i, 0, 0))
    return pl.pallas_call(
        _router_kernel,
        out_shape=(jax.ShapeDtypeStruct((m, d), F32),
                   jax.ShapeDtypeStruct((ne, m), F32),
                   jax.ShapeDtypeStruct((ne, m), I32),
                   jax.ShapeDtypeStruct((n_tiles, ne, 1), F32),
                   jax.ShapeDtypeStruct((n_tiles, ne, 1), F32),
                   jax.ShapeDtypeStruct((ne, 1), F32)),
        grid=(n_tiles,),
        in_specs=[pl.BlockSpec((tm, d), lambda i: (i, 0)), _const_spec((1, d)), _const_spec((ne, d)),
                  _const_spec((ne, 1)), _const_spec((ne, 1)), _const_spec((tm, tm))],
        out_specs=(pl.BlockSpec((tm, d), lambda i: (i, 0)), cols, cols, per_tile, per_tile,
                   _const_spec((ne, 1))),
        scratch_shapes=[pltpu.VMEM((ne, 1), F32)],
        compiler_params=_cparams(("arbitrary",)),
        name=name,
    )(x, g.reshape(1, d), wr.T, br.reshape(ne, 1), cnt0, tri)


def _staging_rows(n_tokens):
    return TOP_K * n_tokens + V7X_SUBLANES * N_EXPERTS


def _expert_run_copies(tile, dst_ref, cnt_ref, n_tokens, make_copy, wait):
    local = jnp.int32(0)
    for e in range(N_EXPERTS):
        count = cnt_ref[tile * N_EXPERTS + e]
        sorted_row = dst_ref[tile * N_EXPERTS + e]
        size = pl.next_power_of_2(n_tokens)
        while size >= V7X_SUBLANES:
            has = (count & size) != 0

            @pl.when(has)
            def _(local=local, sorted_row=sorted_row, size=size):
                copy = make_copy(pl.multiple_of(local, V7X_SUBLANES), pl.multiple_of(sorted_row, V7X_SUBLANES), size)
                if wait:
                    copy.wait()
                else:
                    copy.start()

            step = jnp.where(has, size, 0)
            local = local + step
            sorted_row = sorted_row + step
            size //= 2


def _dispatch_kernel(dst_ref, cnt_ref, hn_ref, lp_ref, xs_in_ref, xs_ref, stage_ref, sem, *, tmd):
    del xs_in_ref
    i = pl.program_id(0)
    n = pl.num_programs(0)
    slot = i % 2

    def copies(tile, s, wait):
        _expert_run_copies(tile, dst_ref, cnt_ref, tmd, lambda local, sorted_row, size: pltpu.make_async_copy(
            stage_ref.at[s, pl.ds(local, size)], xs_ref.at[pl.ds(sorted_row, size)], sem.at[s]), wait)

    @pl.when(i >= 2)
    def _():
        copies(i - 2, slot, True)

    lp = lp_ref[...]
    row = lax.broadcasted_iota(I32, (stage_ref.shape[1], tmd), 0)
    place = (jnp.where(row == lp[0:1, :], 1.0, 0.0) + jnp.where(row == lp[1:2, :], 1.0, 0.0)).astype(BF16)
    stage_ref[slot] = jnp.dot(place, hn_ref[...].astype(BF16), preferred_element_type=F32)
    copies(i, slot, False)

    @pl.when(i == n - 1)
    def _():
        copies(i, slot, True)

        @pl.when(n >= 2)
        def _():
            copies(i - 1, 1 - slot, True)


def _dispatch(hn, lp, dst, cnt, xs, *, name):
    m, d = hn.shape
    ne = lp.shape[0]
    tmd = min(m, MOE_TOKEN_TILE)
    assert m % tmd == 0
    kern = functools.partial(_dispatch_kernel, tmd=tmd)
    grid_spec = pltpu.PrefetchScalarGridSpec(
        num_scalar_prefetch=2,
        grid=(m // tmd,),
        in_specs=[
            pl.BlockSpec((tmd, d), lambda i, dst, cnt: (i, 0)),
            pl.BlockSpec((ne, tmd), lambda i, dst, cnt: (0, i)),
            pl.BlockSpec(memory_space=pl.ANY),
        ],
        out_specs=pl.BlockSpec(memory_space=pl.ANY),
        scratch_shapes=[pltpu.VMEM((2, _staging_rows(tmd), d), F32), pltpu.SemaphoreType.DMA((2,))],
    )
    return pl.pallas_call(
        kern,
        out_shape=jax.ShapeDtypeStruct(xs.shape, xs.dtype),
        grid_spec=grid_spec,
        input_output_aliases={4: 0},
        compiler_params=_cparams(("arbitrary",)),
        name=name,
    )(dst, cnt, hn, lp, xs)


def _moe_ffn_kernel(te_ref, nused_ref, x_ref, wg_ref, wu_ref, wd_ref, o_ref, xb_ref, acc_ref):
    t = pl.program_id(0)
    j = pl.program_id(1)
    last = pl.num_programs(1) - 1
    used = t < nused_ref[0]

    @pl.when(used)
    def _():
        @pl.when(j == 0)
        def _():
            xb_ref[...] = x_ref[...].astype(BF16)
            acc_ref[...] = jnp.zeros_like(acc_ref)

        xb = xb_ref[...]
        gate = jnp.dot(xb, wg_ref[...].astype(BF16), preferred_element_type=F32)
        up = jnp.dot(xb, wu_ref[...].astype(BF16), preferred_element_type=F32)
        act = (jax.nn.silu(gate) * up).astype(BF16)
        acc_ref[...] += jnp.dot(act, wd_ref[...].astype(BF16), preferred_element_type=F32)

        @pl.when(j == last)
        def _():
            o_ref[...] = acc_ref[...]

    @pl.when(jnp.logical_and(jnp.logical_not(used), j == last))
    def _():
        o_ref[...] = jnp.zeros_like(o_ref)


def _moe_ffn(xs, te, n_used, w_gu, w_d):
    n_rows, d = xs.shape
    hid = w_d.shape[1]
    tm, th = MOE_TM, MOE_TH
    nj = hid // th
    nt = n_rows // tm
    assert hid % th == 0 and n_rows % tm == 0

    def jj(t, j, nu):
        return jnp.where(t < nu[0], j, nj - 1)

    grid_spec = pltpu.PrefetchScalarGridSpec(
        num_scalar_prefetch=2,
        grid=(nt, nj),
        in_specs=[
            pl.BlockSpec((tm, d), lambda t, j, te, nu: (jnp.maximum(jnp.minimum(t, nu[0] - 1), 0), 0)),
            pl.BlockSpec((None, d, th), lambda t, j, te, nu: (te[t], 0, jj(t, j, nu))),
            pl.BlockSpec((None, d, th), lambda t, j, te, nu: (te[t], 0, jj(t, j, nu) + nj)),
            pl.BlockSpec((None, th, d), lambda t, j, te, nu: (te[t], jj(t, j, nu), 0)),
        ],
        out_specs=pl.BlockSpec((tm, d), lambda t, j, te, nu: (t, 0)),
        scratch_shapes=[pltpu.VMEM((tm, d), BF16), pltpu.VMEM((tm, d), F32)],
    )
    return pl.pallas_call(
        _moe_ffn_kernel,
        out_shape=jax.ShapeDtypeStruct((n_rows, d), F32),
        grid_spec=grid_spec,
        compiler_params=_cparams(("arbitrary", "arbitrary")),
        name="moe_experts",
    )(te, n_used, xs, w_gu, w_gu, w_d)


def _combine_kernel(dst_ref, cnt_ref, x_ref, gate_ref, lp_ref, ys_ref, o_ref, buf_ref, sem, *, tmc):
    i = pl.program_id(0)
    n = pl.num_programs(0)
    slot = i % 2

    def gather(tile, s, wait):
        _expert_run_copies(tile, dst_ref, cnt_ref, tmc, lambda local, sorted_row, size: pltpu.make_async_copy(
            ys_ref.at[pl.ds(sorted_row, size)], buf_ref.at[s, pl.ds(local, size)], sem.at[s]), wait)

    @pl.when(i == 0)
    def _():
        buf_ref[...] = jnp.zeros_like(buf_ref)
        gather(i, slot, False)

    @pl.when(i + 1 < n)
    def _():
        gather(i + 1, 1 - slot, False)

    gather(i, slot, True)
    g = gate_ref[...]
    lp = lp_ref[...]
    row = lax.broadcasted_iota(I32, (buf_ref.shape[1], tmc), 0)
    w = jnp.where(row == lp[0:1, :], g[0:1, :], 0.0) + jnp.where(row == lp[1:2, :], g[1:2, :], 0.0)
    rows = buf_ref[slot]
    w_hi = w.astype(BF16)
    w_lo = (w - w_hi.astype(F32)).astype(BF16)
    r_hi = rows.astype(BF16)
    r_lo = (rows - r_hi.astype(F32)).astype(BF16)
    tn_dot = lambda a, b: lax.dot_general(a, b, (((0,), (0,)), ((), ())), preferred_element_type=F32)
    o_ref[...] = x_ref[...] + tn_dot(w_hi, r_hi) + tn_dot(w_hi, r_lo) + tn_dot(w_lo, r_hi)


def _combine(x, gate, lp, dst, cnt, ys, *, name):
    m, d = x.shape
    ne = gate.shape[0]
    tmc = min(m, MOE_TOKEN_TILE)
    assert m % tmc == 0
    kern = functools.partial(_combine_kernel, tmc=tmc)
    row = lambda width: pl.BlockSpec((tmc, width), lambda i, dst, cnt: (i, 0))
    cols = pl.BlockSpec((ne, tmc), lambda i, dst, cnt: (0, i))
    grid_spec = pltpu.PrefetchScalarGridSpec(
        num_scalar_prefetch=2,
        grid=(m // tmc,),
        in_specs=[row(d), cols, cols, pl.BlockSpec(memory_space=pl.ANY)],
        out_specs=row(d),
        scratch_shapes=[pltpu.VMEM((2, _staging_rows(tmc), d), F32), pltpu.SemaphoreType.DMA((2,))],
    )
    return pl.pallas_call(
        kern,
        out_shape=jax.ShapeDtypeStruct((m, d), F32),
        grid_spec=grid_spec,
        compiler_params=_cparams(("arbitrary",)),
        name=name,
    )(dst, cnt, x, gate, lp, ys)


def _moe_plan(counts, tile_bases, tile_counts, tm, nt):
    padded = ((counts + tm - 1) // tm) * tm
    ends = jnp.cumsum(padded)
    starts = ends - padded
    dst = [(starts[None, :] + b[:, :, 0].astype(I32)).reshape(-1) for b in tile_bases]
    cnt = [c[:, :, 0].astype(I32).reshape(-1) for c in tile_counts]
    n_used = (ends[-1] // tm).astype(I32)
    tile = jnp.arange(nt, dtype=I32)
    first_row = jnp.minimum(tile, n_used - 1) * tm
    te = jnp.minimum(jnp.sum((first_row[:, None] >= ends[None, :]).astype(I32), axis=1), N_EXPERTS - 1)
    return dst, cnt, te, n_used.reshape(1)


def _block_diag(w):
    n, a, b = w.shape
    return jnp.einsum("nij,nm->nimj", w, jnp.eye(n, dtype=w.dtype)).reshape(n * a, n * b)


def kernel(x_prompt, x_sample, state_conv, state_lru, state_s5_re, state_s5_im, cache_swa0_kv, cache_swa1_kv, cache_swa2_kv, cache_mem_k, cache_mem_v, mem_prompt, norm_mix, norm_xa, norm_ffn, norm_mem, w_in_even, conv_w, conv_b, lru_wa, lru_ba, lru_wx, lru_bx, lru_lam, s5_lam_re, s5_lam_im, s5_log_dt, s5_b_re, s5_b_im, s5_c_re, s5_c_im, s5_d, s5_w_glu, s5_b_glu, w_out_even, w_qkv_odd, q_norm_odd, k_norm_odd, w_o_odd, rel_bias, xa_wq, xa_wkv, xa_qn, xa_kn, xa_wo, ffn_w_gu, ffn_w_down, moe_router_w, moe_router_b, moe_w_gu, moe_w_down):
    bp, s_len, d = x_prompt.shape
    db, t_dec, _ = x_sample.shape
    n_mem = mem_prompt.shape[1]
    tp = SAMPLE_PAD_T
    d_lru = conv_w.shape[-1]
    s5_g, s5_p, s5_h = s5_b_re.shape[1:]
    n_state = s5_g * s5_p
    d_s5 = s5_g * s5_h
    xa_dh = d // XA_HEADS
    caches = (cache_swa0_kv, cache_swa1_kv, cache_swa2_kv)
    bf = lambda w: w.astype(BF16)

    yp = x_prompt
    ys = jnp.pad(x_sample, ((0, 0), (0, tp - t_dec), (0, 0)))

    w_in = bf(w_in_even[0])
    bbr, bbi, apow = _s5_prep(s5_lam_re[0], s5_lam_im[0], s5_log_dt[0], s5_b_re[0], s5_b_im[0])
    n_slab = d_s5 // V7X_LANES
    gl = s5_g // n_slab
    eye_g = jnp.eye(gl, dtype=F32)

    def c_blocks(c_mat):
        blocks = jnp.einsum("cghp,gk->cgpkh", c_mat.reshape(n_slab, gl, s5_h, s5_p), eye_g)
        return bf(blocks.reshape(n_slab, gl * s5_p, V7X_LANES))

    mix_w = dict(
        cw=conv_w[0], cb=conv_b[0].reshape(1, d_lru),
        wa=bf(_block_diag(lru_wa[0])), ba=lru_ba[0].reshape(1, d_lru),
        wx=bf(_block_diag(lru_wx[0])), bx=lru_bx[0].reshape(1, d_lru),
        lam=lru_lam[0].reshape(1, d_lru),
        bbr=bbr, bbi=bbi, apow=apow,
        ccr=c_blocks(s5_c_re[0]), cci=c_blocks(s5_c_im[0]),
        d=s5_d[0].reshape(1, d_s5), wglu=bf(s5_w_glu[0]), bglu=s5_b_glu[0].reshape(1, d_s5),
        wout=bf(w_out_even[0]),
    )
    z_p = _norm_matmul(yp.reshape(bp * s_len, d), norm_mix[0], w_in, tn=w_in.shape[1],
                       name="in_proj_p").reshape(bp, s_len, -1)
    yp, p_lru, p_s5r, p_s5i = _mixer0(
        z_p, yp, jnp.zeros((bp, V7X_SUBLANES, d_lru), F32), jnp.zeros((bp, 1, d_lru), F32),
        jnp.zeros((bp, 1, n_state), F32), jnp.zeros((bp, 1, n_state), F32), mix_w,
        tc=MIX_TC, last_row=MIX_TC - 1)
    z_s = _norm_matmul(ys.reshape(db * tp, d), norm_mix[0], w_in, name="in_proj_s").reshape(db, tp, -1)
    conv_init = jnp.pad(state_conv[0], ((0, 0), (V7X_SUBLANES - (CONV_W - 1), 0), (0, 0)))
    ys, s_lru, s_s5r, s_s5i = _mixer0(
        z_s, ys, conv_init, state_lru[0].reshape(db, 1, d_lru),
        state_s5_re[0].reshape(db, 1, n_state), state_s5_im[0].reshape(db, 1, n_state), mix_w,
        tc=tp, last_row=t_dec - 1)
    p_state_conv = z_p[:, s_len - (CONV_W - 1):, :d_lru][None]
    s_state_conv = z_s[:, t_dec - (CONV_W - 1):t_dec, :d_lru][None]

    p_mk, p_mv = [], []

    def cross_attention(layer, yp, ys):
        kn_gain = jnp.concatenate([jnp.tile(xa_kn[layer], XA_HEADS), jnp.ones((d,), F32)]).reshape(1, 2 * d)
        kv = _norm_matmul(mem_prompt.reshape(bp * n_mem, d), norm_mem[layer], bf(xa_wkv[layer]),
                          head_gain=kn_gain, n_norm_cols=d, dh=xa_dh, name=f"mem_kv{layer}")
        kv = kv.reshape(bp, n_mem, 2 * d)
        mk, mv = kv[:, :, :d], kv[:, :, d:]
        p_mk.append(mk.reshape(bp, n_mem, XA_HEADS, xa_dh))
        p_mv.append(mv.reshape(bp, n_mem, XA_HEADS, xa_dh))
        wq, wo = bf(xa_wq[layer]), bf(xa_wo[layer])
        yp = _xattn(yp, norm_xa[layer], wq, xa_qn[layer], kv, kv, wo, tm=512, name=f"xattn_p{layer}")
        ys = _xattn(ys, norm_xa[layer], wq, xa_qn[layer], cache_mem_k, cache_mem_v, wo,
                    tm=tp, nb=4, name=f"xattn_s{layer}", layer=layer)
        return yp, ys

    yp, ys = cross_attention(0, yp, ys)
    w_gu0, w_d0 = bf(ffn_w_gu[0]), bf(ffn_w_down[0])
    yp = _ffn(yp.reshape(bp * s_len, d), norm_ffn[0], w_gu0, w_d0, name="ffn_p").reshape(bp, s_len, d)
    ys = _ffn(ys.reshape(db * tp, d), norm_ffn[0], w_gu0, w_d0, name="ffn_s").reshape(db, tp, d)

    n_heads = len(WINDOWS) * HG
    d_c = n_heads * DH
    gw = HG * DH
    w_qkv = bf(w_qkv_odd[0])
    qk_gain = jnp.concatenate([jnp.tile(q_norm_odd[0], n_heads), jnp.tile(k_norm_odd[0], n_heads),
                               jnp.ones((d_c,), F32)]).reshape(1, 3 * d_c)
    qkv_p = _norm_matmul(yp.reshape(bp * s_len, d), norm_mix[1], w_qkv, head_gain=qk_gain,
                         n_norm_cols=2 * d_c, dh=DH, slabs=True, tn=d_c, name="qkv_p")
    qkv_s = _norm_matmul(ys.reshape(db * tp, d), norm_mix[1], w_qkv, head_gain=qk_gain,
                         n_norm_cols=2 * d_c, dh=DH, slabs=True, name="qkv_s")
    gs = gw // V7X_LANES
    w_o = bf(w_o_odd[0])
    outs_p, lses_p, outs_s, lses_s, p_swa, s_swa = [], [], [], [], [], []

    def kv_rows(qkv, n_batch, t_len, g, lo, hi):
        ks = (len(WINDOWS) + g) * gs
        vs = (2 * len(WINDOWS) + g) * gs
        slabs = qkv.reshape(-1, n_batch, t_len, V7X_LANES)
        kv = jnp.stack([slabs[ks:ks + gs, :, lo:hi], slabs[vs:vs + gs, :, lo:hi]])
        return jnp.transpose(kv, (2, 3, 0, 1, 4)).reshape(n_batch, hi - lo, 2, HG, DH)

    for g, dil in enumerate(DILATIONS):
        tab = rel_bias[:, g * HG:(g + 1) * HG]
        o, l = _dilated_prompt(qkv_p, tab, g, dil, bp)
        outs_p.append(o)
        lses_p.append(l)
        cache_t = jnp.transpose(caches[g][0], (0, 2, 3, 4, 1))
        o, l, cache_t = _dilated_sample(qkv_s, cache_t, tab, g, dil, t_dec)
        outs_s.append(o)
        lses_s.append(l)
        s_swa.append(jnp.transpose(cache_t, (0, 4, 1, 2, 3))[None])
        win = min(WINDOWS[g], s_len)
        p_swa.append(kv_rows(qkv_p, bp, s_len, g, s_len - win, s_len)[None])
    yp = _merge_wo(outs_p, lses_p, yp.reshape(bp * s_len, d), w_o, name="merge_wo_p").reshape(bp, s_len, d)
    ys = _merge_wo(outs_s, lses_s, ys.reshape(db * tp, d), w_o, name="merge_wo_s").reshape(db, tp, d)

    yp, ys = cross_attention(1, yp, ys)

    yp2 = yp.reshape(bp * s_len, d)
    ys2 = ys[:, :t_dec].reshape(db * t_dec, d)
    zero_cnt = jnp.zeros((N_EXPERTS, 1), F32)
    hn_p, gate_p, lp_p, base_p, tcnt_p, cnt_p = _router(
        yp2, norm_ffn[1], moe_router_w[0], moe_router_b[0], zero_cnt, name="router_p")
    hn_s, gate_s, lp_s, base_s, tcnt_s, cnt_s = _router(
        ys2, norm_ffn[1], moe_router_w[0], moe_router_b[0], cnt_p, name="router_s")
    n_tok = yp2.shape[0] + ys2.shape[0]
    n_runs = N_EXPERTS * (base_p.shape[0] + base_s.shape[0])
    nt = -(-(n_tok * TOP_K + (V7X_SUBLANES - 1) * n_runs) // MOE_TM) + N_EXPERTS
    (dst_p, dst_s), (rc_p, rc_s), te, n_used = _moe_plan(
        cnt_s[:, 0].astype(I32), (base_p, base_s), (tcnt_p, tcnt_s), MOE_TM, nt)
    xs = jnp.zeros((nt * MOE_TM, d), F32)
    xs = _dispatch(hn_p, lp_p, dst_p, rc_p, xs, name="dispatch_p")
    xs = _dispatch(hn_s, lp_s, dst_s, rc_s, xs, name="dispatch_s")
    ysort = _moe_ffn(xs, te, n_used, moe_w_gu[0], moe_w_down[0])
    yp = _combine(yp2, gate_p, lp_p, dst_p, rc_p, ysort, name="moe_combine_p").reshape(bp, s_len, d)
    ys = _combine(ys2, gate_s, lp_s, dst_s, rc_s, ysort, name="moe_combine_s").reshape(db, t_dec, d)

    return (yp, ys,
            p_state_conv, p_lru.reshape(1, bp, d_lru),
            p_s5r.reshape(1, bp, s5_g, s5_p), p_s5i.reshape(1, bp, s5_g, s5_p),
            p_swa[0], p_swa[1], p_swa[2], jnp.stack(p_mk), jnp.stack(p_mv),
            s_state_conv, s_lru.reshape(1, db, d_lru),
            s_s5r.reshape(1, db, s5_g, s5_p), s_s5i.reshape(1, db, s5_g, s5_p),
            s_swa[0], s_swa[1], s_swa[2])
```

```python
import functools
import math

import jax
import jax.numpy as jnp
import numpy as np
from jax import lax
from jax.experimental import pallas as pl
from jax.experimental.pallas import tpu as pltpu

F32 = jnp.float32
BF16 = jnp.bfloat16
I32 = jnp.int32

EPS = 1e-6
NEG = -1e30

V7X_SUBLANES = 8
V7X_LANES = 128
V7X_VMEM_BYTES = 64 * 1024 * 1024
VMEM_LIMIT = V7X_VMEM_BYTES - 8 * 1024 * 1024

LRU_C = 8.0
CONV_W = 4
WINDOWS = (128, 512, 2048)
DILATIONS = (1, 4, 16)
HG = 8
DH = 64
N_STRIDE = 128
QB = 128
REL_BUCKETS = 32
REL_MAX_DIST = WINDOWS[-1]
XA_HEADS = 4
N_EXPERTS = 8
TOP_K = 2
SAMPLE_PAD_T = 16

MIX_TC = 256
MOE_TM = 1024
MOE_TH = 512
MOE_TOKEN_TILE = 256
SLABS_PER_DOT = 2
FFN_TH = 1024


def _cparams(sem):
    return pltpu.CompilerParams(dimension_semantics=sem, vmem_limit_bytes=VMEM_LIMIT)


def _const_spec(shape):
    nd = len(shape)
    return pl.BlockSpec(shape, lambda *_: (0,) * nd)


def _rms(x, g):
    return x * lax.rsqrt(jnp.mean(x * x, axis=-1, keepdims=True) + EPS) * g


def _norm_matmul_kernel(x_ref, g_ref, w_ref, hg_ref, hs_ref, o_ref, hn_ref, *, n_norm_tiles, dh, slabs,
                        row_split):
    j = pl.program_id(1)

    @pl.when(j == 0)
    def _():
        hn_ref[...] = _rms(x_ref[...], g_ref[...]).astype(BF16)

    tm = hn_ref.shape[0]
    rs = tm // row_split

    def emit(r, val):
        rows = slice(r * rs, (r + 1) * rs)
        if slabs:
            for c in range(o_ref.shape[0]):
                o_ref[c, rows, :] = val[:, c * V7X_LANES:(c + 1) * V7X_LANES]
        else:
            o_ref[rows, :] = val

    def tile(r, normed):
        y = jnp.dot(hn_ref[r * rs:(r + 1) * rs, :], w_ref[...], preferred_element_type=F32)
        if normed:
            sw = hs_ref.shape[0]
            parts = []
            for s in range(y.shape[1] // sw):
                part = y[:, s * sw:(s + 1) * sw]
                ssq = jnp.dot((part * part).astype(BF16), hs_ref[...], preferred_element_type=F32)
                parts.append(part * lax.rsqrt(ssq * (1.0 / dh) + EPS) * hg_ref[:, s * sw:(s + 1) * sw])
            y = parts[0] if len(parts) == 1 else jnp.concatenate(parts, axis=-1)
        emit(r, y)

    if n_norm_tiles == 0:
        for r in range(row_split):
            tile(r, False)
    else:
        @pl.when(j < n_norm_tiles)
        def _():
            for r in range(row_split):
                tile(r, True)

        @pl.when(j >= n_norm_tiles)
        def _():
            for r in range(row_split):
                tile(r, False)


def _head_sum_matrix(tn, dh):
    idx = np.arange(tn) // dh
    return jnp.asarray((idx[:, None] == idx[None, :]).astype(np.float32), dtype=BF16)


def _norm_matmul(x, g, w, *, tn=512, head_gain=None, n_norm_cols=0, dh=1, slabs=False, name):
    m, d = x.shape
    n = w.shape[1]
    tm = min(m, 1024)
    assert m % tm == 0 and n % tn == 0 and n_norm_cols % tn == 0
    if head_gain is None:
        head_gain = jnp.ones((1, n), F32)
    hs = _head_sum_matrix(min(tn, 512), dh) if n_norm_cols else jnp.zeros((V7X_SUBLANES, V7X_LANES), BF16)
    row_split = 2 if tm % 512 == 0 else 1
    kern = functools.partial(_norm_matmul_kernel, n_norm_tiles=n_norm_cols // tn, dh=dh, slabs=slabs,
                             row_split=row_split)
    if slabs:
        out_shape = jax.ShapeDtypeStruct((n // V7X_LANES, m, V7X_LANES), F32)
        out_spec = pl.BlockSpec((tn // V7X_LANES, tm, V7X_LANES), lambda i, j: (j, i, 0))
    else:
        out_shape = jax.ShapeDtypeStruct((m, n), F32)
        out_spec = pl.BlockSpec((tm, tn), lambda i, j: (i, j))
    return pl.pallas_call(
        kern,
        out_shape=out_shape,
        grid=(m // tm, n // tn),
        in_specs=[
            pl.BlockSpec((tm, d), lambda i, j: (i, 0)),
            pl.BlockSpec((1, d), lambda i, j: (0, 0)),
            pl.BlockSpec((d, tn), lambda i, j: (0, j)),
            pl.BlockSpec((1, tn), lambda i, j: (0, j)),
            _const_spec(hs.shape),
        ],
        out_specs=out_spec,
        scratch_shapes=[pltpu.VMEM((tm, d), BF16)],
        compiler_params=_cparams(("parallel", "arbitrary")),
        name=name,
    )(x, g.reshape(1, d), w, head_gain, hs)


def _s5_prep_kernel(lre_ref, lim_ref, ldt_ref, bre_ref, bim_ref, bbr_ref, bbi_ref, apow_ref):
    lr = lre_ref[...]
    li = lim_ref[...]
    dt = jnp.exp(ldt_ref[...])
    mag = jnp.exp(lr * dt)
    ab_r = mag * jnp.cos(li * dt)
    ab_i = mag * jnp.sin(li * dt)
    den = lr * lr + li * li
    nr = ab_r - 1.0
    cr = (nr * lr + ab_i * li) / den
    ci = (ab_i * lr - nr * li) / den
    n_slab, _, sw = bre_ref.shape
    for c in range(n_slab):
        cr_c = cr[:, c * sw:(c + 1) * sw]
        ci_c = ci[:, c * sw:(c + 1) * sw]
        b_r = bre_ref[c]
        b_i = bim_ref[c]
        bbr_ref[c] = (cr_c * b_r - ci_c * b_i).astype(BF16)
        bbi_ref[c] = (cr_c * b_i + ci_c * b_r).astype(BF16)

    n = lr.shape[1]
    row = lax.broadcasted_iota(I32, (V7X_SUBLANES, n), 0)

    def power(kf):
        mg = jnp.exp(kf * (lr * dt))
        return mg * jnp.cos(kf * (li * dt)), mg * jnp.sin(kf * (li * dt))

    for i, s in enumerate((1, 2, 4)):
        pr, pi = power(jnp.full((V7X_SUBLANES, n), s, F32))
        keep = row >= s
        apow_ref[2 * i] = jnp.where(keep, pr, 0.0)
        apow_ref[2 * i + 1] = jnp.where(keep, pi, 0.0)
    pr, pi = power((row + 1).astype(F32))
    apow_ref[6] = pr
    apow_ref[7] = pi


def _s5_prep(lam_re, lam_im, log_dt, b_re, b_im):
    g, p, h = b_re.shape
    n = g * p
    n_slab = g * h // V7X_LANES
    gl = g // n_slab
    eye = jnp.eye(gl, dtype=F32)

    def slab_blocks(b):
        return jnp.einsum("cgph,gk->ckhgp", b.reshape(n_slab, gl, p, h), eye).reshape(n_slab, V7X_LANES, gl * p)

    bre_bd = slab_blocks(b_re)
    bim_bd = slab_blocks(b_im)
    ldt = jnp.broadcast_to(log_dt[:, None], (g, p)).reshape(1, n)
    return pl.pallas_call(
        _s5_prep_kernel,
        out_shape=(
            jax.ShapeDtypeStruct(bre_bd.shape, BF16),
            jax.ShapeDtypeStruct(bre_bd.shape, BF16),
            jax.ShapeDtypeStruct((8, V7X_SUBLANES, n), F32),
        ),
        compiler_params=pltpu.CompilerParams(vmem_limit_bytes=VMEM_LIMIT),
        name="s5_prep",
    )(lam_re.reshape(1, n), lam_im.reshape(1, n), ldt, bre_bd, bim_bd)


def _mixer0_kernel(z_ref, x_ref, convi_ref, h0_ref, sr0_ref, si0_ref,
                   cw_ref, cb_ref, wa_ref, ba_ref, wx_ref, bx_ref, lam_ref,
                   bbr_ref, bbi_ref, apow_ref, ccr_ref, cci_ref, d_ref, wglu_ref, bglu_ref, wout_ref,
                   y_ref, hl_ref, srl_ref, sil_ref,
                   ext_ref, hc_ref, src_ref, sic_ref, xr_ref, xi_ref, ha_ref, hb_ref,
                   *, tc, last_row):
    c = pl.program_id(1)
    d_lru = cw_ref.shape[1]
    d_s5 = d_ref.shape[1]
    n_tiles = tc // V7X_SUBLANES

    @pl.when(c == 0)
    def _():
        ext_ref[...] = convi_ref[...]
        hc_ref[...] = h0_ref[...]
        src_ref[...] = sr0_ref[...]
        sic_ref[...] = si0_ref[...]

    z = z_ref[...]
    xa = z[:, :d_lru]
    ga = z[:, d_lru:2 * d_lru]
    u = z[:, 2 * d_lru:]

    ext = jnp.concatenate([ext_ref[...], xa], axis=0)
    xc = cb_ref[...] + xa * cw_ref[CONV_W - 1:CONV_W, :]
    for s in range(1, CONV_W):
        xc = xc + pltpu.roll(ext, s, 0)[V7X_SUBLANES:, :] * cw_ref[CONV_W - 1 - s:CONV_W - s, :]
    ext_ref[...] = xa[tc - V7X_SUBLANES:, :]

    xcb = xc.astype(BF16)
    r = jax.nn.sigmoid(jnp.dot(xcb, wa_ref[...], preferred_element_type=F32) + ba_ref[...])
    ig = jax.nn.sigmoid(jnp.dot(xcb, wx_ref[...], preferred_element_type=F32) + bx_ref[...])
    lam = lam_ref[...]
    softplus_neg = jnp.maximum(-lam, 0.0) + jnp.log1p(jnp.exp(-jnp.abs(lam)))
    log_a = -LRU_C * r * softplus_neg
    a = jnp.exp(log_a)
    bt = jnp.sqrt(-jnp.tanh(log_a) * (a * a + 1.0)) * ig * xc

    a3 = a.reshape(n_tiles, V7X_SUBLANES, d_lru)
    b3 = bt.reshape(n_tiles, V7X_SUBLANES, d_lru)
    row = lax.broadcasted_iota(I32, (1, V7X_SUBLANES, d_lru), 1)
    for s in (1, 2, 4):
        keep = row >= s
        ar = pltpu.roll(a3, s, 1)
        br = pltpu.roll(b3, s, 1)
        b3 = jnp.where(keep, a3 * br + b3, b3)
        a3 = jnp.where(keep, a3 * ar, a3)
    ha_ref[...] = a3.reshape(tc, d_lru)
    hb_ref[...] = b3.reshape(tc, d_lru)

    def lru_tile(i, carry):
        r0 = pl.multiple_of(i * V7X_SUBLANES, V7X_SUBLANES)
        h = ha_ref[pl.ds(r0, V7X_SUBLANES), :] * carry + hb_ref[pl.ds(r0, V7X_SUBLANES), :]
        hb_ref[pl.ds(r0, V7X_SUBLANES), :] = h
        return h[V7X_SUBLANES - 1:, :]

    hc_ref[...] = lax.fori_loop(0, n_tiles, lru_tile, hc_ref[...])
    hs = hb_ref[...]
    hl_ref[...] = hb_ref[last_row:last_row + 1, :]
    ya = hs * jax.nn.gelu(ga)

    ub = u.astype(BF16)
    n_slab, _, sw = bbr_ref.shape
    n_state = n_slab * sw

    def input_dot(w_ref):
        return jnp.concatenate(
            [jnp.dot(ub[:, c * V7X_LANES:(c + 1) * V7X_LANES], w_ref[c], preferred_element_type=F32)
             for c in range(n_slab)], axis=-1)

    xr3 = input_dot(bbr_ref).reshape(n_tiles, V7X_SUBLANES, n_state)
    xi3 = input_dot(bbi_ref).reshape(n_tiles, V7X_SUBLANES, n_state)
    for i, s in enumerate((1, 2, 4)):
        cr = apow_ref[2 * i][None]
        ci = apow_ref[2 * i + 1][None]
        rr = pltpu.roll(xr3, s, 1)
        ri = pltpu.roll(xi3, s, 1)
        xr3, xi3 = xr3 + cr * rr - ci * ri, xi3 + cr * ri + ci * rr
    xr_ref[...] = xr3.reshape(tc, n_state)
    xi_ref[...] = xi3.reshape(tc, n_state)

    def s5_tile(i, carry):
        cr_, ci_ = carry
        r0 = pl.multiple_of(i * V7X_SUBLANES, V7X_SUBLANES)
        pr = apow_ref[6]
        pi = apow_ref[7]
        nr_ = xr_ref[pl.ds(r0, V7X_SUBLANES), :] + pr * cr_ - pi * ci_
        ni_ = xi_ref[pl.ds(r0, V7X_SUBLANES), :] + pr * ci_ + pi * cr_
        xr_ref[pl.ds(r0, V7X_SUBLANES), :] = nr_
        xi_ref[pl.ds(r0, V7X_SUBLANES), :] = ni_
        return nr_[V7X_SUBLANES - 1:, :], ni_[V7X_SUBLANES - 1:, :]

    cr_f, ci_f = lax.fori_loop(0, n_tiles, s5_tile, (src_ref[...], sic_ref[...]))
    src_ref[...] = cr_f
    sic_ref[...] = ci_f
    srl_ref[...] = xr_ref[last_row:last_row + 1, :]
    sil_ref[...] = xi_ref[last_row:last_row + 1, :]

    def output_dot(x_ref_, w_ref):
        return jnp.concatenate(
            [jnp.dot(x_ref_[:, c * sw:(c + 1) * sw].astype(BF16), w_ref[c], preferred_element_type=F32)
             for c in range(n_slab)], axis=-1)

    ys = output_dot(xr_ref, ccr_ref) - output_dot(xi_ref, cci_ref)
    ys = ys + d_ref[...] * u
    gs = jax.nn.gelu(ys)
    yb = gs * jax.nn.sigmoid(jnp.dot(gs.astype(BF16), wglu_ref[...], preferred_element_type=F32)
                             + bglu_ref[...])

    y_ref[...] = (x_ref[...]
                  + jnp.dot(ya.astype(BF16), wout_ref[:d_lru, :], preferred_element_type=F32)
                  + jnp.dot(yb.astype(BF16), wout_ref[d_lru:, :], preferred_element_type=F32))


def _mixer0(z, x, conv_init, h0, sr0, si0, wts, *, tc, last_row):
    bn, t_len, d_in = z.shape
    d = x.shape[2]
    d_lru = wts["cw"].shape[1]
    n_state = wts["apow"].shape[-1]
    assert t_len % tc == 0
    kern = functools.partial(_mixer0_kernel, tc=tc, last_row=last_row)
    wnames = ("cw", "cb", "wa", "ba", "wx", "bx", "lam", "bbr", "bbi", "apow",
              "ccr", "cci", "d", "wglu", "bglu", "wout")
    wlist = [wts[k] for k in wnames]
    per_b = lambda shape: pl.BlockSpec((None,) + shape, lambda b, c: (b,) + (0,) * len(shape))
    in_specs = [
        pl.BlockSpec((None, tc, d_in), lambda b, c: (b, c, 0)),
        pl.BlockSpec((None, tc, d), lambda b, c: (b, c, 0)),
        per_b((V7X_SUBLANES, d_lru)), per_b((1, d_lru)), per_b((1, n_state)), per_b((1, n_state)),
    ] + [_const_spec(w.shape) for w in wlist]
    return pl.pallas_call(
        kern,
        out_shape=(
            jax.ShapeDtypeStruct((bn, t_len, d), F32),
            jax.ShapeDtypeStruct((bn, 1, d_lru), F32),
            jax.ShapeDtypeStruct((bn, 1, n_state), F32),
            jax.ShapeDtypeStruct((bn, 1, n_state), F32),
        ),
        grid=(bn, t_len // tc),
        in_specs=in_specs,
        out_specs=(
            pl.BlockSpec((None, tc, d), lambda b, c: (b, c, 0)),
            per_b((1, d_lru)), per_b((1, n_state)), per_b((1, n_state)),
        ),
        scratch_shapes=[
            pltpu.VMEM((V7X_SUBLANES, d_lru), F32),
            pltpu.VMEM((1, d_lru), F32),
            pltpu.VMEM((1, n_state), F32),
            pltpu.VMEM((1, n_state), F32),
            pltpu.VMEM((tc, n_state), F32),
            pltpu.VMEM((tc, n_state), F32),
            pltpu.VMEM((tc, d_lru), F32),
            pltpu.VMEM((tc, d_lru), F32),
        ],
        compiler_params=_cparams(("parallel", "arbitrary")),
        name="mixer0",
    )(z, x, conv_init, h0, sr0, si0, *wlist)


def _xattn_kernel(x_ref, g_ref, wq_ref, qg_ref, mk_ref, mv_ref, wo_ref, o_ref):
    nb, tm, d = x_ref.shape
    x = x_ref[...].reshape(nb * tm, d)
    q = jnp.dot(_rms(x, g_ref[...]).astype(BF16), wq_ref[...], preferred_element_type=F32)
    dh = qg_ref.shape[1]
    head_major = len(mk_ref.shape) == 4
    rows = []
    for b in range(nb):
        if head_major:
            kb = pltpu.einshape("nhd->hnd", mk_ref[b])
            vb = pltpu.einshape("nhd->hnd", mv_ref[b])
            head_of = lambda arr, h: arr[h]
        else:
            kb, vb = mk_ref[b], mv_ref[b]
            head_of = lambda arr, h: arr[:, h * dh:(h + 1) * dh]
        outs = []
        for h in range(XA_HEADS):
            qn = _rms(q[b * tm:(b + 1) * tm, h * dh:(h + 1) * dh], qg_ref[...]).astype(BF16)
            s = lax.dot_general(qn, head_of(kb, h).astype(BF16), (((1,), (1,)), ((), ())),
                                preferred_element_type=F32) * (dh ** -0.5)
            m = jnp.max(s, axis=-1, keepdims=True)
            p = jnp.exp(s - m)
            den = jnp.sum(p, axis=-1, keepdims=True)
            oh = jnp.dot(p.astype(BF16), head_of(vb, h).astype(BF16), preferred_element_type=F32) / den
            outs.append(oh.astype(BF16))
        rows.append(jnp.concatenate(outs, axis=-1))
    o = rows[0] if nb == 1 else jnp.concatenate(rows, axis=0)
    o_ref[...] = (x + jnp.dot(o, wo_ref[...], preferred_element_type=F32)).reshape(nb, tm, d)


def _xattn(x, g, wq, qg, mk, mv, wo, *, tm, nb=1, name, layer=None):
    bn, t_len, d = x.shape
    assert t_len % tm == 0 and bn % nb == 0
    if layer is None:
        k_spec = pl.BlockSpec((nb, mk.shape[1], d), lambda b, i: (b, 0, 0))
        v_spec = pl.BlockSpec((nb, mv.shape[1], d), lambda b, i: (b, 0, 1))
    else:
        k_spec = v_spec = pl.BlockSpec((None, nb) + mk.shape[2:], lambda b, i: (layer, b, 0, 0, 0))
    return pl.pallas_call(
        _xattn_kernel,
        out_shape=jax.ShapeDtypeStruct(x.shape, F32),
        grid=(bn // nb, t_len // tm),
        in_specs=[
            pl.BlockSpec((nb, tm, d), lambda b, i: (b, i, 0)),
            _const_spec((1, d)),
            _const_spec(wq.shape),
            _const_spec((1, qg.shape[-1])),
            k_spec,
            v_spec,
            _const_spec(wo.shape),
        ],
        out_specs=pl.BlockSpec((nb, tm, d), lambda b, i: (b, i, 0)),
        compiler_params=_cparams(("parallel", "arbitrary")),
        name=name,
    )(x, g.reshape(1, d), wq, qg.reshape(1, -1), mk, mv, wo)


def _ffn_kernel(x_ref, g_ref, wg_ref, wu_ref, wd_ref, o_ref, hn_ref, acc_ref):
    j = pl.program_id(1)

    @pl.when(j == 0)
    def _():
        hn_ref[...] = _rms(x_ref[...], g_ref[...]).astype(BF16)
        acc_ref[...] = jnp.zeros_like(acc_ref)

    hn = hn_ref[...]
    gate = jnp.dot(hn, wg_ref[...], preferred_element_type=F32)
    up = jnp.dot(hn, wu_ref[...], preferred_element_type=F32)
    act = (jax.nn.silu(gate) * up).astype(BF16)
    acc_ref[...] += jnp.dot(act, wd_ref[...], preferred_element_type=F32)

    @pl.when(j == pl.num_programs(1) - 1)
    def _():
        o_ref[...] = x_ref[...] + acc_ref[...]


def _ffn(x, g, w_gu, w_d, *, name):
    m, d = x.shape
    hid = w_d.shape[0]
    th = FFN_TH
    tm = min(m, 1024)
    nj = hid // th
    assert m % tm == 0 and hid % th == 0
    return pl.pallas_call(
        _ffn_kernel,
        out_shape=jax.ShapeDtypeStruct((m, d), F32),
        grid=(m // tm, nj),
        in_specs=[
            pl.BlockSpec((tm, d), lambda i, j: (i, 0)),
            pl.BlockSpec((1, d), lambda i, j: (0, 0)),
            pl.BlockSpec((d, th), lambda i, j: (0, j)),
            pl.BlockSpec((d, th), lambda i, j: (0, j + nj)),
            pl.BlockSpec((th, d), lambda i, j: (j, 0)),
        ],
        out_specs=pl.BlockSpec((tm, d), lambda i, j: (i, 0)),
        scratch_shapes=[pltpu.VMEM((tm, d), BF16), pltpu.VMEM((tm, d), F32)],
        compiler_params=_cparams(("parallel", "arbitrary")),
        name=name,
    )(x, g.reshape(1, d), w_gu, w_gu, w_d)


def _rel_bucket_np(dist):
    dist = np.clip(np.asarray(dist), 0, None)
    max_exact = REL_BUCKETS // 2
    safe = np.maximum(dist, max_exact).astype(np.float32)
    large = max_exact + np.floor(np.log(safe / max_exact) / math.log(REL_MAX_DIST / max_exact)
                                 * (REL_BUCKETS - max_exact)).astype(np.int32)
    large = np.minimum(large, REL_BUCKETS - 1)
    return np.where(dist < max_exact, dist, large).astype(np.int32)


def _dil_kernel(q_ref, k_ref, v_ref, bias_ref, o_ref, l_ref, kprev_ref, vprev_ref, *, d, nblk):
    n = pl.program_id(1)

    @pl.when(n == 0)
    def _():
        kprev_ref[...] = jnp.zeros_like(kprev_ref)
        vprev_ref[...] = jnp.zeros_like(vprev_ref)

    first_sel = jnp.minimum(n, 1)
    lane = lax.broadcasted_iota(I32, (QB, V7X_LANES), 1)
    n_slab = q_ref.shape[0]
    heads_per_slab = V7X_LANES // DH
    kv_head = lax.broadcasted_iota(I32, (2 * QB, SLABS_PER_DOT * V7X_LANES), 1) // DH
    q_head = lax.broadcasted_iota(I32, (QB, SLABS_PER_DOT * V7X_LANES), 1) // DH

    def residue(r, _):
        prev_k = [kprev_ref[r, c] for c in range(n_slab)]
        prev_v = [vprev_ref[r, c] for c in range(n_slab)]
        for blk in range(nblk):
            rows = pl.ds(blk * QB * d + r, QB, stride=d)
            bsel = first_sel if blk == 0 else 1
            lse = jnp.zeros((QB, V7X_LANES), F32)
            for c0 in range(0, n_slab, SLABS_PER_DOT):
                slabs = range(c0, c0 + SLABS_PER_DOT)
                wide = lambda parts: jnp.concatenate(parts, axis=-1)
                q = wide([(q_ref[c, rows, :] * (DH ** -0.5)).astype(BF16) for c in slabs])
                kcs = [k_ref[c, rows, :].astype(BF16) for c in slabs]
                vcs = [v_ref[c, rows, :].astype(BF16) for c in slabs]
                k = jnp.concatenate([wide([prev_k[c] for c in slabs]), wide(kcs)], axis=0)
                v = jnp.concatenate([wide([prev_v[c] for c in slabs]), wide(vcs)], axis=0)
                n_heads = SLABS_PER_DOT * heads_per_slab
                k_heads = jnp.concatenate([jnp.where(kv_head == hh, k, 0) for hh in range(n_heads)], axis=0)
                v_heads = jnp.concatenate([jnp.where(kv_head == hh, v, 0) for hh in range(n_heads)], axis=0)
                s_all = lax.dot_general(q, k_heads, (((1,), (1,)), ((), ())), preferred_element_type=F32)
                probs = []
                den_lanes = jnp.zeros((QB, SLABS_PER_DOT * V7X_LANES), F32)
                for hh in range(n_heads):
                    h = c0 * heads_per_slab + hh
                    s = s_all[:, hh * 2 * QB:(hh + 1) * 2 * QB] + bias_ref[bsel, h]
                    m = jnp.max(s, axis=-1, keepdims=True)
                    p = jnp.exp(s - m)
                    den = jnp.sum(p, axis=-1, keepdims=True)
                    probs.append(p.astype(BF16))
                    den_lanes = jnp.where(q_head == hh, den, den_lanes)
                    lse = jnp.where(lane == h, m + jnp.log(den), lse)
                o = jnp.dot(wide(probs), v_heads, preferred_element_type=F32) / den_lanes
                for i, c in enumerate(slabs):
                    o_ref[c, rows, :] = o[:, i * V7X_LANES:(i + 1) * V7X_LANES]
                    prev_k[c] = kcs[i]
                    prev_v[c] = vcs[i]
            l_ref[rows, :] = lse
        for c in range(n_slab):
            kprev_ref[r, c] = prev_k[c]
            vprev_ref[r, c] = prev_v[c]
        return 0

    lax.fori_loop(0, d, residue, 0, unroll=min(d, 2))


def _dilated_prompt(qkv, tab, g, d, bn):
    n_slabs, m_rows, _ = qkv.shape
    s_len = m_rows // bn
    gw = HG * DH
    gs = gw // V7X_LANES
    nblk = 2 if d == 1 else 1
    chunk = nblk * QB * d
    n_chunks = s_len // chunk
    assert s_len % chunk == 0
    k_col = n_slabs // gs // 3

    qi = np.arange(QB)[:, None]
    ki = np.arange(2 * QB)[None, :]
    dist = qi + QB - ki
    band = (dist >= 0) & (dist <= N_STRIDE)
    first = band & (ki >= QB)
    onehot = (np.arange(REL_BUCKETS)[:, None] == _rel_bucket_np(dist * d).reshape(1, -1)).astype(np.float32)
    bias = jnp.dot(tab.T.astype(F32), jnp.asarray(onehot),
                   precision=lax.Precision.HIGHEST).reshape(HG, QB, 2 * QB)
    bias2 = jnp.stack([jnp.where(first[None], bias, NEG), jnp.where(band[None], bias, NEG)])

    col = lambda section: (lambda b, n: (section * k_col + g, b * n_chunks + n, 0))
    blk = (gs, chunk, V7X_LANES)
    return pl.pallas_call(
        functools.partial(_dil_kernel, d=d, nblk=nblk),
        out_shape=(jax.ShapeDtypeStruct((gs, m_rows, V7X_LANES), F32),
                   jax.ShapeDtypeStruct((m_rows, V7X_LANES), F32)),
        grid=(bn, n_chunks),
        in_specs=[
            pl.BlockSpec(blk, col(0)), pl.BlockSpec(blk, col(1)), pl.BlockSpec(blk, col(2)),
            _const_spec(bias2.shape),
        ],
        out_specs=(pl.BlockSpec(blk, lambda b, n: (0, b * n_chunks + n, 0)),
                   pl.BlockSpec((chunk, V7X_LANES), lambda b, n: (b * n_chunks + n, 0))),
        scratch_shapes=[pltpu.VMEM((d, gs, QB, V7X_LANES), BF16), pltpu.VMEM((d, gs, QB, V7X_LANES), BF16)],
        compiler_params=_cparams(("parallel", "arbitrary")),
        name=f"dilated_prompt_g{g}",
    )(qkv, qkv, qkv, bias2)


def _split_dot(x, w_ref):
    hi = x.astype(BF16)
    lo = (x - hi.astype(F32)).astype(BF16)
    w = w_ref[...]
    return jnp.dot(hi, w, preferred_element_type=F32) + jnp.dot(lo, w, preferred_element_type=F32)


def _dil_sample_kernel(q_ref, k_ref, v_ref, cache_ref, bias_ref, biasn_ref, o_ref, l_ref, cout_ref, *,
                       t_valid, t_pad):
    n_slab = q_ref.shape[0]
    nb = cache_ref.shape[0]
    w = cache_ref.shape[-1]
    lane = lax.broadcasted_iota(I32, (t_pad, V7X_LANES), 1)
    is_new = lax.broadcasted_iota(I32, (DH, V7X_LANES), 1) >= V7X_LANES - t_valid

    def as_last_columns(x):
        shifted = pltpu.roll(x, t_pad - t_valid, 0)
        tile = jnp.concatenate([jnp.zeros((V7X_LANES - t_pad, x.shape[1]), F32), shifted], axis=0)
        return tile.T

    def sequence(b, _):
        rows = pl.ds(pl.multiple_of(b * t_pad, t_pad), t_pad)
        unslab = lambda ref: jnp.concatenate([ref[c, rows, :] for c in range(n_slab)], axis=-1)
        qn = unslab(q_ref)
        kn = unslab(k_ref)
        vn = unslab(v_ref)
        new_cols = (as_last_columns(kn), as_last_columns(vn))
        qb, kb, vb = qn.astype(BF16), kn.astype(BF16), vn.astype(BF16)
        lse = jnp.zeros((t_pad, V7X_LANES), F32)
        outs = []
        for h in range(HG):
            sl = slice(h * DH, (h + 1) * DH)
            k_t = cache_ref[b, 0, h]
            v_t = cache_ref[b, 1, h]
            s = jnp.dot(qb[:, sl], k_t.astype(BF16), preferred_element_type=F32) * (DH ** -0.5) + bias_ref[h]
            sn = lax.dot_general(qb[:, sl], kb[:, sl], (((1,), (1,)), ((), ())),
                                 preferred_element_type=F32) * (DH ** -0.5) + biasn_ref[h]
            m = jnp.maximum(jnp.max(s, axis=-1, keepdims=True), jnp.max(sn, axis=-1, keepdims=True))
            p = jnp.exp(s - m)
            pn = jnp.exp(sn - m)
            den = jnp.sum(p, axis=-1, keepdims=True) + jnp.sum(pn, axis=-1, keepdims=True)
            pv = lax.dot_general(p.astype(BF16), v_t.astype(BF16), (((1,), (1,)), ((), ())),
                                 preferred_element_type=F32)
            outs.append((pv + jnp.dot(pn.astype(BF16), vb[:, sl], preferred_element_type=F32)) / den)
            lse = jnp.where(lane == h, m + jnp.log(den), lse)
            for kv, old in enumerate((k_t, v_t)):
                moved = pltpu.roll(old, w - t_valid, 1)
                cout_ref[b, kv, h] = moved
                cout_ref[b, kv, h, :, w - V7X_LANES:] = jnp.where(is_new, new_cols[kv][sl, :],
                                                                  moved[:, w - V7X_LANES:])
        o = jnp.concatenate(outs, axis=-1)
        for c in range(n_slab):
            o_ref[c, rows, :] = o[:, c * V7X_LANES:(c + 1) * V7X_LANES]
        l_ref[rows, :] = lse
        return 0

    if nb == 1:
        sequence(0, 0)
    else:
        lax.fori_loop(0, nb, sequence, 0)


def _dilated_sample(qkv, cache, tab, g, d, t_valid):
    n_slabs, m_rows, _ = qkv.shape
    db, wb = cache.shape[0], cache.shape[-1]
    t_pad = m_rows // db
    gw = HG * DH
    gs = gw // V7X_LANES
    assert wb == N_STRIDE * d and t_valid <= V7X_LANES and wb % V7X_LANES == 0
    k_col = n_slabs // gs // 3

    def bias_of(j, ok):
        onehot = (np.arange(REL_BUCKETS)[:, None] == _rel_bucket_np(d * j).reshape(1, -1)) & ok.reshape(1, -1)
        looked_up = jnp.dot(tab.T.astype(F32), jnp.asarray(onehot.astype(np.float32)),
                            precision=lax.Precision.HIGHEST).reshape((HG,) + j.shape)
        return jnp.where(ok[None], looked_up, NEG)

    tq = np.arange(t_pad)[:, None]
    pos = np.arange(wb)[None, :]
    jc = (wb + tq - pos) // d
    bias = bias_of(jc, ((wb + tq - pos) % d == 0) & (jc >= 1) & (jc <= N_STRIDE))
    tn = np.arange(t_pad)[None, :]
    jn = (tq - tn) // d
    bias_new = bias_of(jn, (tn <= tq) & ((tq - tn) % d == 0) & (jn <= N_STRIDE) & (tn < t_valid))

    kern = functools.partial(_dil_sample_kernel, t_valid=t_valid, t_pad=t_pad)
    cache_bytes = 4 * math.prod(cache.shape[1:])
    nb = max(1, min(db, (4 * 1024 * 1024) // cache_bytes))
    assert db % nb == 0
    blk = (gs, nb * t_pad, V7X_LANES)
    cblk = pl.BlockSpec((nb,) + cache.shape[1:], lambda b: (b, 0, 0, 0, 0))
    return pl.pallas_call(
        kern,
        out_shape=(jax.ShapeDtypeStruct((gs, m_rows, V7X_LANES), F32),
                   jax.ShapeDtypeStruct((m_rows, V7X_LANES), F32),
                   jax.ShapeDtypeStruct(cache.shape, F32)),
        grid=(db // nb,),
        in_specs=[
            pl.BlockSpec(blk, lambda b: (g, b, 0)),
            pl.BlockSpec(blk, lambda b: (k_col + g, b, 0)),
            pl.BlockSpec(blk, lambda b: (2 * k_col + g, b, 0)),
            cblk, _const_spec(bias.shape), _const_spec(bias_new.shape),
        ],
        out_specs=(pl.BlockSpec(blk, lambda b: (0, b, 0)),
                   pl.BlockSpec((nb * t_pad, V7X_LANES), lambda b: (b, 0)),
                   cblk),
        compiler_params=_cparams(("parallel",)),
        name=f"dilated_sample_g{g}",
    )(qkv, qkv, qkv, cache, bias, bias_new)


def _merge_wo_kernel(o0_ref, o1_ref, o2_ref, l0_ref, l1_ref, l2_ref, x_ref, hexp_ref, w_ref, out_ref):
    l0, l1, l2 = l0_ref[...], l1_ref[...], l2_ref[...]
    m = jnp.maximum(jnp.maximum(l0, l1), l2)
    e0, e1, e2 = jnp.exp(l0 - m), jnp.exp(l1 - m), jnp.exp(l2 - m)
    inv = 1.0 / (e0 + e1 + e2)
    unslab = lambda ref: jnp.concatenate([ref[c] for c in range(ref.shape[0])], axis=-1)
    o = (_split_dot(e0 * inv, hexp_ref) * unslab(o0_ref) + _split_dot(e1 * inv, hexp_ref) * unslab(o1_ref)
         + _split_dot(e2 * inv, hexp_ref) * unslab(o2_ref))
    out_ref[...] = x_ref[...] + jnp.dot(o.astype(BF16), w_ref[...], preferred_element_type=F32)


def _merge_wo(outs, lses, x, w, *, name):
    m, d = x.shape
    gw = w.shape[0]
    tm = min(m, 512)
    assert m % tm == 0
    head = np.arange(gw) // DH
    hexp = jnp.asarray((np.arange(V7X_LANES)[:, None] == head[None, :]).astype(np.float32), dtype=BF16)
    row = lambda width: pl.BlockSpec((tm, width), lambda i: (i, 0))
    slab = pl.BlockSpec((gw // V7X_LANES, tm, V7X_LANES), lambda i: (0, i, 0))
    return pl.pallas_call(
        _merge_wo_kernel,
        out_shape=jax.ShapeDtypeStruct((m, d), F32),
        grid=(m // tm,),
        in_specs=[slab] * 3 + [row(V7X_LANES)] * 3 + [row(d), _const_spec(hexp.shape), _const_spec(w.shape)],
        out_specs=row(d),
        compiler_params=_cparams(("parallel",)),
        name=name,
    )(*outs, *lses, x, hexp, w)


def _router_kernel(x_ref, g_ref, wr_ref, br_ref, cnt0_ref, tri_ref,
                   hn_ref, gate_ref, lp_ref, tbase_ref, tcnt_ref, cnt_ref, run_ref):
    @pl.when(pl.program_id(0) == 0)
    def _():
        run_ref[...] = cnt0_ref[...]

    hn = _rms(x_ref[...], g_ref[...])
    hn_ref[...] = hn
    wr = wr_ref[...]
    h_hi = hn.astype(BF16)
    h_lo = (hn - h_hi.astype(F32)).astype(BF16)
    w_hi = wr.astype(BF16)
    w_lo = (wr - w_hi.astype(F32)).astype(BF16)
    nt_dot = lambda a, b: lax.dot_general(a, b, (((1,), (1,)), ((), ())), preferred_element_type=F32)
    logits = nt_dot(w_hi, h_hi) + nt_dot(w_lo, h_hi) + nt_dot(w_hi, h_lo) + br_ref[...]
    ne = logits.shape[0]
    eidx = lax.broadcasted_iota(I32, logits.shape, 0)
    m1 = jnp.max(logits, axis=0, keepdims=True)
    i1 = jnp.min(jnp.where(logits == m1, eidx, ne), axis=0, keepdims=True)
    rest = jnp.where(eidx == i1, -jnp.inf, logits)
    m2 = jnp.max(rest, axis=0, keepdims=True)
    i2 = jnp.min(jnp.where(rest == m2, eidx, ne), axis=0, keepdims=True)
    e2 = jnp.exp(m2 - m1)
    g1 = 1.0 / (1.0 + e2)
    g2 = e2 / (1.0 + e2)
    gate_ref[...] = jnp.where(eidx == 0, g1, jnp.where(eidx == 1, g2, 0.0))
    chosen = jnp.where(jnp.logical_or(eidx == i1, eidx == i2), 1.0, 0.0)
    before = jnp.dot(chosen.astype(BF16), tri_ref[...], preferred_element_type=F32)
    tile_cnt = jnp.ceil(jnp.sum(chosen, axis=1, keepdims=True) * (1.0 / V7X_SUBLANES)) * V7X_SUBLANES
    ecol = eidx[:, :1]
    lower = jnp.zeros_like(tile_cnt)
    for e in range(ne - 1):
        lower = lower + jnp.where(ecol > e, tile_cnt[e:e + 1, :], 0.0)
    local = before + lower
    l1 = jnp.sum(jnp.where(eidx == i1, local, 0.0), axis=0, keepdims=True)
    l2 = jnp.sum(jnp.where(eidx == i2, local, 0.0), axis=0, keepdims=True)
    lp_ref[...] = jnp.where(eidx == 0, l1, jnp.where(eidx == 1, l2, 0.0)).astype(I32)
    tbase_ref[0] = run_ref[...]
    tcnt_ref[0] = tile_cnt
    run_ref[...] += tile_cnt
    cnt_ref[...] = run_ref[...]


def _router(x, g, wr, br, cnt0, *, name):
    m, d = x.shape
    ne = wr.shape[1]
    tm = min(m, MOE_TOKEN_TILE)
    assert m % tm == 0
    n_tiles = m // tm
    tri = jnp.asarray(np.triu(np.ones((tm, tm), np.float32), 1), dtype=BF16)
    cols = pl.BlockSpec((ne, tm), lambda i: (0, i))
    per_tile = pl.BlockSpec((1, ne, 1), lambda i: (i, 0, 0))
    return pl.pallas_call(
        _router_kernel,
        out_shape=(jax.ShapeDtypeStruct((m, d), F32),
                   jax.ShapeDtypeStruct((ne, m), F32),
                   jax.ShapeDtypeStruct((ne, m), I32),
                   jax.ShapeDtypeStruct((n_tiles, ne, 1), F32),
                   jax.ShapeDtypeStruct((n_tiles, ne, 1), F32),
                   jax.ShapeDtypeStruct((ne, 1), F32)),
        grid=(n_tiles,),
        in_specs=[pl.BlockSpec((tm, d), lambda i: (i, 0)), _const_spec((1, d)), _const_spec((ne, d)),
                  _const_spec((ne, 1)), _const_spec((ne, 1)), _const_spec((tm, tm))],
        out_specs=(pl.BlockSpec((tm, d), lambda i: (i, 0)), cols, cols, per_tile, per_tile,
                   _const_spec((ne, 1))),
        scratch_shapes=[pltpu.VMEM((ne, 1), F32)],
        compiler_params=_cparams(("arbitrary",)),
        name=name,
    )(x, g.reshape(1, d), wr.T, br.reshape(ne, 1), cnt0, tri)


def _staging_rows(n_tokens):
    return TOP_K * n_tokens + V7X_SUBLANES * N_EXPERTS


def _expert_run_copies(tile, dst_ref, cnt_ref, n_tokens, make_copy, wait):
    local = jnp.int32(0)
    for e in range(N_EXPERTS):
        count = cnt_ref[tile * N_EXPERTS + e]
        sorted_row = dst_ref[tile * N_EXPERTS + e]
        size = pl.next_power_of_2(n_tokens)
        while size >= V7X_SUBLANES:
            has = (count & size) != 0

            @pl.when(has)
            def _(local=local, sorted_row=sorted_row, size=size):
                copy = make_copy(pl.multiple_of(local, V7X_SUBLANES), pl.multiple_of(sorted_row, V7X_SUBLANES), size)
                if wait:
                    copy.wait()
                else:
                    copy.start()

            step = jnp.where(has, size, 0)
            local = local + step
            sorted_row = sorted_row + step
            size //= 2


def _dispatch_kernel(dst_ref, cnt_ref, hn_ref, lp_ref, xs_in_ref, xs_ref, stage_ref, sem, *, tmd):
    del xs_in_ref
    i = pl.program_id(0)
    n = pl.num_programs(0)
    slot = i % 2

    def copies(tile, s, wait):
        _expert_run_copies(tile, dst_ref, cnt_ref, tmd, lambda local, sorted_row, size: pltpu.make_async_copy(
            stage_ref.at[s, pl.ds(local, size)], xs_ref.at[pl.ds(sorted_row, size)], sem.at[s]), wait)

    @pl.when(i >= 2)
    def _():
        copies(i - 2, slot, True)

    lp = lp_ref[...]
    row = lax.broadcasted_iota(I32, (stage_ref.shape[1], tmd), 0)
    place = (jnp.where(row == lp[0:1, :], 1.0, 0.0) + jnp.where(row == lp[1:2, :], 1.0, 0.0)).astype(BF16)
    stage_ref[slot] = jnp.dot(place, hn_ref[...].astype(BF16), preferred_element_type=F32)
    copies(i, slot, False)

    @pl.when(i == n - 1)
    def _():
        copies(i, slot, True)

        @pl.when(n >= 2)
        def _():
            copies(i - 1, 1 - slot, True)


def _dispatch(hn, lp, dst, cnt, xs, *, name):
    m, d = hn.shape
    ne = lp.shape[0]
    tmd = min(m, MOE_TOKEN_TILE)
    assert m % tmd == 0
    kern = functools.partial(_dispatch_kernel, tmd=tmd)
    grid_spec = pltpu.PrefetchScalarGridSpec(
        num_scalar_prefetch=2,
        grid=(m // tmd,),
        in_specs=[
            pl.BlockSpec((tmd, d), lambda i, dst, cnt: (i, 0)),
            pl.BlockSpec((ne, tmd), lambda i, dst, cnt: (0, i)),
            pl.BlockSpec(memory_space=pl.ANY),
        ],
        out_specs=pl.BlockSpec(memory_space=pl.ANY),
        scratch_shapes=[pltpu.VMEM((2, _staging_rows(tmd), d), F32), pltpu.SemaphoreType.DMA((2,))],
    )
    return pl.pallas_call(
        kern,
        out_shape=jax.ShapeDtypeStruct(xs.shape, xs.dtype),
        grid_spec=grid_spec,
        input_output_aliases={4: 0},
        compiler_params=_cparams(("arbitrary",)),
        name=name,
    )(dst, cnt, hn, lp, xs)


def _moe_ffn_kernel(te_ref, nused_ref, x_ref, wg_ref, wu_ref, wd_ref, o_ref, xb_ref, acc_ref):
    t = pl.program_id(0)
    j = pl.program_id(1)
    last = pl.num_programs(1) - 1
    used = t < nused_ref[0]

    @pl.when(used)
    def _():
        @pl.when(j == 0)
        def _():
            xb_ref[...] = x_ref[...].astype(BF16)
            acc_ref[...] = jnp.zeros_like(acc_ref)

        xb = xb_ref[...]
        gate = jnp.dot(xb, wg_ref[...].astype(BF16), preferred_element_type=F32)
        up = jnp.dot(xb, wu_ref[...].astype(BF16), preferred_element_type=F32)
        act = (jax.nn.silu(gate) * up).astype(BF16)
        acc_ref[...] += jnp.dot(act, wd_ref[...].astype(BF16), preferred_element_type=F32)

        @pl.when(j == last)
        def _():
            o_ref[...] = acc_ref[...]

    @pl.when(jnp.logical_and(jnp.logical_not(used), j == last))
    def _():
        o_ref[...] = jnp.zeros_like(o_ref)


def _moe_ffn(xs, te, n_used, w_gu, w_d):
    n_rows, d = xs.shape
    hid = w_d.shape[1]
    tm, th = MOE_TM, MOE_TH
    nj = hid // th
    nt = n_rows // tm
    assert hid % th == 0 and n_rows % tm == 0

    def jj(t, j, nu):
        return jnp.where(t < nu[0], j, nj - 1)

    grid_spec = pltpu.PrefetchScalarGridSpec(
        num_scalar_prefetch=2,
        grid=(nt, nj),
        in_specs=[
            pl.BlockSpec((tm, d), lambda t, j, te, nu: (jnp.maximum(jnp.minimum(t, nu[0] - 1), 0), 0)),
            pl.BlockSpec((None, d, th), lambda t, j, te, nu: (te[t], 0, jj(t, j, nu))),
            pl.BlockSpec((None, d, th), lambda t, j, te, nu: (te[t], 0, jj(t, j, nu) + nj)),
            pl.BlockSpec((None, th, d), lambda t, j, te, nu: (te[t], jj(t, j, nu), 0)),
        ],
        out_specs=pl.BlockSpec((tm, d), lambda t, j, te, nu: (t, 0)),
        scratch_shapes=[pltpu.VMEM((tm, d), BF16), pltpu.VMEM((tm, d), F32)],
    )
    return pl.pallas_call(
        _moe_ffn_kernel,
        out_shape=jax.ShapeDtypeStruct((n_rows, d), F32),
        grid_spec=grid_spec,
        compiler_params=_cparams(("arbitrary", "arbitrary")),
        name="moe_experts",
    )(te, n_used, xs, w_gu, w_gu, w_d)


def _combine_kernel(dst_ref, cnt_ref, x_ref, gate_ref, lp_ref, ys_ref, o_ref, buf_ref, sem, *, tmc):
    i = pl.program_id(0)
    n = pl.num_programs(0)
    slot = i % 2

    def gather(tile, s, wait):
        _expert_run_copies(tile, dst_ref, cnt_ref, tmc, lambda local, sorted_row, size: pltpu.make_async_copy(
            ys_ref.at[pl.ds(sorted_row, size)], buf_ref.at[s, pl.ds(local, size)], sem.at[s]), wait)

    @pl.when(i == 0)
    def _():
        buf_ref[...] = jnp.zeros_like(buf_ref)
        gather(i, slot, False)

    @pl.when(i + 1 < n)
    def _():
        gather(i + 1, 1 - slot, False)

    gather(i, slot, True)
    g = gate_ref[...]
    lp = lp_ref[...]
    row = lax.broadcasted_iota(I32, (buf_ref.shape[1], tmc), 0)
    w = jnp.where(row == lp[0:1, :], g[0:1, :], 0.0) + jnp.where(row == lp[1:2, :], g[1:2, :], 0.0)
    rows = buf_ref[slot]
    w_hi = w.astype(BF16)
    w_lo = (w - w_hi.astype(F32)).astype(BF16)
    r_hi = rows.astype(BF16)
    r_lo = (rows - r_hi.astype(F32)).astype(BF16)
    tn_dot = lambda a, b: lax.dot_general(a, b, (((0,), (0,)), ((), ())), preferred_element_type=F32)
    o_ref[...] = x_ref[...] + tn_dot(w_hi, r_hi) + tn_dot(w_hi, r_lo) + tn_dot(w_lo, r_hi)


def _combine(x, gate, lp, dst, cnt, ys, *, name):
    m, d = x.shape
    ne = gate.shape[0]
    tmc = min(m, MOE_TOKEN_TILE)
    assert m % tmc == 0
    kern = functools.partial(_combine_kernel, tmc=tmc)
    row = lambda width: pl.BlockSpec((tmc, width), lambda i, dst, cnt: (i, 0))
    cols = pl.BlockSpec((ne, tmc), lambda i, dst, cnt: (0, i))
    grid_spec = pltpu.PrefetchScalarGridSpec(
        num_scalar_prefetch=2,
        grid=(m // tmc,),
        in_specs=[row(d), cols, cols, pl.BlockSpec(memory_space=pl.ANY)],
        out_specs=row(d),
        scratch_shapes=[pltpu.VMEM((2, _staging_rows(tmc), d), F32), pltpu.SemaphoreType.DMA((2,))],
    )
    return pl.pallas_call(
        kern,
        out_shape=jax.ShapeDtypeStruct((m, d), F32),
        grid_spec=grid_spec,
        compiler_params=_cparams(("arbitrary",)),
        name=name,
    )(dst, cnt, x, gate, lp, ys)


def _moe_plan(counts, tile_bases, tile_counts, tm, nt):
    padded = ((counts + tm - 1) // tm) * tm
    ends = jnp.cumsum(padded)
    starts = ends - padded
    dst = [(starts[None, :] + b[:, :, 0].astype(I32)).reshape(-1) for b in tile_bases]
    cnt = [c[:, :, 0].astype(I32).reshape(-1) for c in tile_counts]
    n_used = (ends[-1] // tm).astype(I32)
    tile = jnp.arange(nt, dtype=I32)
    first_row = jnp.minimum(tile, n_used - 1) * tm
    te = jnp.minimum(jnp.sum((first_row[:, None] >= ends[None, :]).astype(I32), axis=1), N_EXPERTS - 1)
    return dst, cnt, te, n_used.reshape(1)


def _block_diag(w):
    n, a, b = w.shape
    return jnp.einsum("nij,nm->nimj", w, jnp.eye(n, dtype=w.dtype)).reshape(n * a, n * b)


def kernel(x_prompt, x_sample, state_conv, state_lru, state_s5_re, state_s5_im, cache_swa0_kv, cache_swa1_kv, cache_swa2_kv, cache_mem_k, cache_mem_v, mem_prompt, norm_mix, norm_xa, norm_ffn, norm_mem, w_in_even, conv_w, conv_b, lru_wa, lru_ba, lru_wx, lru_bx, lru_lam, s5_lam_re, s5_lam_im, s5_log_dt, s5_b_re, s5_b_im, s5_c_re, s5_c_im, s5_d, s5_w_glu, s5_b_glu, w_out_even, w_qkv_odd, q_norm_odd, k_norm_odd, w_o_odd, rel_bias, xa_wq, xa_wkv, xa_qn, xa_kn, xa_wo, ffn_w_gu, ffn_w_down, moe_router_w, moe_router_b, moe_w_gu, moe_w_down):
    bp, s_len, d = x_prompt.shape
    db, t_dec, _ = x_sample.shape
    n_mem = mem_prompt.shape[1]
    tp = SAMPLE_PAD_T
    d_lru = conv_w.shape[-1]
    s5_g, s5_p, s5_h = s5_b_re.shape[1:]
    n_state = s5_g * s5_p
    d_s5 = s5_g * s5_h
    xa_dh = d // XA_HEADS
    caches = (cache_swa0_kv, cache_swa1_kv, cache_swa2_kv)
    bf = lambda w: w.astype(BF16)

    yp = x_prompt
    ys = jnp.pad(x_sample, ((0, 0), (0, tp - t_dec), (0, 0)))

    w_in = bf(w_in_even[0])
    bbr, bbi, apow = _s5_prep(s5_lam_re[0], s5_lam_im[0], s5_log_dt[0], s5_b_re[0], s5_b_im[0])
    n_slab = d_s5 // V7X_LANES
    gl = s5_g // n_slab
    eye_g = jnp.eye(gl, dtype=F32)

    def c_blocks(c_mat):
        blocks = jnp.einsum("cghp,gk->cgpkh", c_mat.reshape(n_slab, gl, s5_h, s5_p), eye_g)
        return bf(blocks.reshape(n_slab, gl * s5_p, V7X_LANES))

    mix_w = dict(
        cw=conv_w[0], cb=conv_b[0].reshape(1, d_lru),
        wa=bf(_block_diag(lru_wa[0])), ba=lru_ba[0].reshape(1, d_lru),
        wx=bf(_block_diag(lru_wx[0])), bx=lru_bx[0].reshape(1, d_lru),
        lam=lru_lam[0].reshape(1, d_lru),
        bbr=bbr, bbi=bbi, apow=apow,
        ccr=c_blocks(s5_c_re[0]), cci=c_blocks(s5_c_im[0]),
        d=s5_d[0].reshape(1, d_s5), wglu=bf(s5_w_glu[0]), bglu=s5_b_glu[0].reshape(1, d_s5),
        wout=bf(w_out_even[0]),
    )
    z_p = _norm_matmul(yp.reshape(bp * s_len, d), norm_mix[0], w_in, tn=w_in.shape[1],
                       name="in_proj_p").reshape(bp, s_len, -1)
    yp, p_lru, p_s5r, p_s5i = _mixer0(
        z_p, yp, jnp.zeros((bp, V7X_SUBLANES, d_lru), F32), jnp.zeros((bp, 1, d_lru), F32),
        jnp.zeros((bp, 1, n_state), F32), jnp.zeros((bp, 1, n_state), F32), mix_w,
        tc=MIX_TC, last_row=MIX_TC - 1)
    z_s = _norm_matmul(ys.reshape(db * tp, d), norm_mix[0], w_in, name="in_proj_s").reshape(db, tp, -1)
    conv_init = jnp.pad(state_conv[0], ((0, 0), (V7X_SUBLANES - (CONV_W - 1), 0), (0, 0)))
    ys, s_lru, s_s5r, s_s5i = _mixer0(
        z_s, ys, conv_init, state_lru[0].reshape(db, 1, d_lru),
        state_s5_re[0].reshape(db, 1, n_state), state_s5_im[0].reshape(db, 1, n_state), mix_w,
        tc=tp, last_row=t_dec - 1)
    p_state_conv = z_p[:, s_len - (CONV_W - 1):, :d_lru][None]
    s_state_conv = z_s[:, t_dec - (CONV_W - 1):t_dec, :d_lru][None]

    p_mk, p_mv = [], []

    def cross_attention(layer, yp, ys):
        kn_gain = jnp.concatenate([jnp.tile(xa_kn[layer], XA_HEADS), jnp.ones((d,), F32)]).reshape(1, 2 * d)
        kv = _norm_matmul(mem_prompt.reshape(bp * n_mem, d), norm_mem[layer], bf(xa_wkv[layer]),
                          head_gain=kn_gain, n_norm_cols=d, dh=xa_dh, name=f"mem_kv{layer}")
        kv = kv.reshape(bp, n_mem, 2 * d)
        mk, mv = kv[:, :, :d], kv[:, :, d:]
        p_mk.append(mk.reshape(bp, n_mem, XA_HEADS, xa_dh))
        p_mv.append(mv.reshape(bp, n_mem, XA_HEADS, xa_dh))
        wq, wo = bf(xa_wq[layer]), bf(xa_wo[layer])
        yp = _xattn(yp, norm_xa[layer], wq, xa_qn[layer], kv, kv, wo, tm=1024, name=f"xattn_p{layer}")
        ys = _xattn(ys, norm_xa[layer], wq, xa_qn[layer], cache_mem_k, cache_mem_v, wo,
                    tm=tp, nb=4, name=f"xattn_s{layer}", layer=layer)
        return yp, ys

    yp, ys = cross_attention(0, yp, ys)
    w_gu0, w_d0 = bf(ffn_w_gu[0]), bf(ffn_w_down[0])
    yp = _ffn(yp.reshape(bp * s_len, d), norm_ffn[0], w_gu0, w_d0, name="ffn_p").reshape(bp, s_len, d)
    ys = _ffn(ys.reshape(db * tp, d), norm_ffn[0], w_gu0, w_d0, name="ffn_s").reshape(db, tp, d)

    n_heads = len(WINDOWS) * HG
    d_c = n_heads * DH
    gw = HG * DH
    w_qkv = bf(w_qkv_odd[0])
    qk_gain = jnp.concatenate([jnp.tile(q_norm_odd[0], n_heads), jnp.tile(k_norm_odd[0], n_heads),
                               jnp.ones((d_c,), F32)]).reshape(1, 3 * d_c)
    qkv_p = _norm_matmul(yp.reshape(bp * s_len, d), norm_mix[1], w_qkv, head_gain=qk_gain,
                         n_norm_cols=2 * d_c, dh=DH, slabs=True, tn=d_c, name="qkv_p")
    qkv_s = _norm_matmul(ys.reshape(db * tp, d), norm_mix[1], w_qkv, head_gain=qk_gain,
                         n_norm_cols=2 * d_c, dh=DH, slabs=True, name="qkv_s")
    gs = gw // V7X_LANES
    w_o = bf(w_o_odd[0])
    outs_p, lses_p, outs_s, lses_s, p_swa, s_swa = [], [], [], [], [], []

    def kv_rows(qkv, n_batch, t_len, g, lo, hi):
        ks = (len(WINDOWS) + g) * gs
        vs = (2 * len(WINDOWS) + g) * gs
        slabs = qkv.reshape(-1, n_batch, t_len, V7X_LANES)
        kv = jnp.stack([slabs[ks:ks + gs, :, lo:hi], slabs[vs:vs + gs, :, lo:hi]])
        return jnp.transpose(kv, (2, 3, 0, 1, 4)).reshape(n_batch, hi - lo, 2, HG, DH)

    for g, dil in enumerate(DILATIONS):
        tab = rel_bias[:, g * HG:(g + 1) * HG]
        o, l = _dilated_prompt(qkv_p, tab, g, dil, bp)
        outs_p.append(o)
        lses_p.append(l)
        cache_t = jnp.transpose(caches[g][0], (0, 2, 3, 4, 1))
        o, l, cache_t = _dilated_sample(qkv_s, cache_t, tab, g, dil, t_dec)
        outs_s.append(o)
        lses_s.append(l)
        s_swa.append(jnp.transpose(cache_t, (0, 4, 1, 2, 3))[None])
        win = min(WINDOWS[g], s_len)
        p_swa.append(kv_rows(qkv_p, bp, s_len, g, s_len - win, s_len)[None])
    yp = _merge_wo(outs_p, lses_p, yp.reshape(bp * s_len, d), w_o, name="merge_wo_p").reshape(bp, s_len, d)
    ys = _merge_wo(outs_s, lses_s, ys.reshape(db * tp, d), w_o, name="merge_wo_s").reshape(db, tp, d)

    yp, ys = cross_attention(1, yp, ys)

    yp2 = yp.reshape(bp * s_len, d)
    ys2 = ys[:, :t_dec].reshape(db * t_dec, d)
    zero_cnt = jnp.zeros((N_EXPERTS, 1), F32)
    hn_p, gate_p, lp_p, base_p, tcnt_p, cnt_p = _router(
        yp2, norm_ffn[1], moe_router_w[0], moe_router_b[0], zero_cnt, name="router_p")
    hn_s, gate_s, lp_s, base_s, tcnt_s, cnt_s = _router(
        ys2, norm_ffn[1], moe_router_w[0], moe_router_b[0], cnt_p, name="router_s")
    n_tok = yp2.shape[0] + ys2.shape[0]
    n_runs = N_EXPERTS * (base_p.shape[0] + base_s.shape[0])
    nt = -(-(n_tok * TOP_K + (V7X_SUBLANES - 1) * n_runs) // MOE_TM) + N_EXPERTS
    (dst_p, dst_s), (rc_p, rc_s), te, n_used = _moe_plan(
        cnt_s[:, 0].astype(I32), (base_p, base_s), (tcnt_p, tcnt_s), MOE_TM, nt)
    xs = jnp.zeros((nt * MOE_TM, d), F32)
    xs = _dispatch(hn_p, lp_p, dst_p, rc_p, xs, name="dispatch_p")
    xs = _dispatch(hn_s, lp_s, dst_s, rc_s, xs, name="dispatch_s")
    ysort = _moe_ffn(xs, te, n_used, moe_w_gu[0], moe_w_down[0])
    yp = _combine(yp2, gate_p, lp_p, dst_p, rc_p, ysort, name="moe_combine_p").reshape(bp, s_len, d)
    ys = _combine(ys2, gate_s, lp_s, dst_s, rc_s, ysort, name="moe_combine_s").reshape(db, t_dec, d)

    return (yp, ys,
            p_state_conv, p_lru.reshape(1, bp, d_lru),
            p_s5r.reshape(1, bp, s5_g, s5_p), p_s5i.reshape(1, bp, s5_g, s5_p),
            p_swa[0], p_swa[1], p_swa[2], jnp.stack(p_mk), jnp.stack(p_mv),
            s_state_conv, s_lru.reshape(1, db, d_lru),
            s_s5r.reshape(1, db, s5_g, s5_p), s_s5i.reshape(1, db, s5_g, s5_p),
            s_swa[0], s_swa[1], s_swa[2])
```
